```python
import math
import jax
import jax.numpy as jnp
from jax import lax
import numpy as np

D_MODEL = 1024
BATCH = 8
SEQ = 2048
DEPTH = 2

F32 = jnp.float32
CTX_LEN = 256
GRID_W = 64
ATTN_W = 3 * D_MODEL // 4
ATTN_HEAD_DIM = 64
ATTN_HEADS = ATTN_W // (2 * ATTN_HEAD_DIM)
ATTN_QK_W = ATTN_HEADS * 2 * ATTN_HEAD_DIM
ATTN_V_W = ATTN_HEADS * 2 * ATTN_HEAD_DIM
F_W = D_MODEL - ATTN_W
F_GROUP_W = 64
F_GROUPS = F_W // F_GROUP_W
ATTN_IN_W = 2 * ATTN_QK_W + ATTN_V_W + F_W
Q_BLOCK = 128
ROPE_BASE = 10000.0
SSD_W = 3 * D_MODEL // 4
SSD_HEAD_DIM = 64
SSD_HEADS = SSD_W // SSD_HEAD_DIM
SSD_GROUPS = 2
SSD_STATE = 128
SSD_CHUNK = 128
XBC_W = SSD_W + 2 * SSD_GROUPS * SSD_STATE
DT_W = 2 * SSD_HEADS
S5_W = D_MODEL - SSD_W
S5_GROUP_W = 16
S5_GROUPS = S5_W // S5_GROUP_W
S5_STATE = 64
SSM_TAIL_W = XBC_W + DT_W + S5_W
SSM_IN_W = SSD_W + SSM_TAIL_W
CONV_K = 3
D_FF = 2816
ALPHA = (2 * DEPTH) ** 0.25
BETA = (8 * DEPTH) ** -0.25
N_EVEN = (DEPTH + 1) // 2
N_ODD = DEPTH // 2
LN_EPS = 1e-5

kernel_name = 'hybrid_diffattn_fnet_s5_ssd_block'


def _flip(t):
    return jnp.flip(t, axis=1)


def layer_norm(x, g, b):
    xf = x.astype(F32)
    mu = jnp.mean(xf, -1, keepdims=True)
    var = jnp.mean(jnp.square(xf - mu), -1, keepdims=True)
    return ((xf - mu) * lax.rsqrt(var + LN_EPS) * g.astype(F32) + b.astype(F32)).astype(x.dtype)


def rms_norm(x, g):
    xf = x.astype(F32)
    return (xf * lax.rsqrt(jnp.mean(jnp.square(xf), -1, keepdims=True) + LN_EPS) * g.astype(F32)).astype(x.dtype)


def modulate(x, shift, scale):
    return x * (1.0 + scale[:, None]) + shift[:, None]


def dwconv_centered(x, w, b):
    k, ch = w.shape
    pad = (k - 1) // 2
    y = lax.conv_general_dilated(x, w[:, None, :].astype(x.dtype), (1,), [(pad, k - 1 - pad)],
                                 dimension_numbers=('NWC', 'WIO', 'NWC'), feature_group_count=ch)
    return y + b


def conv_ffn(u, w_up, b_up, conv_w, conv_b, w_down, b_down):
    hdn = dwconv_centered(u @ w_up + b_up, conv_w, conv_b)
    val, gate = jnp.split(hdn, 2, axis=-1)
    return (val * jax.nn.silu(gate)) @ w_down + b_down


def axial_rope_tables(n_tokens, dim):
    rows = n_tokens // GRID_W
    row = jnp.repeat(jnp.arange(rows, dtype=F32), GRID_W)
    col = jnp.tile(jnp.arange(GRID_W, dtype=F32), rows)
    n_freq = dim // 4
    inv_freq = ROPE_BASE ** (-jnp.arange(n_freq, dtype=F32) / n_freq)
    ang = jnp.concatenate([row[:, None] * inv_freq, col[:, None] * inv_freq], axis=-1)
    return jnp.cos(ang), jnp.sin(ang)


def apply_rope(t, cos, sin):
    half = t.shape[-1] // 2
    tf = t.astype(F32)
    cs, sn = cos[:, None, None, :], sin[:, None, None, :]
    t1, t2 = tf[..., :half], tf[..., half:]
    return jnp.concatenate([t1 * cs - t2 * sn, t1 * sn + t2 * cs], axis=-1).astype(t.dtype)


def _qk_heads(p):
    return p.reshape(p.shape[0], p.shape[1], ATTN_HEADS, 2, ATTN_HEAD_DIM)


def _v_heads(p):
    return p.reshape(p.shape[0], p.shape[1], ATTN_HEADS, 2 * ATTN_HEAD_DIM)


def diff_attention(q, k, v, lam, subln_g, lam_init):
    bsz, n_q = q.shape[0], q.shape[1]
    nb = n_q // Q_BLOCK
    qb = jnp.moveaxis(q.reshape(bsz, nb, Q_BLOCK, ATTN_HEADS, 2, ATTN_HEAD_DIM), 1, 0)
    scale = ATTN_HEAD_DIM ** -0.5

    def block(q_blk):
        s = jnp.einsum('bqhad,bkhad->bahqk', q_blk, k).astype(F32) * scale
        p = jax.nn.softmax(s, axis=-1)
        w = p[:, 0] - lam * p[:, 1]
        return jnp.einsum('bhqk,bkhe->bqhe', w.astype(v.dtype), v)

    o = lax.map(block, qb)
    o = jnp.moveaxis(o, 0, 1).reshape(bsz, n_q, ATTN_HEADS, 2 * ATTN_HEAD_DIM)
    o = rms_norm(o, subln_g) * (1.0 - lam_init)
    return o.reshape(bsz, n_q, ATTN_V_W)


def fourier_mix(f, w, b):
    bsz, n = f.shape[:2]
    fg = f.astype(F32).reshape(bsz, n, F_GROUPS, F_GROUP_W)
    z = jnp.fft.fft2(fg, axes=(1, 3), norm='ortho').real.astype(f.dtype)
    return jnp.einsum('blgc,gce->blge', z, w).reshape(bsz, n, F_W) + b


def attn_fourier_mixer(u_ctx, u_lat, w_in, lam_vec, subln_g, f_w, f_b, w_out, lam_init, cos, sin, need_ctx):
    lf = lam_vec.astype(F32)
    lam = jnp.exp(jnp.sum(lf[0] * lf[1])) - jnp.exp(jnp.sum(lf[2] * lf[3])) + lam_init
    k0, v0, f0 = ATTN_QK_W, 2 * ATTN_QK_W, 2 * ATTN_QK_W + ATTN_V_W
    p_lat = u_lat @ w_in
    q_l = apply_rope(_qk_heads(p_lat[..., :k0]), cos, sin)
    k_l = apply_rope(_qk_heads(p_lat[..., k0:v0]), cos, sin)
    v_l = _v_heads(p_lat[..., v0:f0])
    kv_c = u_ctx @ w_in[:, k0:f0]
    k_c, v_c = _qk_heads(kv_c[..., :ATTN_QK_W]), _v_heads(kv_c[..., ATTN_QK_W:])
    k_all = jnp.concatenate([k_l, k_c], axis=1)
    v_all = jnp.concatenate([v_l, v_c], axis=1)
    o_l = diff_attention(q_l, k_all, v_all, lam, subln_g, lam_init)
    y_lat = jnp.concatenate([o_l, fourier_mix(p_lat[..., f0:], f_w, f_b)], axis=-1) @ w_out
    y_ctx = None
    if need_ctx:
        q_c = _qk_heads(u_ctx @ w_in[:, :k0])
        o_c = diff_attention(q_c, k_c, v_c, lam, subln_g, lam_init)
        y_ctx = jnp.concatenate([o_c, fourier_mix(u_ctx @ w_in[:, f0:], f_w, f_b)], axis=-1) @ w_out
    return y_ctx, y_lat


def ssd_chunked(x, dt, a, bm, cm, h0, want_y):
    bsz, n, H, P = x.shape
    G, N = bm.shape[2], bm.shape[3]
    R = H // G
    Q = SSD_CHUNK
    nc = n // Q
    xc = x.reshape(bsz, nc, Q, G, R, P)
    dtc = dt.reshape(bsz, nc, Q, G, R)
    bc = bm.reshape(bsz, nc, Q, G, N)
    cc = cm.reshape(bsz, nc, Q, G, N)
    acs = jnp.cumsum(dtc * a.reshape(G, R), axis=2)
    w_end = jnp.exp(acs[:, :, -1:] - acs) * dtc
    states = jnp.einsum('bcjgn,bcjgr,bcjgrp->bcgrpn', bc, w_end, xc)
    chunk_decay = jnp.exp(acs[:, :, -1])
    init = jnp.zeros((bsz, G, R, P, N), F32) if h0 is None else h0

    def step(h, inp):
        s, dec = inp
        return h * dec[..., None, None] + s, h

    h_fin, h_in = lax.scan(step, init, (jnp.moveaxis(states, 1, 0), jnp.moveaxis(chunk_decay, 1, 0)))
    if not want_y:
        return None, h_fin
    h_in = jnp.moveaxis(h_in, 0, 1)
    lower = jnp.tril(jnp.ones((Q, Q), bool))
    seg = acs[:, :, :, None] - acs[:, :, None]
    lmat = jnp.exp(jnp.where(lower[:, :, None, None], seg, -jnp.inf))
    cb = jnp.einsum('bcign,bcjgn->bcijg', cc, bc)
    y_diag = jnp.einsum('bcijgr,bcjgrp->bcigrp', cb[..., None] * lmat * dtc[:, :, None], xc)
    y_off = jnp.einsum('bcign,bcgrpn->bcigrp', cc, h_in) * jnp.exp(acs)[..., None]
    return (y_diag + y_off).reshape(bsz, n, H, P), h_fin


def ssm_tail(tail, conv_w, conv_b, dt_bias):
    bsz, n = tail.shape[:2]
    gn = SSD_GROUPS * SSD_STATE
    xbc = jax.nn.silu(dwconv_centered(tail[..., :XBC_W], conv_w, conv_b)).astype(F32)
    xs = xbc[..., :SSD_W].reshape(bsz, n, SSD_HEADS, SSD_HEAD_DIM)
    bm = xbc[..., SSD_W:SSD_W + gn].reshape(bsz, n, SSD_GROUPS, SSD_STATE)
    cm = xbc[..., SSD_W + gn:].reshape(bsz, n, SSD_GROUPS, SSD_STATE)
    dt = jax.nn.softplus(tail[..., XBC_W:XBC_W + DT_W].astype(F32).reshape(bsz, n, 2, SSD_HEADS)
                         + dt_bias.astype(F32))
    us = tail[..., XBC_W + DT_W:].astype(F32).reshape(bsz, n, S5_GROUPS, S5_GROUP_W)
    return xs, bm, cm, dt, us


def s5_discretize(lam_re, lam_im, log_dt, b_re, b_im):
    lr, li = lam_re.astype(F32), lam_im.astype(F32)
    dt = jnp.exp(log_dt.astype(F32))[..., None]
    mag = jnp.exp(dt * lr)
    ab_re, ab_im = mag * jnp.cos(dt * li), mag * jnp.sin(dt * li)
    den = lr * lr + li * li
    k_re = ((ab_re - 1.0) * lr + ab_im * li) / den
    k_im = (ab_im * lr - (ab_re - 1.0) * li) / den
    br, bi = b_re.astype(F32)[None], b_im.astype(F32)[None]
    bb_re = k_re[..., None] * br - k_im[..., None] * bi
    bb_im = k_re[..., None] * bi + k_im[..., None] * br
    return ab_re, ab_im, bb_re, bb_im


def _cplx_affine_combine(e1, e2):
    a1r, a1i, b1r, b1i = e1
    a2r, a2i, b2r, b2i = e2
    return (a1r * a2r - a1i * a2i, a1r * a2i + a1i * a2r,
            a2r * b1r - a2i * b1i + b2r, a2r * b1i + a2i * b1r + b2i)


def s5_scan(ab_re, ab_im, bu_re, bu_im, h0):
    a_re = jnp.broadcast_to(ab_re, bu_re.shape)
    a_im = jnp.broadcast_to(ab_im, bu_re.shape)
    ca_re, ca_im, h_re, h_im = lax.associative_scan(_cplx_affine_combine, (a_re, a_im, bu_re, bu_im), axis=1)
    if h0 is not None:
        h0r, h0i = h0[0][:, None], h0[1][:, None]
        h_re, h_im = h_re + ca_re * h0r - ca_im * h0i, h_im + ca_re * h0i + ca_im * h0r
    return h_re, h_im


def s5_glu(y, w, b):
    bsz, n = y.shape[:2]
    g = jax.nn.gelu(y.reshape(bsz, n, S5_W))
    return g * jax.nn.sigmoid(g @ w.astype(F32) + b.astype(F32))


def s5_bidir(uc, ul, lam_re, lam_im, log_dt, b_re, b_im, c_re, c_im, s5_dd, glu_w, glu_b, need_ctx):
    ab_re, ab_im, bb_re, bb_im = s5_discretize(lam_re, lam_im, log_dt, b_re, b_im)
    cr, ci = c_re.astype(F32), c_im.astype(F32)
    dd = s5_dd.astype(F32).reshape(S5_GROUPS, S5_GROUP_W)

    def run(u, d, h0):
        bu_re = jnp.einsum('blgh,gph->blgp', u, bb_re[d])
        bu_im = jnp.einsum('blgh,gph->blgp', u, bb_im[d])
        return s5_scan(ab_re[d], ab_im[d], bu_re, bu_im, h0)

    def readout(h, d):
        return jnp.einsum('blgp,ghp->blgh', h[0], cr[d]) - jnp.einsum('blgp,ghp->blgh', h[1], ci[d])

    hc_f = run(uc, 0, None)
    hc_b = run(_flip(uc), 1, None)
    hl_f = run(ul, 0, (hc_f[0][:, -1], hc_f[1][:, -1]))
    hl_b = run(_flip(ul), 1, (hc_b[0][:, -1], hc_b[1][:, -1]))
    out_l = s5_glu(readout(hl_f, 0) + _flip(readout(hl_b, 1)) + dd * ul, glu_w, glu_b)
    out_c = None
    if need_ctx:
        out_c = s5_glu(readout(hc_f, 0) + _flip(readout(hc_b, 1)) + dd * uc, glu_w, glu_b)
    return out_c, out_l


def ssm_merge(y_ssd, z, y_s5, norm_g, w_out):
    bsz, n = z.shape[:2]
    g = rms_norm(y_ssd.reshape(bsz, n, SSD_W) * jax.nn.silu(z.astype(F32)), norm_g)
    return jnp.concatenate([g, y_s5], axis=-1).astype(z.dtype) @ w_out


def ssm_mixer(u_ctx, u_lat, w_in, conv_w, conv_b, a_log, dt_bias, d_skip, norm_g, lam_re, lam_im,
              log_dt, b_re, b_im, c_re, c_im, s5_dd, glu_w, glu_b, w_out, need_ctx):
    p_lat = u_lat @ w_in
    p_ctx = u_ctx @ (w_in if need_ctx else w_in[:, SSD_W:])
    x_l, b_l, c_l, dt_l, us_l = ssm_tail(p_lat[..., SSD_W:], conv_w, conv_b, dt_bias)
    x_c, b_c, c_c, dt_c, us_c = ssm_tail(p_ctx[..., -SSM_TAIL_W:], conv_w, conv_b, dt_bias)
    a = -jnp.exp(a_log.astype(F32))
    dsk = d_skip.astype(F32)[:, None]
    yc_f, hc_f = ssd_chunked(x_c, dt_c[:, :, 0], a[0], b_c, c_c, None, need_ctx)
    yc_b, hc_b = ssd_chunked(_flip(x_c), _flip(dt_c[:, :, 1]), a[1], _flip(b_c), _flip(c_c), None, need_ctx)
    yl_f, _ = ssd_chunked(x_l, dt_l[:, :, 0], a[0], b_l, c_l, hc_f, True)
    yl_b, _ = ssd_chunked(_flip(x_l), _flip(dt_l[:, :, 1]), a[1], _flip(b_l), _flip(c_l), hc_b, True)
    s5_c, s5_l = s5_bidir(us_c, us_l, lam_re, lam_im, log_dt, b_re, b_im, c_re, c_im, s5_dd, glu_w, glu_b, need_ctx)
    y_lat = ssm_merge(yl_f + _flip(yl_b) + dsk * x_l, p_lat[..., :SSD_W], s5_l, norm_g, w_out)
    y_ctx = None
    if need_ctx:
        y_ctx = ssm_merge(yc_f + _flip(yc_b) + dsk * x_c, p_ctx[..., :SSD_W], s5_c, norm_g, w_out)
    return y_ctx, y_lat


def post_norm_layer(h, y_mix, mods, ln_g, ln_b, ffn):
    h = layer_norm(ALPHA * h + mods[2][:, None] * y_mix, ln_g[0], ln_b[0])
    f = conv_ffn(modulate(h, mods[3], mods[4]), *ffn)
    return layer_norm(ALPHA * h + mods[5][:, None] * f, ln_g[1], ln_b[1])


def setup_inputs(seed: int = 0) -> dict:
    key = jax.random.key(seed)
    ks = iter(jax.random.split(key, 64))

    def nrm(shape, std):
        return jax.random.normal(next(ks), shape, F32) * std

    def unif(shape, lo, hi):
        return jax.random.uniform(next(ks), shape, F32, lo, hi)

    D = D_MODEL
    dt_lo, dt_hi = math.log(1e-3), math.log(1e-1)
    ssd_dt0 = jnp.exp(unif((N_ODD, 2, SSD_HEADS), dt_lo, dt_hi))
    n_idx = jnp.arange(S5_STATE, dtype=F32)
    inp = {}
    inp['x'] = nrm((BATCH, SEQ, D), 1.0)
    inp['c'] = nrm((BATCH, D), 1.0)
    inp['ctx'] = nrm((BATCH, CTX_LEN, D), 1.0)
    inp['c_ctx'] = nrm((D,), 1.0)
    inp['ada_w'] = nrm((DEPTH, D, 6 * D), D ** -0.5)
    inp['ada_b'] = nrm((DEPTH, 6 * D), 0.02)
    inp['ln_g'] = 1.0 + nrm((DEPTH, 2, D), 0.02)
    inp['ln_b'] = nrm((DEPTH, 2, D), 0.02)
    inp['ffn_w_up'] = nrm((DEPTH, D, 2 * D_FF), D ** -0.5)
    inp['ffn_b_up'] = nrm((DEPTH, 2 * D_FF), 0.02)
    inp['ffn_conv_w'] = nrm((DEPTH, CONV_K, 2 * D_FF), CONV_K ** -0.5)
    inp['ffn_conv_b'] = nrm((DEPTH, 2 * D_FF), 0.02)
    inp['ffn_w_down'] = nrm((DEPTH, D_FF, D), BETA * D_FF ** -0.5)
    inp['ffn_b_down'] = nrm((DEPTH, D), 0.02)
    inp['attn_w_in'] = nrm((N_EVEN, D, ATTN_IN_W), D ** -0.5)
    inp['attn_lambda'] = nrm((N_EVEN, 4, ATTN_HEAD_DIM), 0.1)
    inp['attn_subln_g'] = 1.0 + nrm((N_EVEN, 2 * ATTN_HEAD_DIM), 0.02)
    inp['fourier_w'] = nrm((N_EVEN, F_GROUPS, F_GROUP_W, F_GROUP_W), F_GROUP_W ** -0.5)
    inp['fourier_b'] = nrm((N_EVEN, F_W), 0.02)
    inp['attn_w_out'] = nrm((N_EVEN, D, D), BETA * D ** -0.5)
    inp['ssm_w_in'] = nrm((N_ODD, D, SSM_IN_W), D ** -0.5)
    inp['ssd_conv_w'] = nrm((N_ODD, CONV_K, XBC_W), CONV_K ** -0.5)
    inp['ssd_conv_b'] = nrm((N_ODD, XBC_W), 0.02)
    inp['ssd_a_log'] = jnp.log(unif((N_ODD, 2, SSD_HEADS), 1.0, 16.0))
    inp['ssd_dt_bias'] = ssd_dt0 + jnp.log(-jnp.expm1(-ssd_dt0))
    inp['ssd_d'] = 1.0 + nrm((N_ODD, SSD_HEADS), 0.02)
    inp['ssd_norm_g'] = 1.0 + nrm((N_ODD, SSD_W), 0.02)
    inp['s5_lambda_re'] = -0.5 + nrm((N_ODD, 2, S5_GROUPS, S5_STATE), 0.01)
    inp['s5_lambda_im'] = math.pi * n_idx + nrm((N_ODD, 2, S5_GROUPS, S5_STATE), 0.01)
    inp['s5_log_dt'] = unif((N_ODD, 2, S5_GROUPS), dt_lo, dt_hi)
    inp['s5_b_re'] = nrm((N_ODD, S5_GROUPS, S5_STATE, S5_GROUP_W), (2.0 * S5_GROUP_W) ** -0.5)
    inp['s5_b_im'] = nrm((N_ODD, S5_GROUPS, S5_STATE, S5_GROUP_W), (2.0 * S5_GROUP_W) ** -0.5)
    inp['s5_c_re'] = nrm((N_ODD, 2, S5_GROUPS, S5_GROUP_W, S5_STATE), S5_STATE ** -0.5)
    inp['s5_c_im'] = nrm((N_ODD, 2, S5_GROUPS, S5_GROUP_W, S5_STATE), S5_STATE ** -0.5)
    inp['s5_d'] = nrm((N_ODD, S5_W), 1.0)
    inp['s5_glu_w'] = nrm((N_ODD, S5_W, S5_W), S5_W ** -0.5)
    inp['s5_glu_b'] = nrm((N_ODD, S5_W), 0.02)
    inp['ssm_w_out'] = nrm((N_ODD, D, D), BETA * D ** -0.5)
    return inp


def reference(x, c, ctx, c_ctx, ada_w, ada_b, ln_g, ln_b, ffn_w_up, ffn_b_up, ffn_conv_w, ffn_conv_b,
              ffn_w_down, ffn_b_down, attn_w_in, attn_lambda, attn_subln_g, fourier_w, fourier_b,
              attn_w_out, ssm_w_in, ssd_conv_w, ssd_conv_b, ssd_a_log, ssd_dt_bias, ssd_d, ssd_norm_g,
              s5_lambda_re, s5_lambda_im, s5_log_dt, s5_b_re, s5_b_im, s5_c_re, s5_c_im, s5_d,
              s5_glu_w, s5_glu_b, ssm_w_out):
    cos, sin = axial_rope_tables(x.shape[1], ATTN_HEAD_DIM)
    s_lat = jax.nn.silu(c)
    s_ctx = jax.nn.silu(c_ctx)[None]
    h_lat, h_ctx = x, ctx
    for l in range(DEPTH):
        last = l == DEPTH - 1
        n_mod = 2 if last else 6
        m_lat = jnp.split(s_lat @ ada_w[l] + ada_b[l], 6, axis=-1)
        m_ctx = jnp.split(s_ctx @ ada_w[l][:, :n_mod * D_MODEL] + ada_b[l][:n_mod * D_MODEL], n_mod, axis=-1)
        u_lat = modulate(h_lat, m_lat[0], m_lat[1])
        u_ctx = modulate(h_ctx, m_ctx[0], m_ctx[1])
        i = l // 2
        if l % 2 == 0:
            y_ctx, y_lat = attn_fourier_mixer(u_ctx, u_lat, attn_w_in[i], attn_lambda[i], attn_subln_g[i],
                                              fourier_w[i], fourier_b[i], attn_w_out[i],
                                              0.8 - 0.6 * math.exp(-0.3 * l), cos, sin, not last)
        else:
            y_ctx, y_lat = ssm_mixer(u_ctx, u_lat, ssm_w_in[i], ssd_conv_w[i], ssd_conv_b[i], ssd_a_log[i],
                                     ssd_dt_bias[i], ssd_d[i], ssd_norm_g[i], s5_lambda_re[i], s5_lambda_im[i],
                                     s5_log_dt[i], s5_b_re[i], s5_b_im[i], s5_c_re[i], s5_c_im[i], s5_d[i],
                                     s5_glu_w[i], s5_glu_b[i], ssm_w_out[i], not last)
        ffn = (ffn_w_up[l], ffn_b_up[l], ffn_conv_w[l], ffn_conv_b[l], ffn_w_down[l], ffn_b_down[l])
        h_lat = post_norm_layer(h_lat, y_lat, m_lat, ln_g[l], ln_b[l], ffn)
        if not last:
            h_ctx = post_norm_layer(h_ctx, y_ctx, m_ctx, ln_g[l], ln_b[l], ffn)
    return h_lat
```

```python
import functools
import math

import numpy as np
import jax
import jax.numpy as jnp
from jax import lax
from jax.experimental import pallas as pl
from jax.experimental.pallas import tpu as pltpu

F32 = jnp.float32
BF16 = jnp.bfloat16

D_MODEL = 1024
DEPTH = 2
GRID_W = 64
ROPE_BASE = 10000.0
LN_EPS = 1e-5
ALPHA = (2 * DEPTH) ** 0.25
ATTN_W = 768
ATTN_HEADS = 6
ATTN_HEAD_DIM = 64
F_W = 256
F_GROUPS = 4
F_GROUP_W = 64
ATTN_IN_W = 2 * ATTN_W + ATTN_W + F_W
SSD_W = 768
SSD_HEADS = 12
SSD_HEAD_DIM = 64
SSD_GROUPS = 2
SSD_STATE = 128
SSD_CHUNK = 128
XBC_W = SSD_W + 2 * SSD_GROUPS * SSD_STATE
DT_W = 2 * SSD_HEADS
DT_PAD = 128
S5_W = 256
S5_GROUPS = 16
S5_GROUP_W = 16
S5_STATE = 64
S5_NSTATE = S5_GROUPS * S5_STATE
SSM_IN_PAD = SSD_W + XBC_W + S5_W + DT_PAD
D_FF = 2816
FF_TILE = 256

ROW_TILE = 256
HALO = 8
MOD_ROWS = 16
VMEM_LIMIT_BYTES = 56 * 1024 * 1024


def _cparams(*sem):
    return pltpu.CompilerParams(dimension_semantics=sem, vmem_limit_bytes=VMEM_LIMIT_BYTES)


def _silu(v):
    return v * jax.nn.sigmoid(v)


def _layer_norm(v, g, b):
    mu = jnp.mean(v, axis=-1, keepdims=True)
    d = v - mu
    var = jnp.mean(d * d, axis=-1, keepdims=True)
    return d * lax.rsqrt(var + LN_EPS) * g + b


def _full(shape):
    nd = len(shape)
    return pl.BlockSpec(shape, lambda *_: (0,) * nd)


def _mod_spec(layer, j, batch, ctx_tiles):
    def idx(b, t):
        row = jnp.where(t < ctx_tiles, batch, b)
        return ((layer * MOD_ROWS + row) * 6 + j, 0, 0)
    return pl.BlockSpec((1, 1, D_MODEL), idx)


def _ada_kernel(c_ref, w_ref, b_ref, o_ref):
    s = _silu(c_ref[...])
    o_ref[0] = jnp.dot(s, w_ref[0], preferred_element_type=F32,
                       precision=lax.Precision.HIGHEST) + b_ref[0]


def _ada_mods(c, c_ctx, ada_w, ada_b):
    batch = c.shape[0]
    nl = ada_w.shape[0]
    c_all = jnp.concatenate(
        [c, c_ctx[None], jnp.zeros((MOD_ROWS - batch - 1, D_MODEL), F32)], axis=0)
    out = pl.pallas_call(
        _ada_kernel,
        grid=(nl, 6),
        in_specs=[_full((MOD_ROWS, D_MODEL)),
                  pl.BlockSpec((1, D_MODEL, D_MODEL), lambda l, j: (l, 0, j)),
                  pl.BlockSpec((1, 1, D_MODEL), lambda l, j: (l, 0, j))],
        out_specs=pl.BlockSpec((1, MOD_ROWS, D_MODEL), lambda l, j: (l, 0, j)),
        out_shape=jax.ShapeDtypeStruct((nl, MOD_ROWS, 6 * D_MODEL), F32),
        compiler_params=_cparams("arbitrary", "arbitrary"),
        name="ada_mods",
    )(c_all, ada_w, ada_b.reshape(nl, 1, 6 * D_MODEL))
    return out.reshape(nl * MOD_ROWS * 6, 1, D_MODEL)


def _inproj_attn_kernel(h_ref, shift_ref, scale_ref, w_ref, cos_ref, sin_ref,
                        q_ref, k_ref, v_ref, f_ref):
    u = (h_ref[0] * (1.0 + scale_ref[0]) + shift_ref[0]).astype(BF16)
    p = jnp.dot(u, w_ref[...], preferred_element_type=F32)
    cos = cos_ref[...]
    sin = sin_ref[...]
    lane = lax.broadcasted_iota(jnp.int32, cos.shape, 1)
    first_half = (lane % ATTN_HEAD_DIM) < (ATTN_HEAD_DIM // 2)

    def rope(blk):
        partner = jnp.where(first_half, pltpu.roll(blk, 128 - 32, 1), pltpu.roll(blk, 32, 1))
        return blk * cos + partner * sin

    qk_scale = ATTN_HEAD_DIM ** -0.5
    for i in range(ATTN_HEADS):
        lo, hi = i * 128, (i + 1) * 128
        q_ref[0, :, lo:hi] = (rope(p[:, lo:hi]) * qk_scale).astype(BF16)
        k_ref[0, :, lo:hi] = rope(p[:, ATTN_W + lo:ATTN_W + hi]).astype(BF16)
    v_ref[0] = p[:, 2 * ATTN_W:3 * ATTN_W].astype(BF16)
    f_ref[0] = p[:, 3 * ATTN_W:].astype(BF16)


def _rope_tables(n_lat, n_ctx):
    rows = n_lat // GRID_W
    row = jnp.repeat(jnp.arange(rows, dtype=F32), GRID_W)
    col = jnp.tile(jnp.arange(GRID_W, dtype=F32), rows)
    n_freq = ATTN_HEAD_DIM // 4
    inv_freq = ROPE_BASE ** (-jnp.arange(n_freq, dtype=F32) / n_freq)
    ang = jnp.concatenate([row[:, None] * inv_freq, col[:, None] * inv_freq], axis=-1)
    cos, sin = jnp.cos(ang), jnp.sin(ang)
    cos128 = jnp.tile(cos, (1, 4))
    sin128 = jnp.tile(jnp.concatenate([-sin, sin], axis=-1), (1, 2))
    cos_all = jnp.concatenate([jnp.ones((n_ctx, 128), F32), cos128], axis=0)
    sin_all = jnp.concatenate([jnp.zeros((n_ctx, 128), F32), sin128], axis=0)
    return cos_all, sin_all


def _inproj_attn(h, mods, layer, w_in, cos, sin, ctx_tiles):
    batch, ta, _ = h.shape
    nt = ta // ROW_TILE
    row = lambda w: pl.BlockSpec((1, ROW_TILE, w), lambda b, t: (b, t, 0))
    tab = pl.BlockSpec((ROW_TILE, 128), lambda b, t: (t, 0))
    return pl.pallas_call(
        _inproj_attn_kernel,
        grid=(batch, nt),
        in_specs=[row(D_MODEL), _mod_spec(layer, 0, batch, ctx_tiles),
                  _mod_spec(layer, 1, batch, ctx_tiles),
                  _full((D_MODEL, ATTN_IN_W)), tab, tab],
        out_specs=[row(ATTN_W), row(ATTN_W), row(ATTN_W), row(F_W)],
        out_shape=[jax.ShapeDtypeStruct((batch, ta, ATTN_W), BF16)] * 3
        + [jax.ShapeDtypeStruct((batch, ta, F_W), BF16)],
        compiler_params=_cparams("parallel", "arbitrary"),
        name="inproj_attn",
    )(h, mods, mods, w_in, cos, sin)


def _attn_kernel(lam_ref, g_ref, q_ref, k_ref, v_ref, o_ref, *, lam_init, ctx_tiles):
    lamv = lam_ref[...]
    l1 = jnp.sum(lamv[0:1] * lamv[1:2], axis=-1, keepdims=True)
    l2 = jnp.sum(lamv[2:3] * lamv[3:4], axis=-1, keepdims=True)
    lam = jnp.exp(l1) - jnp.exp(l2) + lam_init
    q = q_ref[0]
    tq = q.shape[0]
    lane = lax.broadcasted_iota(jnp.int32, q.shape, 1)
    zero = jnp.zeros_like(q)
    q2 = jnp.concatenate([jnp.where(lane < ATTN_HEAD_DIM, q, zero),
                          jnp.where(lane >= ATTN_HEAD_DIM, q, zero)], axis=0)

    def attend(n_keys):
        k = k_ref[0, :n_keys, :]
        v = v_ref[0, :n_keys, :]
        s = lax.dot_general(q2, k, (((1,), (1,)), ((), ())), preferred_element_type=F32)
        m = jnp.max(s, axis=-1, keepdims=True)
        e = jnp.exp(s - m)
        p = e * (1.0 / jnp.sum(e, axis=-1, keepdims=True))
        w = p[:tq] - lam * p[tq:]
        o = jnp.dot(w.astype(BF16), v, preferred_element_type=F32)
        o = o * lax.rsqrt(jnp.mean(o * o, axis=-1, keepdims=True) + LN_EPS)
        o_ref[0] = (o * g_ref[...] * (1.0 - lam_init)).astype(BF16)

    t = pl.program_id(2)
    n_all = k_ref.shape[1]

    @pl.when(t < ctx_tiles)
    def _():
        attend(ctx_tiles * ROW_TILE)

    @pl.when(t >= ctx_tiles)
    def _():
        attend(n_all)


def _attention(q, k, v, lam_vec, subln_g, lam_init, ctx_tiles):
    batch, ta, _ = q.shape
    nt = ta // ROW_TILE
    qspec = pl.BlockSpec((1, ROW_TILE, 128), lambda b, h, t: (b, t, h))
    kvspec = pl.BlockSpec((1, ta, 128), lambda b, h, t: (b, 0, h))
    return pl.pallas_call(
        functools.partial(_attn_kernel, lam_init=lam_init, ctx_tiles=ctx_tiles),
        grid=(batch, ATTN_HEADS, nt),
        in_specs=[_full((4, ATTN_HEAD_DIM)), _full((1, 128)), qspec, kvspec, kvspec],
        out_specs=qspec,
        out_shape=jax.ShapeDtypeStruct((batch, ta, ATTN_W), BF16),
        compiler_params=_cparams("parallel", "arbitrary", "arbitrary"),
        name="diff_attention",
    )(lam_vec, subln_g.reshape(1, 128), q, k, v)


def _dft_tables(n):
    k = np.arange(n, dtype=np.int64)
    ang = 2.0 * np.pi * ((k[:, None] * k[None, :]) % n).astype(np.float64) / n
    return np.cos(ang), np.sin(ang)


def _fourier_kernel(f_ref, cs_ref, dl_ref, dc_ref, w_ref, b_ref, o_ref, a_ref, *, n_ctx, n_lat):
    t = pl.program_id(1)
    ctx_tiles = n_ctx // ROW_TILE

    def stage1(rows0, n):
        a = jnp.dot(f_ref[0, rows0:rows0 + n, :], cs_ref[...], preferred_element_type=F32)
        a_ref[0:n, :] = a[:, :F_W].astype(BF16)
        a_ref[n:2 * n, :] = a[:, F_W:].astype(BF16)

    def stage2(dft, n):
        z = jnp.dot(dft, a_ref[0:2 * n, :], preferred_element_type=F32)
        z = z * (1.0 / math.sqrt(n * F_GROUP_W))
        o = jnp.dot(z.astype(BF16), w_ref[...], preferred_element_type=F32) + b_ref[...]
        o_ref[0] = o.astype(BF16)

    @pl.when(t < ctx_tiles)
    def _():
        stage1(0, n_ctx)
        stage2(dc_ref[...], n_ctx)

    @pl.when(t == ctx_tiles)
    def _():
        stage1(n_ctx, n_lat)

    @pl.when(t >= ctx_tiles)
    def _():
        stage2(dl_ref[...], n_lat)


def _fourier(f, fourier_w, fourier_b, n_ctx):
    batch, ta, _ = f.shape
    n_lat = ta - n_ctx
    assert n_ctx == ROW_TILE
    nt = ta // ROW_TILE
    cc, sc = _dft_tables(F_GROUP_W)
    eye = np.eye(F_GROUPS)
    cs = jnp.asarray(np.concatenate([np.kron(eye, cc), np.kron(eye, sc)], axis=1), BF16)
    cl, sl = _dft_tables(n_lat)
    dft_lat = jnp.asarray(np.concatenate([cl, -sl], axis=1), BF16)
    cx, sx = _dft_tables(n_ctx)
    dft_ctx = jnp.asarray(np.concatenate([cx, -sx], axis=1), BF16)
    w_blk = jnp.einsum('gce,gh->gche', fourier_w, jnp.eye(F_GROUPS, dtype=F32))
    w_blk = w_blk.reshape(F_W, F_W).astype(BF16)
    return pl.pallas_call(
        functools.partial(_fourier_kernel, n_ctx=n_ctx, n_lat=n_lat),
        grid=(batch, nt),
        in_specs=[pl.BlockSpec((1, ta, F_W), lambda b, t: (b, 0, 0)),
                  _full((F_W, 2 * F_W)),
                  pl.BlockSpec((ROW_TILE, 2 * n_lat), lambda b, t: (jnp.maximum(t - 1, 0), 0)),
                  _full((n_ctx, 2 * n_ctx)),
                  _full((F_W, F_W)), _full((1, F_W))],
        out_specs=pl.BlockSpec((1, ROW_TILE, F_W), lambda b, t: (b, t, 0)),
        out_shape=jax.ShapeDtypeStruct((batch, ta, F_W), BF16),
        scratch_shapes=[pltpu.VMEM((2 * n_lat, F_W), BF16)],
        compiler_params=_cparams("parallel", "arbitrary"),
        name="fourier_mix",
    )(f, cs, dft_lat, dft_ctx, w_blk, fourier_b.reshape(1, F_W))


def _outproj_ln_kernel(a_ref, b_ref, h_ref, gate_ref, w_ref, g_ref, beta_ref, o_ref):
    wa = a_ref.shape[2]
    y = jnp.dot(a_ref[0], w_ref[0:wa, :], preferred_element_type=F32)
    y = y + jnp.dot(b_ref[0], w_ref[wa:, :], preferred_element_type=F32)
    o_ref[0] = _layer_norm(ALPHA * h_ref[0] + gate_ref[0] * y, g_ref[...], beta_ref[...])


def _outproj_ln(a, b2, h, mods, layer, w_out, ln_g, ln_b, ctx_tiles):
    batch, ta, _ = h.shape
    nt = ta // ROW_TILE
    row = lambda w: pl.BlockSpec((1, ROW_TILE, w), lambda b, t: (b, t, 0))
    return pl.pallas_call(
        _outproj_ln_kernel,
        grid=(batch, nt),
        in_specs=[row(a.shape[2]), row(b2.shape[2]), row(D_MODEL),
                  _mod_spec(layer, 2, batch, ctx_tiles),
                  _full((D_MODEL, D_MODEL)), _full((1, D_MODEL)), _full((1, D_MODEL))],
        out_specs=row(D_MODEL),
        out_shape=jax.ShapeDtypeStruct((batch, ta, D_MODEL), F32),
        compiler_params=_cparams("parallel", "arbitrary"),
        name="outproj_ln",
    )(a, b2, h, mods, w_out, ln_g.reshape(1, D_MODEL), ln_b.reshape(1, D_MODEL))


def _ffn_kernel(h_ref, hp_ref, hn_ref, shift_ref, scale_ref, gate_ref,
                wup_ref, bup_ref, cw_ref, cb_ref, wdn_ref, bdn_ref, g_ref, beta_ref,
                o_ref, uext_ref, acc_ref, *, ctx_tiles, tile_off, nt_seq):
    t = pl.program_id(1) + tile_off
    seg_first = (t == 0) | (t == ctx_tiles)
    seg_last = (t == nt_seq - 1) | (t == ctx_tiles - 1)
    sc = 1.0 + scale_ref[0]
    sh = shift_ref[0]
    h = h_ref[0]
    tm = h.shape[0]
    uext_ref[0:HALO, :] = hp_ref[0] * sc + sh
    uext_ref[HALO:HALO + tm, :] = h * sc + sh
    uext_ref[HALO + tm:, :] = hn_ref[0] * sc + sh
    u = uext_ref[...].astype(BF16)
    ext = tm + 2 * HALO
    rowi = lax.broadcasted_iota(jnp.int32, (ext, FF_TILE), 0)
    dead = ((rowi == HALO - 1) & seg_first) | ((rowi == HALO + tm) & seg_last)

    def conv_half(col0):
        z = jnp.dot(u, wup_ref[:, col0:col0 + FF_TILE], preferred_element_type=F32)
        z = z + bup_ref[:, col0:col0 + FF_TILE]
        z = jnp.where(dead, 0.0, z)
        zm1 = pltpu.roll(z, 1, 0)[HALO:HALO + tm]
        zp1 = pltpu.roll(z, ext - 1, 0)[HALO:HALO + tm]
        cw = cw_ref[:, col0:col0 + FF_TILE]
        return (cw[0:1] * zm1 + cw[1:2] * z[HALO:HALO + tm] + cw[2:3] * zp1
                + cb_ref[:, col0:col0 + FF_TILE])

    for j in range(D_FF // FF_TILE):
        val = conv_half(j * FF_TILE)
        gat = conv_half(D_FF + j * FF_TILE)
        act = (val * _silu(gat)).astype(BF16)
        part = jnp.dot(act, wdn_ref[j * FF_TILE:(j + 1) * FF_TILE, :],
                       preferred_element_type=F32)
        if j == 0:
            acc_ref[...] = part
        else:
            acc_ref[...] += part
    f = acc_ref[...] + bdn_ref[...]
    o_ref[0] = _layer_norm(ALPHA * h + gate_ref[0] * f, g_ref[...], beta_ref[...])


def _ffn(h, mods, layer, w_up, b_up, conv_w, conv_b, w_down, b_down, ln_g, ln_b,
         ctx_tiles, tile_off):
    batch, ta, _ = h.shape
    nt_seq = ta // ROW_TILE
    nt = nt_seq - tile_off
    hb = ROW_TILE // HALO
    n_hblk = ta // HALO
    mspec = lambda j: pl.BlockSpec(
        (1, 1, D_MODEL),
        lambda b, t: ((layer * MOD_ROWS + jnp.where(t + tile_off < ctx_tiles, batch, b)) * 6 + j, 0, 0))
    return pl.pallas_call(
        functools.partial(_ffn_kernel, ctx_tiles=ctx_tiles, tile_off=tile_off, nt_seq=nt_seq),
        grid=(batch, nt),
        in_specs=[pl.BlockSpec((1, ROW_TILE, D_MODEL), lambda b, t: (b, t + tile_off, 0)),
                  pl.BlockSpec((1, HALO, D_MODEL),
                               lambda b, t: (b, jnp.maximum((t + tile_off) * hb - 1, 0), 0)),
                  pl.BlockSpec((1, HALO, D_MODEL),
                               lambda b, t: (b, jnp.minimum((t + tile_off + 1) * hb, n_hblk - 1), 0)),
                  mspec(3), mspec(4), mspec(5),
                  _full((D_MODEL, 2 * D_FF)), _full((1, 2 * D_FF)),
                  _full((3, 2 * D_FF)), _full((1, 2 * D_FF)),
                  _full((D_FF, D_MODEL)), _full((1, D_MODEL)),
                  _full((1, D_MODEL)), _full((1, D_MODEL))],
        out_specs=pl.BlockSpec((1, ROW_TILE, D_MODEL), lambda b, t: (b, t, 0)),
        out_shape=jax.ShapeDtypeStruct((batch, nt * ROW_TILE, D_MODEL), F32),
        scratch_shapes=[pltpu.VMEM((ROW_TILE + 2 * HALO, D_MODEL), F32),
                        pltpu.VMEM((ROW_TILE, D_MODEL), F32)],
        compiler_params=_cparams("parallel", "arbitrary"),
        name="conv_ffn_ln",
    )(h, h, h, mods, mods, mods, w_up, b_up.reshape(1, -1), conv_w, conv_b.reshape(1, -1),
      w_down, b_down.reshape(1, -1), ln_g.reshape(1, -1), ln_b.reshape(1, -1))


def _inproj_ssm_kernel(h_ref, shift_ref, scale_ref, w_ref, z_ref, xbc_ref, dt_ref, us_ref):
    u = (h_ref[0] * (1.0 + scale_ref[0]) + shift_ref[0]).astype(BF16)
    p = jnp.dot(u, w_ref[...], preferred_element_type=F32)
    z_ref[0] = p[:, :SSD_W]
    xbc_ref[0] = p[:, SSD_W:SSD_W + XBC_W]
    us_ref[...] = p[:, SSD_W + XBC_W:SSD_W + XBC_W + S5_W]
    dt_ref[0] = p[:, SSD_W + XBC_W + S5_W:]


def _inproj_ssm(h, mods, layer, w_in_pad, ctx_tiles):
    batch, ta, _ = h.shape
    nt = ta // ROW_TILE
    row = lambda w: pl.BlockSpec((1, ROW_TILE, w), lambda b, t: (b, t, 0))
    return pl.pallas_call(
        _inproj_ssm_kernel,
        grid=(batch, nt),
        in_specs=[row(D_MODEL), _mod_spec(layer, 0, batch, ctx_tiles),
                  _mod_spec(layer, 1, batch, ctx_tiles), _full((D_MODEL, SSM_IN_PAD))],
        out_specs=[row(SSD_W), row(XBC_W), row(DT_PAD),
                   pl.BlockSpec((ROW_TILE, S5_W), lambda b, t: (t, b))],
        out_shape=[jax.ShapeDtypeStruct((batch, ta, SSD_W), F32),
                   jax.ShapeDtypeStruct((batch, ta, XBC_W), F32),
                   jax.ShapeDtypeStruct((batch, ta, DT_PAD), F32),
                   jax.ShapeDtypeStruct((ta, batch * S5_W), F32)],
        compiler_params=_cparams("parallel", "arbitrary"),
        name="inproj_ssm",
    )(h, mods, mods, w_in_pad)


def _cumsum_rows(v):
    n = v.shape[0]
    row = lax.broadcasted_iota(jnp.int32, v.shape, 0)
    s = 1
    while s < n:
        v = v + jnp.where(row >= s, pltpu.roll(v, s, 0), 0.0)
        s *= 2
    return v


def _expand_heads(v, e_ref):
    hi = v.astype(BF16)
    lo = (v - hi.astype(F32)).astype(BF16)
    e = e_ref[...]
    return (jnp.dot(hi, e, preferred_element_type=F32)
            + jnp.dot(lo, e, preferred_element_type=F32))


def _ssd_kernel(xbc_ref, xp_ref, xn_ref, dt_ref, z_ref, cw_ref, cb_ref, alog_ref, dtb_ref,
                dsk_ref, ng_ref, ef_ref, eb_ref, o_ref,
                xs_ref, cd_ref, st_ref, dec_ref, y_ref, *, n_chunks, ctx_chunks):
    ph = pl.program_id(1)
    c = pl.program_id(2)
    q = SSD_CHUNK
    gw = SSD_W // SSD_GROUPS
    hpg = SSD_HEADS // SSD_GROUPS

    @pl.when(ph == 0)
    def _phase0():
        seg_first = (c == 0) | (c == ctx_chunks)
        seg_last = (c == ctx_chunks - 1) | (c == n_chunks - 1)
        xr = xbc_ref[0]
        prev = jnp.where(seg_first, 0.0, xp_ref[0, HALO - 1:HALO, :])
        nxt = jnp.where(seg_last, 0.0, xn_ref[0, 0:1, :])
        rowi = lax.broadcasted_iota(jnp.int32, xr.shape, 0)
        xm1 = jnp.where(rowi == 0, prev, pltpu.roll(xr, 1, 0))
        xp1 = jnp.where(rowi == q - 1, nxt, pltpu.roll(xr, q - 1, 0))
        cw = cw_ref[...]
        xs = _silu(cw[0:1] * xm1 + cw[1:2] * xr + cw[2:3] * xp1 + cb_ref[...])
        xs_ref[c] = xs

        raw = dt_ref[0] + dtb_ref[...]
        dtv = jnp.maximum(raw, 0.0) + jnp.log1p(jnp.exp(-jnp.abs(raw)))
        adt = dtv * (-jnp.exp(alog_ref[...]))
        cum = _cumsum_rows(adt)
        tot = cum[q - 1:q, :]
        lane = lax.broadcasted_iota(jnp.int32, cum.shape, 1)
        cc = jnp.where(lane < SSD_HEADS, cum, tot - cum + adt)
        cd_ref[c, 0] = cc
        cd_ref[c, 1] = dtv
        w_end = jnp.exp(tot - cc) * dtv
        dec16 = jnp.broadcast_to(jnp.exp(tot), (16, DT_PAD))
        x = xs[:, :SSD_W]
        for d, e_ref in enumerate((ef_ref, eb_ref)):
            wx = (_expand_heads(w_end, e_ref) * x).astype(BF16)
            for g in range(SSD_GROUPS):
                bmt = xs[:, SSD_W + g * SSD_STATE:SSD_W + (g + 1) * SSD_STATE].T.astype(BF16)
                st_ref[c, d, :, g * gw:(g + 1) * gw] = jnp.dot(
                    bmt, wx[:, g * gw:(g + 1) * gw], preferred_element_type=F32)
            dec_ref[c, d] = _expand_heads(dec16, e_ref)[0:8]

    @pl.when((ph == 1) & (c == 0))
    def _recurrence():
        fwd = list(range(n_chunks))
        bwd = list(range(ctx_chunks - 1, -1, -1)) + list(range(n_chunks - 1, ctx_chunks - 1, -1))
        for d, order in enumerate((fwd, bwd)):
            state = jnp.zeros((SSD_STATE, SSD_W), F32)
            for ci in order:
                contrib = st_ref[ci, d]
                st_ref[ci, d] = state
                state = state * dec_ref[ci, d, 0:1, :] + contrib

    @pl.when((ph == 1) & (c >= ctx_chunks))
    def _phase1():
        xs = xs_ref[c]
        x = xs[:, :SSD_W]
        cc = cd_ref[c, 0]
        dtv = cd_ref[c, 1]
        cct = cc.T
        dtt = dtv.T
        ecc = jnp.exp(cc)
        ef = _expand_heads(ecc, ef_ref)
        eb = _expand_heads(ecc, eb_ref)
        rowi = lax.broadcasted_iota(jnp.int32, (q, q), 0)
        coli = lax.broadcasted_iota(jnp.int32, (q, q), 1)
        lower = coli <= rowi
        upper = coli >= rowi
        lane = lax.broadcasted_iota(jnp.int32, (q, 128), 1)
        neg = jnp.float32(-jnp.inf)
        for g in range(SSD_GROUPS):
            bm = xs[:, SSD_W + g * SSD_STATE:SSD_W + (g + 1) * SSD_STATE].astype(BF16)
            cm = xs[:, SSD_W + (SSD_GROUPS + g) * SSD_STATE:
                    SSD_W + (SSD_GROUPS + g + 1) * SSD_STATE].astype(BF16)
            gmat = lax.dot_general(cm, bm, (((1,), (1,)), ((), ())), preferred_element_type=F32)
            sl = slice(g * gw, (g + 1) * gw)
            yoff = (ef[:, sl] * jnp.dot(cm, st_ref[c, 0, :, sl].astype(BF16),
                                        preferred_element_type=F32)
                    + eb[:, sl] * jnp.dot(cm, st_ref[c, 1, :, sl].astype(BF16),
                                          preferred_element_type=F32))
            for pair in range(hpg // 2):
                col0 = g * gw + pair * 128
                xpair = x[:, col0:col0 + 128].astype(BF16)
                res = []
                for hh in range(2):
                    hd = g * hpg + pair * 2 + hh
                    hb_ = SSD_HEADS + hd
                    lf = jnp.exp(jnp.where(lower, cc[:, hd:hd + 1] - cct[hd:hd + 1, :], neg))
                    lb = jnp.exp(jnp.where(upper, cc[:, hb_:hb_ + 1] - cct[hb_:hb_ + 1, :], neg))
                    mt = gmat * (lf * dtt[hd:hd + 1, :] + lb * dtt[hb_:hb_ + 1, :])
                    res.append(jnp.dot(mt.astype(BF16), xpair, preferred_element_type=F32))
                ydiag = jnp.where(lane < SSD_HEAD_DIM, res[0], res[1])
                y_ref[:, col0:col0 + 128] = (ydiag + yoff[:, pair * 128:(pair + 1) * 128]
                                             + dsk_ref[:, col0:col0 + 128] * x[:, col0:col0 + 128])
        gated = y_ref[...] * _silu(z_ref[0])
        normed = gated * lax.rsqrt(jnp.mean(gated * gated, axis=-1, keepdims=True) + LN_EPS)
        o_ref[0] = (normed * ng_ref[...]).astype(BF16)


def _ssd(xbc, dt, z, conv_w, conv_b, a_log, dt_bias, d_skip, norm_g, n_ctx):
    batch, ta, _ = xbc.shape
    q = SSD_CHUNK
    n_chunks = ta // q
    ctx_chunks = n_ctx // q
    hb = q // HALO
    n_hblk = ta // HALO
    pad24 = lambda v: jnp.pad(v.reshape(1, DT_W), ((0, 0), (0, DT_PAD - DT_W)))
    heads = np.arange(SSD_HEADS)
    ef = np.zeros((DT_PAD, SSD_W), np.float32)
    eb = np.zeros((DT_PAD, SSD_W), np.float32)
    for hd in heads:
        ef[hd, hd * SSD_HEAD_DIM:(hd + 1) * SSD_HEAD_DIM] = 1.0
        eb[SSD_HEADS + hd, hd * SSD_HEAD_DIM:(hd + 1) * SSD_HEAD_DIM] = 1.0
    dsk = jnp.repeat(d_skip.astype(F32), SSD_HEAD_DIM).reshape(1, SSD_W)
    chunk = lambda w: pl.BlockSpec((1, q, w), lambda b, ph, c: (b, jnp.where(ph == 0, c, n_chunks - 1), 0))
    return pl.pallas_call(
        functools.partial(_ssd_kernel, n_chunks=n_chunks, ctx_chunks=ctx_chunks),
        grid=(batch, 2, n_chunks),
        in_specs=[chunk(XBC_W),
                  pl.BlockSpec((1, HALO, XBC_W),
                               lambda b, ph, c: (b, jnp.where(ph == 0, jnp.maximum(c * hb - 1, 0), 0), 0)),
                  pl.BlockSpec((1, HALO, XBC_W),
                               lambda b, ph, c: (b, jnp.where(ph == 0, jnp.minimum((c + 1) * hb, n_hblk - 1), 0), 0)),
                  chunk(DT_PAD),
                  pl.BlockSpec((1, q, SSD_W), lambda b, ph, c: (b, jnp.where(ph == 1, c, 0), 0)),
                  _full((3, XBC_W)), _full((1, XBC_W)), _full((1, DT_PAD)), _full((1, DT_PAD)),
                  _full((1, SSD_W)), _full((1, SSD_W)),
                  _full((DT_PAD, SSD_W)), _full((DT_PAD, SSD_W))],
        out_specs=pl.BlockSpec(
            (1, q, SSD_W),
            lambda b, ph, c: (b, jnp.where(ph == 1, jnp.maximum(c - ctx_chunks, 0), 0), 0)),
        out_shape=jax.ShapeDtypeStruct((batch, ta - n_ctx, SSD_W), BF16),
        scratch_shapes=[pltpu.VMEM((n_chunks, q, XBC_W), F32),
                        pltpu.VMEM((n_chunks, 2, q, DT_PAD), F32),
                        pltpu.VMEM((n_chunks, 2, SSD_STATE, SSD_W), F32),
                        pltpu.VMEM((n_chunks, 2, 8, SSD_W), F32),
                        pltpu.VMEM((q, SSD_W), F32)],
        compiler_params=_cparams("parallel", "arbitrary", "arbitrary"),
        name="ssd_bidir",
    )(xbc, xbc, xbc, dt, z, conv_w, conv_b.reshape(1, XBC_W), pad24(a_log), pad24(dt_bias),
      dsk, norm_g.reshape(1, SSD_W), jnp.asarray(ef, BF16), jnp.asarray(eb, BF16))


def _s5_disc_kernel(lr_ref, li_ref, ldt_ref, bre_ref, bim_ref, cre_ref, cim_ref,
                    a_ref, bd_ref, cd_ref):
    lr, li = lr_ref[...], li_ref[...]
    dt = jnp.exp(ldt_ref[...])
    mag = jnp.exp(dt * lr)
    ab_re, ab_im = mag * jnp.cos(dt * li), mag * jnp.sin(dt * li)
    den = lr * lr + li * li
    k_re = ((ab_re - 1.0) * lr + ab_im * li) / den
    k_im = (ab_im * lr - (ab_re - 1.0) * li) / den
    bre, bim = bre_ref[...], bim_ref[...]
    for d in range(2):
        a_ref[d, :, 0:S5_NSTATE] = jnp.broadcast_to(ab_re[d:d + 1], (8, S5_NSTATE))
        a_ref[d, :, S5_NSTATE:] = jnp.broadcast_to(ab_im[d:d + 1], (8, S5_NSTATE))
        kr, ki = k_re[d:d + 1], k_im[d:d + 1]
        bd_ref[d, :, 0:S5_NSTATE] = (kr * bre - ki * bim).astype(BF16)
        bd_ref[d, :, S5_NSTATE:] = (kr * bim + ki * bre).astype(BF16)
        cd_ref[d, 0:S5_NSTATE, :] = cre_ref[d].astype(BF16)
        cd_ref[d, S5_NSTATE:, :] = (-cim_ref[d]).astype(BF16)


def _s5_discretize(lam_re, lam_im, log_dt, b_re, b_im, c_re, c_im):
    eye = jnp.eye(S5_GROUPS, dtype=F32)
    bd = lambda b: jnp.einsum('gph,gk->ghkp', b, eye).reshape(S5_W, S5_NSTATE)
    cd = lambda cc: jnp.einsum('dghp,gk->dgpkh', cc, eye).reshape(2, S5_NSTATE, S5_W)
    ldt = jnp.repeat(log_dt, S5_STATE, axis=-1)
    return pl.pallas_call(
        _s5_disc_kernel,
        out_shape=[jax.ShapeDtypeStruct((2, 8, 2 * S5_NSTATE), F32),
                   jax.ShapeDtypeStruct((2, S5_W, 2 * S5_NSTATE), BF16),
                   jax.ShapeDtypeStruct((2, 2 * S5_NSTATE, S5_W), BF16)],
        compiler_params=pltpu.CompilerParams(vmem_limit_bytes=VMEM_LIMIT_BYTES),
        name="s5_discretize",
    )(lam_re.reshape(2, S5_NSTATE), lam_im.reshape(2, S5_NSTATE), ldt,
      bd(b_re), bd(b_im), cd(c_re), cd(c_im))


S5_TIME_CHUNK = 128
S5_UNROLL = 8


def _s5_scan_kernel(u_ref, a_ref, bd_ref, cd_ref, o_ref, hs_ref, carry_ref, *, batch):
    d = pl.program_id(0)
    j = pl.program_id(1)
    n = S5_NSTATE

    @pl.when(j == 0)
    def _():
        carry_ref[...] = jnp.zeros_like(carry_ref)

    hs_ref[...] = jnp.dot(u_ref[...].astype(BF16), bd_ref[0], preferred_element_type=F32)
    ar = a_ref[0, :, 0:n]
    ai = a_ref[0, :, n:]
    if batch != 8:
        ar = jnp.broadcast_to(ar[0:1], (batch, n))
        ai = jnp.broadcast_to(ai[0:1], (batch, n))

    def body(i, carry):
        hr, hi = carry
        for s in range(S5_UNROLL):
            step = i * S5_UNROLL + s
            step = jnp.where(d == 0, step, S5_TIME_CHUNK - 1 - step)
            rows = pl.ds(pl.multiple_of(step * batch, batch), batch)
            nr = ar * hr - ai * hi + hs_ref[rows, 0:n]
            ni = ar * hi + ai * hr + hs_ref[rows, n:]
            hs_ref[rows, 0:n] = nr
            hs_ref[rows, n:] = ni
            hr, hi = nr, ni
        return hr, hi

    hr, hi = lax.fori_loop(0, S5_TIME_CHUNK // S5_UNROLL, body,
                           (carry_ref[:, 0:n], carry_ref[:, n:]))
    carry_ref[:, 0:n] = hr
    carry_ref[:, n:] = hi
    o_ref[0] = jnp.dot(hs_ref[...].astype(BF16), cd_ref[0], preferred_element_type=F32)


def _s5_scan(us_flat, a, bd, cd, batch, n_ctx):
    rows_total = us_flat.shape[0]
    ta = rows_total // batch
    tc = S5_TIME_CHUNK
    n_chunks = ta // tc
    ctx_chunks = n_ctx // tc
    blk = tc * batch

    def chunk_of(d, j):
        bwd = jnp.where(j < ctx_chunks, ctx_chunks - 1 - j, n_chunks - 1 - (j - ctx_chunks))
        return jnp.where(d == 0, j, bwd)

    return pl.pallas_call(
        functools.partial(_s5_scan_kernel, batch=batch),
        grid=(2, n_chunks),
        in_specs=[pl.BlockSpec((blk, S5_W), lambda d, j: (chunk_of(d, j), 0)),
                  pl.BlockSpec((1, 8, 2 * S5_NSTATE), lambda d, j: (d, 0, 0)),
                  pl.BlockSpec((1, S5_W, 2 * S5_NSTATE), lambda d, j: (d, 0, 0)),
                  pl.BlockSpec((1, 2 * S5_NSTATE, S5_W), lambda d, j: (d, 0, 0))],
        out_specs=pl.BlockSpec((1, blk, S5_W), lambda d, j: (d, chunk_of(d, j), 0)),
        out_shape=jax.ShapeDtypeStruct((2, rows_total, S5_W), F32),
        scratch_shapes=[pltpu.VMEM((blk, 2 * S5_NSTATE), F32),
                        pltpu.VMEM((batch, 2 * S5_NSTATE), F32)],
        compiler_params=_cparams("arbitrary", "arbitrary"),
        name="s5_scan",
    )(us_flat, a, bd, cd)


def _merge_ln_kernel(gs_ref, y5_ref, us_ref, h_ref, gate_ref, dd_ref, gw_ref, gb_ref,
                     w_ref, g_ref, beta_ref, o_ref):
    y5 = y5_ref[0] + y5_ref[1] + dd_ref[...] * us_ref[...]
    ge = jax.nn.gelu(y5)
    s5 = ge * jax.nn.sigmoid(
        jnp.dot(ge.astype(BF16), gw_ref[...], preferred_element_type=F32) + gb_ref[...])
    y = jnp.dot(gs_ref[0], w_ref[0:SSD_W, :], preferred_element_type=F32)
    y = y + jnp.dot(s5.astype(BF16), w_ref[SSD_W:, :], preferred_element_type=F32)
    o_ref[0] = _layer_norm(ALPHA * h_ref[0] + gate_ref[0] * y, g_ref[...], beta_ref[...])


def _merge_ln(g_ssd, y5, us_t, h, mods, layer, s5_d, glu_w, glu_b, w_out, ln_g, ln_b, ctx_tiles):
    batch, ta, _ = h.shape
    nt = ta // ROW_TILE - ctx_tiles
    y5v = y5.reshape(2, ta, batch * S5_W)
    return pl.pallas_call(
        _merge_ln_kernel,
        grid=(batch, nt),
        in_specs=[pl.BlockSpec((1, ROW_TILE, SSD_W), lambda b, t: (b, t, 0)),
                  pl.BlockSpec((2, ROW_TILE, S5_W), lambda b, t: (0, t + ctx_tiles, b)),
                  pl.BlockSpec((ROW_TILE, S5_W), lambda b, t: (t + ctx_tiles, b)),
                  pl.BlockSpec((1, ROW_TILE, D_MODEL), lambda b, t: (b, t + ctx_tiles, 0)),
                  pl.BlockSpec((1, 1, D_MODEL), lambda b, t: ((layer * MOD_ROWS + b) * 6 + 2, 0, 0)),
                  _full((1, S5_W)), _full((S5_W, S5_W)), _full((1, S5_W)),
                  _full((D_MODEL, D_MODEL)), _full((1, D_MODEL)), _full((1, D_MODEL))],
        out_specs=pl.BlockSpec((1, ROW_TILE, D_MODEL), lambda b, t: (b, t, 0)),
        out_shape=jax.ShapeDtypeStruct((batch, nt * ROW_TILE, D_MODEL), F32),
        compiler_params=_cparams("parallel", "arbitrary"),
        name="merge_outproj_ln",
    )(g_ssd, y5v, us_t, h, mods, s5_d.reshape(1, S5_W), glu_w.astype(BF16),
      glu_b.reshape(1, S5_W), w_out, ln_g.reshape(1, -1), ln_b.reshape(1, -1))


def _attn_layer(h, mods, layer, i, n_ctx, p, keep_ctx):
    ctx_tiles = n_ctx // ROW_TILE
    ta = h.shape[1]
    lam_init = 0.8 - 0.6 * math.exp(-0.3 * layer)
    cos, sin = _rope_tables(ta - n_ctx, n_ctx)
    q, k, v, f = _inproj_attn(h, mods, layer, p['attn_w_in'][i].astype(BF16), cos, sin, ctx_tiles)
    o = _attention(q, k, v, p['attn_lambda'][i], p['attn_subln_g'][i], lam_init, ctx_tiles)
    fm = _fourier(f, p['fourier_w'][i], p['fourier_b'][i], n_ctx)
    h1 = _outproj_ln(o, fm, h, mods, layer, p['attn_w_out'][i].astype(BF16),
                     p['ln_g'][layer, 0], p['ln_b'][layer, 0], ctx_tiles)
    return _ffn(h1, mods, layer, p['ffn_w_up'][layer].astype(BF16), p['ffn_b_up'][layer],
                p['ffn_conv_w'][layer], p['ffn_conv_b'][layer],
                p['ffn_w_down'][layer].astype(BF16), p['ffn_b_down'][layer],
                p['ln_g'][layer, 1], p['ln_b'][layer, 1], ctx_tiles, 0 if keep_ctx else ctx_tiles)


def _ssm_layer(h, mods, layer, i, n_ctx, p, keep_ctx):
    assert not keep_ctx, "an SSM layer that must also emit context rows is not implemented"
    ctx_tiles = n_ctx // ROW_TILE
    batch = h.shape[0]
    w = p['ssm_w_in'][i]
    w_pad = jnp.concatenate(
        [w[:, :SSD_W + XBC_W], w[:, SSD_W + XBC_W + DT_W:], w[:, SSD_W + XBC_W:SSD_W + XBC_W + DT_W],
         jnp.zeros((D_MODEL, DT_PAD - DT_W), F32)], axis=1).astype(BF16)
    z, xbc, dt, us_t = _inproj_ssm(h, mods, layer, w_pad, ctx_tiles)
    g_ssd = _ssd(xbc, dt, z, p['ssd_conv_w'][i], p['ssd_conv_b'][i], p['ssd_a_log'][i],
                 p['ssd_dt_bias'][i], p['ssd_d'][i], p['ssd_norm_g'][i], n_ctx)
    a, bd, cd = _s5_discretize(p['s5_lambda_re'][i], p['s5_lambda_im'][i], p['s5_log_dt'][i],
                               p['s5_b_re'][i], p['s5_b_im'][i], p['s5_c_re'][i], p['s5_c_im'][i])
    y5 = _s5_scan(us_t.reshape(-1, S5_W), a, bd, cd, batch, n_ctx)
    h1 = _merge_ln(g_ssd, y5, us_t, h, mods, layer, p['s5_d'][i], p['s5_glu_w'][i],
                   p['s5_glu_b'][i], p['ssm_w_out'][i].astype(BF16),
                   p['ln_g'][layer, 0], p['ln_b'][layer, 0], ctx_tiles)
    return _ffn(h1, mods, layer, p['ffn_w_up'][layer].astype(BF16), p['ffn_b_up'][layer],
                p['ffn_conv_w'][layer], p['ffn_conv_b'][layer],
                p['ffn_w_down'][layer].astype(BF16), p['ffn_b_down'][layer],
                p['ln_g'][layer, 1], p['ln_b'][layer, 1], 0, 0)


def kernel(x, c, ctx, c_ctx, ada_w, ada_b, ln_g, ln_b, ffn_w_up, ffn_b_up, ffn_conv_w, ffn_conv_b, ffn_w_down, ffn_b_down, attn_w_in, attn_lambda, attn_subln_g, fourier_w, fourier_b, attn_w_out, ssm_w_in, ssd_conv_w, ssd_conv_b, ssd_a_log, ssd_dt_bias, ssd_d, ssd_norm_g, s5_lambda_re, s5_lambda_im, s5_log_dt, s5_b_re, s5_b_im, s5_c_re, s5_c_im, s5_d, s5_glu_w, s5_glu_b, ssm_w_out):
    p = dict(ln_g=ln_g, ln_b=ln_b, ffn_w_up=ffn_w_up, ffn_b_up=ffn_b_up, ffn_conv_w=ffn_conv_w,
             ffn_conv_b=ffn_conv_b, ffn_w_down=ffn_w_down, ffn_b_down=ffn_b_down,
             attn_w_in=attn_w_in, attn_lambda=attn_lambda, attn_subln_g=attn_subln_g,
             fourier_w=fourier_w, fourier_b=fourier_b, attn_w_out=attn_w_out, ssm_w_in=ssm_w_in,
             ssd_conv_w=ssd_conv_w, ssd_conv_b=ssd_conv_b, ssd_a_log=ssd_a_log,
             ssd_dt_bias=ssd_dt_bias, ssd_d=ssd_d, ssd_norm_g=ssd_norm_g,
             s5_lambda_re=s5_lambda_re, s5_lambda_im=s5_lambda_im, s5_log_dt=s5_log_dt,
             s5_b_re=s5_b_re, s5_b_im=s5_b_im, s5_c_re=s5_c_re, s5_c_im=s5_c_im, s5_d=s5_d,
             s5_glu_w=s5_glu_w, s5_glu_b=s5_glu_b, ssm_w_out=ssm_w_out)
    batch, n_lat, _ = x.shape
    n_ctx = ctx.shape[1]
    assert n_ctx == ROW_TILE and n_lat % ROW_TILE == 0 and batch < MOD_ROWS
    mods = _ada_mods(c, c_ctx, ada_w, ada_b)
    h = jnp.concatenate([ctx, x], axis=1)
    for layer in range(DEPTH):
        last = layer == DEPTH - 1
        layer_fn = _attn_layer if layer % 2 == 0 else _ssm_layer
        h = layer_fn(h, mods, layer, layer // 2, n_ctx, p, keep_ctx=not last)
    return h
```

```python
import functools
import math

import numpy as np
import jax
import jax.numpy as jnp
from jax import lax
from jax.experimental import pallas as pl
from jax.experimental.pallas import tpu as pltpu

F32 = jnp.float32
BF16 = jnp.bfloat16

D_MODEL = 1024
DEPTH = 2
GRID_W = 64
ROPE_BASE = 10000.0
LN_EPS = 1e-5
ALPHA = (2 * DEPTH) ** 0.25
ATTN_W = 768
ATTN_HEADS = 6
ATTN_HEAD_DIM = 64
F_W = 256
F_GROUPS = 4
F_GROUP_W = 64
ATTN_IN_W = 2 * ATTN_W + ATTN_W + F_W
SSD_W = 768
SSD_HEADS = 12
SSD_HEAD_DIM = 64
SSD_GROUPS = 2
SSD_STATE = 128
SSD_CHUNK = 128
XBC_W = SSD_W + 2 * SSD_GROUPS * SSD_STATE
DT_W = 2 * SSD_HEADS
DT_PAD = 128
S5_W = 256
S5_GROUPS = 16
S5_GROUP_W = 16
S5_STATE = 64
S5_NSTATE = S5_GROUPS * S5_STATE
SSM_IN_PAD = SSD_W + XBC_W + S5_W + DT_PAD
D_FF = 2816
FF_TILE = 256

ROW_TILE = 256
ATTN_Q_SUB = 128
HALO = 8
MOD_ROWS = 16
VMEM_LIMIT_BYTES = 56 * 1024 * 1024


def _cparams(*sem):
    return pltpu.CompilerParams(dimension_semantics=sem, vmem_limit_bytes=VMEM_LIMIT_BYTES)


def _silu(v):
    return v * jax.nn.sigmoid(v)


def _layer_norm(v, g, b):
    mu = jnp.mean(v, axis=-1, keepdims=True)
    d = v - mu
    var = jnp.mean(d * d, axis=-1, keepdims=True)
    return d * lax.rsqrt(var + LN_EPS) * g + b


def _full(shape):
    nd = len(shape)
    return pl.BlockSpec(shape, lambda *_: (0,) * nd)


def _mod_spec(layer, j, batch, ctx_tiles):
    def idx(b, t):
        row = jnp.where(t < ctx_tiles, batch, b)
        return ((layer * MOD_ROWS + row) * 6 + j, 0, 0)
    return pl.BlockSpec((1, 1, D_MODEL), idx)


def _ada_kernel(c_ref, w_ref, b_ref, o_ref):
    s = _silu(c_ref[...])
    o_ref[0] = jnp.dot(s, w_ref[0], preferred_element_type=F32,
                       precision=lax.Precision.HIGHEST) + b_ref[0]


def _ada_mods(c, c_ctx, ada_w, ada_b):
    batch = c.shape[0]
    nl = ada_w.shape[0]
    c_all = jnp.concatenate(
        [c, c_ctx[None], jnp.zeros((MOD_ROWS - batch - 1, D_MODEL), F32)], axis=0)
    out = pl.pallas_call(
        _ada_kernel,
        grid=(nl, 6),
        in_specs=[_full((MOD_ROWS, D_MODEL)),
                  pl.BlockSpec((1, D_MODEL, D_MODEL), lambda l, j: (l, 0, j)),
                  pl.BlockSpec((1, 1, D_MODEL), lambda l, j: (l, 0, j))],
        out_specs=pl.BlockSpec((1, MOD_ROWS, D_MODEL), lambda l, j: (l, 0, j)),
        out_shape=jax.ShapeDtypeStruct((nl, MOD_ROWS, 6 * D_MODEL), F32),
        compiler_params=_cparams("arbitrary", "arbitrary"),
        name="ada_mods",
    )(c_all, ada_w, ada_b.reshape(nl, 1, 6 * D_MODEL))
    return out.reshape(nl * MOD_ROWS * 6, 1, D_MODEL)


def _inproj_attn_kernel(h_ref, shift_ref, scale_ref, w_ref, cos_ref, sin_ref,
                        q_ref, k_ref, v_ref, f_ref):
    u = (h_ref[0] * (1.0 + scale_ref[0]) + shift_ref[0]).astype(BF16)
    p = jnp.dot(u, w_ref[...], preferred_element_type=F32)
    cos = cos_ref[...]
    sin = sin_ref[...]
    lane = lax.broadcasted_iota(jnp.int32, cos.shape, 1)
    first_half = (lane % ATTN_HEAD_DIM) < (ATTN_HEAD_DIM // 2)

    def rope(blk):
        partner = jnp.where(first_half, pltpu.roll(blk, 128 - 32, 1), pltpu.roll(blk, 32, 1))
        return blk * cos + partner * sin

    qk_scale = ATTN_HEAD_DIM ** -0.5 * math.log2(math.e)
    for i in range(ATTN_HEADS):
        lo, hi = i * 128, (i + 1) * 128
        q_ref[0, :, lo:hi] = (rope(p[:, lo:hi]) * qk_scale).astype(BF16)
        k_ref[0, :, lo:hi] = rope(p[:, ATTN_W + lo:ATTN_W + hi]).astype(BF16)
    v_ref[0] = p[:, 2 * ATTN_W:3 * ATTN_W].astype(BF16)
    f_ref[0] = p[:, 3 * ATTN_W:].astype(BF16)


def _rope_tables(n_lat, n_ctx):
    rows = n_lat // GRID_W
    row = jnp.repeat(jnp.arange(rows, dtype=F32), GRID_W)
    col = jnp.tile(jnp.arange(GRID_W, dtype=F32), rows)
    n_freq = ATTN_HEAD_DIM // 4
    inv_freq = ROPE_BASE ** (-jnp.arange(n_freq, dtype=F32) / n_freq)
    ang = jnp.concatenate([row[:, None] * inv_freq, col[:, None] * inv_freq], axis=-1)
    cos, sin = jnp.cos(ang), jnp.sin(ang)
    cos128 = jnp.tile(cos, (1, 4))
    sin128 = jnp.tile(jnp.concatenate([-sin, sin], axis=-1), (1, 2))
    cos_all = jnp.concatenate([jnp.ones((n_ctx, 128), F32), cos128], axis=0)
    sin_all = jnp.concatenate([jnp.zeros((n_ctx, 128), F32), sin128], axis=0)
    return cos_all, sin_all


def _inproj_attn(h, mods, layer, w_in, cos, sin, ctx_tiles):
    batch, ta, _ = h.shape
    nt = ta // ROW_TILE
    row = lambda w: pl.BlockSpec((1, ROW_TILE, w), lambda b, t: (b, t, 0))
    tab = pl.BlockSpec((ROW_TILE, 128), lambda b, t: (t, 0))
    return pl.pallas_call(
        _inproj_attn_kernel,
        grid=(batch, nt),
        in_specs=[row(D_MODEL), _mod_spec(layer, 0, batch, ctx_tiles),
                  _mod_spec(layer, 1, batch, ctx_tiles),
                  _full((D_MODEL, ATTN_IN_W)), tab, tab],
        out_specs=[row(ATTN_W), row(ATTN_W), row(ATTN_W), row(F_W)],
        out_shape=[jax.ShapeDtypeStruct((batch, ta, ATTN_W), BF16)] * 3
        + [jax.ShapeDtypeStruct((batch, ta, F_W), BF16)],
        compiler_params=_cparams("parallel", "arbitrary"),
        name="inproj_attn",
    )(h, mods, mods, w_in, cos, sin)


def _attn_kernel(lam_ref, g_ref, q_ref, k_ref, v_ref, o_ref, *, lam_init, ctx_tiles):
    lamv = lam_ref[...]
    l1 = jnp.sum(lamv[0:1] * lamv[1:2], axis=-1, keepdims=True)
    l2 = jnp.sum(lamv[2:3] * lamv[3:4], axis=-1, keepdims=True)
    lam = jnp.exp(l1) - jnp.exp(l2) + lam_init
    q = q_ref[0]
    tq = q.shape[0]
    lane = lax.broadcasted_iota(jnp.int32, q.shape, 1)
    zero = jnp.zeros_like(q)
    q_lo = jnp.where(lane < ATTN_HEAD_DIM, q, zero)
    q_hi = jnp.where(lane >= ATTN_HEAD_DIM, q, zero)
    sub = ATTN_Q_SUB
    gain = g_ref[...] * (1.0 - lam_init)

    def attend(n_keys):
        k = k_ref[0, :n_keys, :]
        v = v_ref[0, :n_keys, :]
        scores = []
        for i in range(tq // sub):
            rows = slice(i * sub, (i + 1) * sub)
            q2 = jnp.concatenate([q_lo[rows], q_hi[rows]], axis=0)
            scores.append(lax.dot_general(q2, k, (((1,), (1,)), ((), ())),
                                          preferred_element_type=F32))
        for i, s in enumerate(scores):
            e = jnp.exp2(s - jnp.max(s, axis=-1, keepdims=True))
            inv = 1.0 / jnp.sum(e, axis=-1, keepdims=True)
            w = e[:sub] * inv[:sub] - e[sub:] * (lam * inv[sub:])
            o = jnp.dot(w.astype(BF16), v, preferred_element_type=F32)
            o = o * lax.rsqrt(jnp.mean(o * o, axis=-1, keepdims=True) + LN_EPS)
            o_ref[0, i * sub:(i + 1) * sub, :] = (o * gain).astype(BF16)

    t = pl.program_id(2)
    n_all = k_ref.shape[1]

    @pl.when(t < ctx_tiles)
    def _():
        attend(ctx_tiles * ROW_TILE)

    @pl.when(t >= ctx_tiles)
    def _():
        attend(n_all)


def _attention(q, k, v, lam_vec, subln_g, lam_init, ctx_tiles):
    batch, ta, _ = q.shape
    nt = ta // ROW_TILE
    qspec = pl.BlockSpec((1, ROW_TILE, 128), lambda b, h, t: (b, t, h))
    kvspec = pl.BlockSpec((1, ta, 128), lambda b, h, t: (b, 0, h))
    return pl.pallas_call(
        functools.partial(_attn_kernel, lam_init=lam_init, ctx_tiles=ctx_tiles),
        grid=(batch, ATTN_HEADS, nt),
        in_specs=[_full((4, ATTN_HEAD_DIM)), _full((1, 128)), qspec, kvspec, kvspec],
        out_specs=qspec,
        out_shape=jax.ShapeDtypeStruct((batch, ta, ATTN_W), BF16),
        compiler_params=_cparams("parallel", "arbitrary", "arbitrary"),
        name="diff_attention",
    )(lam_vec, subln_g.reshape(1, 128), q, k, v)


def _dft_tables(n):
    k = np.arange(n, dtype=np.int64)
    ang = 2.0 * np.pi * ((k[:, None] * k[None, :]) % n).astype(np.float64) / n
    return np.cos(ang), np.sin(ang)


def _fourier_kernel(f_ref, cs_ref, dl_ref, dc_ref, w_ref, b_ref, o_ref, a_ref, *, n_ctx, n_lat):
    t = pl.program_id(1)
    ctx_tiles = n_ctx // ROW_TILE

    def stage1(rows0, n):
        a = jnp.dot(f_ref[0, rows0:rows0 + n, :], cs_ref[...], preferred_element_type=F32)
        a_ref[0:n, :] = a[:, :F_W].astype(BF16)
        a_ref[n:2 * n, :] = a[:, F_W:].astype(BF16)

    def stage2(dft, n):
        z = jnp.dot(dft, a_ref[0:2 * n, :], preferred_element_type=F32)
        z = z * (1.0 / math.sqrt(n * F_GROUP_W))
        o = jnp.dot(z.astype(BF16), w_ref[...], preferred_element_type=F32) + b_ref[...]
        o_ref[0] = o.astype(BF16)

    @pl.when(t < ctx_tiles)
    def _():
        stage1(0, n_ctx)
        stage2(dc_ref[...], n_ctx)

    @pl.when(t == ctx_tiles)
    def _():
        stage1(n_ctx, n_lat)

    @pl.when(t >= ctx_tiles)
    def _():
        stage2(dl_ref[...], n_lat)


def _fourier(f, fourier_w, fourier_b, n_ctx):
    batch, ta, _ = f.shape
    n_lat = ta - n_ctx
    assert n_ctx == ROW_TILE
    nt = ta // ROW_TILE
    cc, sc = _dft_tables(F_GROUP_W)
    eye = np.eye(F_GROUPS)
    cs = jnp.asarray(np.concatenate([np.kron(eye, cc), np.kron(eye, sc)], axis=1), BF16)
    cl, sl = _dft_tables(n_lat)
    dft_lat = jnp.asarray(np.concatenate([cl, -sl], axis=1), BF16)
    cx, sx = _dft_tables(n_ctx)
    dft_ctx = jnp.asarray(np.concatenate([cx, -sx], axis=1), BF16)
    w_blk = jnp.einsum('gce,gh->gche', fourier_w, jnp.eye(F_GROUPS, dtype=F32))
    w_blk = w_blk.reshape(F_W, F_W).astype(BF16)
    return pl.pallas_call(
        functools.partial(_fourier_kernel, n_ctx=n_ctx, n_lat=n_lat),
        grid=(batch, nt),
        in_specs=[pl.BlockSpec((1, ta, F_W), lambda b, t: (b, 0, 0)),
                  _full((F_W, 2 * F_W)),
                  pl.BlockSpec((ROW_TILE, 2 * n_lat), lambda b, t: (jnp.maximum(t - 1, 0), 0)),
                  _full((n_ctx, 2 * n_ctx)),
                  _full((F_W, F_W)), _full((1, F_W))],
        out_specs=pl.BlockSpec((1, ROW_TILE, F_W), lambda b, t: (b, t, 0)),
        out_shape=jax.ShapeDtypeStruct((batch, ta, F_W), BF16),
        scratch_shapes=[pltpu.VMEM((2 * n_lat, F_W), BF16)],
        compiler_params=_cparams("parallel", "arbitrary"),
        name="fourier_mix",
    )(f, cs, dft_lat, dft_ctx, w_blk, fourier_b.reshape(1, F_W))


def _outproj_ln_kernel(a_ref, b_ref, h_ref, gate_ref, w_ref, g_ref, beta_ref, o_ref):
    wa = a_ref.shape[2]
    y = jnp.dot(a_ref[0], w_ref[0:wa, :], preferred_element_type=F32)
    y = y + jnp.dot(b_ref[0], w_ref[wa:, :], preferred_element_type=F32)
    o_ref[0] = _layer_norm(ALPHA * h_ref[0] + gate_ref[0] * y, g_ref[...], beta_ref[...])


def _outproj_ln(a, b2, h, mods, layer, w_out, ln_g, ln_b, ctx_tiles):
    batch, ta, _ = h.shape
    nt = ta // ROW_TILE
    row = lambda w: pl.BlockSpec((1, ROW_TILE, w), lambda b, t: (b, t, 0))
    return pl.pallas_call(
        _outproj_ln_kernel,
        grid=(batch, nt),
        in_specs=[row(a.shape[2]), row(b2.shape[2]), row(D_MODEL),
                  _mod_spec(layer, 2, batch, ctx_tiles),
                  _full((D_MODEL, D_MODEL)), _full((1, D_MODEL)), _full((1, D_MODEL))],
        out_specs=row(D_MODEL),
        out_shape=jax.ShapeDtypeStruct((batch, ta, D_MODEL), F32),
        compiler_params=_cparams("parallel", "arbitrary"),
        name="outproj_ln",
    )(a, b2, h, mods, w_out, ln_g.reshape(1, D_MODEL), ln_b.reshape(1, D_MODEL))


def _ffn_kernel(h_ref, hp_ref, hn_ref, shift_ref, scale_ref, gate_ref,
                wup_ref, bup_ref, cw_ref, cb_ref, wdn_ref, bdn_ref, g_ref, beta_ref,
                o_ref, uext_ref, ubf_ref, act_ref, *, ctx_tiles, tile_off, nt_seq):
    t = pl.program_id(1) + tile_off
    seg_first = (t == 0) | (t == ctx_tiles)
    seg_last = (t == nt_seq - 1) | (t == ctx_tiles - 1)
    sc = 1.0 + scale_ref[0]
    sh = shift_ref[0]
    h = h_ref[0]
    tm = h.shape[0]
    uext_ref[0:HALO, :] = hp_ref[0] * sc + sh
    uext_ref[HALO:HALO + tm, :] = h * sc + sh
    uext_ref[HALO + tm:, :] = hn_ref[0] * sc + sh
    ubf_ref[...] = uext_ref[...].astype(BF16)
    row8 = lax.broadcasted_iota(jnp.int32, (8, 2 * FF_TILE), 0)
    n_ff = D_FF // FF_TILE

    for j in range(n_ff):
        cols = slice(j * 2 * FF_TILE, (j + 1) * 2 * FF_TILE)
        zr = jnp.dot(ubf_ref[...], wup_ref[:, cols], preferred_element_type=F32)
        bup = bup_ref[:, cols]
        cw = cw_ref[:, cols]
        bias = cb_ref[:, cols] + (cw[0:1] + cw[1:2] + cw[2:3]) * bup
        z0 = zr[HALO:HALO + tm]
        prev = jnp.where(seg_first, -bup, zr[HALO - 1:HALO])
        nxt = jnp.where(seg_last, -bup, zr[HALO + tm:HALO + tm + 1])
        down = pltpu.roll(z0, 1, 0)
        up = pltpu.roll(z0, tm - 1, 0)
        zm1 = jnp.concatenate([jnp.where(row8 == 0, prev, down[0:8]), down[8:]], axis=0)
        zp1 = jnp.concatenate([up[:tm - 8], jnp.where(row8 == 7, nxt, up[tm - 8:])], axis=0)
        hdn = cw[0:1] * zm1 + cw[1:2] * z0 + cw[2:3] * zp1 + bias
        act_ref[:, j * FF_TILE:(j + 1) * FF_TILE] = (
            hdn[:, :FF_TILE] * _silu(hdn[:, FF_TILE:])).astype(BF16)
    f = jnp.dot(act_ref[...], wdn_ref[...], preferred_element_type=F32) + bdn_ref[...]
    o_ref[0] = _layer_norm(ALPHA * h + gate_ref[0] * f, g_ref[...], beta_ref[...])


def _ffn(h, mods, layer, w_up, b_up, conv_w, conv_b, w_down, b_down, ln_g, ln_b,
         ctx_tiles, tile_off):
    batch, ta, _ = h.shape
    nt_seq = ta // ROW_TILE
    nt = nt_seq - tile_off
    hb = ROW_TILE // HALO
    n_hblk = ta // HALO
    n_ff = D_FF // FF_TILE

    def interleave(a):
        lead = a.shape[:-1]
        a = a.reshape(lead + (2, n_ff, FF_TILE))
        return jnp.swapaxes(a, -3, -2).reshape(lead + (2 * D_FF,))

    w_up, b_up, conv_w, conv_b = (interleave(a) for a in (w_up, b_up, conv_w, conv_b))
    mspec = lambda j: pl.BlockSpec(
        (1, 1, D_MODEL),
        lambda b, t: ((layer * MOD_ROWS + jnp.where(t + tile_off < ctx_tiles, batch, b)) * 6 + j, 0, 0))
    return pl.pallas_call(
        functools.partial(_ffn_kernel, ctx_tiles=ctx_tiles, tile_off=tile_off, nt_seq=nt_seq),
        grid=(batch, nt),
        in_specs=[pl.BlockSpec((1, ROW_TILE, D_MODEL), lambda b, t: (b, t + tile_off, 0)),
                  pl.BlockSpec((1, HALO, D_MODEL),
                               lambda b, t: (b, jnp.maximum((t + tile_off) * hb - 1, 0), 0)),
                  pl.BlockSpec((1, HALO, D_MODEL),
                               lambda b, t: (b, jnp.minimum((t + tile_off + 1) * hb, n_hblk - 1), 0)),
                  mspec(3), mspec(4), mspec(5),
                  _full((D_MODEL, 2 * D_FF)), _full((1, 2 * D_FF)),
                  _full((3, 2 * D_FF)), _full((1, 2 * D_FF)),
                  _full((D_FF, D_MODEL)), _full((1, D_MODEL)),
                  _full((1, D_MODEL)), _full((1, D_MODEL))],
        out_specs=pl.BlockSpec((1, ROW_TILE, D_MODEL), lambda b, t: (b, t, 0)),
        out_shape=jax.ShapeDtypeStruct((batch, nt * ROW_TILE, D_MODEL), F32),
        scratch_shapes=[pltpu.VMEM((ROW_TILE + 2 * HALO, D_MODEL), F32),
                        pltpu.VMEM((ROW_TILE + 2 * HALO, D_MODEL), BF16),
                        pltpu.VMEM((ROW_TILE, D_FF), BF16)],
        compiler_params=_cparams("parallel", "arbitrary"),
        name="conv_ffn_ln",
    )(h, h, h, mods, mods, mods, w_up, b_up.reshape(1, -1), conv_w, conv_b.reshape(1, -1),
      w_down, b_down.reshape(1, -1), ln_g.reshape(1, -1), ln_b.reshape(1, -1))


def _inproj_ssm_kernel(h_ref, shift_ref, scale_ref, w_ref, z_ref, xbc_ref, dt_ref, us_ref):
    u = (h_ref[0] * (1.0 + scale_ref[0]) + shift_ref[0]).astype(BF16)
    p = jnp.dot(u, w_ref[...], preferred_element_type=F32)
    z_ref[0] = p[:, :SSD_W]
    xbc_ref[0] = p[:, SSD_W:SSD_W + XBC_W]
    us_ref[...] = p[:, SSD_W + XBC_W:SSD_W + XBC_W + S5_W]
    dt_ref[0] = p[:, SSD_W + XBC_W + S5_W:]


def _inproj_ssm(h, mods, layer, w_in_pad, ctx_tiles):
    batch, ta, _ = h.shape
    nt = ta // ROW_TILE
    row = lambda w: pl.BlockSpec((1, ROW_TILE, w), lambda b, t: (b, t, 0))
    return pl.pallas_call(
        _inproj_ssm_kernel,
        grid=(batch, nt),
        in_specs=[row(D_MODEL), _mod_spec(layer, 0, batch, ctx_tiles),
                  _mod_spec(layer, 1, batch, ctx_tiles), _full((D_MODEL, SSM_IN_PAD))],
        out_specs=[row(SSD_W), row(XBC_W), row(DT_PAD),
                   pl.BlockSpec((ROW_TILE, S5_W), lambda b, t: (t, b))],
        out_shape=[jax.ShapeDtypeStruct((batch, ta, SSD_W), F32),
                   jax.ShapeDtypeStruct((batch, ta, XBC_W), F32),
                   jax.ShapeDtypeStruct((batch, ta, DT_PAD), F32),
                   jax.ShapeDtypeStruct((ta, batch * S5_W), F32)],
        compiler_params=_cparams("parallel", "arbitrary"),
        name="inproj_ssm",
    )(h, mods, mods, w_in_pad)


def _cumsum_rows(v):
    n = v.shape[0]
    row = lax.broadcasted_iota(jnp.int32, v.shape, 0)
    s = 1
    while s < n:
        v = v + jnp.where(row >= s, pltpu.roll(v, s, 0), 0.0)
        s *= 2
    return v


def _expand_heads(v, e_ref):
    hi = v.astype(BF16)
    lo = (v - hi.astype(F32)).astype(BF16)
    e = e_ref[...]
    return (jnp.dot(hi, e, preferred_element_type=F32)
            + jnp.dot(lo, e, preferred_element_type=F32))


def _ssd_kernel(xbc_ref, xp_ref, xn_ref, dt_ref, z_ref, cw_ref, cb_ref, alog_ref, dtb_ref,
                dsk_ref, ng_ref, ef_ref, eb_ref, o_ref,
                xs_ref, cd_ref, st_ref, dec_ref, y_ref, *, n_chunks, ctx_chunks):
    ph = pl.program_id(1)
    c = pl.program_id(2)
    q = SSD_CHUNK
    gw = SSD_W // SSD_GROUPS
    hpg = SSD_HEADS // SSD_GROUPS

    @pl.when(ph == 0)
    def _phase0():
        seg_first = (c == 0) | (c == ctx_chunks)
        seg_last = (c == ctx_chunks - 1) | (c == n_chunks - 1)
        xr = xbc_ref[0]
        prev = jnp.where(seg_first, 0.0, xp_ref[0, HALO - 1:HALO, :])
        nxt = jnp.where(seg_last, 0.0, xn_ref[0, 0:1, :])
        rowi = lax.broadcasted_iota(jnp.int32, xr.shape, 0)
        xm1 = jnp.where(rowi == 0, prev, pltpu.roll(xr, 1, 0))
        xp1 = jnp.where(rowi == q - 1, nxt, pltpu.roll(xr, q - 1, 0))
        cw = cw_ref[...]
        xs = _silu(cw[0:1] * xm1 + cw[1:2] * xr + cw[2:3] * xp1 + cb_ref[...])
        xs_ref[c] = xs

        raw = dt_ref[0] + dtb_ref[...]
        dtv = jnp.maximum(raw, 0.0) + jnp.log1p(jnp.exp(-jnp.abs(raw)))
        adt = dtv * (-jnp.exp(alog_ref[...]))
        cum = _cumsum_rows(adt)
        tot = cum[q - 1:q, :]
        lane = lax.broadcasted_iota(jnp.int32, cum.shape, 1)
        cc = jnp.where(lane < SSD_HEADS, cum, tot - cum + adt)
        cd_ref[c, 0] = cc
        cd_ref[c, 1] = dtv
        w_end = jnp.exp(tot - cc) * dtv
        dec16 = jnp.broadcast_to(jnp.exp(tot), (16, DT_PAD))
        x = xs[:, :SSD_W]
        for d, e_ref in enumerate((ef_ref, eb_ref)):
            wx = (_expand_heads(w_end, e_ref) * x).astype(BF16)
            for g in range(SSD_GROUPS):
                bmt = xs[:, SSD_W + g * SSD_STATE:SSD_W + (g + 1) * SSD_STATE].T.astype(BF16)
                st_ref[c, d, :, g * gw:(g + 1) * gw] = jnp.dot(
                    bmt, wx[:, g * gw:(g + 1) * gw], preferred_element_type=F32)
            dec_ref[c, d] = _expand_heads(dec16, e_ref)[0:8]

    @pl.when((ph == 1) & (c == 0))
    def _recurrence():
        fwd = list(range(n_chunks))
        bwd = list(range(ctx_chunks - 1, -1, -1)) + list(range(n_chunks - 1, ctx_chunks - 1, -1))
        for d, order in enumerate((fwd, bwd)):
            state = jnp.zeros((SSD_STATE, SSD_W), F32)
            for ci in order:
                contrib = st_ref[ci, d]
                st_ref[ci, d] = state
                state = state * dec_ref[ci, d, 0:1, :] + contrib

    @pl.when((ph == 1) & (c >= ctx_chunks))
    def _phase1():
        xs = xs_ref[c]
        x = xs[:, :SSD_W]
        cc = cd_ref[c, 0]
        dtv = cd_ref[c, 1]
        cct = cc.T
        dtt = dtv.T
        ecc = jnp.exp(cc)
        ef = _expand_heads(ecc, ef_ref)
        eb = _expand_heads(ecc, eb_ref)
        rowi = lax.broadcasted_iota(jnp.int32, (q, q), 0)
        coli = lax.broadcasted_iota(jnp.int32, (q, q), 1)
        lower = coli <= rowi
        upper = coli >= rowi
        lane = lax.broadcasted_iota(jnp.int32, (q, 128), 1)
        neg = jnp.float32(-jnp.inf)
        for g in range(SSD_GROUPS):
            bm = xs[:, SSD_W + g * SSD_STATE:SSD_W + (g + 1) * SSD_STATE].astype(BF16)
            cm = xs[:, SSD_W + (SSD_GROUPS + g) * SSD_STATE:
                    SSD_W + (SSD_GROUPS + g + 1) * SSD_STATE].astype(BF16)
            gmat = lax.dot_general(cm, bm, (((1,), (1,)), ((), ())), preferred_element_type=F32)
            sl = slice(g * gw, (g + 1) * gw)
            yoff = (ef[:, sl] * jnp.dot(cm, st_ref[c, 0, :, sl].astype(BF16),
                                        preferred_element_type=F32)
                    + eb[:, sl] * jnp.dot(cm, st_ref[c, 1, :, sl].astype(BF16),
                                          preferred_element_type=F32))
            for pair in range(hpg // 2):
                col0 = g * gw + pair * 128
                xpair = x[:, col0:col0 + 128].astype(BF16)
                res = []
                for hh in range(2):
                    hd = g * hpg + pair * 2 + hh
                    hb_ = SSD_HEADS + hd
                    lf = jnp.exp(jnp.where(lower, cc[:, hd:hd + 1] - cct[hd:hd + 1, :], neg))
                    lb = jnp.exp(jnp.where(upper, cc[:, hb_:hb_ + 1] - cct[hb_:hb_ + 1, :], neg))
                    mt = gmat * (lf * dtt[hd:hd + 1, :] + lb * dtt[hb_:hb_ + 1, :])
                    res.append(jnp.dot(mt.astype(BF16), xpair, preferred_element_type=F32))
                ydiag = jnp.where(lane < SSD_HEAD_DIM, res[0], res[1])
                y_ref[:, col0:col0 + 128] = (ydiag + yoff[:, pair * 128:(pair + 1) * 128]
                                             + dsk_ref[:, col0:col0 + 128] * x[:, col0:col0 + 128])
        gated = y_ref[...] * _silu(z_ref[0])
        normed = gated * lax.rsqrt(jnp.mean(gated * gated, axis=-1, keepdims=True) + LN_EPS)
        o_ref[0] = (normed * ng_ref[...]).astype(BF16)


def _ssd(xbc, dt, z, conv_w, conv_b, a_log, dt_bias, d_skip, norm_g, n_ctx):
    batch, ta, _ = xbc.shape
    q = SSD_CHUNK
    n_chunks = ta // q
    ctx_chunks = n_ctx // q
    hb = q // HALO
    n_hblk = ta // HALO
    pad24 = lambda v: jnp.pad(v.reshape(1, DT_W), ((0, 0), (0, DT_PAD - DT_W)))
    heads = np.arange(SSD_HEADS)
    ef = np.zeros((DT_PAD, SSD_W), np.float32)
    eb = np.zeros((DT_PAD, SSD_W), np.float32)
    for hd in heads:
        ef[hd, hd * SSD_HEAD_DIM:(hd + 1) * SSD_HEAD_DIM] = 1.0
        eb[SSD_HEADS + hd, hd * SSD_HEAD_DIM:(hd + 1) * SSD_HEAD_DIM] = 1.0
    dsk = jnp.repeat(d_skip.astype(F32), SSD_HEAD_DIM).reshape(1, SSD_W)
    chunk = lambda w: pl.BlockSpec((1, q, w), lambda b, ph, c: (b, jnp.where(ph == 0, c, n_chunks - 1), 0))
    return pl.pallas_call(
        functools.partial(_ssd_kernel, n_chunks=n_chunks, ctx_chunks=ctx_chunks),
        grid=(batch, 2, n_chunks),
        in_specs=[chunk(XBC_W),
                  pl.BlockSpec((1, HALO, XBC_W),
                               lambda b, ph, c: (b, jnp.where(ph == 0, jnp.maximum(c * hb - 1, 0), 0), 0)),
                  pl.BlockSpec((1, HALO, XBC_W),
                               lambda b, ph, c: (b, jnp.where(ph == 0, jnp.minimum((c + 1) * hb, n_hblk - 1), 0), 0)),
                  chunk(DT_PAD),
                  pl.BlockSpec((1, q, SSD_W), lambda b, ph, c: (b, jnp.where(ph == 1, c, 0), 0)),
                  _full((3, XBC_W)), _full((1, XBC_W)), _full((1, DT_PAD)), _full((1, DT_PAD)),
                  _full((1, SSD_W)), _full((1, SSD_W)),
                  _full((DT_PAD, SSD_W)), _full((DT_PAD, SSD_W))],
        out_specs=pl.BlockSpec(
            (1, q, SSD_W),
            lambda b, ph, c: (b, jnp.where(ph == 1, jnp.maximum(c - ctx_chunks, 0), 0), 0)),
        out_shape=jax.ShapeDtypeStruct((batch, ta - n_ctx, SSD_W), BF16),
        scratch_shapes=[pltpu.VMEM((n_chunks, q, XBC_W), F32),
                        pltpu.VMEM((n_chunks, 2, q, DT_PAD), F32),
                        pltpu.VMEM((n_chunks, 2, SSD_STATE, SSD_W), F32),
                        pltpu.VMEM((n_chunks, 2, 8, SSD_W), F32),
                        pltpu.VMEM((q, SSD_W), F32)],
        compiler_params=_cparams("parallel", "arbitrary", "arbitrary"),
        name="ssd_bidir",
    )(xbc, xbc, xbc, dt, z, conv_w, conv_b.reshape(1, XBC_W), pad24(a_log), pad24(dt_bias),
      dsk, norm_g.reshape(1, SSD_W), jnp.asarray(ef, BF16), jnp.asarray(eb, BF16))


def _s5_disc_kernel(lr_ref, li_ref, ldt_ref, bre_ref, bim_ref, cre_ref, cim_ref,
                    a_ref, bd_ref, cd_ref):
    lr, li = lr_ref[...], li_ref[...]
    dt = jnp.exp(ldt_ref[...])
    mag = jnp.exp(dt * lr)
    ab_re, ab_im = mag * jnp.cos(dt * li), mag * jnp.sin(dt * li)
    den = lr * lr + li * li
    k_re = ((ab_re - 1.0) * lr + ab_im * li) / den
    k_im = (ab_im * lr - (ab_re - 1.0) * li) / den
    bre, bim = bre_ref[...], bim_ref[...]
    for d in range(2):
        a_ref[d, :, 0:S5_NSTATE] = jnp.broadcast_to(ab_re[d:d + 1], (8, S5_NSTATE))
        a_ref[d, :, S5_NSTATE:] = jnp.broadcast_to(ab_im[d:d + 1], (8, S5_NSTATE))
        kr, ki = k_re[d:d + 1], k_im[d:d + 1]
        bd_ref[d, :, 0:S5_NSTATE] = (kr * bre - ki * bim).astype(BF16)
        bd_ref[d, :, S5_NSTATE:] = (kr * bim + ki * bre).astype(BF16)
        cd_ref[d, 0:S5_NSTATE, :] = cre_ref[d].astype(BF16)
        cd_ref[d, S5_NSTATE:, :] = (-cim_ref[d]).astype(BF16)


def _s5_discretize(lam_re, lam_im, log_dt, b_re, b_im, c_re, c_im):
    eye = jnp.eye(S5_GROUPS, dtype=F32)
    bd = lambda b: jnp.einsum('gph,gk->ghkp', b, eye).reshape(S5_W, S5_NSTATE)
    cd = lambda cc: jnp.einsum('dghp,gk->dgpkh', cc, eye).reshape(2, S5_NSTATE, S5_W)
    ldt = jnp.repeat(log_dt, S5_STATE, axis=-1)
    return pl.pallas_call(
        _s5_disc_kernel,
        out_shape=[jax.ShapeDtypeStruct((2, 8, 2 * S5_NSTATE), F32),
                   jax.ShapeDtypeStruct((2, S5_W, 2 * S5_NSTATE), BF16),
                   jax.ShapeDtypeStruct((2, 2 * S5_NSTATE, S5_W), BF16)],
        compiler_params=pltpu.CompilerParams(vmem_limit_bytes=VMEM_LIMIT_BYTES),
        name="s5_discretize",
    )(lam_re.reshape(2, S5_NSTATE), lam_im.reshape(2, S5_NSTATE), ldt,
      bd(b_re), bd(b_im), cd(c_re), cd(c_im))


S5_TIME_CHUNK = 128
S5_UNROLL = 8


def _s5_scan_kernel(u_ref, a_ref, bd_ref, cd_ref, o_ref, hs_ref, carry_ref, *, batch):
    d = pl.program_id(0)
    j = pl.program_id(1)
    n = S5_NSTATE

    @pl.when(j == 0)
    def _():
        carry_ref[...] = jnp.zeros_like(carry_ref)

    hs_ref[...] = jnp.dot(u_ref[...].astype(BF16), bd_ref[0], preferred_element_type=F32)
    ar = a_ref[0, :, 0:n]
    ai = a_ref[0, :, n:]
    if batch != 8:
        ar = jnp.broadcast_to(ar[0:1], (batch, n))
        ai = jnp.broadcast_to(ai[0:1], (batch, n))

    def body(i, carry):
        hr, hi = carry
        for s in range(S5_UNROLL):
            step = i * S5_UNROLL + s
            step = jnp.where(d == 0, step, S5_TIME_CHUNK - 1 - step)
            rows = pl.ds(pl.multiple_of(step * batch, batch), batch)
            nr = ar * hr - ai * hi + hs_ref[rows, 0:n]
            ni = ar * hi + ai * hr + hs_ref[rows, n:]
            hs_ref[rows, 0:n] = nr
            hs_ref[rows, n:] = ni
            hr, hi = nr, ni
        return hr, hi

    hr, hi = lax.fori_loop(0, S5_TIME_CHUNK // S5_UNROLL, body,
                           (carry_ref[:, 0:n], carry_ref[:, n:]))
    carry_ref[:, 0:n] = hr
    carry_ref[:, n:] = hi
    o_ref[0] = jnp.dot(hs_ref[...].astype(BF16), cd_ref[0], preferred_element_type=F32)


def _s5_scan(us_flat, a, bd, cd, batch, n_ctx):
    rows_total = us_flat.shape[0]
    ta = rows_total // batch
    tc = S5_TIME_CHUNK
    n_chunks = ta // tc
    ctx_chunks = n_ctx // tc
    blk = tc * batch

    def chunk_of(d, j):
        bwd = jnp.where(j < ctx_chunks, ctx_chunks - 1 - j, n_chunks - 1 - (j - ctx_chunks))
        return jnp.where(d == 0, j, bwd)

    return pl.pallas_call(
        functools.partial(_s5_scan_kernel, batch=batch),
        grid=(2, n_chunks),
        in_specs=[pl.BlockSpec((blk, S5_W), lambda d, j: (chunk_of(d, j), 0)),
                  pl.BlockSpec((1, 8, 2 * S5_NSTATE), lambda d, j: (d, 0, 0)),
                  pl.BlockSpec((1, S5_W, 2 * S5_NSTATE), lambda d, j: (d, 0, 0)),
                  pl.BlockSpec((1, 2 * S5_NSTATE, S5_W), lambda d, j: (d, 0, 0))],
        out_specs=pl.BlockSpec((1, blk, S5_W), lambda d, j: (d, chunk_of(d, j), 0)),
        out_shape=jax.ShapeDtypeStruct((2, rows_total, S5_W), F32),
        scratch_shapes=[pltpu.VMEM((blk, 2 * S5_NSTATE), F32),
                        pltpu.VMEM((batch, 2 * S5_NSTATE), F32)],
        compiler_params=_cparams("arbitrary", "arbitrary"),
        name="s5_scan",
    )(us_flat, a, bd, cd)


def _merge_ln_kernel(gs_ref, y5_ref, us_ref, h_ref, gate_ref, dd_ref, gw_ref, gb_ref,
                     w_ref, g_ref, beta_ref, o_ref):
    y5 = y5_ref[0] + y5_ref[1] + dd_ref[...] * us_ref[...]
    ge = jax.nn.gelu(y5)
    s5 = ge * jax.nn.sigmoid(
        jnp.dot(ge.astype(BF16), gw_ref[...], preferred_element_type=F32) + gb_ref[...])
    y = jnp.dot(gs_ref[0], w_ref[0:SSD_W, :], preferred_element_type=F32)
    y = y + jnp.dot(s5.astype(BF16), w_ref[SSD_W:, :], preferred_element_type=F32)
    o_ref[0] = _layer_norm(ALPHA * h_ref[0] + gate_ref[0] * y, g_ref[...], beta_ref[...])


def _merge_ln(g_ssd, y5, us_t, h, mods, layer, s5_d, glu_w, glu_b, w_out, ln_g, ln_b, ctx_tiles):
    batch, ta, _ = h.shape
    nt = ta // ROW_TILE - ctx_tiles
    y5v = y5.reshape(2, ta, batch * S5_W)
    return pl.pallas_call(
        _merge_ln_kernel,
        grid=(batch, nt),
        in_specs=[pl.BlockSpec((1, ROW_TILE, SSD_W), lambda b, t: (b, t, 0)),
                  pl.BlockSpec((2, ROW_TILE, S5_W), lambda b, t: (0, t + ctx_tiles, b)),
                  pl.BlockSpec((ROW_TILE, S5_W), lambda b, t: (t + ctx_tiles, b)),
                  pl.BlockSpec((1, ROW_TILE, D_MODEL), lambda b, t: (b, t + ctx_tiles, 0)),
                  pl.BlockSpec((1, 1, D_MODEL), lambda b, t: ((layer * MOD_ROWS + b) * 6 + 2, 0, 0)),
                  _full((1, S5_W)), _full((S5_W, S5_W)), _full((1, S5_W)),
                  _full((D_MODEL, D_MODEL)), _full((1, D_MODEL)), _full((1, D_MODEL))],
        out_specs=pl.BlockSpec((1, ROW_TILE, D_MODEL), lambda b, t: (b, t, 0)),
        out_shape=jax.ShapeDtypeStruct((batch, nt * ROW_TILE, D_MODEL), F32),
        compiler_params=_cparams("parallel", "arbitrary"),
        name="merge_outproj_ln",
    )(g_ssd, y5v, us_t, h, mods, s5_d.reshape(1, S5_W), glu_w.astype(BF16),
      glu_b.reshape(1, S5_W), w_out, ln_g.reshape(1, -1), ln_b.reshape(1, -1))


def _attn_layer(h, mods, layer, i, n_ctx, p, keep_ctx):
    ctx_tiles = n_ctx // ROW_TILE
    ta = h.shape[1]
    lam_init = 0.8 - 0.6 * math.exp(-0.3 * layer)
    cos, sin = _rope_tables(ta - n_ctx, n_ctx)
    q, k, v, f = _inproj_attn(h, mods, layer, p['attn_w_in'][i].astype(BF16), cos, sin, ctx_tiles)
    o = _attention(q, k, v, p['attn_lambda'][i], p['attn_subln_g'][i], lam_init, ctx_tiles)
    fm = _fourier(f, p['fourier_w'][i], p['fourier_b'][i], n_ctx)
    h1 = _outproj_ln(o, fm, h, mods, layer, p['attn_w_out'][i].astype(BF16),
                     p['ln_g'][layer, 0], p['ln_b'][layer, 0], ctx_tiles)
    return _ffn(h1, mods, layer, p['ffn_w_up'][layer].astype(BF16), p['ffn_b_up'][layer],
                p['ffn_conv_w'][layer], p['ffn_conv_b'][layer],
                p['ffn_w_down'][layer].astype(BF16), p['ffn_b_down'][layer],
                p['ln_g'][layer, 1], p['ln_b'][layer, 1], ctx_tiles, 0 if keep_ctx else ctx_tiles)


def _ssm_layer(h, mods, layer, i, n_ctx, p, keep_ctx):
    assert not keep_ctx, "an SSM layer that must also emit context rows is not implemented"
    ctx_tiles = n_ctx // ROW_TILE
    batch = h.shape[0]
    w = p['ssm_w_in'][i]
    w_pad = jnp.concatenate(
        [w[:, :SSD_W + XBC_W], w[:, SSD_W + XBC_W + DT_W:], w[:, SSD_W + XBC_W:SSD_W + XBC_W + DT_W],
         jnp.zeros((D_MODEL, DT_PAD - DT_W), F32)], axis=1).astype(BF16)
    z, xbc, dt, us_t = _inproj_ssm(h, mods, layer, w_pad, ctx_tiles)
    g_ssd = _ssd(xbc, dt, z, p['ssd_conv_w'][i], p['ssd_conv_b'][i], p['ssd_a_log'][i],
                 p['ssd_dt_bias'][i], p['ssd_d'][i], p['ssd_norm_g'][i], n_ctx)
    a, bd, cd = _s5_discretize(p['s5_lambda_re'][i], p['s5_lambda_im'][i], p['s5_log_dt'][i],
                               p['s5_b_re'][i], p['s5_b_im'][i], p['s5_c_re'][i], p['s5_c_im'][i])
    y5 = _s5_scan(us_t.reshape(-1, S5_W), a, bd, cd, batch, n_ctx)
    h1 = _merge_ln(g_ssd, y5, us_t, h, mods, layer, p['s5_d'][i], p['s5_glu_w'][i],
                   p['s5_glu_b'][i], p['ssm_w_out'][i].astype(BF16),
                   p['ln_g'][layer, 0], p['ln_b'][layer, 0], ctx_tiles)
    return _ffn(h1, mods, layer, p['ffn_w_up'][layer].astype(BF16), p['ffn_b_up'][layer],
                p['ffn_conv_w'][layer], p['ffn_conv_b'][layer],
                p['ffn_w_down'][layer].astype(BF16), p['ffn_b_down'][layer],
                p['ln_g'][layer, 1], p['ln_b'][layer, 1], 0, 0)


def kernel(x, c, ctx, c_ctx, ada_w, ada_b, ln_g, ln_b, ffn_w_up, ffn_b_up, ffn_conv_w, ffn_conv_b, ffn_w_down, ffn_b_down, attn_w_in, attn_lambda, attn_subln_g, fourier_w, fourier_b, attn_w_out, ssm_w_in, ssd_conv_w, ssd_conv_b, ssd_a_log, ssd_dt_bias, ssd_d, ssd_norm_g, s5_lambda_re, s5_lambda_im, s5_log_dt, s5_b_re, s5_b_im, s5_c_re, s5_c_im, s5_d, s5_glu_w, s5_glu_b, ssm_w_out):
    p = dict(ln_g=ln_g, ln_b=ln_b, ffn_w_up=ffn_w_up, ffn_b_up=ffn_b_up, ffn_conv_w=ffn_conv_w,
             ffn_conv_b=ffn_conv_b, ffn_w_down=ffn_w_down, ffn_b_down=ffn_b_down,
             attn_w_in=attn_w_in, attn_lambda=attn_lambda, attn_subln_g=attn_subln_g,
             fourier_w=fourier_w, fourier_b=fourier_b, attn_w_out=attn_w_out, ssm_w_in=ssm_w_in,
             ssd_conv_w=ssd_conv_w, ssd_conv_b=ssd_conv_b, ssd_a_log=ssd_a_log,
             ssd_dt_bias=ssd_dt_bias, ssd_d=ssd_d, ssd_norm_g=ssd_norm_g,
             s5_lambda_re=s5_lambda_re, s5_lambda_im=s5_lambda_im, s5_log_dt=s5_log_dt,
             s5_b_re=s5_b_re, s5_b_im=s5_b_im, s5_c_re=s5_c_re, s5_c_im=s5_c_im, s5_d=s5_d,
             s5_glu_w=s5_glu_w, s5_glu_b=s5_glu_b, ssm_w_out=ssm_w_out)
    batch, n_lat, _ = x.shape
    n_ctx = ctx.shape[1]
    assert n_ctx == ROW_TILE and n_lat % ROW_TILE == 0 and batch < MOD_ROWS
    mods = _ada_mods(c, c_ctx, ada_w, ada_b)
    h = jnp.concatenate([ctx, x], axis=1)
    for layer in range(DEPTH):
        last = layer == DEPTH - 1
        layer_fn = _attn_layer if layer % 2 == 0 else _ssm_layer
        h = layer_fn(h, mods, layer, layer // 2, n_ctx, p, keep_ctx=not last)
    return h
```

```python
import functools
import math

import numpy as np
import jax
import jax.numpy as jnp
from jax import lax
from jax.experimental import pallas as pl
from jax.experimental.pallas import tpu as pltpu

F32 = jnp.float32
BF16 = jnp.bfloat16

D_MODEL = 1024
DEPTH = 2
GRID_W = 64
ROPE_BASE = 10000.0
LN_EPS = 1e-5
ALPHA = (2 * DEPTH) ** 0.25
ATTN_W = 768
ATTN_HEADS = 6
ATTN_HEAD_DIM = 64
F_W = 256
F_GROUPS = 4
F_GROUP_W = 64
ATTN_IN_W = 2 * ATTN_W + ATTN_W + F_W
SSD_W = 768
SSD_HEADS = 12
SSD_HEAD_DIM = 64
SSD_GROUPS = 2
SSD_STATE = 128
SSD_CHUNK = 128
XBC_W = SSD_W + 2 * SSD_GROUPS * SSD_STATE
DT_W = 2 * SSD_HEADS
DT_PAD = 128
S5_W = 256
S5_GROUPS = 16
S5_GROUP_W = 16
S5_STATE = 64
S5_NSTATE = S5_GROUPS * S5_STATE
SSM_IN_PAD = SSD_W + XBC_W + S5_W + DT_PAD
D_FF = 2816
FF_TILE = 256

ROW_TILE = 256
ATTN_Q_SUB = 128
HALO = 8
MOD_ROWS = 16
VMEM_LIMIT_BYTES = 56 * 1024 * 1024


def _cparams(*sem):
    return pltpu.CompilerParams(dimension_semantics=sem, vmem_limit_bytes=VMEM_LIMIT_BYTES)


def _silu(v):
    return v * jax.nn.sigmoid(v)


def _layer_norm(v, g, b):
    mu = jnp.mean(v, axis=-1, keepdims=True)
    d = v - mu
    var = jnp.mean(d * d, axis=-1, keepdims=True)
    return d * lax.rsqrt(var + LN_EPS) * g + b


def _full(shape):
    nd = len(shape)
    return pl.BlockSpec(shape, lambda *_: (0,) * nd)


def _mod_spec(layer, j, batch, ctx_tiles):
    def idx(b, t):
        row = jnp.where(t < ctx_tiles, batch, b)
        return ((layer * MOD_ROWS + row) * 6 + j, 0, 0)
    return pl.BlockSpec((1, 1, D_MODEL), idx)


def _ada_kernel(c_ref, w_ref, b_ref, o_ref):
    s = _silu(c_ref[...])
    o_ref[0] = jnp.dot(s, w_ref[0], preferred_element_type=F32,
                       precision=lax.Precision.HIGHEST) + b_ref[0]


def _ada_mods(c, c_ctx, ada_w, ada_b):
    batch = c.shape[0]
    nl = ada_w.shape[0]
    c_all = jnp.concatenate(
        [c, c_ctx[None], jnp.zeros((MOD_ROWS - batch - 1, D_MODEL), F32)], axis=0)
    out = pl.pallas_call(
        _ada_kernel,
        grid=(nl, 6),
        in_specs=[_full((MOD_ROWS, D_MODEL)),
                  pl.BlockSpec((1, D_MODEL, D_MODEL), lambda l, j: (l, 0, j)),
                  pl.BlockSpec((1, 1, D_MODEL), lambda l, j: (l, 0, j))],
        out_specs=pl.BlockSpec((1, MOD_ROWS, D_MODEL), lambda l, j: (l, 0, j)),
        out_shape=jax.ShapeDtypeStruct((nl, MOD_ROWS, 6 * D_MODEL), F32),
        compiler_params=_cparams("arbitrary", "arbitrary"),
        name="ada_mods",
    )(c_all, ada_w, ada_b.reshape(nl, 1, 6 * D_MODEL))
    return out.reshape(nl * MOD_ROWS * 6, 1, D_MODEL)


def _inproj_attn_kernel(hc_ref, hl_ref, shift_ref, scale_ref, w_ref, cos_ref, sin_ref,
                        q_ref, k_ref, v_ref, f_ref, *, ctx_tiles):
    h = jnp.where(pl.program_id(1) < ctx_tiles, hc_ref[0], hl_ref[0])
    u = (h * (1.0 + scale_ref[0]) + shift_ref[0]).astype(BF16)
    p = jnp.dot(u, w_ref[...], preferred_element_type=F32)
    cos = cos_ref[...]
    sin = sin_ref[...]
    lane = lax.broadcasted_iota(jnp.int32, cos.shape, 1)
    first_half = (lane % ATTN_HEAD_DIM) < (ATTN_HEAD_DIM // 2)

    def rope(blk):
        partner = jnp.where(first_half, pltpu.roll(blk, 128 - 32, 1), pltpu.roll(blk, 32, 1))
        return blk * cos + partner * sin

    qk_scale = ATTN_HEAD_DIM ** -0.5 * math.log2(math.e)
    for i in range(ATTN_HEADS):
        lo, hi = i * 128, (i + 1) * 128
        q_ref[0, :, lo:hi] = (rope(p[:, lo:hi]) * qk_scale).astype(BF16)
        k_ref[0, :, lo:hi] = rope(p[:, ATTN_W + lo:ATTN_W + hi]).astype(BF16)
    v_ref[0] = p[:, 2 * ATTN_W:3 * ATTN_W].astype(BF16)
    f_ref[0] = p[:, 3 * ATTN_W:].astype(BF16)


def _rope_tables(n_lat, n_ctx):
    rows = n_lat // GRID_W
    row = jnp.repeat(jnp.arange(rows, dtype=F32), GRID_W)
    col = jnp.tile(jnp.arange(GRID_W, dtype=F32), rows)
    n_freq = ATTN_HEAD_DIM // 4
    inv_freq = ROPE_BASE ** (-jnp.arange(n_freq, dtype=F32) / n_freq)
    ang = jnp.concatenate([row[:, None] * inv_freq, col[:, None] * inv_freq], axis=-1)
    cos, sin = jnp.cos(ang), jnp.sin(ang)
    cos128 = jnp.tile(cos, (1, 4))
    sin128 = jnp.tile(jnp.concatenate([-sin, sin], axis=-1), (1, 2))
    cos_all = jnp.concatenate([jnp.ones((n_ctx, 128), F32), cos128], axis=0)
    sin_all = jnp.concatenate([jnp.zeros((n_ctx, 128), F32), sin128], axis=0)
    return cos_all, sin_all


def _split_specs(width, ctx_tiles):
    return [pl.BlockSpec((1, ROW_TILE, width), lambda b, t: (b, jnp.minimum(t, ctx_tiles - 1), 0)),
            pl.BlockSpec((1, ROW_TILE, width), lambda b, t: (b, jnp.maximum(t - ctx_tiles, 0), 0))]


def _inproj_attn(h_ctx, h_lat, mods, layer, w_in, cos, sin, ctx_tiles):
    batch = h_lat.shape[0]
    ta = h_ctx.shape[1] + h_lat.shape[1]
    nt = ta // ROW_TILE
    row = lambda w: pl.BlockSpec((1, ROW_TILE, w), lambda b, t: (b, t, 0))
    tab = pl.BlockSpec((ROW_TILE, 128), lambda b, t: (t, 0))
    return pl.pallas_call(
        functools.partial(_inproj_attn_kernel, ctx_tiles=ctx_tiles),
        grid=(batch, nt),
        in_specs=_split_specs(D_MODEL, ctx_tiles)
        + [_mod_spec(layer, 0, batch, ctx_tiles), _mod_spec(layer, 1, batch, ctx_tiles),
           _full((D_MODEL, ATTN_IN_W)), tab, tab],
        out_specs=[row(ATTN_W), row(ATTN_W), row(ATTN_W), row(F_W)],
        out_shape=[jax.ShapeDtypeStruct((batch, ta, ATTN_W), BF16)] * 3
        + [jax.ShapeDtypeStruct((batch, ta, F_W), BF16)],
        compiler_params=_cparams("parallel", "arbitrary"),
        name="inproj_attn",
    )(h_ctx, h_lat, mods, mods, w_in, cos, sin)


def _diff_lambda(lam_ref, lam_init):
    lamv = lam_ref[...]
    l1 = jnp.sum(lamv[0:1] * lamv[1:2], axis=-1, keepdims=True)
    l2 = jnp.sum(lamv[2:3] * lamv[3:4], axis=-1, keepdims=True)
    return jnp.exp(l1) - jnp.exp(l2) + lam_init


def _stack_maps(q):
    lane = lax.broadcasted_iota(jnp.int32, q.shape, 1)
    zero = jnp.zeros_like(q)
    return jnp.concatenate([jnp.where(lane < ATTN_HEAD_DIM, q, zero),
                            jnp.where(lane >= ATTN_HEAD_DIM, q, zero)], axis=0)


def _scores(q2, k):
    return lax.dot_general(q2, k, (((1,), (1,)), ((), ())), preferred_element_type=F32)


def _diff_softmax_pv(load_s0, m0, load_s1, m1, lam, v, gain):
    e0 = jnp.exp2(load_s0() - m0)
    e1 = jnp.exp2(load_s1() - m1)
    c0 = 1.0 / jnp.sum(e0, axis=-1, keepdims=True)
    c1 = lam / jnp.sum(e1, axis=-1, keepdims=True)
    w = e0 * c0 - e1 * c1
    o = jnp.dot(w.astype(BF16), v, preferred_element_type=F32)
    o = o * lax.rsqrt(jnp.mean(o * o, axis=-1, keepdims=True) + LN_EPS)
    return (o * gain).astype(BF16)


def _attn_ctx_kernel(lam_ref, g_ref, q_ref, k_ref, v_ref, o_ref, *, lam_init):
    lam = _diff_lambda(lam_ref, lam_init)
    tq = q_ref.shape[1]
    s = _scores(_stack_maps(q_ref[0]), k_ref[0])
    m = jnp.max(s, axis=-1, keepdims=True)
    o_ref[0] = _diff_softmax_pv(lambda: s[:tq], m[:tq], lambda: s[tq:], m[tq:], lam, v_ref[0],
                                g_ref[...] * (1.0 - lam_init))


def _attn_lat_kernel(lam_ref, g_ref, q_ref, k_ref, v_ref, o_ref, sa_ref, ma_ref, sb_ref, mb_ref,
                     *, lam_init):
    t = pl.program_id(2)
    tq = q_ref.shape[1]
    sub = ATTN_Q_SUB

    @pl.when(t == 0)
    def _():
        sb_ref[...] = jnp.zeros_like(sb_ref)
        mb_ref[...] = jnp.zeros_like(mb_ref)

    def step(s_new, m_new, s_old, m_old):
        lam = _diff_lambda(lam_ref, lam_init)
        gain = g_ref[...] * (1.0 - lam_init)
        s = _scores(_stack_maps(q_ref[0]), k_ref[0])
        s_new[...] = s
        m_new[...] = jnp.max(s, axis=-1, keepdims=True)
        v = v_ref[0]
        for i in range(tq // sub):
            r0 = slice(i * sub, (i + 1) * sub)
            r1 = slice(tq + i * sub, tq + (i + 1) * sub)
            o_ref[0, r0, :] = _diff_softmax_pv(
                functools.partial(s_old.__getitem__, (r0, slice(None))), m_old[r0, :],
                functools.partial(s_old.__getitem__, (r1, slice(None))), m_old[r1, :],
                lam, v, gain)

    @pl.when(t % 2 == 0)
    def _():
        step(sa_ref, ma_ref, sb_ref, mb_ref)

    @pl.when(t % 2 == 1)
    def _():
        step(sb_ref, mb_ref, sa_ref, ma_ref)


def _attention(q, k, v, lam_vec, subln_g, lam_init, ctx_tiles):
    batch, ta, _ = q.shape
    n_ctx = ctx_tiles * ROW_TILE
    nt = ta // ROW_TILE - ctx_tiles
    g = subln_g.reshape(1, 128)
    small = [_full((4, ATTN_HEAD_DIM)), _full((1, 128))]
    cspec = pl.BlockSpec((1, n_ctx, 128), lambda b, h: (b, 0, h))
    o_ctx = pl.pallas_call(
        functools.partial(_attn_ctx_kernel, lam_init=lam_init),
        grid=(batch, ATTN_HEADS),
        in_specs=small + [cspec, cspec, cspec],
        out_specs=cspec,
        out_shape=jax.ShapeDtypeStruct((batch, n_ctx, ATTN_W), BF16),
        compiler_params=_cparams("parallel", "arbitrary"),
        name="diff_attention_ctx",
    )(lam_vec, g, q, k, v)
    kvspec = pl.BlockSpec((1, ta, 128), lambda b, h, t: (b, 0, h))
    o_lat = pl.pallas_call(
        functools.partial(_attn_lat_kernel, lam_init=lam_init),
        grid=(batch, ATTN_HEADS, nt + 1),
        in_specs=small + [
            pl.BlockSpec((1, ROW_TILE, 128),
                         lambda b, h, t: (b, ctx_tiles + jnp.minimum(t, nt - 1), h)),
            kvspec, kvspec],
        out_specs=pl.BlockSpec((1, ROW_TILE, 128), lambda b, h, t: (b, jnp.maximum(t - 1, 0), h)),
        out_shape=jax.ShapeDtypeStruct((batch, nt * ROW_TILE, ATTN_W), BF16),
        scratch_shapes=[pltpu.VMEM((2 * ROW_TILE, ta), F32), pltpu.VMEM((2 * ROW_TILE, 1), F32),
                        pltpu.VMEM((2 * ROW_TILE, ta), F32), pltpu.VMEM((2 * ROW_TILE, 1), F32)],
        compiler_params=_cparams("parallel", "arbitrary", "arbitrary"),
        name="diff_attention",
    )(lam_vec, g, q, k, v)
    return o_ctx, o_lat


def _dft_tables(n):
    k = np.arange(n, dtype=np.int64)
    ang = 2.0 * np.pi * ((k[:, None] * k[None, :]) % n).astype(np.float64) / n
    return np.cos(ang), np.sin(ang)


def _fourier_kernel(f_ref, cs_ref, dl_ref, dc_ref, w_ref, b_ref, o_ref, a_ref, *, n_ctx, n_lat):
    t = pl.program_id(1)
    ctx_tiles = n_ctx // ROW_TILE

    def stage1(rows0, n):
        a = jnp.dot(f_ref[0, rows0:rows0 + n, :], cs_ref[...], preferred_element_type=F32)
        a_ref[0:n, :] = a[:, :F_W].astype(BF16)
        a_ref[n:2 * n, :] = a[:, F_W:].astype(BF16)

    def stage2(dft, n):
        z = jnp.dot(dft, a_ref[0:2 * n, :], preferred_element_type=F32)
        z = z * (1.0 / math.sqrt(n * F_GROUP_W))
        o = jnp.dot(z.astype(BF16), w_ref[...], preferred_element_type=F32) + b_ref[...]
        o_ref[0] = o.astype(BF16)

    @pl.when(t < ctx_tiles)
    def _():
        stage1(0, n_ctx)
        stage2(dc_ref[...], n_ctx)

    @pl.when(t == ctx_tiles)
    def _():
        stage1(n_ctx, n_lat)

    @pl.when(t >= ctx_tiles)
    def _():
        stage2(dl_ref[...], n_lat)


def _fourier(f, fourier_w, fourier_b, n_ctx):
    batch, ta, _ = f.shape
    n_lat = ta - n_ctx
    assert n_ctx == ROW_TILE
    nt = ta // ROW_TILE
    cc, sc = _dft_tables(F_GROUP_W)
    eye = np.eye(F_GROUPS)
    cs = jnp.asarray(np.concatenate([np.kron(eye, cc), np.kron(eye, sc)], axis=1), BF16)
    cl, sl = _dft_tables(n_lat)
    dft_lat = jnp.asarray(np.concatenate([cl, -sl], axis=1), BF16)
    cx, sx = _dft_tables(n_ctx)
    dft_ctx = jnp.asarray(np.concatenate([cx, -sx], axis=1), BF16)
    w_blk = jnp.einsum('gce,gh->gche', fourier_w, jnp.eye(F_GROUPS, dtype=F32))
    w_blk = w_blk.reshape(F_W, F_W).astype(BF16)
    return pl.pallas_call(
        functools.partial(_fourier_kernel, n_ctx=n_ctx, n_lat=n_lat),
        grid=(batch, nt),
        in_specs=[pl.BlockSpec((1, ta, F_W), lambda b, t: (b, 0, 0)),
                  _full((F_W, 2 * F_W)),
                  pl.BlockSpec((ROW_TILE, 2 * n_lat), lambda b, t: (jnp.maximum(t - 1, 0), 0)),
                  _full((n_ctx, 2 * n_ctx)),
                  _full((F_W, F_W)), _full((1, F_W))],
        out_specs=pl.BlockSpec((1, ROW_TILE, F_W), lambda b, t: (b, t, 0)),
        out_shape=jax.ShapeDtypeStruct((batch, ta, F_W), BF16),
        scratch_shapes=[pltpu.VMEM((2 * n_lat, F_W), BF16)],
        compiler_params=_cparams("parallel", "arbitrary"),
        name="fourier_mix",
    )(f, cs, dft_lat, dft_ctx, w_blk, fourier_b.reshape(1, F_W))


def _outproj_ln_kernel(ac_ref, al_ref, b_ref, hc_ref, hl_ref, gate_ref, w_ref, g_ref, beta_ref,
                       o_ref, *, ctx_tiles):
    is_ctx = pl.program_id(1) < ctx_tiles
    a = jnp.where(is_ctx, ac_ref[0], al_ref[0])
    h = jnp.where(is_ctx, hc_ref[0], hl_ref[0])
    wa = a.shape[1]
    y = jnp.dot(a, w_ref[0:wa, :], preferred_element_type=F32)
    y = y + jnp.dot(b_ref[0], w_ref[wa:, :], preferred_element_type=F32)
    o_ref[0] = _layer_norm(ALPHA * h + gate_ref[0] * y, g_ref[...], beta_ref[...])


def _outproj_ln(a_ctx, a_lat, b2, h_ctx, h_lat, mods, layer, w_out, ln_g, ln_b, ctx_tiles):
    batch, ta, _ = b2.shape
    nt = ta // ROW_TILE
    row = lambda w: pl.BlockSpec((1, ROW_TILE, w), lambda b, t: (b, t, 0))
    return pl.pallas_call(
        functools.partial(_outproj_ln_kernel, ctx_tiles=ctx_tiles),
        grid=(batch, nt),
        in_specs=_split_specs(a_lat.shape[2], ctx_tiles) + [row(b2.shape[2])]
        + _split_specs(D_MODEL, ctx_tiles)
        + [_mod_spec(layer, 2, batch, ctx_tiles),
           _full((D_MODEL, D_MODEL)), _full((1, D_MODEL)), _full((1, D_MODEL))],
        out_specs=row(D_MODEL),
        out_shape=jax.ShapeDtypeStruct((batch, ta, D_MODEL), F32),
        compiler_params=_cparams("parallel", "arbitrary"),
        name="outproj_ln",
    )(a_ctx, a_lat, b2, h_ctx, h_lat, mods, w_out, ln_g.reshape(1, D_MODEL),
      ln_b.reshape(1, D_MODEL))


def _ffn_kernel(h_ref, hp_ref, hn_ref, shift_ref, scale_ref, gate_ref,
                wup_ref, bup_ref, cw_ref, cb_ref, wdn_ref, bdn_ref, g_ref, beta_ref,
                o_ref, uext_ref, ubf_ref, act_ref, *, ctx_tiles, tile_off, nt_seq):
    t = pl.program_id(1) + tile_off
    seg_first = (t == 0) | (t == ctx_tiles)
    seg_last = (t == nt_seq - 1) | (t == ctx_tiles - 1)
    sc = 1.0 + scale_ref[0]
    sh = shift_ref[0]
    h = h_ref[0]
    tm = h.shape[0]
    uext_ref[0:HALO, :] = hp_ref[0] * sc + sh
    uext_ref[HALO:HALO + tm, :] = h * sc + sh
    uext_ref[HALO + tm:, :] = hn_ref[0] * sc + sh
    ubf_ref[...] = uext_ref[...].astype(BF16)
    row8 = lax.broadcasted_iota(jnp.int32, (8, FF_TILE), 0)

    def hidden(col0):
        cols = slice(col0, col0 + FF_TILE)
        zr = jnp.dot(ubf_ref[...], wup_ref[:, cols], preferred_element_type=F32)
        bup = bup_ref[:, cols]
        cw = cw_ref[:, cols]
        bias = cb_ref[:, cols] + (cw[0:1] + cw[1:2] + cw[2:3]) * bup
        z0 = zr[HALO:HALO + tm]
        prev = jnp.where(seg_first, -bup, zr[HALO - 1:HALO])
        nxt = jnp.where(seg_last, -bup, zr[HALO + tm:HALO + tm + 1])
        down = pltpu.roll(z0, 1, 0)
        up = pltpu.roll(z0, tm - 1, 0)
        zm1 = jnp.concatenate([jnp.where(row8 == 0, prev, down[0:8]), down[8:]], axis=0)
        zp1 = jnp.concatenate([up[:tm - 8], jnp.where(row8 == 7, nxt, up[tm - 8:])], axis=0)
        return cw[0:1] * zm1 + cw[1:2] * z0 + cw[2:3] * zp1 + bias

    for j in range(D_FF // FF_TILE):
        val = hidden(j * FF_TILE)
        gat = hidden(D_FF + j * FF_TILE)
        act_ref[:, j * FF_TILE:(j + 1) * FF_TILE] = (val * _silu(gat)).astype(BF16)
    f = jnp.dot(act_ref[...], wdn_ref[...], preferred_element_type=F32) + bdn_ref[...]
    o_ref[0] = _layer_norm(ALPHA * h + gate_ref[0] * f, g_ref[...], beta_ref[...])


def _ffn(h, mods, layer, w_up, b_up, conv_w, conv_b, w_down, b_down, ln_g, ln_b,
         ctx_tiles, tile_off):
    batch, ta, _ = h.shape
    nt_seq = ta // ROW_TILE
    nt = nt_seq - tile_off
    hb = ROW_TILE // HALO
    n_hblk = ta // HALO
    mspec = lambda j: pl.BlockSpec(
        (1, 1, D_MODEL),
        lambda b, t: ((layer * MOD_ROWS + jnp.where(t + tile_off < ctx_tiles, batch, b)) * 6 + j, 0, 0))
    return pl.pallas_call(
        functools.partial(_ffn_kernel, ctx_tiles=ctx_tiles, tile_off=tile_off, nt_seq=nt_seq),
        grid=(batch, nt),
        in_specs=[pl.BlockSpec((1, ROW_TILE, D_MODEL), lambda b, t: (b, t + tile_off, 0)),
                  pl.BlockSpec((1, HALO, D_MODEL),
                               lambda b, t: (b, jnp.maximum((t + tile_off) * hb - 1, 0), 0)),
                  pl.BlockSpec((1, HALO, D_MODEL),
                               lambda b, t: (b, jnp.minimum((t + tile_off + 1) * hb, n_hblk - 1), 0)),
                  mspec(3), mspec(4), mspec(5),
                  _full((D_MODEL, 2 * D_FF)), _full((1, 2 * D_FF)),
                  _full((3, 2 * D_FF)), _full((1, 2 * D_FF)),
                  _full((D_FF, D_MODEL)), _full((1, D_MODEL)),
                  _full((1, D_MODEL)), _full((1, D_MODEL))],
        out_specs=pl.BlockSpec((1, ROW_TILE, D_MODEL), lambda b, t: (b, t, 0)),
        out_shape=jax.ShapeDtypeStruct((batch, nt * ROW_TILE, D_MODEL), F32),
        scratch_shapes=[pltpu.VMEM((ROW_TILE + 2 * HALO, D_MODEL), F32),
                        pltpu.VMEM((ROW_TILE + 2 * HALO, D_MODEL), BF16),
                        pltpu.VMEM((ROW_TILE, D_FF), BF16)],
        compiler_params=_cparams("parallel", "arbitrary"),
        name="conv_ffn_ln",
    )(h, h, h, mods, mods, mods, w_up, b_up.reshape(1, -1), conv_w, conv_b.reshape(1, -1),
      w_down, b_down.reshape(1, -1), ln_g.reshape(1, -1), ln_b.reshape(1, -1))


def _inproj_ssm_kernel(h_ref, shift_ref, scale_ref, w_ref, z_ref, xbc_ref, dt_ref, us_ref):
    u = (h_ref[0] * (1.0 + scale_ref[0]) + shift_ref[0]).astype(BF16)
    p = jnp.dot(u, w_ref[...], preferred_element_type=F32)
    z_ref[0] = p[:, :SSD_W]
    xbc_ref[0] = p[:, SSD_W:SSD_W + XBC_W]
    us_ref[...] = p[:, SSD_W + XBC_W:SSD_W + XBC_W + S5_W]
    dt_ref[0] = p[:, SSD_W + XBC_W + S5_W:]


def _inproj_ssm(h, mods, layer, w_in_pad, ctx_tiles):
    batch, ta, _ = h.shape
    nt = ta // ROW_TILE
    row = lambda w: pl.BlockSpec((1, ROW_TILE, w), lambda b, t: (b, t, 0))
    return pl.pallas_call(
        _inproj_ssm_kernel,
        grid=(batch, nt),
        in_specs=[row(D_MODEL), _mod_spec(layer, 0, batch, ctx_tiles),
                  _mod_spec(layer, 1, batch, ctx_tiles), _full((D_MODEL, SSM_IN_PAD))],
        out_specs=[row(SSD_W), row(XBC_W), row(DT_PAD),
                   pl.BlockSpec((ROW_TILE, S5_W), lambda b, t: (t, b))],
        out_shape=[jax.ShapeDtypeStruct((batch, ta, SSD_W), F32),
                   jax.ShapeDtypeStruct((batch, ta, XBC_W), F32),
                   jax.ShapeDtypeStruct((batch, ta, DT_PAD), F32),
                   jax.ShapeDtypeStruct((ta, batch * S5_W), F32)],
        compiler_params=_cparams("parallel", "arbitrary"),
        name="inproj_ssm",
    )(h, mods, mods, w_in_pad)


def _cumsum_rows(v):
    n = v.shape[0]
    row = lax.broadcasted_iota(jnp.int32, v.shape, 0)
    s = 1
    while s < n:
        v = v + jnp.where(row >= s, pltpu.roll(v, s, 0), 0.0)
        s *= 2
    return v


def _expand_heads(v, e_ref):
    hi = v.astype(BF16)
    lo = (v - hi.astype(F32)).astype(BF16)
    e = e_ref[...]
    return (jnp.dot(hi, e, preferred_element_type=F32)
            + jnp.dot(lo, e, preferred_element_type=F32))


def _ssd_kernel(xbc_ref, xp_ref, xn_ref, dt_ref, z_ref, cw_ref, cb_ref, alog_ref, dtb_ref,
                dsk_ref, ng_ref, ef_ref, eb_ref, o_ref,
                xs_ref, cd_ref, st_ref, dec_ref, y_ref, *, n_chunks, ctx_chunks):
    ph = pl.program_id(1)
    c = pl.program_id(2)
    q = SSD_CHUNK
    gw = SSD_W // SSD_GROUPS
    hpg = SSD_HEADS // SSD_GROUPS

    @pl.when(ph == 0)
    def _phase0():
        seg_first = (c == 0) | (c == ctx_chunks)
        seg_last = (c == ctx_chunks - 1) | (c == n_chunks - 1)
        xr = xbc_ref[0]
        prev = jnp.where(seg_first, 0.0, xp_ref[0, HALO - 1:HALO, :])
        nxt = jnp.where(seg_last, 0.0, xn_ref[0, 0:1, :])
        rowi = lax.broadcasted_iota(jnp.int32, xr.shape, 0)
        xm1 = jnp.where(rowi == 0, prev, pltpu.roll(xr, 1, 0))
        xp1 = jnp.where(rowi == q - 1, nxt, pltpu.roll(xr, q - 1, 0))
        cw = cw_ref[...]
        xs = _silu(cw[0:1] * xm1 + cw[1:2] * xr + cw[2:3] * xp1 + cb_ref[...])
        xs_ref[c] = xs

        raw = dt_ref[0] + dtb_ref[...]
        dtv = jnp.maximum(raw, 0.0) + jnp.log1p(jnp.exp(-jnp.abs(raw)))
        adt = dtv * (-jnp.exp(alog_ref[...]))
        cum = _cumsum_rows(adt)
        tot = cum[q - 1:q, :]
        lane = lax.broadcasted_iota(jnp.int32, cum.shape, 1)
        cc = jnp.where(lane < SSD_HEADS, cum, tot - cum + adt)
        cd_ref[c, 0] = cc
        cd_ref[c, 1] = dtv
        w_end = jnp.exp(tot - cc) * dtv
        dec16 = jnp.broadcast_to(jnp.exp(tot), (16, DT_PAD))
        x = xs[:, :SSD_W]
        for d, e_ref in enumerate((ef_ref, eb_ref)):
            wx = (_expand_heads(w_end, e_ref) * x).astype(BF16)
            for g in range(SSD_GROUPS):
                bmt = xs[:, SSD_W + g * SSD_STATE:SSD_W + (g + 1) * SSD_STATE].T.astype(BF16)
                st_ref[c, d, :, g * gw:(g + 1) * gw] = jnp.dot(
                    bmt, wx[:, g * gw:(g + 1) * gw], preferred_element_type=F32)
            dec_ref[c, d] = _expand_heads(dec16, e_ref)[0:8]

    @pl.when((ph == 1) & (c == 0))
    def _recurrence():
        fwd = list(range(n_chunks))
        bwd = list(range(ctx_chunks - 1, -1, -1)) + list(range(n_chunks - 1, ctx_chunks - 1, -1))
        for d, order in enumerate((fwd, bwd)):
            state = jnp.zeros((SSD_STATE, SSD_W), F32)
            for ci in order:
                contrib = st_ref[ci, d]
                st_ref[ci, d] = state
                state = state * dec_ref[ci, d, 0:1, :] + contrib

    @pl.when((ph == 1) & (c >= ctx_chunks))
    def _phase1():
        xs = xs_ref[c]
        x = xs[:, :SSD_W]
        cc = cd_ref[c, 0]
        dtv = cd_ref[c, 1]
        cct = cc.T
        dtt = dtv.T
        ecc = jnp.exp(cc)
        ef = _expand_heads(ecc, ef_ref)
        eb = _expand_heads(ecc, eb_ref)
        rowi = lax.broadcasted_iota(jnp.int32, (q, q), 0)
        coli = lax.broadcasted_iota(jnp.int32, (q, q), 1)
        lower = coli <= rowi
        upper = coli >= rowi
        lane = lax.broadcasted_iota(jnp.int32, (q, 128), 1)
        neg = jnp.float32(-jnp.inf)
        for g in range(SSD_GROUPS):
            bm = xs[:, SSD_W + g * SSD_STATE:SSD_W + (g + 1) * SSD_STATE].astype(BF16)
            cm = xs[:, SSD_W + (SSD_GROUPS + g) * SSD_STATE:
                    SSD_W + (SSD_GROUPS + g + 1) * SSD_STATE].astype(BF16)
            gmat = lax.dot_general(cm, bm, (((1,), (1,)), ((), ())), preferred_element_type=F32)
            sl = slice(g * gw, (g + 1) * gw)
            yoff = (ef[:, sl] * jnp.dot(cm, st_ref[c, 0, :, sl].astype(BF16),
                                        preferred_element_type=F32)
                    + eb[:, sl] * jnp.dot(cm, st_ref[c, 1, :, sl].astype(BF16),
                                          preferred_element_type=F32))
            for pair in range(hpg // 2):
                col0 = g * gw + pair * 128
                xpair = x[:, col0:col0 + 128].astype(BF16)
                res = []
                for hh in range(2):
                    hd = g * hpg + pair * 2 + hh
                    hb_ = SSD_HEADS + hd
                    lf = jnp.exp(jnp.where(lower, cc[:, hd:hd + 1] - cct[hd:hd + 1, :], neg))
                    lb = jnp.exp(jnp.where(upper, cc[:, hb_:hb_ + 1] - cct[hb_:hb_ + 1, :], neg))
                    mt = gmat * (lf * dtt[hd:hd + 1, :] + lb * dtt[hb_:hb_ + 1, :])
                    res.append(jnp.dot(mt.astype(BF16), xpair, preferred_element_type=F32))
                ydiag = jnp.where(lane < SSD_HEAD_DIM, res[0], res[1])
                y_ref[:, col0:col0 + 128] = (ydiag + yoff[:, pair * 128:(pair + 1) * 128]
                                             + dsk_ref[:, col0:col0 + 128] * x[:, col0:col0 + 128])
        gated = y_ref[...] * _silu(z_ref[0])
        normed = gated * lax.rsqrt(jnp.mean(gated * gated, axis=-1, keepdims=True) + LN_EPS)
        o_ref[0] = (normed * ng_ref[...]).astype(BF16)


def _ssd(xbc, dt, z, conv_w, conv_b, a_log, dt_bias, d_skip, norm_g, n_ctx):
    batch, ta, _ = xbc.shape
    q = SSD_CHUNK
    n_chunks = ta // q
    ctx_chunks = n_ctx // q
    hb = q // HALO
    n_hblk = ta // HALO
    pad24 = lambda v: jnp.pad(v.reshape(1, DT_W), ((0, 0), (0, DT_PAD - DT_W)))
    heads = np.arange(SSD_HEADS)
    ef = np.zeros((DT_PAD, SSD_W), np.float32)
    eb = np.zeros((DT_PAD, SSD_W), np.float32)
    for hd in heads:
        ef[hd, hd * SSD_HEAD_DIM:(hd + 1) * SSD_HEAD_DIM] = 1.0
        eb[SSD_HEADS + hd, hd * SSD_HEAD_DIM:(hd + 1) * SSD_HEAD_DIM] = 1.0
    dsk = jnp.repeat(d_skip.astype(F32), SSD_HEAD_DIM).reshape(1, SSD_W)
    chunk = lambda w: pl.BlockSpec((1, q, w), lambda b, ph, c: (b, jnp.where(ph == 0, c, n_chunks - 1), 0))
    return pl.pallas_call(
        functools.partial(_ssd_kernel, n_chunks=n_chunks, ctx_chunks=ctx_chunks),
        grid=(batch, 2, n_chunks),
        in_specs=[chunk(XBC_W),
                  pl.BlockSpec((1, HALO, XBC_W),
                               lambda b, ph, c: (b, jnp.where(ph == 0, jnp.maximum(c * hb - 1, 0), 0), 0)),
                  pl.BlockSpec((1, HALO, XBC_W),
                               lambda b, ph, c: (b, jnp.where(ph == 0, jnp.minimum((c + 1) * hb, n_hblk - 1), 0), 0)),
                  chunk(DT_PAD),
                  pl.BlockSpec((1, q, SSD_W), lambda b, ph, c: (b, jnp.where(ph == 1, c, 0), 0)),
                  _full((3, XBC_W)), _full((1, XBC_W)), _full((1, DT_PAD)), _full((1, DT_PAD)),
                  _full((1, SSD_W)), _full((1, SSD_W)),
                  _full((DT_PAD, SSD_W)), _full((DT_PAD, SSD_W))],
        out_specs=pl.BlockSpec(
            (1, q, SSD_W),
            lambda b, ph, c: (b, jnp.where(ph == 1, jnp.maximum(c - ctx_chunks, 0), 0), 0)),
        out_shape=jax.ShapeDtypeStruct((batch, ta - n_ctx, SSD_W), BF16),
        scratch_shapes=[pltpu.VMEM((n_chunks, q, XBC_W), F32),
                        pltpu.VMEM((n_chunks, 2, q, DT_PAD), F32),
                        pltpu.VMEM((n_chunks, 2, SSD_STATE, SSD_W), F32),
                        pltpu.VMEM((n_chunks, 2, 8, SSD_W), F32),
                        pltpu.VMEM((q, SSD_W), F32)],
        compiler_params=_cparams("parallel", "arbitrary", "arbitrary"),
        name="ssd_bidir",
    )(xbc, xbc, xbc, dt, z, conv_w, conv_b.reshape(1, XBC_W), pad24(a_log), pad24(dt_bias),
      dsk, norm_g.reshape(1, SSD_W), jnp.asarray(ef, BF16), jnp.asarray(eb, BF16))


def _s5_disc_kernel(lr_ref, li_ref, ldt_ref, bre_ref, bim_ref, cre_ref, cim_ref,
                    a_ref, bd_ref, cd_ref):
    lr, li = lr_ref[...], li_ref[...]
    dt = jnp.exp(ldt_ref[...])
    mag = jnp.exp(dt * lr)
    ab_re, ab_im = mag * jnp.cos(dt * li), mag * jnp.sin(dt * li)
    den = lr * lr + li * li
    k_re = ((ab_re - 1.0) * lr + ab_im * li) / den
    k_im = (ab_im * lr - (ab_re - 1.0) * li) / den
    bre, bim = bre_ref[...], bim_ref[...]
    for d in range(2):
        a_ref[d, :, 0:S5_NSTATE] = jnp.broadcast_to(ab_re[d:d + 1], (8, S5_NSTATE))
        a_ref[d, :, S5_NSTATE:] = jnp.broadcast_to(ab_im[d:d + 1], (8, S5_NSTATE))
        kr, ki = k_re[d:d + 1], k_im[d:d + 1]
        bd_ref[d, :, 0:S5_NSTATE] = (kr * bre - ki * bim).astype(BF16)
        bd_ref[d, :, S5_NSTATE:] = (kr * bim + ki * bre).astype(BF16)
        cd_ref[d, 0:S5_NSTATE, :] = cre_ref[d].astype(BF16)
        cd_ref[d, S5_NSTATE:, :] = (-cim_ref[d]).astype(BF16)


def _s5_discretize(lam_re, lam_im, log_dt, b_re, b_im, c_re, c_im):
    eye = jnp.eye(S5_GROUPS, dtype=F32)
    bd = lambda b: jnp.einsum('gph,gk->ghkp', b, eye).reshape(S5_W, S5_NSTATE)
    cd = lambda cc: jnp.einsum('dghp,gk->dgpkh', cc, eye).reshape(2, S5_NSTATE, S5_W)
    ldt = jnp.repeat(log_dt, S5_STATE, axis=-1)
    return pl.pallas_call(
        _s5_disc_kernel,
        out_shape=[jax.ShapeDtypeStruct((2, 8, 2 * S5_NSTATE), F32),
                   jax.ShapeDtypeStruct((2, S5_W, 2 * S5_NSTATE), BF16),
                   jax.ShapeDtypeStruct((2, 2 * S5_NSTATE, S5_W), BF16)],
        compiler_params=pltpu.CompilerParams(vmem_limit_bytes=VMEM_LIMIT_BYTES),
        name="s5_discretize",
    )(lam_re.reshape(2, S5_NSTATE), lam_im.reshape(2, S5_NSTATE), ldt,
      bd(b_re), bd(b_im), cd(c_re), cd(c_im))


S5_TIME_CHUNK = 128
S5_UNROLL = 8


def _s5_scan_kernel(u_ref, a_ref, bd_ref, cd_ref, o_ref, hs_ref, carry_ref, *, batch):
    d = pl.program_id(0)
    j = pl.program_id(1)
    n = S5_NSTATE

    @pl.when(j == 0)
    def _():
        carry_ref[...] = jnp.zeros_like(carry_ref)

    half = hs_ref.shape[0] // 2
    for r in (0, half):
        hs_ref[r:r + half, :] = jnp.dot(u_ref[r:r + half, :].astype(BF16), bd_ref[0],
                                        preferred_element_type=F32)
    ar = a_ref[0, :, 0:n]
    ai = a_ref[0, :, n:]
    if batch != 8:
        ar = jnp.broadcast_to(ar[0:1], (batch, n))
        ai = jnp.broadcast_to(ai[0:1], (batch, n))

    def body(i, carry):
        hr, hi = carry
        for s in range(S5_UNROLL):
            step = i * S5_UNROLL + s
            step = jnp.where(d == 0, step, S5_TIME_CHUNK - 1 - step)
            rows = pl.ds(pl.multiple_of(step * batch, batch), batch)
            nr = ar * hr - ai * hi + hs_ref[rows, 0:n]
            ni = ar * hi + ai * hr + hs_ref[rows, n:]
            hs_ref[rows, 0:n] = nr
            hs_ref[rows, n:] = ni
            hr, hi = nr, ni
        return hr, hi

    hr, hi = lax.fori_loop(0, S5_TIME_CHUNK // S5_UNROLL, body,
                           (carry_ref[:, 0:n], carry_ref[:, n:]))
    carry_ref[:, 0:n] = hr
    carry_ref[:, n:] = hi
    for r in (0, half):
        o_ref[0, r:r + half, :] = jnp.dot(hs_ref[r:r + half, :].astype(BF16), cd_ref[0],
                                          preferred_element_type=F32)


def _s5_scan(us_flat, a, bd, cd, batch, n_ctx):
    rows_total = us_flat.shape[0]
    ta = rows_total // batch
    tc = S5_TIME_CHUNK
    n_chunks = ta // tc
    ctx_chunks = n_ctx // tc
    blk = tc * batch

    def chunk_of(d, j):
        bwd = jnp.where(j < ctx_chunks, ctx_chunks - 1 - j, n_chunks - 1 - (j - ctx_chunks))
        return jnp.where(d == 0, j, bwd)

    return pl.pallas_call(
        functools.partial(_s5_scan_kernel, batch=batch),
        grid=(2, n_chunks),
        in_specs=[pl.BlockSpec((blk, S5_W), lambda d, j: (chunk_of(d, j), 0)),
                  pl.BlockSpec((1, 8, 2 * S5_NSTATE), lambda d, j: (d, 0, 0)),
                  pl.BlockSpec((1, S5_W, 2 * S5_NSTATE), lambda d, j: (d, 0, 0)),
                  pl.BlockSpec((1, 2 * S5_NSTATE, S5_W), lambda d, j: (d, 0, 0))],
        out_specs=pl.BlockSpec((1, blk, S5_W), lambda d, j: (d, chunk_of(d, j), 0)),
        out_shape=jax.ShapeDtypeStruct((2, rows_total, S5_W), F32),
        scratch_shapes=[pltpu.VMEM((blk, 2 * S5_NSTATE), F32),
                        pltpu.VMEM((batch, 2 * S5_NSTATE), F32)],
        compiler_params=_cparams("arbitrary", "arbitrary"),
        name="s5_scan",
    )(us_flat, a, bd, cd)


def _merge_ln_kernel(gs_ref, y5_ref, us_ref, h_ref, gate_ref, dd_ref, gw_ref, gb_ref,
                     w_ref, g_ref, beta_ref, o_ref):
    y5 = y5_ref[0] + y5_ref[1] + dd_ref[...] * us_ref[...]
    ge = jax.nn.gelu(y5)
    s5 = ge * jax.nn.sigmoid(
        jnp.dot(ge.astype(BF16), gw_ref[...], preferred_element_type=F32) + gb_ref[...])
    y = jnp.dot(gs_ref[0], w_ref[0:SSD_W, :], preferred_element_type=F32)
    y = y + jnp.dot(s5.astype(BF16), w_ref[SSD_W:, :], preferred_element_type=F32)
    o_ref[0] = _layer_norm(ALPHA * h_ref[0] + gate_ref[0] * y, g_ref[...], beta_ref[...])


def _merge_ln(g_ssd, y5, us_t, h, mods, layer, s5_d, glu_w, glu_b, w_out, ln_g, ln_b, ctx_tiles):
    batch, ta, _ = h.shape
    nt = ta // ROW_TILE - ctx_tiles
    y5v = y5.reshape(2, ta, batch * S5_W)
    return pl.pallas_call(
        _merge_ln_kernel,
        grid=(batch, nt),
        in_specs=[pl.BlockSpec((1, ROW_TILE, SSD_W), lambda b, t: (b, t, 0)),
                  pl.BlockSpec((2, ROW_TILE, S5_W), lambda b, t: (0, t + ctx_tiles, b)),
                  pl.BlockSpec((ROW_TILE, S5_W), lambda b, t: (t + ctx_tiles, b)),
                  pl.BlockSpec((1, ROW_TILE, D_MODEL), lambda b, t: (b, t + ctx_tiles, 0)),
                  pl.BlockSpec((1, 1, D_MODEL), lambda b, t: ((layer * MOD_ROWS + b) * 6 + 2, 0, 0)),
                  _full((1, S5_W)), _full((S5_W, S5_W)), _full((1, S5_W)),
                  _full((D_MODEL, D_MODEL)), _full((1, D_MODEL)), _full((1, D_MODEL))],
        out_specs=pl.BlockSpec((1, ROW_TILE, D_MODEL), lambda b, t: (b, t, 0)),
        out_shape=jax.ShapeDtypeStruct((batch, nt * ROW_TILE, D_MODEL), F32),
        compiler_params=_cparams("parallel", "arbitrary"),
        name="merge_outproj_ln",
    )(g_ssd, y5v, us_t, h, mods, s5_d.reshape(1, S5_W), glu_w.astype(BF16),
      glu_b.reshape(1, S5_W), w_out, ln_g.reshape(1, -1), ln_b.reshape(1, -1))


def _attn_layer(h_ctx, h_lat, mods, layer, i, p, keep_ctx):
    n_ctx = h_ctx.shape[1]
    ctx_tiles = n_ctx // ROW_TILE
    lam_init = 0.8 - 0.6 * math.exp(-0.3 * layer)
    cos, sin = _rope_tables(h_lat.shape[1], n_ctx)
    q, k, v, f = _inproj_attn(h_ctx, h_lat, mods, layer, p['attn_w_in'][i].astype(BF16),
                              cos, sin, ctx_tiles)
    o_ctx, o_lat = _attention(q, k, v, p['attn_lambda'][i], p['attn_subln_g'][i], lam_init,
                              ctx_tiles)
    fm = _fourier(f, p['fourier_w'][i], p['fourier_b'][i], n_ctx)
    h1 = _outproj_ln(o_ctx, o_lat, fm, h_ctx, h_lat, mods, layer,
                     p['attn_w_out'][i].astype(BF16),
                     p['ln_g'][layer, 0], p['ln_b'][layer, 0], ctx_tiles)
    return _ffn(h1, mods, layer, p['ffn_w_up'][layer].astype(BF16), p['ffn_b_up'][layer],
                p['ffn_conv_w'][layer], p['ffn_conv_b'][layer],
                p['ffn_w_down'][layer].astype(BF16), p['ffn_b_down'][layer],
                p['ln_g'][layer, 1], p['ln_b'][layer, 1], ctx_tiles, 0 if keep_ctx else ctx_tiles)


def _ssm_layer(h, mods, layer, i, n_ctx, p, keep_ctx):
    assert not keep_ctx, "an SSM layer that must also emit context rows is not implemented"
    ctx_tiles = n_ctx // ROW_TILE
    batch = h.shape[0]
    w = p['ssm_w_in'][i]
    w_pad = jnp.concatenate(
        [w[:, :SSD_W + XBC_W], w[:, SSD_W + XBC_W + DT_W:], w[:, SSD_W + XBC_W:SSD_W + XBC_W + DT_W],
         jnp.zeros((D_MODEL, DT_PAD - DT_W), F32)], axis=1).astype(BF16)
    z, xbc, dt, us_t = _inproj_ssm(h, mods, layer, w_pad, ctx_tiles)
    g_ssd = _ssd(xbc, dt, z, p['ssd_conv_w'][i], p['ssd_conv_b'][i], p['ssd_a_log'][i],
                 p['ssd_dt_bias'][i], p['ssd_d'][i], p['ssd_norm_g'][i], n_ctx)
    a, bd, cd = _s5_discretize(p['s5_lambda_re'][i], p['s5_lambda_im'][i], p['s5_log_dt'][i],
                               p['s5_b_re'][i], p['s5_b_im'][i], p['s5_c_re'][i], p['s5_c_im'][i])
    y5 = _s5_scan(us_t.reshape(-1, S5_W), a, bd, cd, batch, n_ctx)
    h1 = _merge_ln(g_ssd, y5, us_t, h, mods, layer, p['s5_d'][i], p['s5_glu_w'][i],
                   p['s5_glu_b'][i], p['ssm_w_out'][i].astype(BF16),
                   p['ln_g'][layer, 0], p['ln_b'][layer, 0], ctx_tiles)
    return _ffn(h1, mods, layer, p['ffn_w_up'][layer].astype(BF16), p['ffn_b_up'][layer],
                p['ffn_conv_w'][layer], p['ffn_conv_b'][layer],
                p['ffn_w_down'][layer].astype(BF16), p['ffn_b_down'][layer],
                p['ln_g'][layer, 1], p['ln_b'][layer, 1], 0, 0)


def kernel(x, c, ctx, c_ctx, ada_w, ada_b, ln_g, ln_b, ffn_w_up, ffn_b_up, ffn_conv_w, ffn_conv_b, ffn_w_down, ffn_b_down, attn_w_in, attn_lambda, attn_subln_g, fourier_w, fourier_b, attn_w_out, ssm_w_in, ssd_conv_w, ssd_conv_b, ssd_a_log, ssd_dt_bias, ssd_d, ssd_norm_g, s5_lambda_re, s5_lambda_im, s5_log_dt, s5_b_re, s5_b_im, s5_c_re, s5_c_im, s5_d, s5_glu_w, s5_glu_b, ssm_w_out):
    p = dict(ln_g=ln_g, ln_b=ln_b, ffn_w_up=ffn_w_up, ffn_b_up=ffn_b_up, ffn_conv_w=ffn_conv_w,
             ffn_conv_b=ffn_conv_b, ffn_w_down=ffn_w_down, ffn_b_down=ffn_b_down,
             attn_w_in=attn_w_in, attn_lambda=attn_lambda, attn_subln_g=attn_subln_g,
             fourier_w=fourier_w, fourier_b=fourier_b, attn_w_out=attn_w_out, ssm_w_in=ssm_w_in,
             ssd_conv_w=ssd_conv_w, ssd_conv_b=ssd_conv_b, ssd_a_log=ssd_a_log,
             ssd_dt_bias=ssd_dt_bias, ssd_d=ssd_d, ssd_norm_g=ssd_norm_g,
             s5_lambda_re=s5_lambda_re, s5_lambda_im=s5_lambda_im, s5_log_dt=s5_log_dt,
             s5_b_re=s5_b_re, s5_b_im=s5_b_im, s5_c_re=s5_c_re, s5_c_im=s5_c_im, s5_d=s5_d,
             s5_glu_w=s5_glu_w, s5_glu_b=s5_glu_b, ssm_w_out=ssm_w_out)
    batch, n_lat, _ = x.shape
    n_ctx = ctx.shape[1]
    assert n_ctx == ROW_TILE and n_lat % ROW_TILE == 0 and batch < MOD_ROWS
    mods = _ada_mods(c, c_ctx, ada_w, ada_b)
    assert DEPTH == 2
    h = _attn_layer(ctx, x, mods, 0, 0, p, keep_ctx=True)
    return _ssm_layer(h, mods, 1, 0, n_ctx, p, keep_ctx=False)
```

```python
import functools
import math

import numpy as np
import jax
import jax.numpy as jnp
from jax import lax
from jax.experimental import pallas as pl
from jax.experimental.pallas import tpu as pltpu

F32 = jnp.float32
BF16 = jnp.bfloat16

D_MODEL = 1024
DEPTH = 2
GRID_W = 64
ROPE_BASE = 10000.0
LN_EPS = 1e-5
ALPHA = (2 * DEPTH) ** 0.25
ATTN_W = 768
ATTN_HEADS = 6
ATTN_HEAD_DIM = 64
F_W = 256
F_GROUPS = 4
F_GROUP_W = 64
ATTN_IN_W = 2 * ATTN_W + ATTN_W + F_W
SSD_W = 768
SSD_HEADS = 12
SSD_HEAD_DIM = 64
SSD_GROUPS = 2
SSD_STATE = 128
SSD_CHUNK = 128
XBC_W = SSD_W + 2 * SSD_GROUPS * SSD_STATE
DT_W = 2 * SSD_HEADS
DT_PAD = 128
S5_W = 256
S5_GROUPS = 16
S5_GROUP_W = 16
S5_STATE = 64
S5_NSTATE = S5_GROUPS * S5_STATE
SSM_IN_PAD = SSD_W + XBC_W + S5_W + DT_PAD
D_FF = 2816
FF_TILE = 256

ROW_TILE = 256
ATTN_Q_SUB = 128
ATTN_ITEM_TILES = 2
HALO = 8
MOD_ROWS = 16
VMEM_LIMIT_BYTES = 56 * 1024 * 1024


def _cparams(*sem):
    return pltpu.CompilerParams(dimension_semantics=sem, vmem_limit_bytes=VMEM_LIMIT_BYTES)


def _silu(v):
    return v * jax.nn.sigmoid(v)


def _layer_norm(v, g, b):
    mu = jnp.mean(v, axis=-1, keepdims=True)
    d = v - mu
    var = jnp.mean(d * d, axis=-1, keepdims=True)
    return d * lax.rsqrt(var + LN_EPS) * g + b


def _full(shape):
    nd = len(shape)
    return pl.BlockSpec(shape, lambda *_: (0,) * nd)


def _mod_spec(layer, j, batch, ctx_tiles):
    def idx(b, t):
        row = jnp.where(t < ctx_tiles, batch, b)
        return ((layer * MOD_ROWS + row) * 6 + j, 0, 0)
    return pl.BlockSpec((1, 1, D_MODEL), idx)


def _ada_kernel(c_ref, w_ref, b_ref, o_ref):
    s = _silu(c_ref[...])
    o_ref[0] = jnp.dot(s, w_ref[0], preferred_element_type=F32,
                       precision=lax.Precision.HIGHEST) + b_ref[0]


def _ada_mods(c, c_ctx, ada_w, ada_b):
    batch = c.shape[0]
    nl = ada_w.shape[0]
    c_all = jnp.concatenate(
        [c, c_ctx[None], jnp.zeros((MOD_ROWS - batch - 1, D_MODEL), F32)], axis=0)
    out = pl.pallas_call(
        _ada_kernel,
        grid=(nl, 6),
        in_specs=[_full((MOD_ROWS, D_MODEL)),
                  pl.BlockSpec((1, D_MODEL, D_MODEL), lambda l, j: (l, 0, j)),
                  pl.BlockSpec((1, 1, D_MODEL), lambda l, j: (l, 0, j))],
        out_specs=pl.BlockSpec((1, MOD_ROWS, D_MODEL), lambda l, j: (l, 0, j)),
        out_shape=jax.ShapeDtypeStruct((nl, MOD_ROWS, 6 * D_MODEL), F32),
        compiler_params=_cparams("arbitrary", "arbitrary"),
        name="ada_mods",
    )(c_all, ada_w, ada_b.reshape(nl, 1, 6 * D_MODEL))
    return out.reshape(nl * MOD_ROWS * 6, 1, D_MODEL)


def _inproj_attn_kernel(hc_ref, hl_ref, shift_ref, scale_ref, w_ref, cos_ref, sin_ref,
                        q_ref, k_ref, v_ref, f_ref, *, ctx_tiles):
    h = jnp.where(pl.program_id(1) < ctx_tiles, hc_ref[0], hl_ref[0])
    u = (h * (1.0 + scale_ref[0]) + shift_ref[0]).astype(BF16)
    p = jnp.dot(u, w_ref[...], preferred_element_type=F32)
    cos = cos_ref[...]
    sin = sin_ref[...]
    lane = lax.broadcasted_iota(jnp.int32, cos.shape, 1)
    first_half = (lane % ATTN_HEAD_DIM) < (ATTN_HEAD_DIM // 2)

    def rope(blk):
        partner = jnp.where(first_half, pltpu.roll(blk, 128 - 32, 1), pltpu.roll(blk, 32, 1))
        return blk * cos + partner * sin

    qk_scale = ATTN_HEAD_DIM ** -0.5 * math.log2(math.e)
    for i in range(ATTN_HEADS):
        lo, hi = i * 128, (i + 1) * 128
        q_ref[0, :, lo:hi] = (rope(p[:, lo:hi]) * qk_scale).astype(BF16)
        k_ref[0, :, lo:hi] = rope(p[:, ATTN_W + lo:ATTN_W + hi]).astype(BF16)
    v_ref[0] = p[:, 2 * ATTN_W:3 * ATTN_W].astype(BF16)
    f_ref[0] = p[:, 3 * ATTN_W:].astype(BF16)


def _rope_tables(n_lat, n_ctx):
    rows = n_lat // GRID_W
    row = jnp.repeat(jnp.arange(rows, dtype=F32), GRID_W)
    col = jnp.tile(jnp.arange(GRID_W, dtype=F32), rows)
    n_freq = ATTN_HEAD_DIM // 4
    inv_freq = ROPE_BASE ** (-jnp.arange(n_freq, dtype=F32) / n_freq)
    ang = jnp.concatenate([row[:, None] * inv_freq, col[:, None] * inv_freq], axis=-1)
    cos, sin = jnp.cos(ang), jnp.sin(ang)
    cos128 = jnp.tile(cos, (1, 4))
    sin128 = jnp.tile(jnp.concatenate([-sin, sin], axis=-1), (1, 2))
    cos_all = jnp.concatenate([jnp.ones((n_ctx, 128), F32), cos128], axis=0)
    sin_all = jnp.concatenate([jnp.zeros((n_ctx, 128), F32), sin128], axis=0)
    return cos_all, sin_all


def _split_specs(width, ctx_tiles):
    return [pl.BlockSpec((1, ROW_TILE, width), lambda b, t: (b, jnp.minimum(t, ctx_tiles - 1), 0)),
            pl.BlockSpec((1, ROW_TILE, width), lambda b, t: (b, jnp.maximum(t - ctx_tiles, 0), 0))]


def _inproj_attn(h_ctx, h_lat, mods, layer, w_in, cos, sin, ctx_tiles):
    batch = h_lat.shape[0]
    ta = h_ctx.shape[1] + h_lat.shape[1]
    nt = ta // ROW_TILE
    row = lambda w: pl.BlockSpec((1, ROW_TILE, w), lambda b, t: (b, t, 0))
    tab = pl.BlockSpec((ROW_TILE, 128), lambda b, t: (t, 0))
    return pl.pallas_call(
        functools.partial(_inproj_attn_kernel, ctx_tiles=ctx_tiles),
        grid=(batch, nt),
        in_specs=_split_specs(D_MODEL, ctx_tiles)
        + [_mod_spec(layer, 0, batch, ctx_tiles), _mod_spec(layer, 1, batch, ctx_tiles),
           _full((D_MODEL, ATTN_IN_W)), tab, tab],
        out_specs=[row(ATTN_W), row(ATTN_W), row(ATTN_W), row(F_W)],
        out_shape=[jax.ShapeDtypeStruct((batch, ta, ATTN_W), BF16)] * 3
        + [jax.ShapeDtypeStruct((batch, ta, F_W), BF16)],
        compiler_params=_cparams("parallel", "arbitrary"),
        name="inproj_attn",
    )(h_ctx, h_lat, mods, mods, w_in, cos, sin)


def _diff_lambda(lam_ref, lam_init):
    lamv = lam_ref[...]
    l1 = jnp.sum(lamv[0:1] * lamv[1:2], axis=-1, keepdims=True)
    l2 = jnp.sum(lamv[2:3] * lamv[3:4], axis=-1, keepdims=True)
    return jnp.exp(l1) - jnp.exp(l2) + lam_init


def _stack_maps(q):
    lane = lax.broadcasted_iota(jnp.int32, q.shape, 1)
    zero = jnp.zeros_like(q)
    return jnp.concatenate([jnp.where(lane < ATTN_HEAD_DIM, q, zero),
                            jnp.where(lane >= ATTN_HEAD_DIM, q, zero)], axis=0)


def _scores(q2, k):
    return lax.dot_general(q2, k, (((1,), (1,)), ((), ())), preferred_element_type=F32)


def _diff_softmax_pv(load_s0, m0, load_s1, m1, lam, v, gain):
    e0 = jnp.exp2(load_s0() - m0)
    e1 = jnp.exp2(load_s1() - m1)
    c0 = 1.0 / jnp.sum(e0, axis=-1, keepdims=True)
    c1 = lam / jnp.sum(e1, axis=-1, keepdims=True)
    w = e0 * c0 - e1 * c1
    o = jnp.dot(w.astype(BF16), v, preferred_element_type=F32)
    o = o * lax.rsqrt(jnp.mean(o * o, axis=-1, keepdims=True) + LN_EPS)
    return (o * gain).astype(BF16)


def _attn_ctx_kernel(lam_ref, g_ref, q_ref, k_ref, v_ref, o_ref, *, lam_init):
    lam = _diff_lambda(lam_ref, lam_init)
    gain = g_ref[...] * (1.0 - lam_init)
    tq = q_ref.shape[1]
    for hd in range(ATTN_HEADS):
        cols = slice(hd * 128, (hd + 1) * 128)
        s = _scores(_stack_maps(q_ref[0, :, cols]), k_ref[0, :, cols])
        m = jnp.max(s, axis=-1, keepdims=True)

        def half(lo, s=s):
            return lambda: s[lo:lo + tq]

        o_ref[0, :, cols] = _diff_softmax_pv(half(0), m[:tq], half(tq), m[tq:], lam,
                                             v_ref[0, :, cols], gain)


def _attn_lat_kernel(lam_ref, g_ref, k_ref, v_ref, *rest, lam_init):
    q_refs = rest[:ATTN_ITEM_TILES]
    o_ref, sa_ref, ma_ref, sb_ref, mb_ref = rest[ATTN_ITEM_TILES:]
    t = pl.program_id(0)
    tq = ATTN_ITEM_TILES * ROW_TILE
    sub = ATTN_Q_SUB

    @pl.when(t == 0)
    def _():
        sb_ref[...] = jnp.zeros_like(sb_ref)
        mb_ref[...] = jnp.zeros_like(mb_ref)

    def step(s_new, m_new, s_old, m_old):
        lam = _diff_lambda(lam_ref, lam_init)
        gain = g_ref[...] * (1.0 - lam_init)
        q = jnp.concatenate([q_ref[0] for q_ref in q_refs], axis=0)
        s = _scores(_stack_maps(q), k_ref[0])
        s_new[...] = s
        m_new[...] = jnp.max(s, axis=-1, keepdims=True)
        v = v_ref[0]
        for i in range(tq // sub):
            r0 = slice(i * sub, (i + 1) * sub)
            r1 = slice(tq + i * sub, tq + (i + 1) * sub)
            o_ref[0, r0, :] = _diff_softmax_pv(
                functools.partial(s_old.__getitem__, (r0, slice(None))), m_old[r0, :],
                functools.partial(s_old.__getitem__, (r1, slice(None))), m_old[r1, :],
                lam, v, gain)

    @pl.when(t % 2 == 0)
    def _():
        step(sa_ref, ma_ref, sb_ref, mb_ref)

    @pl.when(t % 2 == 1)
    def _():
        step(sb_ref, mb_ref, sa_ref, ma_ref)


def _attention(q, k, v, lam_vec, subln_g, lam_init, ctx_tiles):
    batch, ta, _ = q.shape
    n_ctx = ctx_tiles * ROW_TILE
    nt = ta // ROW_TILE - ctx_tiles
    g = subln_g.reshape(1, 128)
    small = [_full((4, ATTN_HEAD_DIM)), _full((1, 128))]
    cspec = pl.BlockSpec((1, n_ctx, ATTN_W), lambda b: (b, 0, 0))
    o_ctx = pl.pallas_call(
        functools.partial(_attn_ctx_kernel, lam_init=lam_init),
        grid=(batch,),
        in_specs=small + [cspec, cspec, cspec],
        out_specs=cspec,
        out_shape=jax.ShapeDtypeStruct((batch, n_ctx, ATTN_W), BF16),
        compiler_params=_cparams("arbitrary"),
        name="diff_attention_ctx",
    )(lam_vec, g, q, k, v)

    per = ATTN_ITEM_TILES
    assert nt % per == 0
    ni = nt // per
    n_items = batch * ATTN_HEADS * ni

    def item(j):
        return j // (ATTN_HEADS * ni), (j // ni) % ATTN_HEADS, j % ni

    def score_item(j):
        return item(jnp.minimum(j, n_items - 1))

    def finish_item(j):
        return item(jnp.maximum(j - 1, 0))

    def q_spec(part):
        def idx(j):
            b, h, t = score_item(j)
            return b, ctx_tiles + t * per + part, h
        return pl.BlockSpec((1, ROW_TILE, 128), idx)

    def k_idx(j):
        b, h, _ = score_item(j)
        return b, 0, h

    def v_idx(j):
        b, h, _ = finish_item(j)
        return b, 0, h

    def o_idx(j):
        b, h, t = finish_item(j)
        return b, t, h

    o_lat = pl.pallas_call(
        functools.partial(_attn_lat_kernel, lam_init=lam_init),
        grid=(n_items + 1,),
        in_specs=small + [pl.BlockSpec((1, ta, 128), k_idx), pl.BlockSpec((1, ta, 128), v_idx)]
        + [q_spec(part) for part in range(per)],
        out_specs=pl.BlockSpec((1, per * ROW_TILE, 128), o_idx),
        out_shape=jax.ShapeDtypeStruct((batch, nt * ROW_TILE, ATTN_W), BF16),
        scratch_shapes=[pltpu.VMEM((2 * per * ROW_TILE, ta), F32),
                        pltpu.VMEM((2 * per * ROW_TILE, 1), F32)] * 2,
        compiler_params=_cparams("arbitrary"),
        name="diff_attention",
    )(lam_vec, g, k, v, *([q] * per))
    return o_ctx, o_lat


def _dft_tables(n):
    k = np.arange(n, dtype=np.int64)
    ang = 2.0 * np.pi * ((k[:, None] * k[None, :]) % n).astype(np.float64) / n
    return np.cos(ang), np.sin(ang)


def _fourier_kernel(f_ref, cs_ref, dl_ref, dc_ref, w_ref, b_ref, o_ref, a_ref, *, n_ctx, n_lat):
    t = pl.program_id(1)
    ctx_tiles = n_ctx // ROW_TILE

    def stage1(rows0, n):
        a = jnp.dot(f_ref[0, rows0:rows0 + n, :], cs_ref[...], preferred_element_type=F32)
        a_ref[0:n, :] = a[:, :F_W].astype(BF16)
        a_ref[n:2 * n, :] = a[:, F_W:].astype(BF16)

    def stage2(dft, n):
        z = jnp.dot(dft, a_ref[0:2 * n, :], preferred_element_type=F32)
        z = z * (1.0 / math.sqrt(n * F_GROUP_W))
        o = jnp.dot(z.astype(BF16), w_ref[...], preferred_element_type=F32) + b_ref[...]
        o_ref[0] = o.astype(BF16)

    @pl.when(t < ctx_tiles)
    def _():
        stage1(0, n_ctx)
        stage2(dc_ref[...], n_ctx)

    @pl.when(t == ctx_tiles)
    def _():
        stage1(n_ctx, n_lat)

    @pl.when(t >= ctx_tiles)
    def _():
        stage2(dl_ref[...], n_lat)


def _fourier(f, fourier_w, fourier_b, n_ctx):
    batch, ta, _ = f.shape
    n_lat = ta - n_ctx
    assert n_ctx == ROW_TILE
    nt = ta // ROW_TILE
    cc, sc = _dft_tables(F_GROUP_W)
    eye = np.eye(F_GROUPS)
    cs = jnp.asarray(np.concatenate([np.kron(eye, cc), np.kron(eye, sc)], axis=1), BF16)
    cl, sl = _dft_tables(n_lat)
    dft_lat = jnp.asarray(np.concatenate([cl, -sl], axis=1), BF16)
    cx, sx = _dft_tables(n_ctx)
    dft_ctx = jnp.asarray(np.concatenate([cx, -sx], axis=1), BF16)
    w_blk = jnp.einsum('gce,gh->gche', fourier_w, jnp.eye(F_GROUPS, dtype=F32))
    w_blk = w_blk.reshape(F_W, F_W).astype(BF16)
    return pl.pallas_call(
        functools.partial(_fourier_kernel, n_ctx=n_ctx, n_lat=n_lat),
        grid=(batch, nt),
        in_specs=[pl.BlockSpec((1, ta, F_W), lambda b, t: (b, 0, 0)),
                  _full((F_W, 2 * F_W)),
                  pl.BlockSpec((ROW_TILE, 2 * n_lat), lambda b, t: (jnp.maximum(t - 1, 0), 0)),
                  _full((n_ctx, 2 * n_ctx)),
                  _full((F_W, F_W)), _full((1, F_W))],
        out_specs=pl.BlockSpec((1, ROW_TILE, F_W), lambda b, t: (b, t, 0)),
        out_shape=jax.ShapeDtypeStruct((batch, ta, F_W), BF16),
        scratch_shapes=[pltpu.VMEM((2 * n_lat, F_W), BF16)],
        compiler_params=_cparams("parallel", "arbitrary"),
        name="fourier_mix",
    )(f, cs, dft_lat, dft_ctx, w_blk, fourier_b.reshape(1, F_W))


def _outproj_ln_kernel(ac_ref, al_ref, b_ref, hc_ref, hl_ref, gate_ref, w_ref, g_ref, beta_ref,
                       o_ref, *, ctx_tiles):
    is_ctx = pl.program_id(1) < ctx_tiles
    a = jnp.where(is_ctx, ac_ref[0], al_ref[0])
    h = jnp.where(is_ctx, hc_ref[0], hl_ref[0])
    wa = a.shape[1]
    y = jnp.dot(a, w_ref[0:wa, :], preferred_element_type=F32)
    y = y + jnp.dot(b_ref[0], w_ref[wa:, :], preferred_element_type=F32)
    o_ref[0] = _layer_norm(ALPHA * h + gate_ref[0] * y, g_ref[...], beta_ref[...])


def _outproj_ln(a_ctx, a_lat, b2, h_ctx, h_lat, mods, layer, w_out, ln_g, ln_b, ctx_tiles):
    batch, ta, _ = b2.shape
    nt = ta // ROW_TILE
    row = lambda w: pl.BlockSpec((1, ROW_TILE, w), lambda b, t: (b, t, 0))
    return pl.pallas_call(
        functools.partial(_outproj_ln_kernel, ctx_tiles=ctx_tiles),
        grid=(batch, nt),
        in_specs=_split_specs(a_lat.shape[2], ctx_tiles) + [row(b2.shape[2])]
        + _split_specs(D_MODEL, ctx_tiles)
        + [_mod_spec(layer, 2, batch, ctx_tiles),
           _full((D_MODEL, D_MODEL)), _full((1, D_MODEL)), _full((1, D_MODEL))],
        out_specs=row(D_MODEL),
        out_shape=jax.ShapeDtypeStruct((batch, ta, D_MODEL), F32),
        compiler_params=_cparams("parallel", "arbitrary"),
        name="outproj_ln",
    )(a_ctx, a_lat, b2, h_ctx, h_lat, mods, w_out, ln_g.reshape(1, D_MODEL),
      ln_b.reshape(1, D_MODEL))


def _ffn_kernel(h_ref, hp_ref, hn_ref, shift_ref, scale_ref, gate_ref,
                wup_ref, bup_ref, cw_ref, cb_ref, wdn_ref, bdn_ref, g_ref, beta_ref,
                o_ref, uext_ref, ubf_ref, act_ref, *, ctx_tiles, tile_off, nt_seq):
    t = pl.program_id(1) + tile_off
    seg_first = (t == 0) | (t == ctx_tiles)
    seg_last = (t == nt_seq - 1) | (t == ctx_tiles - 1)
    sc = 1.0 + scale_ref[0]
    sh = shift_ref[0]
    h = h_ref[0]
    tm = h.shape[0]
    uext_ref[0:HALO, :] = hp_ref[0] * sc + sh
    uext_ref[HALO:HALO + tm, :] = h * sc + sh
    uext_ref[HALO + tm:, :] = hn_ref[0] * sc + sh
    ubf_ref[...] = uext_ref[...].astype(BF16)
    row8 = lax.broadcasted_iota(jnp.int32, (8, FF_TILE), 0)

    def hidden(col0):
        cols = slice(col0, col0 + FF_TILE)
        zr = jnp.dot(ubf_ref[...], wup_ref[:, cols], preferred_element_type=F32)
        bup = bup_ref[:, cols]
        cw = cw_ref[:, cols]
        bias = cb_ref[:, cols] + (cw[0:1] + cw[1:2] + cw[2:3]) * bup
        z0 = zr[HALO:HALO + tm]
        prev = jnp.where(seg_first, -bup, zr[HALO - 1:HALO])
        nxt = jnp.where(seg_last, -bup, zr[HALO + tm:HALO + tm + 1])
        down = pltpu.roll(z0, 1, 0)
        up = pltpu.roll(z0, tm - 1, 0)
        zm1 = jnp.concatenate([jnp.where(row8 == 0, prev, down[0:8]), down[8:]], axis=0)
        zp1 = jnp.concatenate([up[:tm - 8], jnp.where(row8 == 7, nxt, up[tm - 8:])], axis=0)
        return cw[0:1] * zm1 + cw[1:2] * z0 + cw[2:3] * zp1 + bias

    for j in range(D_FF // FF_TILE):
        val = hidden(j * FF_TILE)
        gat = hidden(D_FF + j * FF_TILE)
        act_ref[:, j * FF_TILE:(j + 1) * FF_TILE] = (val * _silu(gat)).astype(BF16)
    f = jnp.dot(act_ref[...], wdn_ref[...], preferred_element_type=F32) + bdn_ref[...]
    o_ref[0] = _layer_norm(ALPHA * h + gate_ref[0] * f, g_ref[...], beta_ref[...])


def _ffn(h, mods, layer, w_up, b_up, conv_w, conv_b, w_down, b_down, ln_g, ln_b,
         ctx_tiles, tile_off):
    batch, ta, _ = h.shape
    nt_seq = ta // ROW_TILE
    nt = nt_seq - tile_off
    hb = ROW_TILE // HALO
    n_hblk = ta // HALO
    mspec = lambda j: pl.BlockSpec(
        (1, 1, D_MODEL),
        lambda b, t: ((layer * MOD_ROWS + jnp.where(t + tile_off < ctx_tiles, batch, b)) * 6 + j, 0, 0))
    return pl.pallas_call(
        functools.partial(_ffn_kernel, ctx_tiles=ctx_tiles, tile_off=tile_off, nt_seq=nt_seq),
        grid=(batch, nt),
        in_specs=[pl.BlockSpec((1, ROW_TILE, D_MODEL), lambda b, t: (b, t + tile_off, 0)),
                  pl.BlockSpec((1, HALO, D_MODEL),
                               lambda b, t: (b, jnp.maximum((t + tile_off) * hb - 1, 0), 0)),
                  pl.BlockSpec((1, HALO, D_MODEL),
                               lambda b, t: (b, jnp.minimum((t + tile_off + 1) * hb, n_hblk - 1), 0)),
                  mspec(3), mspec(4), mspec(5),
                  _full((D_MODEL, 2 * D_FF)), _full((1, 2 * D_FF)),
                  _full((3, 2 * D_FF)), _full((1, 2 * D_FF)),
                  _full((D_FF, D_MODEL)), _full((1, D_MODEL)),
                  _full((1, D_MODEL)), _full((1, D_MODEL))],
        out_specs=pl.BlockSpec((1, ROW_TILE, D_MODEL), lambda b, t: (b, t, 0)),
        out_shape=jax.ShapeDtypeStruct((batch, nt * ROW_TILE, D_MODEL), F32),
        scratch_shapes=[pltpu.VMEM((ROW_TILE + 2 * HALO, D_MODEL), F32),
                        pltpu.VMEM((ROW_TILE + 2 * HALO, D_MODEL), BF16),
                        pltpu.VMEM((ROW_TILE, D_FF), BF16)],
        compiler_params=_cparams("parallel", "arbitrary"),
        name="conv_ffn_ln",
    )(h, h, h, mods, mods, mods, w_up, b_up.reshape(1, -1), conv_w, conv_b.reshape(1, -1),
      w_down, b_down.reshape(1, -1), ln_g.reshape(1, -1), ln_b.reshape(1, -1))


def _time_major_rows(b, batch):
    return pl.ds(b, ROW_TILE, stride=batch)


def _inproj_ssm_kernel(h_ref, shift_ref, scale_ref, w_ref, z_ref, xbc_ref, dt_ref, us_ref,
                       *, batch):
    u = (h_ref[0] * (1.0 + scale_ref[0]) + shift_ref[0]).astype(BF16)
    p = jnp.dot(u, w_ref[...], preferred_element_type=F32)
    z_ref[0] = p[:, :SSD_W]
    xbc_ref[0] = p[:, SSD_W:SSD_W + XBC_W]
    dt_ref[0] = p[:, SSD_W + XBC_W + S5_W:]
    rows = _time_major_rows(pl.program_id(1), batch)
    for j in range(S5_W // 128):
        col0 = SSD_W + XBC_W + j * 128
        us_ref[j, rows, :] = p[:, col0:col0 + 128]


def _inproj_ssm(h, mods, layer, w_in_pad, ctx_tiles):
    batch, ta, _ = h.shape
    nt = ta // ROW_TILE
    row = lambda w: pl.BlockSpec((1, ROW_TILE, w), lambda t, b: (b, t, 0))
    mspec = lambda j: pl.BlockSpec(
        (1, 1, D_MODEL),
        lambda t, b: ((layer * MOD_ROWS + jnp.where(t < ctx_tiles, batch, b)) * 6 + j, 0, 0))
    return pl.pallas_call(
        functools.partial(_inproj_ssm_kernel, batch=batch),
        grid=(nt, batch),
        in_specs=[row(D_MODEL), mspec(0), mspec(1), _full((D_MODEL, SSM_IN_PAD))],
        out_specs=[row(SSD_W), row(XBC_W), row(DT_PAD),
                   pl.BlockSpec((S5_W // 128, ROW_TILE * batch, 128), lambda t, b: (0, t, 0))],
        out_shape=[jax.ShapeDtypeStruct((batch, ta, SSD_W), F32),
                   jax.ShapeDtypeStruct((batch, ta, XBC_W), F32),
                   jax.ShapeDtypeStruct((batch, ta, DT_PAD), F32),
                   jax.ShapeDtypeStruct((S5_W // 128, ta * batch, 128), F32)],
        compiler_params=_cparams("arbitrary", "arbitrary"),
        name="inproj_ssm",
    )(h, mods, mods, w_in_pad)


def _cumsum_rows(v):
    n = v.shape[0]
    row = lax.broadcasted_iota(jnp.int32, v.shape, 0)
    s = 1
    while s < n:
        v = v + jnp.where(row >= s, pltpu.roll(v, s, 0), 0.0)
        s *= 2
    return v


def _expand_heads(v, e_ref):
    hi = v.astype(BF16)
    lo = (v - hi.astype(F32)).astype(BF16)
    e = e_ref[...]
    return (jnp.dot(hi, e, preferred_element_type=F32)
            + jnp.dot(lo, e, preferred_element_type=F32))


def _ssd_kernel(xbc_ref, xp_ref, xn_ref, dt_ref, z_ref, cw_ref, cb_ref, alog_ref, dtb_ref,
                dsk_ref, ng_ref, ef_ref, eb_ref, o_ref,
                xs_ref, cd_ref, st_ref, dec_ref, y_ref, *, n_chunks, ctx_chunks):
    ph = pl.program_id(1)
    c = pl.program_id(2)
    q = SSD_CHUNK
    gw = SSD_W // SSD_GROUPS
    hpg = SSD_HEADS // SSD_GROUPS

    @pl.when(ph == 0)
    def _phase0():
        seg_first = (c == 0) | (c == ctx_chunks)
        seg_last = (c == ctx_chunks - 1) | (c == n_chunks - 1)
        xr = xbc_ref[0]
        prev = jnp.where(seg_first, 0.0, xp_ref[0, HALO - 1:HALO, :])
        nxt = jnp.where(seg_last, 0.0, xn_ref[0, 0:1, :])
        rowi = lax.broadcasted_iota(jnp.int32, xr.shape, 0)
        xm1 = jnp.where(rowi == 0, prev, pltpu.roll(xr, 1, 0))
        xp1 = jnp.where(rowi == q - 1, nxt, pltpu.roll(xr, q - 1, 0))
        cw = cw_ref[...]
        xs = _silu(cw[0:1] * xm1 + cw[1:2] * xr + cw[2:3] * xp1 + cb_ref[...])
        xs_ref[c] = xs

        raw = dt_ref[0] + dtb_ref[...]
        dtv = jnp.maximum(raw, 0.0) + jnp.log1p(jnp.exp(-jnp.abs(raw)))
        adt = dtv * (-jnp.exp(alog_ref[...]))
        cum = _cumsum_rows(adt)
        tot = cum[q - 1:q, :]
        lane = lax.broadcasted_iota(jnp.int32, cum.shape, 1)
        cc = jnp.where(lane < SSD_HEADS, cum, tot - cum + adt)
        cd_ref[c, 0] = cc
        cd_ref[c, 1] = dtv
        w_end = jnp.exp(tot - cc) * dtv
        dec16 = jnp.broadcast_to(jnp.exp(tot), (16, DT_PAD))
        x = xs[:, :SSD_W]
        for d, e_ref in enumerate((ef_ref, eb_ref)):
            wx = (_expand_heads(w_end, e_ref) * x).astype(BF16)
            for g in range(SSD_GROUPS):
                bmt = xs[:, SSD_W + g * SSD_STATE:SSD_W + (g + 1) * SSD_STATE].T.astype(BF16)
                st_ref[c, d, :, g * gw:(g + 1) * gw] = jnp.dot(
                    bmt, wx[:, g * gw:(g + 1) * gw], preferred_element_type=F32)
            dec_ref[c, d] = _expand_heads(dec16, e_ref)[0:8]

    @pl.when((ph == 1) & (c == 0))
    def _recurrence():
        fwd = list(range(n_chunks))
        bwd = list(range(ctx_chunks - 1, -1, -1)) + list(range(n_chunks - 1, ctx_chunks - 1, -1))
        for d, order in enumerate((fwd, bwd)):
            state = jnp.zeros((SSD_STATE, SSD_W), F32)
            for ci in order:
                contrib = st_ref[ci, d]
                st_ref[ci, d] = state
                state = state * dec_ref[ci, d, 0:1, :] + contrib

    @pl.when((ph == 1) & (c >= ctx_chunks))
    def _phase1():
        xs = xs_ref[c]
        x = xs[:, :SSD_W]
        cc = cd_ref[c, 0]
        dtv = cd_ref[c, 1]
        cct = cc.T
        dtt = dtv.T
        ecc = jnp.exp(cc)
        ef = _expand_heads(ecc, ef_ref)
        eb = _expand_heads(ecc, eb_ref)
        rowi = lax.broadcasted_iota(jnp.int32, (q, q), 0)
        coli = lax.broadcasted_iota(jnp.int32, (q, q), 1)
        lower = coli <= rowi
        upper = coli >= rowi
        lane = lax.broadcasted_iota(jnp.int32, (q, 128), 1)
        neg = jnp.float32(-jnp.inf)
        for g in range(SSD_GROUPS):
            bm = xs[:, SSD_W + g * SSD_STATE:SSD_W + (g + 1) * SSD_STATE].astype(BF16)
            cm = xs[:, SSD_W + (SSD_GROUPS + g) * SSD_STATE:
                    SSD_W + (SSD_GROUPS + g + 1) * SSD_STATE].astype(BF16)
            gmat = lax.dot_general(cm, bm, (((1,), (1,)), ((), ())), preferred_element_type=F32)
            sl = slice(g * gw, (g + 1) * gw)
            yoff = (ef[:, sl] * jnp.dot(cm, st_ref[c, 0, :, sl].astype(BF16),
                                        preferred_element_type=F32)
                    + eb[:, sl] * jnp.dot(cm, st_ref[c, 1, :, sl].astype(BF16),
                                          preferred_element_type=F32))
            for pair in range(hpg // 2):
                col0 = g * gw + pair * 128
                xpair = x[:, col0:col0 + 128].astype(BF16)
                res = []
                for hh in range(2):
                    hd = g * hpg + pair * 2 + hh
                    hb_ = SSD_HEADS + hd
                    lf = jnp.exp(jnp.where(lower, cc[:, hd:hd + 1] - cct[hd:hd + 1, :], neg))
                    lb = jnp.exp(jnp.where(upper, cc[:, hb_:hb_ + 1] - cct[hb_:hb_ + 1, :], neg))
                    mt = gmat * (lf * dtt[hd:hd + 1, :] + lb * dtt[hb_:hb_ + 1, :])
                    res.append(jnp.dot(mt.astype(BF16), xpair, preferred_element_type=F32))
                ydiag = jnp.where(lane < SSD_HEAD_DIM, res[0], res[1])
                y_ref[:, col0:col0 + 128] = (ydiag + yoff[:, pair * 128:(pair + 1) * 128]
                                             + dsk_ref[:, col0:col0 + 128] * x[:, col0:col0 + 128])
        gated = y_ref[...] * _silu(z_ref[0])
        normed = gated * lax.rsqrt(jnp.mean(gated * gated, axis=-1, keepdims=True) + LN_EPS)
        o_ref[0] = (normed * ng_ref[...]).astype(BF16)


def _ssd(xbc, dt, z, conv_w, conv_b, a_log, dt_bias, d_skip, norm_g, n_ctx):
    batch, ta, _ = xbc.shape
    q = SSD_CHUNK
    n_chunks = ta // q
    ctx_chunks = n_ctx // q
    hb = q // HALO
    n_hblk = ta // HALO
    pad24 = lambda v: jnp.pad(v.reshape(1, DT_W), ((0, 0), (0, DT_PAD - DT_W)))
    heads = np.arange(SSD_HEADS)
    ef = np.zeros((DT_PAD, SSD_W), np.float32)
    eb = np.zeros((DT_PAD, SSD_W), np.float32)
    for hd in heads:
        ef[hd, hd * SSD_HEAD_DIM:(hd + 1) * SSD_HEAD_DIM] = 1.0
        eb[SSD_HEADS + hd, hd * SSD_HEAD_DIM:(hd + 1) * SSD_HEAD_DIM] = 1.0
    dsk = jnp.repeat(d_skip.astype(F32), SSD_HEAD_DIM).reshape(1, SSD_W)
    chunk = lambda w: pl.BlockSpec((1, q, w), lambda b, ph, c: (b, jnp.where(ph == 0, c, n_chunks - 1), 0))
    return pl.pallas_call(
        functools.partial(_ssd_kernel, n_chunks=n_chunks, ctx_chunks=ctx_chunks),
        grid=(batch, 2, n_chunks),
        in_specs=[chunk(XBC_W),
                  pl.BlockSpec((1, HALO, XBC_W),
                               lambda b, ph, c: (b, jnp.where(ph == 0, jnp.maximum(c * hb - 1, 0), 0), 0)),
                  pl.BlockSpec((1, HALO, XBC_W),
                               lambda b, ph, c: (b, jnp.where(ph == 0, jnp.minimum((c + 1) * hb, n_hblk - 1), 0), 0)),
                  chunk(DT_PAD),
                  pl.BlockSpec((1, q, SSD_W), lambda b, ph, c: (b, jnp.where(ph == 1, c, 0), 0)),
                  _full((3, XBC_W)), _full((1, XBC_W)), _full((1, DT_PAD)), _full((1, DT_PAD)),
                  _full((1, SSD_W)), _full((1, SSD_W)),
                  _full((DT_PAD, SSD_W)), _full((DT_PAD, SSD_W))],
        out_specs=pl.BlockSpec(
            (1, q, SSD_W),
            lambda b, ph, c: (b, jnp.where(ph == 1, jnp.maximum(c - ctx_chunks, 0), 0), 0)),
        out_shape=jax.ShapeDtypeStruct((batch, ta - n_ctx, SSD_W), BF16),
        scratch_shapes=[pltpu.VMEM((n_chunks, q, XBC_W), F32),
                        pltpu.VMEM((n_chunks, 2, q, DT_PAD), F32),
                        pltpu.VMEM((n_chunks, 2, SSD_STATE, SSD_W), F32),
                        pltpu.VMEM((n_chunks, 2, 8, SSD_W), F32),
                        pltpu.VMEM((q, SSD_W), F32)],
        compiler_params=_cparams("parallel", "arbitrary", "arbitrary"),
        name="ssd_bidir",
    )(xbc, xbc, xbc, dt, z, conv_w, conv_b.reshape(1, XBC_W), pad24(a_log), pad24(dt_bias),
      dsk, norm_g.reshape(1, SSD_W), jnp.asarray(ef, BF16), jnp.asarray(eb, BF16))


def _s5_disc_kernel(lr_ref, li_ref, ldt_ref, bre_ref, bim_ref, cre_ref, cim_ref,
                    a_ref, bd_ref, cd_ref):
    lr, li = lr_ref[...], li_ref[...]
    dt = jnp.exp(ldt_ref[...])
    mag = jnp.exp(dt * lr)
    ab_re, ab_im = mag * jnp.cos(dt * li), mag * jnp.sin(dt * li)
    den = lr * lr + li * li
    k_re = ((ab_re - 1.0) * lr + ab_im * li) / den
    k_im = (ab_im * lr - (ab_re - 1.0) * li) / den
    bre, bim = bre_ref[...], bim_ref[...]
    for d in range(2):
        a_ref[d, :, 0:S5_NSTATE] = jnp.broadcast_to(ab_re[d:d + 1], (8, S5_NSTATE))
        a_ref[d, :, S5_NSTATE:] = jnp.broadcast_to(ab_im[d:d + 1], (8, S5_NSTATE))
        kr, ki = k_re[d:d + 1], k_im[d:d + 1]
        bd_ref[d, :, 0:S5_NSTATE] = (kr * bre - ki * bim).astype(BF16)
        bd_ref[d, :, S5_NSTATE:] = (kr * bim + ki * bre).astype(BF16)
        cd_ref[d, 0:S5_NSTATE, :] = cre_ref[d].astype(BF16)
        cd_ref[d, S5_NSTATE:, :] = (-cim_ref[d]).astype(BF16)


def _s5_discretize(lam_re, lam_im, log_dt, b_re, b_im, c_re, c_im):
    eye = jnp.eye(S5_GROUPS, dtype=F32)
    bd = lambda b: jnp.einsum('gph,gk->ghkp', b, eye).reshape(S5_W, S5_NSTATE)
    cd = lambda cc: jnp.einsum('dghp,gk->dgpkh', cc, eye).reshape(2, S5_NSTATE, S5_W)
    ldt = jnp.repeat(log_dt, S5_STATE, axis=-1)
    return pl.pallas_call(
        _s5_disc_kernel,
        out_shape=[jax.ShapeDtypeStruct((2, 8, 2 * S5_NSTATE), F32),
                   jax.ShapeDtypeStruct((2, S5_W, 2 * S5_NSTATE), BF16),
                   jax.ShapeDtypeStruct((2, 2 * S5_NSTATE, S5_W), BF16)],
        compiler_params=pltpu.CompilerParams(vmem_limit_bytes=VMEM_LIMIT_BYTES),
        name="s5_discretize",
    )(lam_re.reshape(2, S5_NSTATE), lam_im.reshape(2, S5_NSTATE), ldt,
      bd(b_re), bd(b_im), cd(c_re), cd(c_im))


S5_TIME_CHUNK = 128
S5_UNROLL = 8


def _s5_scan_kernel(u_ref, a_ref, bd_ref, cd_ref, o_ref, hs_ref, carry_ref, *, batch):
    d = pl.program_id(0)
    j = pl.program_id(1)
    n = S5_NSTATE

    @pl.when(j == 0)
    def _():
        carry_ref[...] = jnp.zeros_like(carry_ref)

    half = hs_ref.shape[0] // 2
    n_slab = S5_W // 128
    for r in (0, half):
        u = jnp.concatenate([u_ref[s, r:r + half, :] for s in range(n_slab)], axis=1)
        hs_ref[r:r + half, :] = jnp.dot(u.astype(BF16), bd_ref[0], preferred_element_type=F32)
    ar = a_ref[0, :, 0:n]
    ai = a_ref[0, :, n:]
    if batch != 8:
        ar = jnp.broadcast_to(ar[0:1], (batch, n))
        ai = jnp.broadcast_to(ai[0:1], (batch, n))

    def body(i, carry):
        hr, hi = carry
        for s in range(S5_UNROLL):
            step = i * S5_UNROLL + s
            step = jnp.where(d == 0, step, S5_TIME_CHUNK - 1 - step)
            rows = pl.ds(pl.multiple_of(step * batch, batch), batch)
            nr = ar * hr - ai * hi + hs_ref[rows, 0:n]
            ni = ar * hi + ai * hr + hs_ref[rows, n:]
            hs_ref[rows, 0:n] = nr
            hs_ref[rows, n:] = ni
            hr, hi = nr, ni
        return hr, hi

    hr, hi = lax.fori_loop(0, S5_TIME_CHUNK // S5_UNROLL, body,
                           (carry_ref[:, 0:n], carry_ref[:, n:]))
    carry_ref[:, 0:n] = hr
    carry_ref[:, n:] = hi
    for r in (0, half):
        y = jnp.dot(hs_ref[r:r + half, :].astype(BF16), cd_ref[0], preferred_element_type=F32)
        for s in range(n_slab):
            o_ref[0, s, r:r + half, :] = y[:, s * 128:(s + 1) * 128]


def _s5_scan(us_flat, a, bd, cd, batch, n_ctx):
    n_slab, rows_total, _ = us_flat.shape
    ta = rows_total // batch
    tc = S5_TIME_CHUNK
    n_chunks = ta // tc
    ctx_chunks = n_ctx // tc
    blk = tc * batch

    def chunk_of(d, j):
        bwd = jnp.where(j < ctx_chunks, ctx_chunks - 1 - j, n_chunks - 1 - (j - ctx_chunks))
        return jnp.where(d == 0, j, bwd)

    return pl.pallas_call(
        functools.partial(_s5_scan_kernel, batch=batch),
        grid=(2, n_chunks),
        in_specs=[pl.BlockSpec((n_slab, blk, 128), lambda d, j: (0, chunk_of(d, j), 0)),
                  pl.BlockSpec((1, 8, 2 * S5_NSTATE), lambda d, j: (d, 0, 0)),
                  pl.BlockSpec((1, S5_W, 2 * S5_NSTATE), lambda d, j: (d, 0, 0)),
                  pl.BlockSpec((1, 2 * S5_NSTATE, S5_W), lambda d, j: (d, 0, 0))],
        out_specs=pl.BlockSpec((1, n_slab, blk, 128), lambda d, j: (d, 0, chunk_of(d, j), 0)),
        out_shape=jax.ShapeDtypeStruct((2, n_slab, rows_total, 128), F32),
        scratch_shapes=[pltpu.VMEM((blk, 2 * S5_NSTATE), F32),
                        pltpu.VMEM((batch, 2 * S5_NSTATE), F32)],
        compiler_params=_cparams("arbitrary", "arbitrary"),
        name="s5_scan",
    )(us_flat, a, bd, cd)


def _merge_ln_kernel(gs_ref, y5_ref, us_ref, h_ref, gate_ref, dd_ref, gw_ref, gb_ref,
                     w_ref, g_ref, beta_ref, o_ref, *, batch):
    rows = _time_major_rows(pl.program_id(1), batch)
    y5 = jnp.concatenate([y5_ref[0, s, rows, :] + y5_ref[1, s, rows, :]
                          for s in range(S5_W // 128)], axis=1)
    us = jnp.concatenate([us_ref[s, rows, :] for s in range(S5_W // 128)], axis=1)
    ge = jax.nn.gelu(y5 + dd_ref[...] * us)
    s5 = ge * jax.nn.sigmoid(
        jnp.dot(ge.astype(BF16), gw_ref[...], preferred_element_type=F32) + gb_ref[...])
    y = jnp.dot(gs_ref[0], w_ref[0:SSD_W, :], preferred_element_type=F32)
    y = y + jnp.dot(s5.astype(BF16), w_ref[SSD_W:, :], preferred_element_type=F32)
    o_ref[0] = _layer_norm(ALPHA * h_ref[0] + gate_ref[0] * y, g_ref[...], beta_ref[...])


def _merge_ln(g_ssd, y5, us_t, h, mods, layer, s5_d, glu_w, glu_b, w_out, ln_g, ln_b, ctx_tiles):
    batch, ta, _ = h.shape
    nt = ta // ROW_TILE - ctx_tiles
    n_slab = S5_W // 128
    tm_rows = ROW_TILE * batch
    return pl.pallas_call(
        functools.partial(_merge_ln_kernel, batch=batch),
        grid=(nt, batch),
        in_specs=[pl.BlockSpec((1, ROW_TILE, SSD_W), lambda t, b: (b, t, 0)),
                  pl.BlockSpec((2, n_slab, tm_rows, 128), lambda t, b: (0, 0, t + ctx_tiles, 0)),
                  pl.BlockSpec((n_slab, tm_rows, 128), lambda t, b: (0, t + ctx_tiles, 0)),
                  pl.BlockSpec((1, ROW_TILE, D_MODEL), lambda t, b: (b, t + ctx_tiles, 0)),
                  pl.BlockSpec((1, 1, D_MODEL), lambda t, b: ((layer * MOD_ROWS + b) * 6 + 2, 0, 0)),
                  _full((1, S5_W)), _full((S5_W, S5_W)), _full((1, S5_W)),
                  _full((D_MODEL, D_MODEL)), _full((1, D_MODEL)), _full((1, D_MODEL))],
        out_specs=pl.BlockSpec((1, ROW_TILE, D_MODEL), lambda t, b: (b, t, 0)),
        out_shape=jax.ShapeDtypeStruct((batch, nt * ROW_TILE, D_MODEL), F32),
        compiler_params=_cparams("arbitrary", "arbitrary"),
        name="merge_outproj_ln",
    )(g_ssd, y5, us_t, h, mods, s5_d.reshape(1, S5_W), glu_w.astype(BF16),
      glu_b.reshape(1, S5_W), w_out, ln_g.reshape(1, -1), ln_b.reshape(1, -1))


def _attn_layer(h_ctx, h_lat, mods, layer, i, p, keep_ctx):
    n_ctx = h_ctx.shape[1]
    ctx_tiles = n_ctx // ROW_TILE
    lam_init = 0.8 - 0.6 * math.exp(-0.3 * layer)
    cos, sin = _rope_tables(h_lat.shape[1], n_ctx)
    q, k, v, f = _inproj_attn(h_ctx, h_lat, mods, layer, p['attn_w_in'][i].astype(BF16),
                              cos, sin, ctx_tiles)
    o_ctx, o_lat = _attention(q, k, v, p['attn_lambda'][i], p['attn_subln_g'][i], lam_init,
                              ctx_tiles)
    fm = _fourier(f, p['fourier_w'][i], p['fourier_b'][i], n_ctx)
    h1 = _outproj_ln(o_ctx, o_lat, fm, h_ctx, h_lat, mods, layer,
                     p['attn_w_out'][i].astype(BF16),
                     p['ln_g'][layer, 0], p['ln_b'][layer, 0], ctx_tiles)
    return _ffn(h1, mods, layer, p['ffn_w_up'][layer].astype(BF16), p['ffn_b_up'][layer],
                p['ffn_conv_w'][layer], p['ffn_conv_b'][layer],
                p['ffn_w_down'][layer].astype(BF16), p['ffn_b_down'][layer],
                p['ln_g'][layer, 1], p['ln_b'][layer, 1], ctx_tiles, 0 if keep_ctx else ctx_tiles)


def _ssm_layer(h, mods, layer, i, n_ctx, p, keep_ctx):
    assert not keep_ctx, "an SSM layer that must also emit context rows is not implemented"
    ctx_tiles = n_ctx // ROW_TILE
    batch = h.shape[0]
    w = p['ssm_w_in'][i]
    w_pad = jnp.concatenate(
        [w[:, :SSD_W + XBC_W], w[:, SSD_W + XBC_W + DT_W:], w[:, SSD_W + XBC_W:SSD_W + XBC_W + DT_W],
         jnp.zeros((D_MODEL, DT_PAD - DT_W), F32)], axis=1).astype(BF16)
    z, xbc, dt, us_t = _inproj_ssm(h, mods, layer, w_pad, ctx_tiles)
    g_ssd = _ssd(xbc, dt, z, p['ssd_conv_w'][i], p['ssd_conv_b'][i], p['ssd_a_log'][i],
                 p['ssd_dt_bias'][i], p['ssd_d'][i], p['ssd_norm_g'][i], n_ctx)
    a, bd, cd = _s5_discretize(p['s5_lambda_re'][i], p['s5_lambda_im'][i], p['s5_log_dt'][i],
                               p['s5_b_re'][i], p['s5_b_im'][i], p['s5_c_re'][i], p['s5_c_im'][i])
    y5 = _s5_scan(us_t, a, bd, cd, batch, n_ctx)
    h1 = _merge_ln(g_ssd, y5, us_t, h, mods, layer, p['s5_d'][i], p['s5_glu_w'][i],
                   p['s5_glu_b'][i], p['ssm_w_out'][i].astype(BF16),
                   p['ln_g'][layer, 0], p['ln_b'][layer, 0], ctx_tiles)
    return _ffn(h1, mods, layer, p['ffn_w_up'][layer].astype(BF16), p['ffn_b_up'][layer],
                p['ffn_conv_w'][layer], p['ffn_conv_b'][layer],
                p['ffn_w_down'][layer].astype(BF16), p['ffn_b_down'][layer],
                p['ln_g'][layer, 1], p['ln_b'][layer, 1], 0, 0)


def kernel(x, c, ctx, c_ctx, ada_w, ada_b, ln_g, ln_b, ffn_w_up, ffn_b_up, ffn_conv_w, ffn_conv_b, ffn_w_down, ffn_b_down, attn_w_in, attn_lambda, attn_subln_g, fourier_w, fourier_b, attn_w_out, ssm_w_in, ssd_conv_w, ssd_conv_b, ssd_a_log, ssd_dt_bias, ssd_d, ssd_norm_g, s5_lambda_re, s5_lambda_im, s5_log_dt, s5_b_re, s5_b_im, s5_c_re, s5_c_im, s5_d, s5_glu_w, s5_glu_b, ssm_w_out):
    p = dict(ln_g=ln_g, ln_b=ln_b, ffn_w_up=ffn_w_up, ffn_b_up=ffn_b_up, ffn_conv_w=ffn_conv_w,
             ffn_conv_b=ffn_conv_b, ffn_w_down=ffn_w_down, ffn_b_down=ffn_b_down,
             attn_w_in=attn_w_in, attn_lambda=attn_lambda, attn_subln_g=attn_subln_g,
             fourier_w=fourier_w, fourier_b=fourier_b, attn_w_out=attn_w_out, ssm_w_in=ssm_w_in,
             ssd_conv_w=ssd_conv_w, ssd_conv_b=ssd_conv_b, ssd_a_log=ssd_a_log,
             ssd_dt_bias=ssd_dt_bias, ssd_d=ssd_d, ssd_norm_g=ssd_norm_g,
             s5_lambda_re=s5_lambda_re, s5_lambda_im=s5_lambda_im, s5_log_dt=s5_log_dt,
             s5_b_re=s5_b_re, s5_b_im=s5_b_im, s5_c_re=s5_c_re, s5_c_im=s5_c_im, s5_d=s5_d,
             s5_glu_w=s5_glu_w, s5_glu_b=s5_glu_b, ssm_w_out=ssm_w_out)
    batch, n_lat, _ = x.shape
    n_ctx = ctx.shape[1]
    assert n_ctx == ROW_TILE and n_lat % ROW_TILE == 0 and batch < MOD_ROWS
    mods = _ada_mods(c, c_ctx, ada_w, ada_b)
    assert DEPTH == 2
    h = _attn_layer(ctx, x, mods, 0, 0, p, keep_ctx=True)
    return _ssm_layer(h, mods, 1, 0, n_ctx, p, keep_ctx=False)
```

```python
import functools
import math

import numpy as np
import jax
import jax.numpy as jnp
from jax import lax
from jax.experimental import pallas as pl
from jax.experimental.pallas import tpu as pltpu

F32 = jnp.float32
BF16 = jnp.bfloat16

D_MODEL = 1024
DEPTH = 2
GRID_W = 64
ROPE_BASE = 10000.0
LN_EPS = 1e-5
ALPHA = (2 * DEPTH) ** 0.25
ATTN_W = 768
ATTN_HEADS = 6
ATTN_HEAD_DIM = 64
F_W = 256
F_GROUPS = 4
F_GROUP_W = 64
ATTN_IN_W = 2 * ATTN_W + ATTN_W + F_W
SSD_W = 768
SSD_HEADS = 12
SSD_HEAD_DIM = 64
SSD_GROUPS = 2
SSD_STATE = 128
SSD_CHUNK = 128
XBC_W = SSD_W + 2 * SSD_GROUPS * SSD_STATE
DT_W = 2 * SSD_HEADS
DT_PAD = 128
S5_W = 256
S5_GROUPS = 16
S5_GROUP_W = 16
S5_STATE = 64
S5_NSTATE = S5_GROUPS * S5_STATE
SSM_IN_PAD = SSD_W + XBC_W + S5_W + DT_PAD
D_FF = 2816
FF_TILE = 256

ROW_TILE = 256
ATTN_Q_SUB = 128
ATTN_ITEM_TILES = 2
FFN_LAT_ROWS = 512
HALO = 8
MOD_ROWS = 16
VMEM_LIMIT_BYTES = 56 * 1024 * 1024


def _cparams(*sem):
    return pltpu.CompilerParams(dimension_semantics=sem, vmem_limit_bytes=VMEM_LIMIT_BYTES)


def _silu(v):
    return v * jax.nn.sigmoid(v)


def _layer_norm(v, g, b):
    mu = jnp.mean(v, axis=-1, keepdims=True)
    d = v - mu
    var = jnp.mean(d * d, axis=-1, keepdims=True)
    return d * lax.rsqrt(var + LN_EPS) * g + b


def _full(shape):
    nd = len(shape)
    return pl.BlockSpec(shape, lambda *_: (0,) * nd)


def _mod_spec(layer, j, batch, ctx_tiles):
    def idx(b, t):
        row = jnp.where(t < ctx_tiles, batch, b)
        return ((layer * MOD_ROWS + row) * 6 + j, 0, 0)
    return pl.BlockSpec((1, 1, D_MODEL), idx)


def _ada_kernel(c_ref, w_ref, b_ref, o_ref):
    s = _silu(c_ref[...])
    o_ref[0] = jnp.dot(s, w_ref[0], preferred_element_type=F32,
                       precision=lax.Precision.HIGHEST) + b_ref[0]


def _ada_mods(c, c_ctx, ada_w, ada_b):
    batch = c.shape[0]
    nl = ada_w.shape[0]
    c_all = jnp.concatenate(
        [c, c_ctx[None], jnp.zeros((MOD_ROWS - batch - 1, D_MODEL), F32)], axis=0)
    out = pl.pallas_call(
        _ada_kernel,
        grid=(nl, 6),
        in_specs=[_full((MOD_ROWS, D_MODEL)),
                  pl.BlockSpec((1, D_MODEL, D_MODEL), lambda l, j: (l, 0, j)),
                  pl.BlockSpec((1, 1, D_MODEL), lambda l, j: (l, 0, j))],
        out_specs=pl.BlockSpec((1, MOD_ROWS, D_MODEL), lambda l, j: (l, 0, j)),
        out_shape=jax.ShapeDtypeStruct((nl, MOD_ROWS, 6 * D_MODEL), F32),
        compiler_params=_cparams("arbitrary", "arbitrary"),
        name="ada_mods",
    )(c_all, ada_w, ada_b.reshape(nl, 1, 6 * D_MODEL))
    return out.reshape(nl * MOD_ROWS * 6, 1, D_MODEL)


def _inproj_attn_kernel(hc_ref, hl_ref, shift_ref, scale_ref, w_ref, cos_ref, sin_ref,
                        q_ref, k_ref, v_ref, f_ref, *, ctx_tiles):
    h = jnp.where(pl.program_id(1) < ctx_tiles, hc_ref[0], hl_ref[0])
    u = (h * (1.0 + scale_ref[0]) + shift_ref[0]).astype(BF16)
    p = jnp.dot(u, w_ref[...], preferred_element_type=F32)
    cos = cos_ref[...]
    sin = sin_ref[...]
    lane = lax.broadcasted_iota(jnp.int32, cos.shape, 1)
    first_half = (lane % ATTN_HEAD_DIM) < (ATTN_HEAD_DIM // 2)

    def rope(blk):
        partner = jnp.where(first_half, pltpu.roll(blk, 128 - 32, 1), pltpu.roll(blk, 32, 1))
        return blk * cos + partner * sin

    qk_scale = ATTN_HEAD_DIM ** -0.5 * math.log2(math.e)
    for i in range(ATTN_HEADS):
        lo, hi = i * 128, (i + 1) * 128
        q_ref[0, :, lo:hi] = (rope(p[:, lo:hi]) * qk_scale).astype(BF16)
        k_ref[0, :, lo:hi] = rope(p[:, ATTN_W + lo:ATTN_W + hi]).astype(BF16)
    v_ref[0] = p[:, 2 * ATTN_W:3 * ATTN_W].astype(BF16)
    f_ref[0] = p[:, 3 * ATTN_W:].astype(BF16)


def _rope_tables(n_lat, n_ctx):
    rows = n_lat // GRID_W
    row = jnp.repeat(jnp.arange(rows, dtype=F32), GRID_W)
    col = jnp.tile(jnp.arange(GRID_W, dtype=F32), rows)
    n_freq = ATTN_HEAD_DIM // 4
    inv_freq = ROPE_BASE ** (-jnp.arange(n_freq, dtype=F32) / n_freq)
    ang = jnp.concatenate([row[:, None] * inv_freq, col[:, None] * inv_freq], axis=-1)
    cos, sin = jnp.cos(ang), jnp.sin(ang)
    cos128 = jnp.tile(cos, (1, 4))
    sin128 = jnp.tile(jnp.concatenate([-sin, sin], axis=-1), (1, 2))
    cos_all = jnp.concatenate([jnp.ones((n_ctx, 128), F32), cos128], axis=0)
    sin_all = jnp.concatenate([jnp.zeros((n_ctx, 128), F32), sin128], axis=0)
    return cos_all, sin_all


def _split_specs(width, ctx_tiles):
    return [pl.BlockSpec((1, ROW_TILE, width), lambda b, t: (b, jnp.minimum(t, ctx_tiles - 1), 0)),
            pl.BlockSpec((1, ROW_TILE, width), lambda b, t: (b, jnp.maximum(t - ctx_tiles, 0), 0))]


def _inproj_attn(h_ctx, h_lat, mods, layer, w_in, cos, sin, ctx_tiles):
    batch = h_lat.shape[0]
    ta = h_ctx.shape[1] + h_lat.shape[1]
    nt = ta // ROW_TILE
    row = lambda w: pl.BlockSpec((1, ROW_TILE, w), lambda b, t: (b, t, 0))
    tab = pl.BlockSpec((ROW_TILE, 128), lambda b, t: (t, 0))
    return pl.pallas_call(
        functools.partial(_inproj_attn_kernel, ctx_tiles=ctx_tiles),
        grid=(batch, nt),
        in_specs=_split_specs(D_MODEL, ctx_tiles)
        + [_mod_spec(layer, 0, batch, ctx_tiles), _mod_spec(layer, 1, batch, ctx_tiles),
           _full((D_MODEL, ATTN_IN_W)), tab, tab],
        out_specs=[row(ATTN_W), row(ATTN_W), row(ATTN_W), row(F_W)],
        out_shape=[jax.ShapeDtypeStruct((batch, ta, ATTN_W), BF16)] * 3
        + [jax.ShapeDtypeStruct((batch, ta, F_W), BF16)],
        compiler_params=_cparams("parallel", "arbitrary"),
        name="inproj_attn",
    )(h_ctx, h_lat, mods, mods, w_in, cos, sin)


def _diff_lambda(lam_ref, lam_init):
    lamv = lam_ref[...]
    l1 = jnp.sum(lamv[0:1] * lamv[1:2], axis=-1, keepdims=True)
    l2 = jnp.sum(lamv[2:3] * lamv[3:4], axis=-1, keepdims=True)
    return jnp.exp(l1) - jnp.exp(l2) + lam_init


def _stack_maps(q):
    lane = lax.broadcasted_iota(jnp.int32, q.shape, 1)
    zero = jnp.zeros_like(q)
    return jnp.concatenate([jnp.where(lane < ATTN_HEAD_DIM, q, zero),
                            jnp.where(lane >= ATTN_HEAD_DIM, q, zero)], axis=0)


def _scores(q2, k):
    return lax.dot_general(q2, k, (((1,), (1,)), ((), ())), preferred_element_type=F32)


def _diff_softmax_pv(load_s0, m0, load_s1, m1, lam, v, gain):
    e0 = jnp.exp2(load_s0() - m0)
    e1 = jnp.exp2(load_s1() - m1)
    l0 = jnp.sum(e0, axis=-1, keepdims=True)
    l1 = jnp.sum(e1, axis=-1, keepdims=True)
    w = e0 - e1 * (lam * l0 / l1)
    o = jnp.dot(w.astype(BF16), v, preferred_element_type=F32) * (1.0 / l0)
    o = o * lax.rsqrt(jnp.mean(o * o, axis=-1, keepdims=True) + LN_EPS)
    return (o * gain).astype(BF16)


def _attn_ctx_kernel(lam_ref, g_ref, q_ref, k_ref, v_ref, o_ref, *, lam_init):
    lam = _diff_lambda(lam_ref, lam_init)
    gain = g_ref[...] * (1.0 - lam_init)
    tq = q_ref.shape[1]
    for hd in range(ATTN_HEADS):
        cols = slice(hd * 128, (hd + 1) * 128)
        s = _scores(_stack_maps(q_ref[0, :, cols]), k_ref[0, :, cols])
        m = jnp.max(s, axis=-1, keepdims=True)

        def half(lo, s=s):
            return lambda: s[lo:lo + tq]

        o_ref[0, :, cols] = _diff_softmax_pv(half(0), m[:tq], half(tq), m[tq:], lam,
                                             v_ref[0, :, cols], gain)


def _attn_lat_kernel(lam_ref, g_ref, k_ref, v_ref, *rest, lam_init):
    q_refs = rest[:ATTN_ITEM_TILES]
    o_ref, sa_ref, ma_ref, sb_ref, mb_ref = rest[ATTN_ITEM_TILES:]
    t = pl.program_id(0)
    tq = ATTN_ITEM_TILES * ROW_TILE
    sub = ATTN_Q_SUB

    @pl.when(t == 0)
    def _():
        sb_ref[...] = jnp.zeros_like(sb_ref)
        mb_ref[...] = jnp.zeros_like(mb_ref)

    def step(s_new, m_new, s_old, m_old):
        lam = _diff_lambda(lam_ref, lam_init)
        gain = g_ref[...] * (1.0 - lam_init)
        q = jnp.concatenate([q_ref[0] for q_ref in q_refs], axis=0)
        s = _scores(_stack_maps(q), k_ref[0])
        s_new[...] = s
        m_new[...] = jnp.max(s, axis=-1, keepdims=True)
        v = v_ref[0]
        for i in range(tq // sub):
            r0 = slice(i * sub, (i + 1) * sub)
            r1 = slice(tq + i * sub, tq + (i + 1) * sub)
            o_ref[0, r0, :] = _diff_softmax_pv(
                functools.partial(s_old.__getitem__, (r0, slice(None))), m_old[r0, :],
                functools.partial(s_old.__getitem__, (r1, slice(None))), m_old[r1, :],
                lam, v, gain)

    @pl.when(t % 2 == 0)
    def _():
        step(sa_ref, ma_ref, sb_ref, mb_ref)

    @pl.when(t % 2 == 1)
    def _():
        step(sb_ref, mb_ref, sa_ref, ma_ref)


def _attention(q, k, v, lam_vec, subln_g, lam_init, ctx_tiles):
    batch, ta, _ = q.shape
    n_ctx = ctx_tiles * ROW_TILE
    nt = ta // ROW_TILE - ctx_tiles
    g = subln_g.reshape(1, 128)
    small = [_full((4, ATTN_HEAD_DIM)), _full((1, 128))]
    cspec = pl.BlockSpec((1, n_ctx, ATTN_W), lambda b: (b, 0, 0))
    o_ctx = pl.pallas_call(
        functools.partial(_attn_ctx_kernel, lam_init=lam_init),
        grid=(batch,),
        in_specs=small + [cspec, cspec, cspec],
        out_specs=cspec,
        out_shape=jax.ShapeDtypeStruct((batch, n_ctx, ATTN_W), BF16),
        compiler_params=_cparams("arbitrary"),
        name="diff_attention_ctx",
    )(lam_vec, g, q, k, v)

    per = ATTN_ITEM_TILES
    assert nt % per == 0
    ni = nt // per
    n_items = batch * ATTN_HEADS * ni

    def item(j):
        return j // (ATTN_HEADS * ni), (j // ni) % ATTN_HEADS, j % ni

    def score_item(j):
        return item(jnp.minimum(j, n_items - 1))

    def finish_item(j):
        return item(jnp.maximum(j - 1, 0))

    def q_spec(part):
        def idx(j):
            b, h, t = score_item(j)
            return b, ctx_tiles + t * per + part, h
        return pl.BlockSpec((1, ROW_TILE, 128), idx)

    def k_idx(j):
        b, h, _ = score_item(j)
        return b, 0, h

    def v_idx(j):
        b, h, _ = finish_item(j)
        return b, 0, h

    def o_idx(j):
        b, h, t = finish_item(j)
        return b, t, h

    o_lat = pl.pallas_call(
        functools.partial(_attn_lat_kernel, lam_init=lam_init),
        grid=(n_items + 1,),
        in_specs=small + [pl.BlockSpec((1, ta, 128), k_idx), pl.BlockSpec((1, ta, 128), v_idx)]
        + [q_spec(part) for part in range(per)],
        out_specs=pl.BlockSpec((1, per * ROW_TILE, 128), o_idx),
        out_shape=jax.ShapeDtypeStruct((batch, nt * ROW_TILE, ATTN_W), BF16),
        scratch_shapes=[pltpu.VMEM((2 * per * ROW_TILE, ta), F32),
                        pltpu.VMEM((2 * per * ROW_TILE, 1), F32)] * 2,
        compiler_params=_cparams("arbitrary"),
        name="diff_attention",
    )(lam_vec, g, k, v, *([q] * per))
    return o_ctx, o_lat


def _dft_tables(n):
    k = np.arange(n, dtype=np.int64)
    ang = 2.0 * np.pi * ((k[:, None] * k[None, :]) % n).astype(np.float64) / n
    return np.cos(ang), np.sin(ang)


def _fourier_kernel(f_ref, cs_ref, dl_ref, dc_ref, w_ref, b_ref, o_ref, ac_ref, al_ref,
                    *, n_ctx, n_lat):
    t = pl.program_id(0)
    b = pl.program_id(1)
    ctx_tiles = n_ctx // ROW_TILE

    def stage1(rows0, n):
        a = jnp.dot(f_ref[0, rows0:rows0 + n, :], cs_ref[...], preferred_element_type=F32)
        return a[:, :F_W].astype(BF16), a[:, F_W:].astype(BF16)

    def stage2(dft, a, n):
        z = jnp.dot(dft, a, preferred_element_type=F32)
        z = z * (1.0 / math.sqrt(n * F_GROUP_W))
        o = jnp.dot(z.astype(BF16), w_ref[...], preferred_element_type=F32) + b_ref[...]
        o_ref[0] = o.astype(BF16)

    @pl.when(t < ctx_tiles)
    def _():
        ac_ref[0:n_ctx, :], ac_ref[n_ctx:, :] = stage1(0, n_ctx)
        stage2(dc_ref[...], ac_ref[...], n_ctx)

    @pl.when(t == ctx_tiles)
    def _():
        al_ref[b, 0:n_lat, :], al_ref[b, n_lat:, :] = stage1(n_ctx, n_lat)

    @pl.when(t >= ctx_tiles)
    def _():
        stage2(dl_ref[...], al_ref[b], n_lat)


def _fourier(f, fourier_w, fourier_b, n_ctx):
    batch, ta, _ = f.shape
    n_lat = ta - n_ctx
    assert n_ctx == ROW_TILE
    nt = ta // ROW_TILE
    ctx_tiles = n_ctx // ROW_TILE
    cc, sc = _dft_tables(F_GROUP_W)
    eye = np.eye(F_GROUPS)
    cs = jnp.asarray(np.concatenate([np.kron(eye, cc), np.kron(eye, sc)], axis=1), BF16)
    cl, sl = _dft_tables(n_lat)
    dft_lat = jnp.asarray(np.concatenate([cl, -sl], axis=1), BF16)
    cx, sx = _dft_tables(n_ctx)
    dft_ctx = jnp.asarray(np.concatenate([cx, -sx], axis=1), BF16)
    w_blk = jnp.einsum('gce,gh->gche', fourier_w, jnp.eye(F_GROUPS, dtype=F32))
    w_blk = w_blk.reshape(F_W, F_W).astype(BF16)
    return pl.pallas_call(
        functools.partial(_fourier_kernel, n_ctx=n_ctx, n_lat=n_lat),
        grid=(nt, batch),
        in_specs=[
                  pl.BlockSpec((1, ta, F_W),
                               lambda t, b: (jnp.where(t <= ctx_tiles, b, batch - 1), 0, 0)),
                  _full((F_W, 2 * F_W)),
                  pl.BlockSpec((ROW_TILE, 2 * n_lat),
                               lambda t, b: (jnp.maximum(t - ctx_tiles, 0), 0)),
                  _full((n_ctx, 2 * n_ctx)),
                  _full((F_W, F_W)), _full((1, F_W))],
        out_specs=pl.BlockSpec((1, ROW_TILE, F_W), lambda t, b: (b, t, 0)),
        out_shape=jax.ShapeDtypeStruct((batch, ta, F_W), BF16),
        scratch_shapes=[pltpu.VMEM((2 * n_ctx, F_W), BF16),
                        pltpu.VMEM((batch, 2 * n_lat, F_W), BF16)],
        compiler_params=_cparams("arbitrary", "arbitrary"),
        name="fourier_mix",
    )(f, cs, dft_lat, dft_ctx, w_blk, fourier_b.reshape(1, F_W))


def _outproj_ln_kernel(ac_ref, al_ref, b_ref, hc_ref, hl_ref, gate_ref, w_ref, g_ref, beta_ref,
                       o_ref, *, ctx_tiles):
    is_ctx = pl.program_id(1) < ctx_tiles
    a = jnp.where(is_ctx, ac_ref[0], al_ref[0])
    h = jnp.where(is_ctx, hc_ref[0], hl_ref[0])
    wa = a.shape[1]
    y = jnp.dot(a, w_ref[0:wa, :], preferred_element_type=F32)
    y = y + jnp.dot(b_ref[0], w_ref[wa:, :], preferred_element_type=F32)
    o_ref[0] = _layer_norm(ALPHA * h + gate_ref[0] * y, g_ref[...], beta_ref[...])


def _outproj_ln(a_ctx, a_lat, b2, h_ctx, h_lat, mods, layer, w_out, ln_g, ln_b, ctx_tiles):
    batch, ta, _ = b2.shape
    nt = ta // ROW_TILE
    row = lambda w: pl.BlockSpec((1, ROW_TILE, w), lambda b, t: (b, t, 0))
    return pl.pallas_call(
        functools.partial(_outproj_ln_kernel, ctx_tiles=ctx_tiles),
        grid=(batch, nt),
        in_specs=_split_specs(a_lat.shape[2], ctx_tiles) + [row(b2.shape[2])]
        + _split_specs(D_MODEL, ctx_tiles)
        + [_mod_spec(layer, 2, batch, ctx_tiles),
           _full((D_MODEL, D_MODEL)), _full((1, D_MODEL)), _full((1, D_MODEL))],
        out_specs=row(D_MODEL),
        out_shape=jax.ShapeDtypeStruct((batch, ta, D_MODEL), F32),
        compiler_params=_cparams("parallel", "arbitrary"),
        name="outproj_ln",
    )(a_ctx, a_lat, b2, h_ctx, h_lat, mods, w_out, ln_g.reshape(1, D_MODEL),
      ln_b.reshape(1, D_MODEL))


def _ffn_kernel(h_ref, hp_ref, hn_ref, shift_ref, scale_ref, gate_ref,
                wup_ref, bup_ref, cw_ref, cb_ref, wdn_ref, bdn_ref, g_ref, beta_ref,
                o_ref, uext_ref, ubf_ref, act_ref, *, ctx_tiles, tile_off, nt_seq):
    t = pl.program_id(1) + tile_off
    seg_first = (t == 0) | (t == ctx_tiles)
    seg_last = (t == nt_seq - 1) | (t == ctx_tiles - 1)
    sc = 1.0 + scale_ref[0]
    sh = shift_ref[0]
    h = h_ref[0]
    tm = h.shape[0]
    uext_ref[0:HALO, :] = hp_ref[0] * sc + sh
    uext_ref[HALO:HALO + tm, :] = h * sc + sh
    uext_ref[HALO + tm:, :] = hn_ref[0] * sc + sh
    ubf_ref[...] = uext_ref[...].astype(BF16)
    row8 = lax.broadcasted_iota(jnp.int32, (8, FF_TILE), 0)

    def hidden(col0):
        cols = slice(col0, col0 + FF_TILE)
        zr = jnp.dot(ubf_ref[...], wup_ref[:, cols], preferred_element_type=F32)
        bup = bup_ref[:, cols]
        cw = cw_ref[:, cols]
        bias = cb_ref[:, cols] + (cw[0:1] + cw[1:2] + cw[2:3]) * bup
        z0 = zr[HALO:HALO + tm]
        prev = jnp.where(seg_first, -bup, zr[HALO - 1:HALO])
        nxt = jnp.where(seg_last, -bup, zr[HALO + tm:HALO + tm + 1])
        down = pltpu.roll(z0, 1, 0)
        up = pltpu.roll(z0, tm - 1, 0)
        zm1 = jnp.concatenate([jnp.where(row8 == 0, prev, down[0:8]), down[8:]], axis=0)
        zp1 = jnp.concatenate([up[:tm - 8], jnp.where(row8 == 7, nxt, up[tm - 8:])], axis=0)
        return cw[0:1] * zm1 + cw[1:2] * z0 + cw[2:3] * zp1 + bias

    for j in range(D_FF // FF_TILE):
        val = hidden(j * FF_TILE)
        gat = hidden(D_FF + j * FF_TILE)
        act_ref[:, j * FF_TILE:(j + 1) * FF_TILE] = (val * _silu(gat)).astype(BF16)
    f = jnp.dot(act_ref[...], wdn_ref[...], preferred_element_type=F32) + bdn_ref[...]
    o_ref[0] = _layer_norm(ALPHA * h + gate_ref[0] * f, g_ref[...], beta_ref[...])


def _ffn(h, mods, layer, w_up, b_up, conv_w, conv_b, w_down, b_down, ln_g, ln_b,
         ctx_tiles, tile_off, rows=ROW_TILE):
    batch, ta, _ = h.shape
    nt_seq = ta // rows
    nt = nt_seq - tile_off
    hb = rows // HALO
    n_hblk = ta // HALO
    mspec = lambda j: pl.BlockSpec(
        (1, 1, D_MODEL),
        lambda b, t: ((layer * MOD_ROWS + jnp.where(t + tile_off < ctx_tiles, batch, b)) * 6 + j, 0, 0))
    resident = lambda shape: pl.BlockSpec(shape, lambda *_: (0,) * len(shape),
                                          pipeline_mode=pl.Buffered(1))
    return pl.pallas_call(
        functools.partial(_ffn_kernel, ctx_tiles=ctx_tiles, tile_off=tile_off, nt_seq=nt_seq),
        grid=(batch, nt),
        in_specs=[pl.BlockSpec((1, rows, D_MODEL), lambda b, t: (b, t + tile_off, 0)),
                  pl.BlockSpec((1, HALO, D_MODEL),
                               lambda b, t: (b, jnp.maximum((t + tile_off) * hb - 1, 0), 0)),
                  pl.BlockSpec((1, HALO, D_MODEL),
                               lambda b, t: (b, jnp.minimum((t + tile_off + 1) * hb, n_hblk - 1), 0)),
                  mspec(3), mspec(4), mspec(5),
                  resident((D_MODEL, 2 * D_FF)), _full((1, 2 * D_FF)),
                  _full((3, 2 * D_FF)), _full((1, 2 * D_FF)),
                  resident((D_FF, D_MODEL)), _full((1, D_MODEL)),
                  _full((1, D_MODEL)), _full((1, D_MODEL))],
        out_specs=pl.BlockSpec((1, rows, D_MODEL), lambda b, t: (b, t, 0)),
        out_shape=jax.ShapeDtypeStruct((batch, nt * rows, D_MODEL), F32),
        scratch_shapes=[pltpu.VMEM((rows + 2 * HALO, D_MODEL), F32),
                        pltpu.VMEM((rows + 2 * HALO, D_MODEL), BF16),
                        pltpu.VMEM((rows, D_FF), BF16)],
        compiler_params=_cparams("parallel", "arbitrary"),
        name="conv_ffn_ln",
    )(h, h, h, mods, mods, mods, w_up, b_up.reshape(1, -1), conv_w, conv_b.reshape(1, -1),
      w_down, b_down.reshape(1, -1), ln_g.reshape(1, -1), ln_b.reshape(1, -1))


def _time_major_rows(b, batch):
    return pl.ds(b, ROW_TILE, stride=batch)


def _inproj_ssm_kernel(h_ref, shift_ref, scale_ref, w_ref, z_ref, xbc_ref, dt_ref, us_ref,
                       *, batch):
    u = (h_ref[0] * (1.0 + scale_ref[0]) + shift_ref[0]).astype(BF16)
    p = jnp.dot(u, w_ref[...], preferred_element_type=F32)
    z_ref[0] = p[:, :SSD_W]
    xbc_ref[0] = p[:, SSD_W:SSD_W + XBC_W]
    dt_ref[0] = p[:, SSD_W + XBC_W + S5_W:]
    rows = _time_major_rows(pl.program_id(1), batch)
    for j in range(S5_W // 128):
        col0 = SSD_W + XBC_W + j * 128
        us_ref[j, rows, :] = p[:, col0:col0 + 128]


def _inproj_ssm(h, mods, layer, w_in_pad, ctx_tiles):
    batch, ta, _ = h.shape
    nt = ta // ROW_TILE
    row = lambda w: pl.BlockSpec((1, ROW_TILE, w), lambda t, b: (b, t, 0))
    mspec = lambda j: pl.BlockSpec(
        (1, 1, D_MODEL),
        lambda t, b: ((layer * MOD_ROWS + jnp.where(t < ctx_tiles, batch, b)) * 6 + j, 0, 0))
    return pl.pallas_call(
        functools.partial(_inproj_ssm_kernel, batch=batch),
        grid=(nt, batch),
        in_specs=[row(D_MODEL), mspec(0), mspec(1), _full((D_MODEL, SSM_IN_PAD))],
        out_specs=[row(SSD_W), row(XBC_W), row(DT_PAD),
                   pl.BlockSpec((S5_W // 128, ROW_TILE * batch, 128), lambda t, b: (0, t, 0))],
        out_shape=[jax.ShapeDtypeStruct((batch, ta, SSD_W), F32),
                   jax.ShapeDtypeStruct((batch, ta, XBC_W), F32),
                   jax.ShapeDtypeStruct((batch, ta, DT_PAD), F32),
                   jax.ShapeDtypeStruct((S5_W // 128, ta * batch, 128), F32)],
        compiler_params=_cparams("arbitrary", "arbitrary"),
        name="inproj_ssm",
    )(h, mods, mods, w_in_pad)


def _cumsum_rows(v):
    n = v.shape[0]
    row = lax.broadcasted_iota(jnp.int32, v.shape, 0)
    s = 1
    while s < n:
        v = v + jnp.where(row >= s, pltpu.roll(v, s, 0), 0.0)
        s *= 2
    return v


def _expand_heads(v, e_ref):
    hi = v.astype(BF16)
    lo = (v - hi.astype(F32)).astype(BF16)
    e = e_ref[...]
    return (jnp.dot(hi, e, preferred_element_type=F32)
            + jnp.dot(lo, e, preferred_element_type=F32))


def _ssd_kernel(xbc_ref, xp_ref, xn_ref, dt_ref, z_ref, cw_ref, cb_ref, alog_ref, dtb_ref,
                dsk_ref, ng_ref, ef_ref, eb_ref, o_ref,
                xs_ref, cd_ref, st_ref, dec_ref, y_ref, *, n_tiles, ctx_tiles):
    s = pl.program_id(1)
    q = SSD_CHUNK
    cpt = ROW_TILE // q
    n_chunks = n_tiles * cpt
    ctx_chunks = ctx_tiles * cpt
    gw = SSD_W // SSD_GROUPS
    hpg = SSD_HEADS // SSD_GROUPS

    @pl.when(s < n_tiles)
    def _phase0():
        p = s
        seg_first = (p == 0) | (p == ctx_tiles)
        seg_last = (p == ctx_tiles - 1) | (p == n_tiles - 1)
        xr = xbc_ref[0]
        prev = jnp.where(seg_first, 0.0, xp_ref[0, HALO - 1:HALO, :])
        nxt = jnp.where(seg_last, 0.0, xn_ref[0, 0:1, :])
        row8 = lax.broadcasted_iota(jnp.int32, (8, XBC_W), 0)
        down = pltpu.roll(xr, 1, 0)
        up = pltpu.roll(xr, ROW_TILE - 1, 0)
        xm1 = jnp.concatenate([jnp.where(row8 == 0, prev, down[0:8]), down[8:]], axis=0)
        xp1 = jnp.concatenate([up[:ROW_TILE - 8], jnp.where(row8 == 7, nxt, up[ROW_TILE - 8:])],
                              axis=0)
        cw = cw_ref[...]
        xs_tile = _silu(cw[0:1] * xm1 + cw[1:2] * xr + cw[2:3] * xp1 + cb_ref[...])
        xs_ref[p] = xs_tile

        raw = dt_ref[0] + dtb_ref[...]
        dt_tile = jnp.maximum(raw, 0.0) + jnp.log1p(jnp.exp(-jnp.abs(raw)))
        a_row = -jnp.exp(alog_ref[...])
        for i in range(cpt):
            c = p * cpt + i
            xs = xs_tile[i * q:(i + 1) * q]
            dtv = dt_tile[i * q:(i + 1) * q]
            adt = dtv * a_row
            cum = _cumsum_rows(adt)
            tot = cum[q - 1:q, :]
            lane = lax.broadcasted_iota(jnp.int32, cum.shape, 1)
            cc = jnp.where(lane < SSD_HEADS, cum, tot - cum + adt)
            cd_ref[c, 0] = cc
            cd_ref[c, 1] = dtv
            w_end = jnp.exp(tot - cc) * dtv
            dec16 = jnp.broadcast_to(jnp.exp(tot), (16, DT_PAD))
            x = xs[:, :SSD_W]
            for d, e_ref in enumerate((ef_ref, eb_ref)):
                wx = (_expand_heads(w_end, e_ref) * x).astype(BF16)
                for g in range(SSD_GROUPS):
                    bmt = xs[:, SSD_W + g * SSD_STATE:SSD_W + (g + 1) * SSD_STATE].T.astype(BF16)
                    st_ref[c, d, :, g * gw:(g + 1) * gw] = jnp.dot(
                        bmt, wx[:, g * gw:(g + 1) * gw], preferred_element_type=F32)
                dec_ref[c, d] = _expand_heads(dec16, e_ref)[0:8]

    @pl.when(s == n_tiles)
    def _recurrence():
        fwd = list(range(n_chunks))
        bwd = list(range(ctx_chunks - 1, -1, -1)) + list(range(n_chunks - 1, ctx_chunks - 1, -1))
        for d, order in enumerate((fwd, bwd)):
            for col0 in range(0, SSD_W, 128):
                cols = slice(col0, col0 + 128)
                state = jnp.zeros((SSD_STATE, 128), F32)
                for ci in order:
                    contrib = st_ref[ci, d, :, cols]
                    st_ref[ci, d, :, cols] = state
                    state = state * dec_ref[ci, d, 0:1, cols] + contrib

    @pl.when(s >= n_tiles)
    def _phase1():
        p = s - n_tiles + ctx_tiles
        rowi = lax.broadcasted_iota(jnp.int32, (q, q), 0)
        coli = lax.broadcasted_iota(jnp.int32, (q, q), 1)
        lower = coli <= rowi
        upper = coli >= rowi
        lane = lax.broadcasted_iota(jnp.int32, (q, 128), 1)
        neg = jnp.float32(-jnp.inf)
        for i in range(cpt):
            c = p * cpt + i
            rows = slice(i * q, (i + 1) * q)
            xs = xs_ref[p, rows, :]
            x = xs[:, :SSD_W]
            cc = cd_ref[c, 0]
            dtv = cd_ref[c, 1]
            cct = cc.T
            dtt = dtv.T
            ecc = jnp.exp(cc)
            ef = _expand_heads(ecc, ef_ref)
            eb = _expand_heads(ecc, eb_ref)
            for g in range(SSD_GROUPS):
                bm = xs[:, SSD_W + g * SSD_STATE:SSD_W + (g + 1) * SSD_STATE].astype(BF16)
                cm = xs[:, SSD_W + (SSD_GROUPS + g) * SSD_STATE:
                        SSD_W + (SSD_GROUPS + g + 1) * SSD_STATE].astype(BF16)
                gmat = lax.dot_general(cm, bm, (((1,), (1,)), ((), ())),
                                       preferred_element_type=F32)
                sl = slice(g * gw, (g + 1) * gw)
                yoff = (ef[:, sl] * jnp.dot(cm, st_ref[c, 0, :, sl].astype(BF16),
                                            preferred_element_type=F32)
                        + eb[:, sl] * jnp.dot(cm, st_ref[c, 1, :, sl].astype(BF16),
                                              preferred_element_type=F32))
                for pair in range(hpg // 2):
                    col0 = g * gw + pair * 128
                    xpair = x[:, col0:col0 + 128].astype(BF16)
                    res = []
                    for hh in range(2):
                        hd = g * hpg + pair * 2 + hh
                        hb_ = SSD_HEADS + hd
                        lf = jnp.exp(jnp.where(lower, cc[:, hd:hd + 1] - cct[hd:hd + 1, :], neg))
                        lb = jnp.exp(jnp.where(upper, cc[:, hb_:hb_ + 1] - cct[hb_:hb_ + 1, :],
                                               neg))
                        mt = gmat * (lf * dtt[hd:hd + 1, :] + lb * dtt[hb_:hb_ + 1, :])
                        res.append(jnp.dot(mt.astype(BF16), xpair, preferred_element_type=F32))
                    ydiag = jnp.where(lane < SSD_HEAD_DIM, res[0], res[1])
                    y_ref[rows, col0:col0 + 128] = (
                        ydiag + yoff[:, pair * 128:(pair + 1) * 128]
                        + dsk_ref[:, col0:col0 + 128] * x[:, col0:col0 + 128])
        gated = y_ref[...] * _silu(z_ref[0])
        normed = gated * lax.rsqrt(jnp.mean(gated * gated, axis=-1, keepdims=True) + LN_EPS)
        o_ref[0] = (normed * ng_ref[...]).astype(BF16)


def _ssd(xbc, dt, z, conv_w, conv_b, a_log, dt_bias, d_skip, norm_g, n_ctx):
    batch, ta, _ = xbc.shape
    q = SSD_CHUNK
    n_chunks = ta // q
    n_tiles = ta // ROW_TILE
    ctx_tiles = n_ctx // ROW_TILE
    hb = ROW_TILE // HALO
    n_hblk = ta // HALO
    pad24 = lambda v: jnp.pad(v.reshape(1, DT_W), ((0, 0), (0, DT_PAD - DT_W)))
    heads = np.arange(SSD_HEADS)
    ef = np.zeros((DT_PAD, SSD_W), np.float32)
    eb = np.zeros((DT_PAD, SSD_W), np.float32)
    for hd in heads:
        ef[hd, hd * SSD_HEAD_DIM:(hd + 1) * SSD_HEAD_DIM] = 1.0
        eb[SSD_HEADS + hd, hd * SSD_HEAD_DIM:(hd + 1) * SSD_HEAD_DIM] = 1.0
    dsk = jnp.repeat(d_skip.astype(F32), SSD_HEAD_DIM).reshape(1, SSD_W)
    ph0 = lambda s: s < n_tiles
    tile = lambda w: pl.BlockSpec(
        (1, ROW_TILE, w), lambda b, s: (b, jnp.where(ph0(s), s, n_tiles - 1), 0))
    return pl.pallas_call(
        functools.partial(_ssd_kernel, n_tiles=n_tiles, ctx_tiles=ctx_tiles),
        grid=(batch, 2 * n_tiles - ctx_tiles),
        in_specs=[tile(XBC_W),
                  pl.BlockSpec((1, HALO, XBC_W),
                               lambda b, s: (b, jnp.where(ph0(s), jnp.maximum(s * hb - 1, 0), 0), 0)),
                  pl.BlockSpec((1, HALO, XBC_W),
                               lambda b, s: (b, jnp.where(ph0(s), jnp.minimum((s + 1) * hb, n_hblk - 1), 0), 0)),
                  tile(DT_PAD),
                  pl.BlockSpec((1, ROW_TILE, SSD_W),
                               lambda b, s: (b, jnp.where(ph0(s), 0, s - n_tiles + ctx_tiles), 0)),
                  _full((3, XBC_W)), _full((1, XBC_W)), _full((1, DT_PAD)), _full((1, DT_PAD)),
                  _full((1, SSD_W)), _full((1, SSD_W)),
                  _full((DT_PAD, SSD_W)), _full((DT_PAD, SSD_W))],
        out_specs=pl.BlockSpec(
            (1, ROW_TILE, SSD_W), lambda b, s: (b, jnp.where(ph0(s), 0, s - n_tiles), 0)),
        out_shape=jax.ShapeDtypeStruct((batch, ta - n_ctx, SSD_W), BF16),
        scratch_shapes=[pltpu.VMEM((n_tiles, ROW_TILE, XBC_W), F32),
                        pltpu.VMEM((n_chunks, 2, q, DT_PAD), F32),
                        pltpu.VMEM((n_chunks, 2, SSD_STATE, SSD_W), F32),
                        pltpu.VMEM((n_chunks, 2, 8, SSD_W), F32),
                        pltpu.VMEM((ROW_TILE, SSD_W), F32)],
        compiler_params=_cparams("parallel", "arbitrary"),
        name="ssd_bidir",
    )(xbc, xbc, xbc, dt, z, conv_w, conv_b.reshape(1, XBC_W), pad24(a_log), pad24(dt_bias),
      dsk, norm_g.reshape(1, SSD_W), jnp.asarray(ef, BF16), jnp.asarray(eb, BF16))


def _s5_disc_kernel(lr_ref, li_ref, ldt_ref, bre_ref, bim_ref, cre_ref, cim_ref,
                    a_ref, bd_ref, cd_ref):
    lr, li = lr_ref[...], li_ref[...]
    dt = jnp.exp(ldt_ref[...])
    mag = jnp.exp(dt * lr)
    ab_re, ab_im = mag * jnp.cos(dt * li), mag * jnp.sin(dt * li)
    den = lr * lr + li * li
    k_re = ((ab_re - 1.0) * lr + ab_im * li) / den
    k_im = (ab_im * lr - (ab_re - 1.0) * li) / den
    bre, bim = bre_ref[...], bim_ref[...]
    for d in range(2):
        a_ref[d, :, 0:S5_NSTATE] = jnp.broadcast_to(ab_re[d:d + 1], (8, S5_NSTATE))
        a_ref[d, :, S5_NSTATE:] = jnp.broadcast_to(ab_im[d:d + 1], (8, S5_NSTATE))
        kr, ki = k_re[d:d + 1], k_im[d:d + 1]
        bd_ref[d, :, 0:S5_NSTATE] = (kr * bre - ki * bim).astype(BF16)
        bd_ref[d, :, S5_NSTATE:] = (kr * bim + ki * bre).astype(BF16)
        cd_ref[d, 0:S5_NSTATE, :] = cre_ref[d].astype(BF16)
        cd_ref[d, S5_NSTATE:, :] = (-cim_ref[d]).astype(BF16)


def _s5_discretize(lam_re, lam_im, log_dt, b_re, b_im, c_re, c_im):
    eye = jnp.eye(S5_GROUPS, dtype=F32)
    bd = lambda b: jnp.einsum('gph,gk->ghkp', b, eye).reshape(S5_W, S5_NSTATE)
    cd = lambda cc: jnp.einsum('dghp,gk->dgpkh', cc, eye).reshape(2, S5_NSTATE, S5_W)
    ldt = jnp.repeat(log_dt, S5_STATE, axis=-1)
    return pl.pallas_call(
        _s5_disc_kernel,
        out_shape=[jax.ShapeDtypeStruct((2, 8, 2 * S5_NSTATE), F32),
                   jax.ShapeDtypeStruct((2, S5_W, 2 * S5_NSTATE), BF16),
                   jax.ShapeDtypeStruct((2, 2 * S5_NSTATE, S5_W), BF16)],
        compiler_params=pltpu.CompilerParams(vmem_limit_bytes=VMEM_LIMIT_BYTES),
        name="s5_discretize",
    )(lam_re.reshape(2, S5_NSTATE), lam_im.reshape(2, S5_NSTATE), ldt,
      bd(b_re), bd(b_im), cd(c_re), cd(c_im))


S5_TIME_CHUNK = 128
S5_UNROLL = 8


def _s5_scan_kernel(u_ref, a_ref, bd_ref, cd_ref, o_ref, hs_ref, carry_ref, *, batch):
    d = pl.program_id(0)
    j = pl.program_id(1)
    n = S5_NSTATE

    @pl.when(j == 0)
    def _():
        carry_ref[...] = jnp.zeros_like(carry_ref)

    half = hs_ref.shape[0] // 2
    n_slab = S5_W // 128
    for r in (0, half):
        u = jnp.concatenate([u_ref[s, r:r + half, :] for s in range(n_slab)], axis=1)
        hs_ref[r:r + half, :] = jnp.dot(u.astype(BF16), bd_ref[0], preferred_element_type=F32)
    ar = a_ref[0, :, 0:n]
    ai = a_ref[0, :, n:]
    if batch != 8:
        ar = jnp.broadcast_to(ar[0:1], (batch, n))
        ai = jnp.broadcast_to(ai[0:1], (batch, n))

    def body(i, carry):
        hr, hi = carry
        for s in range(S5_UNROLL):
            step = i * S5_UNROLL + s
            step = jnp.where(d == 0, step, S5_TIME_CHUNK - 1 - step)
            rows = pl.ds(pl.multiple_of(step * batch, batch), batch)
            nr = ar * hr - ai * hi + hs_ref[rows, 0:n]
            ni = ar * hi + ai * hr + hs_ref[rows, n:]
            hs_ref[rows, 0:n] = nr
            hs_ref[rows, n:] = ni
            hr, hi = nr, ni
        return hr, hi

    hr, hi = lax.fori_loop(0, S5_TIME_CHUNK // S5_UNROLL, body,
                           (carry_ref[:, 0:n], carry_ref[:, n:]))
    carry_ref[:, 0:n] = hr
    carry_ref[:, n:] = hi
    for r in (0, half):
        y = jnp.dot(hs_ref[r:r + half, :].astype(BF16), cd_ref[0], preferred_element_type=F32)
        for s in range(n_slab):
            o_ref[0, s, r:r + half, :] = y[:, s * 128:(s + 1) * 128]


def _s5_scan(us_flat, a, bd, cd, batch, n_ctx):
    n_slab, rows_total, _ = us_flat.shape
    ta = rows_total // batch
    tc = S5_TIME_CHUNK
    n_chunks = ta // tc
    ctx_chunks = n_ctx // tc
    blk = tc * batch

    def chunk_of(d, j):
        bwd = jnp.where(j < ctx_chunks, ctx_chunks - 1 - j, n_chunks - 1 - (j - ctx_chunks))
        return jnp.where(d == 0, j, bwd)

    return pl.pallas_call(
        functools.partial(_s5_scan_kernel, batch=batch),
        grid=(2, n_chunks),
        in_specs=[pl.BlockSpec((n_slab, blk, 128), lambda d, j: (0, chunk_of(d, j), 0)),
                  pl.BlockSpec((1, 8, 2 * S5_NSTATE), lambda d, j: (d, 0, 0)),
                  pl.BlockSpec((1, S5_W, 2 * S5_NSTATE), lambda d, j: (d, 0, 0)),
                  pl.BlockSpec((1, 2 * S5_NSTATE, S5_W), lambda d, j: (d, 0, 0))],
        out_specs=pl.BlockSpec((1, n_slab, blk, 128), lambda d, j: (d, 0, chunk_of(d, j), 0)),
        out_shape=jax.ShapeDtypeStruct((2, n_slab, rows_total, 128), F32),
        scratch_shapes=[pltpu.VMEM((blk, 2 * S5_NSTATE), F32),
                        pltpu.VMEM((batch, 2 * S5_NSTATE), F32)],
        compiler_params=_cparams("arbitrary", "arbitrary"),
        name="s5_scan",
    )(us_flat, a, bd, cd)


def _merge_ln_kernel(gs_ref, y5_ref, us_ref, h_ref, gate_ref, dd_ref, gw_ref, gb_ref,
                     w_ref, g_ref, beta_ref, o_ref, *, batch):
    rows = _time_major_rows(pl.program_id(1), batch)
    y5 = jnp.concatenate([y5_ref[0, s, rows, :] + y5_ref[1, s, rows, :]
                          for s in range(S5_W // 128)], axis=1)
    us = jnp.concatenate([us_ref[s, rows, :] for s in range(S5_W // 128)], axis=1)
    ge = jax.nn.gelu(y5 + dd_ref[...] * us)
    s5 = ge * jax.nn.sigmoid(
        jnp.dot(ge.astype(BF16), gw_ref[...], preferred_element_type=F32) + gb_ref[...])
    y = jnp.dot(gs_ref[0], w_ref[0:SSD_W, :], preferred_element_type=F32)
    y = y + jnp.dot(s5.astype(BF16), w_ref[SSD_W:, :], preferred_element_type=F32)
    o_ref[0] = _layer_norm(ALPHA * h_ref[0] + gate_ref[0] * y, g_ref[...], beta_ref[...])


def _merge_ln(g_ssd, y5, us_t, h, mods, layer, s5_d, glu_w, glu_b, w_out, ln_g, ln_b, ctx_tiles):
    batch, ta, _ = h.shape
    nt = ta // ROW_TILE - ctx_tiles
    n_slab = S5_W // 128
    tm_rows = ROW_TILE * batch
    return pl.pallas_call(
        functools.partial(_merge_ln_kernel, batch=batch),
        grid=(nt, batch),
        in_specs=[pl.BlockSpec((1, ROW_TILE, SSD_W), lambda t, b: (b, t, 0)),
                  pl.BlockSpec((2, n_slab, tm_rows, 128), lambda t, b: (0, 0, t + ctx_tiles, 0)),
                  pl.BlockSpec((n_slab, tm_rows, 128), lambda t, b: (0, t + ctx_tiles, 0)),
                  pl.BlockSpec((1, ROW_TILE, D_MODEL), lambda t, b: (b, t + ctx_tiles, 0)),
                  pl.BlockSpec((1, 1, D_MODEL), lambda t, b: ((layer * MOD_ROWS + b) * 6 + 2, 0, 0)),
                  _full((1, S5_W)), _full((S5_W, S5_W)), _full((1, S5_W)),
                  _full((D_MODEL, D_MODEL)), _full((1, D_MODEL)), _full((1, D_MODEL))],
        out_specs=pl.BlockSpec((1, ROW_TILE, D_MODEL), lambda t, b: (b, t, 0)),
        out_shape=jax.ShapeDtypeStruct((batch, nt * ROW_TILE, D_MODEL), F32),
        compiler_params=_cparams("arbitrary", "arbitrary"),
        name="merge_outproj_ln",
    )(g_ssd, y5, us_t, h, mods, s5_d.reshape(1, S5_W), glu_w.astype(BF16),
      glu_b.reshape(1, S5_W), w_out, ln_g.reshape(1, -1), ln_b.reshape(1, -1))


def _attn_layer(h_ctx, h_lat, mods, layer, i, p, keep_ctx):
    n_ctx = h_ctx.shape[1]
    ctx_tiles = n_ctx // ROW_TILE
    lam_init = 0.8 - 0.6 * math.exp(-0.3 * layer)
    cos, sin = _rope_tables(h_lat.shape[1], n_ctx)
    q, k, v, f = _inproj_attn(h_ctx, h_lat, mods, layer, p['attn_w_in'][i].astype(BF16),
                              cos, sin, ctx_tiles)
    o_ctx, o_lat = _attention(q, k, v, p['attn_lambda'][i], p['attn_subln_g'][i], lam_init,
                              ctx_tiles)
    fm = _fourier(f, p['fourier_w'][i], p['fourier_b'][i], n_ctx)
    h1 = _outproj_ln(o_ctx, o_lat, fm, h_ctx, h_lat, mods, layer,
                     p['attn_w_out'][i].astype(BF16),
                     p['ln_g'][layer, 0], p['ln_b'][layer, 0], ctx_tiles)
    return _ffn(h1, mods, layer, p['ffn_w_up'][layer].astype(BF16), p['ffn_b_up'][layer],
                p['ffn_conv_w'][layer], p['ffn_conv_b'][layer],
                p['ffn_w_down'][layer].astype(BF16), p['ffn_b_down'][layer],
                p['ln_g'][layer, 1], p['ln_b'][layer, 1], ctx_tiles, 0 if keep_ctx else ctx_tiles)


def _ssm_layer(h, mods, layer, i, n_ctx, p, keep_ctx):
    assert not keep_ctx, "an SSM layer that must also emit context rows is not implemented"
    ctx_tiles = n_ctx // ROW_TILE
    batch = h.shape[0]
    w = p['ssm_w_in'][i]
    w_pad = jnp.concatenate(
        [w[:, :SSD_W + XBC_W], w[:, SSD_W + XBC_W + DT_W:], w[:, SSD_W + XBC_W:SSD_W + XBC_W + DT_W],
         jnp.zeros((D_MODEL, DT_PAD - DT_W), F32)], axis=1).astype(BF16)
    z, xbc, dt, us_t = _inproj_ssm(h, mods, layer, w_pad, ctx_tiles)
    g_ssd = _ssd(xbc, dt, z, p['ssd_conv_w'][i], p['ssd_conv_b'][i], p['ssd_a_log'][i],
                 p['ssd_dt_bias'][i], p['ssd_d'][i], p['ssd_norm_g'][i], n_ctx)
    a, bd, cd = _s5_discretize(p['s5_lambda_re'][i], p['s5_lambda_im'][i], p['s5_log_dt'][i],
                               p['s5_b_re'][i], p['s5_b_im'][i], p['s5_c_re'][i], p['s5_c_im'][i])
    y5 = _s5_scan(us_t, a, bd, cd, batch, n_ctx)
    h1 = _merge_ln(g_ssd, y5, us_t, h, mods, layer, p['s5_d'][i], p['s5_glu_w'][i],
                   p['s5_glu_b'][i], p['ssm_w_out'][i].astype(BF16),
                   p['ln_g'][layer, 0], p['ln_b'][layer, 0], ctx_tiles)
    return _ffn(h1, mods, layer, p['ffn_w_up'][layer].astype(BF16), p['ffn_b_up'][layer],
                p['ffn_conv_w'][layer], p['ffn_conv_b'][layer],
                p['ffn_w_down'][layer].astype(BF16), p['ffn_b_down'][layer],
                p['ln_g'][layer, 1], p['ln_b'][layer, 1], 0, 0, rows=FFN_LAT_ROWS)


def kernel(x, c, ctx, c_ctx, ada_w, ada_b, ln_g, ln_b, ffn_w_up, ffn_b_up, ffn_conv_w, ffn_conv_b, ffn_w_down, ffn_b_down, attn_w_in, attn_lambda, attn_subln_g, fourier_w, fourier_b, attn_w_out, ssm_w_in, ssd_conv_w, ssd_conv_b, ssd_a_log, ssd_dt_bias, ssd_d, ssd_norm_g, s5_lambda_re, s5_lambda_im, s5_log_dt, s5_b_re, s5_b_im, s5_c_re, s5_c_im, s5_d, s5_glu_w, s5_glu_b, ssm_w_out):
    p = dict(ln_g=ln_g, ln_b=ln_b, ffn_w_up=ffn_w_up, ffn_b_up=ffn_b_up, ffn_conv_w=ffn_conv_w,
             ffn_conv_b=ffn_conv_b, ffn_w_down=ffn_w_down, ffn_b_down=ffn_b_down,
             attn_w_in=attn_w_in, attn_lambda=attn_lambda, attn_subln_g=attn_subln_g,
             fourier_w=fourier_w, fourier_b=fourier_b, attn_w_out=attn_w_out, ssm_w_in=ssm_w_in,
             ssd_conv_w=ssd_conv_w, ssd_conv_b=ssd_conv_b, ssd_a_log=ssd_a_log,
             ssd_dt_bias=ssd_dt_bias, ssd_d=ssd_d, ssd_norm_g=ssd_norm_g,
             s5_lambda_re=s5_lambda_re, s5_lambda_im=s5_lambda_im, s5_log_dt=s5_log_dt,
             s5_b_re=s5_b_re, s5_b_im=s5_b_im, s5_c_re=s5_c_re, s5_c_im=s5_c_im, s5_d=s5_d,
             s5_glu_w=s5_glu_w, s5_glu_b=s5_glu_b, ssm_w_out=ssm_w_out)
    batch, n_lat, _ = x.shape
    n_ctx = ctx.shape[1]
    assert n_ctx == ROW_TILE and n_lat % ROW_TILE == 0 and batch < MOD_ROWS
    mods = _ada_mods(c, c_ctx, ada_w, ada_b)
    assert DEPTH == 2
    h = _attn_layer(ctx, x, mods, 0, 0, p, keep_ctx=True)
    return _ssm_layer(h, mods, 1, 0, n_ctx, p, keep_ctx=False)
```

```python
import functools
import math

import numpy as np
import jax
import jax.numpy as jnp
from jax import lax
from jax.experimental import pallas as pl
from jax.experimental.pallas import tpu as pltpu

F32 = jnp.float32
BF16 = jnp.bfloat16

D_MODEL = 1024
DEPTH = 2
GRID_W = 64
ROPE_BASE = 10000.0
LN_EPS = 1e-5
ALPHA = (2 * DEPTH) ** 0.25
ATTN_W = 768
ATTN_HEADS = 6
ATTN_HEAD_DIM = 64
F_W = 256
F_GROUPS = 4
F_GROUP_W = 64
ATTN_IN_W = 2 * ATTN_W + ATTN_W + F_W
SSD_W = 768
SSD_HEADS = 12
SSD_HEAD_DIM = 64
SSD_GROUPS = 2
SSD_STATE = 128
SSD_CHUNK = 128
XBC_W = SSD_W + 2 * SSD_GROUPS * SSD_STATE
DT_W = 2 * SSD_HEADS
DT_PAD = 128
S5_W = 256
S5_GROUPS = 16
S5_GROUP_W = 16
S5_STATE = 64
S5_NSTATE = S5_GROUPS * S5_STATE
SSM_IN_W = SSD_W + XBC_W + DT_W + S5_W
SSM_IN_PAD = SSD_W + XBC_W + S5_W + DT_PAD
D_FF = 2816
FF_TILE = 256

ROW_TILE = 256
ATTN_Q_SUB = 128
ATTN_ITEM_TILES = 2
FFN_LAT_ROWS = 512
HALO = 8
MOD_ROWS = 16
VMEM_LIMIT_BYTES = 56 * 1024 * 1024


def _cparams(*sem):
    return pltpu.CompilerParams(dimension_semantics=sem, vmem_limit_bytes=VMEM_LIMIT_BYTES)


def _silu(v):
    return v * jax.nn.sigmoid(v)


def _layer_norm(v, g, b):
    mu = jnp.mean(v, axis=-1, keepdims=True)
    d = v - mu
    var = jnp.mean(d * d, axis=-1, keepdims=True)
    return d * lax.rsqrt(var + LN_EPS) * g + b


def _full(shape):
    nd = len(shape)
    return pl.BlockSpec(shape, lambda *_: (0,) * nd)


def _resident(shape):
    nd = len(shape)
    return pl.BlockSpec(shape, lambda *_: (0,) * nd, pipeline_mode=pl.Buffered(1))


def _first_step():
    return (pl.program_id(0) == 0) & (pl.program_id(1) == 0)


def _mod_spec(layer, j, batch, ctx_tiles):
    def idx(b, t):
        row = jnp.where(t < ctx_tiles, batch, b)
        return ((layer * MOD_ROWS + row) * 6 + j, 0, 0)
    return pl.BlockSpec((1, 1, D_MODEL), idx)


def _ada_kernel(c_ref, w_ref, b_ref, o_ref):
    s = _silu(c_ref[...])
    w = w_ref[0]
    s_hi = s.astype(BF16)
    s_lo = (s - s_hi.astype(F32)).astype(BF16)
    w_hi = w.astype(BF16)
    w_lo = (w - w_hi.astype(F32)).astype(BF16)
    dot = functools.partial(jnp.dot, preferred_element_type=F32)
    o_ref[0] = dot(s_hi, w_hi) + (dot(s_lo, w_hi) + dot(s_hi, w_lo)) + b_ref[0]


def _ada_mods(c, c_ctx, ada_w, ada_b):
    batch = c.shape[0]
    nl = ada_w.shape[0]
    c_all = jnp.concatenate(
        [c, c_ctx[None], jnp.zeros((MOD_ROWS - batch - 1, D_MODEL), F32)], axis=0)
    out = pl.pallas_call(
        _ada_kernel,
        grid=(nl, 6),
        in_specs=[_full((MOD_ROWS, D_MODEL)),
                  pl.BlockSpec((1, D_MODEL, D_MODEL), lambda l, j: (l, 0, j)),
                  pl.BlockSpec((1, 1, D_MODEL), lambda l, j: (l, 0, j))],
        out_specs=pl.BlockSpec((1, MOD_ROWS, D_MODEL), lambda l, j: (l, 0, j)),
        out_shape=jax.ShapeDtypeStruct((nl, MOD_ROWS, 6 * D_MODEL), F32),
        compiler_params=_cparams("arbitrary", "arbitrary"),
        name="ada_mods",
    )(c_all, ada_w, ada_b.reshape(nl, 1, 6 * D_MODEL))
    return out.reshape(nl * MOD_ROWS * 6, 1, D_MODEL)


def _inproj_attn_kernel(hc_ref, hl_ref, shift_ref, scale_ref, w_ref, cos_ref, sin_ref,
                        q_ref, k_ref, v_ref, f_ref, wbf_ref, *, ctx_tiles):
    @pl.when(_first_step())
    def _():
        wbf_ref[...] = w_ref[...].astype(BF16)

    h = jnp.where(pl.program_id(1) < ctx_tiles, hc_ref[0], hl_ref[0])
    u = (h * (1.0 + scale_ref[0]) + shift_ref[0]).astype(BF16)
    p = jnp.dot(u, wbf_ref[...], preferred_element_type=F32)
    cos = cos_ref[...]
    sin = sin_ref[...]
    lane = lax.broadcasted_iota(jnp.int32, cos.shape, 1)
    first_half = (lane % ATTN_HEAD_DIM) < (ATTN_HEAD_DIM // 2)

    def rope(blk):
        partner = jnp.where(first_half, pltpu.roll(blk, 128 - 32, 1), pltpu.roll(blk, 32, 1))
        return blk * cos + partner * sin

    qk_scale = ATTN_HEAD_DIM ** -0.5 * math.log2(math.e)
    for i in range(ATTN_HEADS):
        lo, hi = i * 128, (i + 1) * 128
        q_ref[0, :, lo:hi] = (rope(p[:, lo:hi]) * qk_scale).astype(BF16)
        k_ref[0, :, lo:hi] = rope(p[:, ATTN_W + lo:ATTN_W + hi]).astype(BF16)
    v_ref[0] = p[:, 2 * ATTN_W:3 * ATTN_W].astype(BF16)
    f_ref[0] = p[:, 3 * ATTN_W:].astype(BF16)


def _rope_tables(n_lat, n_ctx):
    rows = n_lat // GRID_W
    row = jnp.repeat(jnp.arange(rows, dtype=F32), GRID_W)
    col = jnp.tile(jnp.arange(GRID_W, dtype=F32), rows)
    n_freq = ATTN_HEAD_DIM // 4
    inv_freq = ROPE_BASE ** (-jnp.arange(n_freq, dtype=F32) / n_freq)
    ang = jnp.concatenate([row[:, None] * inv_freq, col[:, None] * inv_freq], axis=-1)
    cos, sin = jnp.cos(ang), jnp.sin(ang)
    cos128 = jnp.tile(cos, (1, 4))
    sin128 = jnp.tile(jnp.concatenate([-sin, sin], axis=-1), (1, 2))
    cos_all = jnp.concatenate([jnp.ones((n_ctx, 128), F32), cos128], axis=0)
    sin_all = jnp.concatenate([jnp.zeros((n_ctx, 128), F32), sin128], axis=0)
    return cos_all, sin_all


def _split_specs(width, ctx_tiles):
    return [pl.BlockSpec((1, ROW_TILE, width), lambda b, t: (b, jnp.minimum(t, ctx_tiles - 1), 0)),
            pl.BlockSpec((1, ROW_TILE, width), lambda b, t: (b, jnp.maximum(t - ctx_tiles, 0), 0))]


def _inproj_attn(h_ctx, h_lat, mods, layer, w_in, cos, sin, ctx_tiles):
    batch = h_lat.shape[0]
    ta = h_ctx.shape[1] + h_lat.shape[1]
    nt = ta // ROW_TILE
    row = lambda w: pl.BlockSpec((1, ROW_TILE, w), lambda b, t: (b, t, 0))
    tab = pl.BlockSpec((ROW_TILE, 128), lambda b, t: (t, 0))
    return pl.pallas_call(
        functools.partial(_inproj_attn_kernel, ctx_tiles=ctx_tiles),
        grid=(batch, nt),
        in_specs=_split_specs(D_MODEL, ctx_tiles)
        + [_mod_spec(layer, 0, batch, ctx_tiles), _mod_spec(layer, 1, batch, ctx_tiles),
           _resident((D_MODEL, ATTN_IN_W)), tab, tab],
        out_specs=[row(ATTN_W), row(ATTN_W), row(ATTN_W), row(F_W)],
        out_shape=[jax.ShapeDtypeStruct((batch, ta, ATTN_W), BF16)] * 3
        + [jax.ShapeDtypeStruct((batch, ta, F_W), BF16)],
        scratch_shapes=[pltpu.VMEM((D_MODEL, ATTN_IN_W), BF16)],
        compiler_params=_cparams("arbitrary", "arbitrary"),
        name="inproj_attn",
    )(h_ctx, h_lat, mods, mods, w_in, cos, sin)


def _diff_lambda(lam_ref, lam_init):
    lamv = lam_ref[...]
    l1 = jnp.sum(lamv[0:1] * lamv[1:2], axis=-1, keepdims=True)
    l2 = jnp.sum(lamv[2:3] * lamv[3:4], axis=-1, keepdims=True)
    return jnp.exp(l1) - jnp.exp(l2) + lam_init


def _stack_maps(q):
    lane = lax.broadcasted_iota(jnp.int32, q.shape, 1)
    zero = jnp.zeros_like(q)
    return jnp.concatenate([jnp.where(lane < ATTN_HEAD_DIM, q, zero),
                            jnp.where(lane >= ATTN_HEAD_DIM, q, zero)], axis=0)


def _scores(q2, k):
    return lax.dot_general(q2, k, (((1,), (1,)), ((), ())), preferred_element_type=F32)


def _diff_softmax_pv(load_s0, m0, load_s1, m1, lam, v, gain):
    e0 = jnp.exp2(load_s0() - m0)
    e1 = jnp.exp2(load_s1() - m1)
    l0 = jnp.sum(e0, axis=-1, keepdims=True)
    l1 = jnp.sum(e1, axis=-1, keepdims=True)
    w = e0 - e1 * (lam * l0 / l1)
    o = jnp.dot(w.astype(BF16), v, preferred_element_type=F32) * (1.0 / l0)
    o = o * lax.rsqrt(jnp.mean(o * o, axis=-1, keepdims=True) + LN_EPS)
    return (o * gain).astype(BF16)


def _attn_ctx_kernel(lam_ref, g_ref, q_ref, k_ref, v_ref, o_ref, *, lam_init):
    lam = _diff_lambda(lam_ref, lam_init)
    gain = g_ref[...] * (1.0 - lam_init)
    tq = q_ref.shape[1]
    for hd in range(ATTN_HEADS):
        cols = slice(hd * 128, (hd + 1) * 128)
        s = _scores(_stack_maps(q_ref[0, :, cols]), k_ref[0, :, cols])
        m = jnp.max(s, axis=-1, keepdims=True)

        def half(lo, s=s):
            return lambda: s[lo:lo + tq]

        o_ref[0, :, cols] = _diff_softmax_pv(half(0), m[:tq], half(tq), m[tq:], lam,
                                             v_ref[0, :, cols], gain)


def _attn_lat_kernel(lam_ref, g_ref, k_ref, v_ref, *rest, lam_init):
    q_refs = rest[:ATTN_ITEM_TILES]
    o_ref, sa_ref, ma_ref, sb_ref, mb_ref = rest[ATTN_ITEM_TILES:]
    t = pl.program_id(0)
    tq = ATTN_ITEM_TILES * ROW_TILE
    sub = ATTN_Q_SUB

    @pl.when(t == 0)
    def _():
        sb_ref[...] = jnp.zeros_like(sb_ref)
        mb_ref[...] = jnp.zeros_like(mb_ref)

    def step(s_new, m_new, s_old, m_old):
        lam = _diff_lambda(lam_ref, lam_init)
        gain = g_ref[...] * (1.0 - lam_init)
        q = jnp.concatenate([q_ref[0] for q_ref in q_refs], axis=0)
        s = _scores(_stack_maps(q), k_ref[0])
        s_new[...] = s
        m_new[...] = jnp.max(s, axis=-1, keepdims=True)
        v = v_ref[0]
        for i in range(tq // sub):
            r0 = slice(i * sub, (i + 1) * sub)
            r1 = slice(tq + i * sub, tq + (i + 1) * sub)
            o_ref[0, r0, :] = _diff_softmax_pv(
                functools.partial(s_old.__getitem__, (r0, slice(None))), m_old[r0, :],
                functools.partial(s_old.__getitem__, (r1, slice(None))), m_old[r1, :],
                lam, v, gain)

    @pl.when(t % 2 == 0)
    def _():
        step(sa_ref, ma_ref, sb_ref, mb_ref)

    @pl.when(t % 2 == 1)
    def _():
        step(sb_ref, mb_ref, sa_ref, ma_ref)


def _attention(q, k, v, lam_vec, subln_g, lam_init, ctx_tiles):
    batch, ta, _ = q.shape
    n_ctx = ctx_tiles * ROW_TILE
    nt = ta // ROW_TILE - ctx_tiles
    g = subln_g.reshape(1, 128)
    small = [_full((4, ATTN_HEAD_DIM)), _full((1, 128))]
    cspec = pl.BlockSpec((1, n_ctx, ATTN_W), lambda b: (b, 0, 0))
    o_ctx = pl.pallas_call(
        functools.partial(_attn_ctx_kernel, lam_init=lam_init),
        grid=(batch,),
        in_specs=small + [cspec, cspec, cspec],
        out_specs=cspec,
        out_shape=jax.ShapeDtypeStruct((batch, n_ctx, ATTN_W), BF16),
        compiler_params=_cparams("arbitrary"),
        name="diff_attention_ctx",
    )(lam_vec, g, q, k, v)

    per = ATTN_ITEM_TILES
    assert nt % per == 0
    ni = nt // per
    n_items = batch * ATTN_HEADS * ni

    def item(j):
        return j // (ATTN_HEADS * ni), (j // ni) % ATTN_HEADS, j % ni

    def score_item(j):
        return item(jnp.minimum(j, n_items - 1))

    def finish_item(j):
        return item(jnp.maximum(j - 1, 0))

    def q_spec(part):
        def idx(j):
            b, h, t = score_item(j)
            return b, ctx_tiles + t * per + part, h
        return pl.BlockSpec((1, ROW_TILE, 128), idx)

    def k_idx(j):
        b, h, _ = score_item(j)
        return b, 0, h

    def v_idx(j):
        b, h, _ = finish_item(j)
        return b, 0, h

    def o_idx(j):
        b, h, t = finish_item(j)
        return b, t, h

    o_lat = pl.pallas_call(
        functools.partial(_attn_lat_kernel, lam_init=lam_init),
        grid=(n_items + 1,),
        in_specs=small + [pl.BlockSpec((1, ta, 128), k_idx), pl.BlockSpec((1, ta, 128), v_idx)]
        + [q_spec(part) for part in range(per)],
        out_specs=pl.BlockSpec((1, per * ROW_TILE, 128), o_idx),
        out_shape=jax.ShapeDtypeStruct((batch, nt * ROW_TILE, ATTN_W), BF16),
        scratch_shapes=[pltpu.VMEM((2 * per * ROW_TILE, ta), F32),
                        pltpu.VMEM((2 * per * ROW_TILE, 1), F32)] * 2,
        compiler_params=_cparams("arbitrary"),
        name="diff_attention",
    )(lam_vec, g, k, v, *([q] * per))
    return o_ctx, o_lat


def _dft_tables(n):
    k = np.arange(n, dtype=np.int64)
    ang = 2.0 * np.pi * ((k[:, None] * k[None, :]) % n).astype(np.float64) / n
    return np.cos(ang), np.sin(ang)


def _fourier_kernel(f_ref, cs_ref, dl_ref, dc_ref, w_ref, b_ref, o_ref, ac_ref, al_ref,
                    *, n_ctx, n_lat):
    t = pl.program_id(0)
    b = pl.program_id(1)
    ctx_tiles = n_ctx // ROW_TILE

    def stage1(rows0, n):
        a = jnp.dot(f_ref[0, rows0:rows0 + n, :], cs_ref[...], preferred_element_type=F32)
        return a[:, :F_W].astype(BF16), a[:, F_W:].astype(BF16)

    def stage2(dft, a, n):
        z = jnp.dot(dft, a, preferred_element_type=F32)
        z = z * (1.0 / math.sqrt(n * F_GROUP_W))
        o = jnp.dot(z.astype(BF16), w_ref[...], preferred_element_type=F32) + b_ref[...]
        o_ref[0] = o.astype(BF16)

    @pl.when(t < ctx_tiles)
    def _():
        ac_ref[0:n_ctx, :], ac_ref[n_ctx:, :] = stage1(0, n_ctx)
        stage2(dc_ref[...], ac_ref[...], n_ctx)

    @pl.when(t == ctx_tiles)
    def _():
        al_ref[b, 0:n_lat, :], al_ref[b, n_lat:, :] = stage1(n_ctx, n_lat)

    @pl.when(t >= ctx_tiles)
    def _():
        stage2(dl_ref[...], al_ref[b], n_lat)


def _fourier(f, fourier_w, fourier_b, n_ctx):
    batch, ta, _ = f.shape
    n_lat = ta - n_ctx
    assert n_ctx == ROW_TILE
    nt = ta // ROW_TILE
    ctx_tiles = n_ctx // ROW_TILE
    cc, sc = _dft_tables(F_GROUP_W)
    eye = np.eye(F_GROUPS)
    cs = jnp.asarray(np.concatenate([np.kron(eye, cc), np.kron(eye, sc)], axis=1), BF16)
    cl, sl = _dft_tables(n_lat)
    dft_lat = jnp.asarray(np.concatenate([cl, -sl], axis=1), BF16)
    cx, sx = _dft_tables(n_ctx)
    dft_ctx = jnp.asarray(np.concatenate([cx, -sx], axis=1), BF16)
    w_blk = jnp.einsum('gce,gh->gche', fourier_w, jnp.eye(F_GROUPS, dtype=F32))
    w_blk = w_blk.reshape(F_W, F_W).astype(BF16)
    return pl.pallas_call(
        functools.partial(_fourier_kernel, n_ctx=n_ctx, n_lat=n_lat),
        grid=(nt, batch),
        in_specs=[
                  pl.BlockSpec((1, ta, F_W),
                               lambda t, b: (jnp.where(t <= ctx_tiles, b, batch - 1), 0, 0)),
                  _full((F_W, 2 * F_W)),
                  pl.BlockSpec((ROW_TILE, 2 * n_lat),
                               lambda t, b: (jnp.maximum(t - ctx_tiles, 0), 0)),
                  _full((n_ctx, 2 * n_ctx)),
                  _full((F_W, F_W)), _full((1, F_W))],
        out_specs=pl.BlockSpec((1, ROW_TILE, F_W), lambda t, b: (b, t, 0)),
        out_shape=jax.ShapeDtypeStruct((batch, ta, F_W), BF16),
        scratch_shapes=[pltpu.VMEM((2 * n_ctx, F_W), BF16),
                        pltpu.VMEM((batch, 2 * n_lat, F_W), BF16)],
        compiler_params=_cparams("arbitrary", "arbitrary"),
        name="fourier_mix",
    )(f, cs, dft_lat, dft_ctx, w_blk, fourier_b.reshape(1, F_W))


def _outproj_ln_kernel(ac_ref, al_ref, b_ref, hc_ref, hl_ref, gate_ref, w_ref, g_ref, beta_ref,
                       o_ref, wbf_ref, *, ctx_tiles):
    @pl.when(_first_step())
    def _():
        wbf_ref[...] = w_ref[...].astype(BF16)

    is_ctx = pl.program_id(1) < ctx_tiles
    a = jnp.where(is_ctx, ac_ref[0], al_ref[0])
    h = jnp.where(is_ctx, hc_ref[0], hl_ref[0])
    wa = a.shape[1]
    y = jnp.dot(a, wbf_ref[0:wa, :], preferred_element_type=F32)
    y = y + jnp.dot(b_ref[0], wbf_ref[wa:, :], preferred_element_type=F32)
    o_ref[0] = _layer_norm(ALPHA * h + gate_ref[0] * y, g_ref[...], beta_ref[...])


def _outproj_ln(a_ctx, a_lat, b2, h_ctx, h_lat, mods, layer, w_out, ln_g, ln_b, ctx_tiles):
    batch, ta, _ = b2.shape
    nt = ta // ROW_TILE
    row = lambda w: pl.BlockSpec((1, ROW_TILE, w), lambda b, t: (b, t, 0))
    return pl.pallas_call(
        functools.partial(_outproj_ln_kernel, ctx_tiles=ctx_tiles),
        grid=(batch, nt),
        in_specs=_split_specs(a_lat.shape[2], ctx_tiles) + [row(b2.shape[2])]
        + _split_specs(D_MODEL, ctx_tiles)
        + [_mod_spec(layer, 2, batch, ctx_tiles),
           _resident((D_MODEL, D_MODEL)), _full((1, D_MODEL)), _full((1, D_MODEL))],
        out_specs=row(D_MODEL),
        out_shape=jax.ShapeDtypeStruct((batch, ta, D_MODEL), F32),
        scratch_shapes=[pltpu.VMEM((D_MODEL, D_MODEL), BF16)],
        compiler_params=_cparams("arbitrary", "arbitrary"),
        name="outproj_ln",
    )(a_ctx, a_lat, b2, h_ctx, h_lat, mods, w_out, ln_g.reshape(1, D_MODEL),
      ln_b.reshape(1, D_MODEL))


def _ffn_kernel(h_ref, hp_ref, hn_ref, shift_ref, scale_ref, gate_ref,
                wup_ref, bup_ref, cw_ref, cb_ref, wdn_ref, bdn_ref, g_ref, beta_ref,
                o_ref, uext_ref, ubf_ref, act_ref, *, ctx_tiles, tile_off, nt_seq):
    t = pl.program_id(1) + tile_off
    seg_first = (t == 0) | (t == ctx_tiles)
    seg_last = (t == nt_seq - 1) | (t == ctx_tiles - 1)
    sc = 1.0 + scale_ref[0]
    sh = shift_ref[0]
    h = h_ref[0]
    tm = h.shape[0]
    uext_ref[0:HALO, :] = hp_ref[0] * sc + sh
    uext_ref[HALO:HALO + tm, :] = h * sc + sh
    uext_ref[HALO + tm:, :] = hn_ref[0] * sc + sh
    ubf_ref[...] = uext_ref[...].astype(BF16)
    row8 = lax.broadcasted_iota(jnp.int32, (8, FF_TILE), 0)

    def hidden(col0):
        cols = slice(col0, col0 + FF_TILE)
        zr = jnp.dot(ubf_ref[...], wup_ref[:, cols], preferred_element_type=F32)
        bup = bup_ref[:, cols]
        cw = cw_ref[:, cols]
        bias = cb_ref[:, cols] + (cw[0:1] + cw[1:2] + cw[2:3]) * bup
        z0 = zr[HALO:HALO + tm]
        prev = jnp.where(seg_first, -bup, zr[HALO - 1:HALO])
        nxt = jnp.where(seg_last, -bup, zr[HALO + tm:HALO + tm + 1])
        down = pltpu.roll(z0, 1, 0)
        up = pltpu.roll(z0, tm - 1, 0)
        zm1 = jnp.concatenate([jnp.where(row8 == 0, prev, down[0:8]), down[8:]], axis=0)
        zp1 = jnp.concatenate([up[:tm - 8], jnp.where(row8 == 7, nxt, up[tm - 8:])], axis=0)
        return cw[0:1] * zm1 + cw[1:2] * z0 + cw[2:3] * zp1 + bias

    for j in range(D_FF // FF_TILE):
        val = hidden(j * FF_TILE)
        gat = hidden(D_FF + j * FF_TILE)
        act_ref[:, j * FF_TILE:(j + 1) * FF_TILE] = (val * _silu(gat)).astype(BF16)
    f = jnp.dot(act_ref[...], wdn_ref[...], preferred_element_type=F32) + bdn_ref[...]
    o_ref[0] = _layer_norm(ALPHA * h + gate_ref[0] * f, g_ref[...], beta_ref[...])


def _ffn(h, mods, layer, w_up, b_up, conv_w, conv_b, w_down, b_down, ln_g, ln_b,
         ctx_tiles, tile_off, rows=ROW_TILE):
    batch, ta, _ = h.shape
    nt_seq = ta // rows
    nt = nt_seq - tile_off
    hb = rows // HALO
    n_hblk = ta // HALO
    mspec = lambda j: pl.BlockSpec(
        (1, 1, D_MODEL),
        lambda b, t: ((layer * MOD_ROWS + jnp.where(t + tile_off < ctx_tiles, batch, b)) * 6 + j, 0, 0))
    resident = lambda shape: pl.BlockSpec(shape, lambda *_: (0,) * len(shape),
                                          pipeline_mode=pl.Buffered(1))
    return pl.pallas_call(
        functools.partial(_ffn_kernel, ctx_tiles=ctx_tiles, tile_off=tile_off, nt_seq=nt_seq),
        grid=(batch, nt),
        in_specs=[pl.BlockSpec((1, rows, D_MODEL), lambda b, t: (b, t + tile_off, 0)),
                  pl.BlockSpec((1, HALO, D_MODEL),
                               lambda b, t: (b, jnp.maximum((t + tile_off) * hb - 1, 0), 0)),
                  pl.BlockSpec((1, HALO, D_MODEL),
                               lambda b, t: (b, jnp.minimum((t + tile_off + 1) * hb, n_hblk - 1), 0)),
                  mspec(3), mspec(4), mspec(5),
                  resident((D_MODEL, 2 * D_FF)), _full((1, 2 * D_FF)),
                  _full((3, 2 * D_FF)), _full((1, 2 * D_FF)),
                  resident((D_FF, D_MODEL)), _full((1, D_MODEL)),
                  _full((1, D_MODEL)), _full((1, D_MODEL))],
        out_specs=pl.BlockSpec((1, rows, D_MODEL), lambda b, t: (b, t, 0)),
        out_shape=jax.ShapeDtypeStruct((batch, nt * rows, D_MODEL), F32),
        scratch_shapes=[pltpu.VMEM((rows + 2 * HALO, D_MODEL), F32),
                        pltpu.VMEM((rows + 2 * HALO, D_MODEL), BF16),
                        pltpu.VMEM((rows, D_FF), BF16)],
        compiler_params=_cparams("parallel", "arbitrary"),
        name="conv_ffn_ln",
    )(h, h, h, mods, mods, mods, w_up, b_up.reshape(1, -1), conv_w, conv_b.reshape(1, -1),
      w_down, b_down.reshape(1, -1), ln_g.reshape(1, -1), ln_b.reshape(1, -1))


def _time_major_rows(b, batch):
    return pl.ds(b, ROW_TILE, stride=batch)


def _inproj_ssm_kernel(h_ref, shift_ref, scale_ref, w_ref, z_ref, xbc_ref, dt_ref, us_ref,
                       wbf_ref, *, batch):
    @pl.when(_first_step())
    def _():
        head = SSD_W + XBC_W
        wbf_ref[:, 0:head] = w_ref[:, 0:head].astype(BF16)
        wbf_ref[:, head:head + S5_W] = w_ref[:, head + DT_W:head + DT_W + S5_W].astype(BF16)
        wbf_ref[:, head + S5_W:] = jnp.zeros((D_MODEL, DT_PAD), BF16)
        wbf_ref[:, head + S5_W:head + S5_W + DT_W] = w_ref[:, head:head + DT_W].astype(BF16)

    u = (h_ref[0] * (1.0 + scale_ref[0]) + shift_ref[0]).astype(BF16)
    p = jnp.dot(u, wbf_ref[...], preferred_element_type=F32)
    z_ref[0] = p[:, :SSD_W]
    xbc_ref[0] = p[:, SSD_W:SSD_W + XBC_W]
    dt_ref[0] = p[:, SSD_W + XBC_W + S5_W:]
    rows = _time_major_rows(pl.program_id(1), batch)
    for j in range(S5_W // 128):
        col0 = SSD_W + XBC_W + j * 128
        us_ref[j, rows, :] = p[:, col0:col0 + 128]


def _inproj_ssm(h, mods, layer, w_in, ctx_tiles):
    batch, ta, _ = h.shape
    nt = ta // ROW_TILE
    row = lambda w: pl.BlockSpec((1, ROW_TILE, w), lambda t, b: (b, t, 0))
    mspec = lambda j: pl.BlockSpec(
        (1, 1, D_MODEL),
        lambda t, b: ((layer * MOD_ROWS + jnp.where(t < ctx_tiles, batch, b)) * 6 + j, 0, 0))
    return pl.pallas_call(
        functools.partial(_inproj_ssm_kernel, batch=batch),
        grid=(nt, batch),
        in_specs=[row(D_MODEL), mspec(0), mspec(1), _resident((D_MODEL, SSM_IN_W))],
        out_specs=[row(SSD_W), row(XBC_W), row(DT_PAD),
                   pl.BlockSpec((S5_W // 128, ROW_TILE * batch, 128), lambda t, b: (0, t, 0))],
        out_shape=[jax.ShapeDtypeStruct((batch, ta, SSD_W), F32),
                   jax.ShapeDtypeStruct((batch, ta, XBC_W), F32),
                   jax.ShapeDtypeStruct((batch, ta, DT_PAD), F32),
                   jax.ShapeDtypeStruct((S5_W // 128, ta * batch, 128), F32)],
        scratch_shapes=[pltpu.VMEM((D_MODEL, SSM_IN_PAD), BF16)],
        compiler_params=_cparams("arbitrary", "arbitrary"),
        name="inproj_ssm",
    )(h, mods, mods, w_in)


def _cumsum_rows(v):
    n = v.shape[0]
    row = lax.broadcasted_iota(jnp.int32, v.shape, 0)
    s = 1
    while s < n:
        v = v + jnp.where(row >= s, pltpu.roll(v, s, 0), 0.0)
        s *= 2
    return v


def _expand_heads(v, e_ref):
    hi = v.astype(BF16)
    lo = (v - hi.astype(F32)).astype(BF16)
    e = e_ref[...]
    return (jnp.dot(hi, e, preferred_element_type=F32)
            + jnp.dot(lo, e, preferred_element_type=F32))


def _ssd_kernel(xbc_ref, xp_ref, xn_ref, dt_ref, z_ref, cw_ref, cb_ref, alog_ref, dtb_ref,
                dsk_ref, ng_ref, ef_ref, eb_ref, o_ref,
                xs_ref, cd_ref, st_ref, dec_ref, y_ref, *, n_tiles, ctx_tiles):
    s = pl.program_id(1)
    q = SSD_CHUNK
    cpt = ROW_TILE // q
    n_chunks = n_tiles * cpt
    ctx_chunks = ctx_tiles * cpt
    gw = SSD_W // SSD_GROUPS
    hpg = SSD_HEADS // SSD_GROUPS

    @pl.when(s < n_tiles)
    def _phase0():
        p = s
        seg_first = (p == 0) | (p == ctx_tiles)
        seg_last = (p == ctx_tiles - 1) | (p == n_tiles - 1)
        xr = xbc_ref[0]
        prev = jnp.where(seg_first, 0.0, xp_ref[0, HALO - 1:HALO, :])
        nxt = jnp.where(seg_last, 0.0, xn_ref[0, 0:1, :])
        row8 = lax.broadcasted_iota(jnp.int32, (8, XBC_W), 0)
        down = pltpu.roll(xr, 1, 0)
        up = pltpu.roll(xr, ROW_TILE - 1, 0)
        xm1 = jnp.concatenate([jnp.where(row8 == 0, prev, down[0:8]), down[8:]], axis=0)
        xp1 = jnp.concatenate([up[:ROW_TILE - 8], jnp.where(row8 == 7, nxt, up[ROW_TILE - 8:])],
                              axis=0)
        cw = cw_ref[...]
        xs_tile = _silu(cw[0:1] * xm1 + cw[1:2] * xr + cw[2:3] * xp1 + cb_ref[...])
        xs_ref[p] = xs_tile

        raw = dt_ref[0] + dtb_ref[...]
        dt_tile = jnp.maximum(raw, 0.0) + jnp.log1p(jnp.exp(-jnp.abs(raw)))
        a_row = -jnp.exp(alog_ref[...])
        for i in range(cpt):
            c = p * cpt + i
            xs = xs_tile[i * q:(i + 1) * q]
            dtv = dt_tile[i * q:(i + 1) * q]
            adt = dtv * a_row
            cum = _cumsum_rows(adt)
            tot = cum[q - 1:q, :]
            lane = lax.broadcasted_iota(jnp.int32, cum.shape, 1)
            cc = jnp.where(lane < SSD_HEADS, cum, tot - cum + adt)
            cd_ref[c, 0] = cc
            cd_ref[c, 1] = dtv
            w_end = jnp.exp(tot - cc) * dtv
            dec16 = jnp.broadcast_to(jnp.exp(tot), (16, DT_PAD))
            x = xs[:, :SSD_W]
            for d, e_ref in enumerate((ef_ref, eb_ref)):
                wx = (_expand_heads(w_end, e_ref) * x).astype(BF16)
                for g in range(SSD_GROUPS):
                    bmt = xs[:, SSD_W + g * SSD_STATE:SSD_W + (g + 1) * SSD_STATE].T.astype(BF16)
                    st_ref[c, d, :, g * gw:(g + 1) * gw] = jnp.dot(
                        bmt, wx[:, g * gw:(g + 1) * gw], preferred_element_type=F32)
                dec_ref[c, d] = _expand_heads(dec16, e_ref)[0:8]

    @pl.when(s == n_tiles)
    def _recurrence():
        fwd = list(range(n_chunks))
        bwd = list(range(ctx_chunks - 1, -1, -1)) + list(range(n_chunks - 1, ctx_chunks - 1, -1))
        for d, order in enumerate((fwd, bwd)):
            for col0 in range(0, SSD_W, 128):
                cols = slice(col0, col0 + 128)
                state = jnp.zeros((SSD_STATE, 128), F32)
                for ci in order:
                    contrib = st_ref[ci, d, :, cols]
                    st_ref[ci, d, :, cols] = state
                    state = state * dec_ref[ci, d, 0:1, cols] + contrib

    @pl.when(s >= n_tiles)
    def _phase1():
        p = s - n_tiles + ctx_tiles
        rowi = lax.broadcasted_iota(jnp.int32, (q, q), 0)
        coli = lax.broadcasted_iota(jnp.int32, (q, q), 1)
        lower = coli <= rowi
        upper = coli >= rowi
        lane = lax.broadcasted_iota(jnp.int32, (q, 128), 1)
        neg = jnp.float32(-jnp.inf)
        for i in range(cpt):
            c = p * cpt + i
            rows = slice(i * q, (i + 1) * q)
            xs = xs_ref[p, rows, :]
            x = xs[:, :SSD_W]
            cc = cd_ref[c, 0]
            dtv = cd_ref[c, 1]
            cct = cc.T
            dtt = dtv.T
            ecc = jnp.exp(cc)
            ef = _expand_heads(ecc, ef_ref)
            eb = _expand_heads(ecc, eb_ref)
            for g in range(SSD_GROUPS):
                bm = xs[:, SSD_W + g * SSD_STATE:SSD_W + (g + 1) * SSD_STATE].astype(BF16)
                cm = xs[:, SSD_W + (SSD_GROUPS + g) * SSD_STATE:
                        SSD_W + (SSD_GROUPS + g + 1) * SSD_STATE].astype(BF16)
                gmat = lax.dot_general(cm, bm, (((1,), (1,)), ((), ())),
                                       preferred_element_type=F32)
                sl = slice(g * gw, (g + 1) * gw)
                yoff = (ef[:, sl] * jnp.dot(cm, st_ref[c, 0, :, sl].astype(BF16),
                                            preferred_element_type=F32)
                        + eb[:, sl] * jnp.dot(cm, st_ref[c, 1, :, sl].astype(BF16),
                                              preferred_element_type=F32))
                for pair in range(hpg // 2):
                    col0 = g * gw + pair * 128
                    xpair = x[:, col0:col0 + 128].astype(BF16)
                    res = []
                    for hh in range(2):
                        hd = g * hpg + pair * 2 + hh
                        hb_ = SSD_HEADS + hd
                        lf = jnp.exp(jnp.where(lower, cc[:, hd:hd + 1] - cct[hd:hd + 1, :], neg))
                        lb = jnp.exp(jnp.where(upper, cc[:, hb_:hb_ + 1] - cct[hb_:hb_ + 1, :],
                                               neg))
                        mt = gmat * (lf * dtt[hd:hd + 1, :] + lb * dtt[hb_:hb_ + 1, :])
                        res.append(jnp.dot(mt.astype(BF16), xpair, preferred_element_type=F32))
                    ydiag = jnp.where(lane < SSD_HEAD_DIM, res[0], res[1])
                    y_ref[rows, col0:col0 + 128] = (
                        ydiag + yoff[:, pair * 128:(pair + 1) * 128]
                        + dsk_ref[:, col0:col0 + 128] * x[:, col0:col0 + 128])
        gated = y_ref[...] * _silu(z_ref[0])
        normed = gated * lax.rsqrt(jnp.mean(gated * gated, axis=-1, keepdims=True) + LN_EPS)
        o_ref[0] = (normed * ng_ref[...]).astype(BF16)


def _ssd(xbc, dt, z, conv_w, conv_b, a_log, dt_bias, d_skip, norm_g, n_ctx):
    batch, ta, _ = xbc.shape
    q = SSD_CHUNK
    n_chunks = ta // q
    n_tiles = ta // ROW_TILE
    ctx_tiles = n_ctx // ROW_TILE
    hb = ROW_TILE // HALO
    n_hblk = ta // HALO
    pad24 = lambda v: jnp.pad(v.reshape(1, DT_W), ((0, 0), (0, DT_PAD - DT_W)))
    heads = np.arange(SSD_HEADS)
    ef = np.zeros((DT_PAD, SSD_W), np.float32)
    eb = np.zeros((DT_PAD, SSD_W), np.float32)
    for hd in heads:
        ef[hd, hd * SSD_HEAD_DIM:(hd + 1) * SSD_HEAD_DIM] = 1.0
        eb[SSD_HEADS + hd, hd * SSD_HEAD_DIM:(hd + 1) * SSD_HEAD_DIM] = 1.0
    dsk = jnp.repeat(d_skip.astype(F32), SSD_HEAD_DIM).reshape(1, SSD_W)
    ph0 = lambda s: s < n_tiles
    tile = lambda w: pl.BlockSpec(
        (1, ROW_TILE, w), lambda b, s: (b, jnp.where(ph0(s), s, n_tiles - 1), 0))
    return pl.pallas_call(
        functools.partial(_ssd_kernel, n_tiles=n_tiles, ctx_tiles=ctx_tiles),
        grid=(batch, 2 * n_tiles - ctx_tiles),
        in_specs=[tile(XBC_W),
                  pl.BlockSpec((1, HALO, XBC_W),
                               lambda b, s: (b, jnp.where(ph0(s), jnp.maximum(s * hb - 1, 0), 0), 0)),
                  pl.BlockSpec((1, HALO, XBC_W),
                               lambda b, s: (b, jnp.where(ph0(s), jnp.minimum((s + 1) * hb, n_hblk - 1), 0), 0)),
                  tile(DT_PAD),
                  pl.BlockSpec((1, ROW_TILE, SSD_W),
                               lambda b, s: (b, jnp.where(ph0(s), 0, s - n_tiles + ctx_tiles), 0)),
                  _full((3, XBC_W)), _full((1, XBC_W)), _full((1, DT_PAD)), _full((1, DT_PAD)),
                  _full((1, SSD_W)), _full((1, SSD_W)),
                  _full((DT_PAD, SSD_W)), _full((DT_PAD, SSD_W))],
        out_specs=pl.BlockSpec(
            (1, ROW_TILE, SSD_W), lambda b, s: (b, jnp.where(ph0(s), 0, s - n_tiles), 0)),
        out_shape=jax.ShapeDtypeStruct((batch, ta - n_ctx, SSD_W), BF16),
        scratch_shapes=[pltpu.VMEM((n_tiles, ROW_TILE, XBC_W), F32),
                        pltpu.VMEM((n_chunks, 2, q, DT_PAD), F32),
                        pltpu.VMEM((n_chunks, 2, SSD_STATE, SSD_W), F32),
                        pltpu.VMEM((n_chunks, 2, 8, SSD_W), F32),
                        pltpu.VMEM((ROW_TILE, SSD_W), F32)],
        compiler_params=_cparams("parallel", "arbitrary"),
        name="ssd_bidir",
    )(xbc, xbc, xbc, dt, z, conv_w, conv_b.reshape(1, XBC_W), pad24(a_log), pad24(dt_bias),
      dsk, norm_g.reshape(1, SSD_W), jnp.asarray(ef, BF16), jnp.asarray(eb, BF16))


def _s5_disc_kernel(lr_ref, li_ref, ldt_ref, bre_ref, bim_ref, cre_ref, cim_ref,
                    a_ref, bd_ref, cd_ref):
    lr, li = lr_ref[...], li_ref[...]
    dt = jnp.exp(ldt_ref[...])
    mag = jnp.exp(dt * lr)
    ab_re, ab_im = mag * jnp.cos(dt * li), mag * jnp.sin(dt * li)
    den = lr * lr + li * li
    k_re = ((ab_re - 1.0) * lr + ab_im * li) / den
    k_im = (ab_im * lr - (ab_re - 1.0) * li) / den
    bre, bim = bre_ref[...], bim_ref[...]
    for d in range(2):
        a_ref[d, :, 0:S5_NSTATE] = jnp.broadcast_to(ab_re[d:d + 1], (8, S5_NSTATE))
        a_ref[d, :, S5_NSTATE:] = jnp.broadcast_to(ab_im[d:d + 1], (8, S5_NSTATE))
        kr, ki = k_re[d:d + 1], k_im[d:d + 1]
        bd_ref[d, :, 0:S5_NSTATE] = (kr * bre - ki * bim).astype(BF16)
        bd_ref[d, :, S5_NSTATE:] = (kr * bim + ki * bre).astype(BF16)
        cd_ref[d, 0:S5_NSTATE, :] = cre_ref[d].astype(BF16)
        cd_ref[d, S5_NSTATE:, :] = (-cim_ref[d]).astype(BF16)


def _s5_discretize(lam_re, lam_im, log_dt, b_re, b_im, c_re, c_im):
    eye = jnp.eye(S5_GROUPS, dtype=F32)
    bd = lambda b: jnp.einsum('gph,gk->ghkp', b, eye).reshape(S5_W, S5_NSTATE)
    cd = lambda cc: jnp.einsum('dghp,gk->dgpkh', cc, eye).reshape(2, S5_NSTATE, S5_W)
    ldt = jnp.repeat(log_dt, S5_STATE, axis=-1)
    return pl.pallas_call(
        _s5_disc_kernel,
        out_shape=[jax.ShapeDtypeStruct((2, 8, 2 * S5_NSTATE), F32),
                   jax.ShapeDtypeStruct((2, S5_W, 2 * S5_NSTATE), BF16),
                   jax.ShapeDtypeStruct((2, 2 * S5_NSTATE, S5_W), BF16)],
        compiler_params=pltpu.CompilerParams(vmem_limit_bytes=VMEM_LIMIT_BYTES),
        name="s5_discretize",
    )(lam_re.reshape(2, S5_NSTATE), lam_im.reshape(2, S5_NSTATE), ldt,
      bd(b_re), bd(b_im), cd(c_re), cd(c_im))


S5_TIME_CHUNK = 128
S5_UNROLL = 8


def _s5_scan_kernel(u_ref, a_ref, bd_ref, cd_ref, o_ref, hs_ref, carry_ref, *, batch):
    d = pl.program_id(0)
    j = pl.program_id(1)
    n = S5_NSTATE

    @pl.when(j == 0)
    def _():
        carry_ref[...] = jnp.zeros_like(carry_ref)

    half = hs_ref.shape[0] // 2
    n_slab = S5_W // 128
    for r in (0, half):
        u = jnp.concatenate([u_ref[s, r:r + half, :] for s in range(n_slab)], axis=1)
        hs_ref[r:r + half, :] = jnp.dot(u.astype(BF16), bd_ref[0], preferred_element_type=F32)
    ar = a_ref[0, :, 0:n]
    ai = a_ref[0, :, n:]
    if batch != 8:
        ar = jnp.broadcast_to(ar[0:1], (batch, n))
        ai = jnp.broadcast_to(ai[0:1], (batch, n))

    def body(i, carry):
        hr, hi = carry
        for s in range(S5_UNROLL):
            step = i * S5_UNROLL + s
            step = jnp.where(d == 0, step, S5_TIME_CHUNK - 1 - step)
            rows = pl.ds(pl.multiple_of(step * batch, batch), batch)
            nr = ar * hr - ai * hi + hs_ref[rows, 0:n]
            ni = ar * hi + ai * hr + hs_ref[rows, n:]
            hs_ref[rows, 0:n] = nr
            hs_ref[rows, n:] = ni
            hr, hi = nr, ni
        return hr, hi

    hr, hi = lax.fori_loop(0, S5_TIME_CHUNK // S5_UNROLL, body,
                           (carry_ref[:, 0:n], carry_ref[:, n:]))
    carry_ref[:, 0:n] = hr
    carry_ref[:, n:] = hi
    for r in (0, half):
        y = jnp.dot(hs_ref[r:r + half, :].astype(BF16), cd_ref[0], preferred_element_type=F32)
        for s in range(n_slab):
            o_ref[0, s, r:r + half, :] = y[:, s * 128:(s + 1) * 128]


def _s5_scan(us_flat, a, bd, cd, batch, n_ctx):
    n_slab, rows_total, _ = us_flat.shape
    ta = rows_total // batch
    tc = S5_TIME_CHUNK
    n_chunks = ta // tc
    ctx_chunks = n_ctx // tc
    blk = tc * batch

    def chunk_of(d, j):
        bwd = jnp.where(j < ctx_chunks, ctx_chunks - 1 - j, n_chunks - 1 - (j - ctx_chunks))
        return jnp.where(d == 0, j, bwd)

    return pl.pallas_call(
        functools.partial(_s5_scan_kernel, batch=batch),
        grid=(2, n_chunks),
        in_specs=[pl.BlockSpec((n_slab, blk, 128), lambda d, j: (0, chunk_of(d, j), 0)),
                  pl.BlockSpec((1, 8, 2 * S5_NSTATE), lambda d, j: (d, 0, 0)),
                  pl.BlockSpec((1, S5_W, 2 * S5_NSTATE), lambda d, j: (d, 0, 0)),
                  pl.BlockSpec((1, 2 * S5_NSTATE, S5_W), lambda d, j: (d, 0, 0))],
        out_specs=pl.BlockSpec((1, n_slab, blk, 128), lambda d, j: (d, 0, chunk_of(d, j), 0)),
        out_shape=jax.ShapeDtypeStruct((2, n_slab, rows_total, 128), F32),
        scratch_shapes=[pltpu.VMEM((blk, 2 * S5_NSTATE), F32),
                        pltpu.VMEM((batch, 2 * S5_NSTATE), F32)],
        compiler_params=_cparams("arbitrary", "arbitrary"),
        name="s5_scan",
    )(us_flat, a, bd, cd)


def _merge_ln_kernel(gs_ref, y5_ref, us_ref, h_ref, gate_ref, dd_ref, gw_ref, gb_ref,
                     w_ref, g_ref, beta_ref, o_ref, wbf_ref, *, batch):
    @pl.when(_first_step())
    def _():
        wbf_ref[...] = w_ref[...].astype(BF16)

    rows = _time_major_rows(pl.program_id(1), batch)
    y5 = jnp.concatenate([y5_ref[0, s, rows, :] + y5_ref[1, s, rows, :]
                          for s in range(S5_W // 128)], axis=1)
    us = jnp.concatenate([us_ref[s, rows, :] for s in range(S5_W // 128)], axis=1)
    ge = jax.nn.gelu(y5 + dd_ref[...] * us)
    s5 = ge * jax.nn.sigmoid(
        jnp.dot(ge.astype(BF16), gw_ref[...], preferred_element_type=F32) + gb_ref[...])
    y = jnp.dot(gs_ref[0], wbf_ref[0:SSD_W, :], preferred_element_type=F32)
    y = y + jnp.dot(s5.astype(BF16), wbf_ref[SSD_W:, :], preferred_element_type=F32)
    o_ref[0] = _layer_norm(ALPHA * h_ref[0] + gate_ref[0] * y, g_ref[...], beta_ref[...])


def _merge_ln(g_ssd, y5, us_t, h, mods, layer, s5_d, glu_w, glu_b, w_out, ln_g, ln_b, ctx_tiles):
    batch, ta, _ = h.shape
    nt = ta // ROW_TILE - ctx_tiles
    n_slab = S5_W // 128
    tm_rows = ROW_TILE * batch
    return pl.pallas_call(
        functools.partial(_merge_ln_kernel, batch=batch),
        grid=(nt, batch),
        in_specs=[pl.BlockSpec((1, ROW_TILE, SSD_W), lambda t, b: (b, t, 0)),
                  pl.BlockSpec((2, n_slab, tm_rows, 128), lambda t, b: (0, 0, t + ctx_tiles, 0)),
                  pl.BlockSpec((n_slab, tm_rows, 128), lambda t, b: (0, t + ctx_tiles, 0)),
                  pl.BlockSpec((1, ROW_TILE, D_MODEL), lambda t, b: (b, t + ctx_tiles, 0)),
                  pl.BlockSpec((1, 1, D_MODEL), lambda t, b: ((layer * MOD_ROWS + b) * 6 + 2, 0, 0)),
                  _full((1, S5_W)), _full((S5_W, S5_W)), _full((1, S5_W)),
                  _resident((D_MODEL, D_MODEL)), _full((1, D_MODEL)), _full((1, D_MODEL))],
        out_specs=pl.BlockSpec((1, ROW_TILE, D_MODEL), lambda t, b: (b, t, 0)),
        out_shape=jax.ShapeDtypeStruct((batch, nt * ROW_TILE, D_MODEL), F32),
        scratch_shapes=[pltpu.VMEM((D_MODEL, D_MODEL), BF16)],
        compiler_params=_cparams("arbitrary", "arbitrary"),
        name="merge_outproj_ln",
    )(g_ssd, y5, us_t, h, mods, s5_d.reshape(1, S5_W), glu_w.astype(BF16),
      glu_b.reshape(1, S5_W), w_out, ln_g.reshape(1, -1), ln_b.reshape(1, -1))


def _attn_layer(h_ctx, h_lat, mods, layer, i, p, keep_ctx):
    n_ctx = h_ctx.shape[1]
    ctx_tiles = n_ctx // ROW_TILE
    lam_init = 0.8 - 0.6 * math.exp(-0.3 * layer)
    cos, sin = _rope_tables(h_lat.shape[1], n_ctx)
    q, k, v, f = _inproj_attn(h_ctx, h_lat, mods, layer, p['attn_w_in'][i],
                              cos, sin, ctx_tiles)
    o_ctx, o_lat = _attention(q, k, v, p['attn_lambda'][i], p['attn_subln_g'][i], lam_init,
                              ctx_tiles)
    fm = _fourier(f, p['fourier_w'][i], p['fourier_b'][i], n_ctx)
    h1 = _outproj_ln(o_ctx, o_lat, fm, h_ctx, h_lat, mods, layer,
                     p['attn_w_out'][i],
                     p['ln_g'][layer, 0], p['ln_b'][layer, 0], ctx_tiles)
    return _ffn(h1, mods, layer, p['ffn_w_up'][layer].astype(BF16), p['ffn_b_up'][layer],
                p['ffn_conv_w'][layer], p['ffn_conv_b'][layer],
                p['ffn_w_down'][layer].astype(BF16), p['ffn_b_down'][layer],
                p['ln_g'][layer, 1], p['ln_b'][layer, 1], ctx_tiles, 0 if keep_ctx else ctx_tiles)


def _ssm_layer(h, mods, layer, i, n_ctx, p, keep_ctx):
    assert not keep_ctx, "an SSM layer that must also emit context rows is not implemented"
    ctx_tiles = n_ctx // ROW_TILE
    batch = h.shape[0]
    z, xbc, dt, us_t = _inproj_ssm(h, mods, layer, p['ssm_w_in'][i], ctx_tiles)
    g_ssd = _ssd(xbc, dt, z, p['ssd_conv_w'][i], p['ssd_conv_b'][i], p['ssd_a_log'][i],
                 p['ssd_dt_bias'][i], p['ssd_d'][i], p['ssd_norm_g'][i], n_ctx)
    a, bd, cd = _s5_discretize(p['s5_lambda_re'][i], p['s5_lambda_im'][i], p['s5_log_dt'][i],
                               p['s5_b_re'][i], p['s5_b_im'][i], p['s5_c_re'][i], p['s5_c_im'][i])
    y5 = _s5_scan(us_t, a, bd, cd, batch, n_ctx)
    h1 = _merge_ln(g_ssd, y5, us_t, h, mods, layer, p['s5_d'][i], p['s5_glu_w'][i],
                   p['s5_glu_b'][i], p['ssm_w_out'][i],
                   p['ln_g'][layer, 0], p['ln_b'][layer, 0], ctx_tiles)
    return _ffn(h1, mods, layer, p['ffn_w_up'][layer].astype(BF16), p['ffn_b_up'][layer],
                p['ffn_conv_w'][layer], p['ffn_conv_b'][layer],
                p['ffn_w_down'][layer].astype(BF16), p['ffn_b_down'][layer],
                p['ln_g'][layer, 1], p['ln_b'][layer, 1], 0, 0, rows=FFN_LAT_ROWS)


def kernel(x, c, ctx, c_ctx, ada_w, ada_b, ln_g, ln_b, ffn_w_up, ffn_b_up, ffn_conv_w, ffn_conv_b, ffn_w_down, ffn_b_down, attn_w_in, attn_lambda, attn_subln_g, fourier_w, fourier_b, attn_w_out, ssm_w_in, ssd_conv_w, ssd_conv_b, ssd_a_log, ssd_dt_bias, ssd_d, ssd_norm_g, s5_lambda_re, s5_lambda_im, s5_log_dt, s5_b_re, s5_b_im, s5_c_re, s5_c_im, s5_d, s5_glu_w, s5_glu_b, ssm_w_out):
    p = dict(ln_g=ln_g, ln_b=ln_b, ffn_w_up=ffn_w_up, ffn_b_up=ffn_b_up, ffn_conv_w=ffn_conv_w,
             ffn_conv_b=ffn_conv_b, ffn_w_down=ffn_w_down, ffn_b_down=ffn_b_down,
             attn_w_in=attn_w_in, attn_lambda=attn_lambda, attn_subln_g=attn_subln_g,
             fourier_w=fourier_w, fourier_b=fourier_b, attn_w_out=attn_w_out, ssm_w_in=ssm_w_in,
             ssd_conv_w=ssd_conv_w, ssd_conv_b=ssd_conv_b, ssd_a_log=ssd_a_log,
             ssd_dt_bias=ssd_dt_bias, ssd_d=ssd_d, ssd_norm_g=ssd_norm_g,
             s5_lambda_re=s5_lambda_re, s5_lambda_im=s5_lambda_im, s5_log_dt=s5_log_dt,
             s5_b_re=s5_b_re, s5_b_im=s5_b_im, s5_c_re=s5_c_re, s5_c_im=s5_c_im, s5_d=s5_d,
             s5_glu_w=s5_glu_w, s5_glu_b=s5_glu_b, ssm_w_out=ssm_w_out)
    batch, n_lat, _ = x.shape
    n_ctx = ctx.shape[1]
    assert n_ctx == ROW_TILE and n_lat % ROW_TILE == 0 and batch < MOD_ROWS
    mods = _ada_mods(c, c_ctx, ada_w, ada_b)
    assert DEPTH == 2
    h = _attn_layer(ctx, x, mods, 0, 0, p, keep_ctx=True)
    return _ssm_layer(h, mods, 1, 0, n_ctx, p, keep_ctx=False)
```

```python
import functools
import math

import numpy as np
import jax
import jax.numpy as jnp
from jax import lax
from jax.experimental import pallas as pl
from jax.experimental.pallas import tpu as pltpu

F32 = jnp.float32
BF16 = jnp.bfloat16

D_MODEL = 1024
DEPTH = 2
GRID_W = 64
ROPE_BASE = 10000.0
LN_EPS = 1e-5
ALPHA = (2 * DEPTH) ** 0.25
ATTN_W = 768
ATTN_HEADS = 6
ATTN_HEAD_DIM = 64
F_W = 256
F_GROUPS = 4
F_GROUP_W = 64
ATTN_IN_W = 2 * ATTN_W + ATTN_W + F_W
SSD_W = 768
SSD_HEADS = 12
SSD_HEAD_DIM = 64
SSD_GROUPS = 2
SSD_STATE = 128
SSD_CHUNK = 128
XBC_W = SSD_W + 2 * SSD_GROUPS * SSD_STATE
DT_W = 2 * SSD_HEADS
DT_PAD = 128
S5_W = 256
S5_GROUPS = 16
S5_GROUP_W = 16
S5_STATE = 64
S5_NSTATE = S5_GROUPS * S5_STATE
SSM_IN_W = SSD_W + XBC_W + DT_W + S5_W
SSM_IN_PAD = SSD_W + XBC_W + S5_W + DT_PAD
D_FF = 2816
FF_TILE = 256

ROW_TILE = 256
ATTN_Q_SUB = 128
ATTN_ITEM_TILES = 2
FFN_LAT_ROWS = 512
HALO = 8
MOD_ROWS = 16
VMEM_LIMIT_BYTES = 56 * 1024 * 1024


def _cparams(*sem):
    return pltpu.CompilerParams(dimension_semantics=sem, vmem_limit_bytes=VMEM_LIMIT_BYTES)


def _silu(v):
    return v * jax.nn.sigmoid(v)


def _layer_norm(v, g, b):
    mu = jnp.mean(v, axis=-1, keepdims=True)
    d = v - mu
    var = jnp.mean(d * d, axis=-1, keepdims=True)
    return d * lax.rsqrt(var + LN_EPS) * g + b


def _full(shape):
    nd = len(shape)
    return pl.BlockSpec(shape, lambda *_: (0,) * nd)


def _resident(shape):
    nd = len(shape)
    return pl.BlockSpec(shape, lambda *_: (0,) * nd, pipeline_mode=pl.Buffered(1))


def _first_step():
    return (pl.program_id(0) == 0) & (pl.program_id(1) == 0)


def _mod_spec(layer, j, batch, ctx_tiles):
    def idx(b, t):
        row = jnp.where(t < ctx_tiles, batch, b)
        return ((layer * MOD_ROWS + row) * 6 + j, 0, 0)
    return pl.BlockSpec((1, 1, D_MODEL), idx)


def _ada_kernel(c_ref, w_ref, b_ref, o_ref):
    s = _silu(c_ref[...])
    w = w_ref[0]
    s_hi = s.astype(BF16)
    s_lo = (s - s_hi.astype(F32)).astype(BF16)
    w_hi = w.astype(BF16)
    w_lo = (w - w_hi.astype(F32)).astype(BF16)
    dot = functools.partial(jnp.dot, preferred_element_type=F32)
    o_ref[0] = dot(s_hi, w_hi) + (dot(s_lo, w_hi) + dot(s_hi, w_lo)) + b_ref[0]


def _ada_mods(c, c_ctx, ada_w, ada_b):
    batch = c.shape[0]
    nl = ada_w.shape[0]
    c_all = jnp.concatenate(
        [c, c_ctx[None], jnp.zeros((MOD_ROWS - batch - 1, D_MODEL), F32)], axis=0)
    out = pl.pallas_call(
        _ada_kernel,
        grid=(nl, 6),
        in_specs=[_full((MOD_ROWS, D_MODEL)),
                  pl.BlockSpec((1, D_MODEL, D_MODEL), lambda l, j: (l, 0, j)),
                  pl.BlockSpec((1, 1, D_MODEL), lambda l, j: (l, 0, j))],
        out_specs=pl.BlockSpec((1, MOD_ROWS, D_MODEL), lambda l, j: (l, 0, j)),
        out_shape=jax.ShapeDtypeStruct((nl, MOD_ROWS, 6 * D_MODEL), F32),
        compiler_params=_cparams("arbitrary", "arbitrary"),
        name="ada_mods",
    )(c_all, ada_w, ada_b.reshape(nl, 1, 6 * D_MODEL))
    return out.reshape(nl * MOD_ROWS * 6, 1, D_MODEL)


def _inproj_attn_kernel(hc_ref, hl_ref, shift_ref, scale_ref, w_ref, cos_ref, sin_ref,
                        q_ref, k_ref, v_ref, f_ref, wbf_ref, *, ctx_tiles):
    @pl.when(_first_step())
    def _():
        wbf_ref[...] = w_ref[...].astype(BF16)

    h = jnp.where(pl.program_id(1) < ctx_tiles, hc_ref[0], hl_ref[0])
    u = (h * (1.0 + scale_ref[0]) + shift_ref[0]).astype(BF16)
    p = jnp.dot(u, wbf_ref[...], preferred_element_type=F32)
    cos = cos_ref[...]
    sin = sin_ref[...]
    lane = lax.broadcasted_iota(jnp.int32, cos.shape, 1)
    first_half = (lane % ATTN_HEAD_DIM) < (ATTN_HEAD_DIM // 2)

    def rope(blk):
        partner = jnp.where(first_half, pltpu.roll(blk, 128 - 32, 1), pltpu.roll(blk, 32, 1))
        return blk * cos + partner * sin

    qk_scale = ATTN_HEAD_DIM ** -0.5 * math.log2(math.e)
    for i in range(ATTN_HEADS):
        lo, hi = i * 128, (i + 1) * 128
        q_ref[0, :, lo:hi] = (rope(p[:, lo:hi]) * qk_scale).astype(BF16)
        k_ref[0, :, lo:hi] = rope(p[:, ATTN_W + lo:ATTN_W + hi]).astype(BF16)
    v_ref[0] = p[:, 2 * ATTN_W:3 * ATTN_W].astype(BF16)
    f_ref[0] = p[:, 3 * ATTN_W:].astype(BF16)


def _rope_tables(n_lat, n_ctx):
    rows = n_lat // GRID_W
    row = jnp.repeat(jnp.arange(rows, dtype=F32), GRID_W)
    col = jnp.tile(jnp.arange(GRID_W, dtype=F32), rows)
    n_freq = ATTN_HEAD_DIM // 4
    inv_freq = ROPE_BASE ** (-jnp.arange(n_freq, dtype=F32) / n_freq)
    ang = jnp.concatenate([row[:, None] * inv_freq, col[:, None] * inv_freq], axis=-1)
    cos, sin = jnp.cos(ang), jnp.sin(ang)
    cos128 = jnp.tile(cos, (1, 4))
    sin128 = jnp.tile(jnp.concatenate([-sin, sin], axis=-1), (1, 2))
    cos_all = jnp.concatenate([jnp.ones((n_ctx, 128), F32), cos128], axis=0)
    sin_all = jnp.concatenate([jnp.zeros((n_ctx, 128), F32), sin128], axis=0)
    return cos_all, sin_all


def _split_specs(width, ctx_tiles):
    return [pl.BlockSpec((1, ROW_TILE, width), lambda b, t: (b, jnp.minimum(t, ctx_tiles - 1), 0)),
            pl.BlockSpec((1, ROW_TILE, width), lambda b, t: (b, jnp.maximum(t - ctx_tiles, 0), 0))]


def _inproj_attn(h_ctx, h_lat, mods, layer, w_in, cos, sin, ctx_tiles):
    batch = h_lat.shape[0]
    ta = h_ctx.shape[1] + h_lat.shape[1]
    nt = ta // ROW_TILE
    row = lambda w: pl.BlockSpec((1, ROW_TILE, w), lambda b, t: (b, t, 0))
    tab = pl.BlockSpec((ROW_TILE, 128), lambda b, t: (t, 0))
    return pl.pallas_call(
        functools.partial(_inproj_attn_kernel, ctx_tiles=ctx_tiles),
        grid=(batch, nt),
        in_specs=_split_specs(D_MODEL, ctx_tiles)
        + [_mod_spec(layer, 0, batch, ctx_tiles), _mod_spec(layer, 1, batch, ctx_tiles),
           _resident((D_MODEL, ATTN_IN_W)), tab, tab],
        out_specs=[row(ATTN_W), row(ATTN_W), row(ATTN_W), row(F_W)],
        out_shape=[jax.ShapeDtypeStruct((batch, ta, ATTN_W), BF16)] * 3
        + [jax.ShapeDtypeStruct((batch, ta, F_W), BF16)],
        scratch_shapes=[pltpu.VMEM((D_MODEL, ATTN_IN_W), BF16)],
        compiler_params=_cparams("arbitrary", "arbitrary"),
        name="inproj_attn",
    )(h_ctx, h_lat, mods, mods, w_in, cos, sin)


def _diff_lambda(lam_ref, lam_init):
    lamv = lam_ref[...]
    l1 = jnp.sum(lamv[0:1] * lamv[1:2], axis=-1, keepdims=True)
    l2 = jnp.sum(lamv[2:3] * lamv[3:4], axis=-1, keepdims=True)
    return jnp.exp(l1) - jnp.exp(l2) + lam_init


def _stack_maps(q):
    lane = lax.broadcasted_iota(jnp.int32, q.shape, 1)
    zero = jnp.zeros_like(q)
    return jnp.concatenate([jnp.where(lane < ATTN_HEAD_DIM, q, zero),
                            jnp.where(lane >= ATTN_HEAD_DIM, q, zero)], axis=0)


def _scores(q2, k):
    return lax.dot_general(q2, k, (((1,), (1,)), ((), ())), preferred_element_type=F32)


def _diff_softmax_pv(load_s0, m0, load_s1, m1, lam, v, gain):
    e0 = jnp.exp2(load_s0() - m0)
    e1 = jnp.exp2(load_s1() - m1)
    l0 = jnp.sum(e0, axis=-1, keepdims=True)
    l1 = jnp.sum(e1, axis=-1, keepdims=True)
    w = e0 - e1 * (lam * l0 / l1)
    o = jnp.dot(w.astype(BF16), v, preferred_element_type=F32) * (1.0 / l0)
    o = o * lax.rsqrt(jnp.mean(o * o, axis=-1, keepdims=True) + LN_EPS)
    return (o * gain).astype(BF16)


def _attn_ctx_kernel(lam_ref, g_ref, q_ref, k_ref, v_ref, o_ref, *, lam_init):
    lam = _diff_lambda(lam_ref, lam_init)
    gain = g_ref[...] * (1.0 - lam_init)
    tq = q_ref.shape[1]
    for hd in range(ATTN_HEADS):
        cols = slice(hd * 128, (hd + 1) * 128)
        s = _scores(_stack_maps(q_ref[0, :, cols]), k_ref[0, :, cols])
        m = jnp.max(s, axis=-1, keepdims=True)

        def half(lo, s=s):
            return lambda: s[lo:lo + tq]

        o_ref[0, :, cols] = _diff_softmax_pv(half(0), m[:tq], half(tq), m[tq:], lam,
                                             v_ref[0, :, cols], gain)


def _attn_lat_kernel(lam_ref, g_ref, k_ref, v_ref, *rest, lam_init):
    q_refs = rest[:ATTN_ITEM_TILES]
    o_ref, sa_ref, ma_ref, sb_ref, mb_ref = rest[ATTN_ITEM_TILES:]
    t = pl.program_id(0)
    tq = ATTN_ITEM_TILES * ROW_TILE
    sub = ATTN_Q_SUB

    @pl.when(t == 0)
    def _():
        sb_ref[...] = jnp.zeros_like(sb_ref)
        mb_ref[...] = jnp.zeros_like(mb_ref)

    def step(s_new, m_new, s_old, m_old):
        lam = _diff_lambda(lam_ref, lam_init)
        gain = g_ref[...] * (1.0 - lam_init)
        q = jnp.concatenate([q_ref[0] for q_ref in q_refs], axis=0)
        s = _scores(_stack_maps(q), k_ref[0])
        s_new[...] = s
        m_new[...] = jnp.max(s, axis=-1, keepdims=True)
        v = v_ref[0]
        for i in range(tq // sub):
            r0 = slice(i * sub, (i + 1) * sub)
            r1 = slice(tq + i * sub, tq + (i + 1) * sub)
            o_ref[0, r0, :] = _diff_softmax_pv(
                functools.partial(s_old.__getitem__, (r0, slice(None))), m_old[r0, :],
                functools.partial(s_old.__getitem__, (r1, slice(None))), m_old[r1, :],
                lam, v, gain)

    @pl.when(t % 2 == 0)
    def _():
        step(sa_ref, ma_ref, sb_ref, mb_ref)

    @pl.when(t % 2 == 1)
    def _():
        step(sb_ref, mb_ref, sa_ref, ma_ref)


def _attention(q, k, v, lam_vec, subln_g, lam_init, ctx_tiles):
    batch, ta, _ = q.shape
    n_ctx = ctx_tiles * ROW_TILE
    nt = ta // ROW_TILE - ctx_tiles
    g = subln_g.reshape(1, 128)
    small = [_full((4, ATTN_HEAD_DIM)), _full((1, 128))]
    cspec = pl.BlockSpec((1, n_ctx, ATTN_W), lambda b: (b, 0, 0))
    o_ctx = pl.pallas_call(
        functools.partial(_attn_ctx_kernel, lam_init=lam_init),
        grid=(batch,),
        in_specs=small + [cspec, cspec, cspec],
        out_specs=cspec,
        out_shape=jax.ShapeDtypeStruct((batch, n_ctx, ATTN_W), BF16),
        compiler_params=_cparams("arbitrary"),
        name="diff_attention_ctx",
    )(lam_vec, g, q, k, v)

    per = ATTN_ITEM_TILES
    assert nt % per == 0
    ni = nt // per
    n_items = batch * ATTN_HEADS * ni

    def item(j):
        return j // (ATTN_HEADS * ni), (j // ni) % ATTN_HEADS, j % ni

    def score_item(j):
        return item(jnp.minimum(j, n_items - 1))

    def finish_item(j):
        return item(jnp.maximum(j - 1, 0))

    def q_spec(part):
        def idx(j):
            b, h, t = score_item(j)
            return b, ctx_tiles + t * per + part, h
        return pl.BlockSpec((1, ROW_TILE, 128), idx)

    def k_idx(j):
        b, h, _ = score_item(j)
        return b, 0, h

    def v_idx(j):
        b, h, _ = finish_item(j)
        return b, 0, h

    def o_idx(j):
        b, h, t = finish_item(j)
        return b, t, h

    o_lat = pl.pallas_call(
        functools.partial(_attn_lat_kernel, lam_init=lam_init),
        grid=(n_items + 1,),
        in_specs=small + [pl.BlockSpec((1, ta, 128), k_idx), pl.BlockSpec((1, ta, 128), v_idx)]
        + [q_spec(part) for part in range(per)],
        out_specs=pl.BlockSpec((1, per * ROW_TILE, 128), o_idx),
        out_shape=jax.ShapeDtypeStruct((batch, nt * ROW_TILE, ATTN_W), BF16),
        scratch_shapes=[pltpu.VMEM((2 * per * ROW_TILE, ta), F32),
                        pltpu.VMEM((2 * per * ROW_TILE, 1), F32)] * 2,
        compiler_params=_cparams("arbitrary"),
        name="diff_attention",
    )(lam_vec, g, k, v, *([q] * per))
    return o_ctx, o_lat


def _dft_tables(n):
    k = np.arange(n, dtype=np.int64)
    ang = 2.0 * np.pi * ((k[:, None] * k[None, :]) % n).astype(np.float64) / n
    return np.cos(ang), np.sin(ang)


def _fourier_kernel(f_ref, cs_ref, dl_ref, dc_ref, w_ref, b_ref, o_ref, ac_ref, al_ref,
                    *, n_ctx, n_lat):
    t = pl.program_id(0)
    b = pl.program_id(1)
    ctx_tiles = n_ctx // ROW_TILE

    def stage1(rows0, n):
        a = jnp.dot(f_ref[0, rows0:rows0 + n, :], cs_ref[...], preferred_element_type=F32)
        return a[:, :F_W].astype(BF16), a[:, F_W:].astype(BF16)

    def stage2(dft, a, n):
        z = jnp.dot(dft, a, preferred_element_type=F32)
        z = z * (1.0 / math.sqrt(n * F_GROUP_W))
        o = jnp.dot(z.astype(BF16), w_ref[...], preferred_element_type=F32) + b_ref[...]
        o_ref[0] = o.astype(BF16)

    @pl.when(t < ctx_tiles)
    def _():
        ac_ref[0:n_ctx, :], ac_ref[n_ctx:, :] = stage1(0, n_ctx)
        stage2(dc_ref[...], ac_ref[...], n_ctx)

    @pl.when(t == ctx_tiles)
    def _():
        al_ref[b, 0:n_lat, :], al_ref[b, n_lat:, :] = stage1(n_ctx, n_lat)

    @pl.when(t >= ctx_tiles)
    def _():
        stage2(dl_ref[...], al_ref[b], n_lat)


def _fourier(f, fourier_w, fourier_b, n_ctx):
    batch, ta, _ = f.shape
    n_lat = ta - n_ctx
    assert n_ctx == ROW_TILE
    nt = ta // ROW_TILE
    ctx_tiles = n_ctx // ROW_TILE
    cc, sc = _dft_tables(F_GROUP_W)
    eye = np.eye(F_GROUPS)
    cs = jnp.asarray(np.concatenate([np.kron(eye, cc), np.kron(eye, sc)], axis=1), BF16)
    cl, sl = _dft_tables(n_lat)
    dft_lat = jnp.asarray(np.concatenate([cl, -sl], axis=1), BF16)
    cx, sx = _dft_tables(n_ctx)
    dft_ctx = jnp.asarray(np.concatenate([cx, -sx], axis=1), BF16)
    w_blk = jnp.einsum('gce,gh->gche', fourier_w, jnp.eye(F_GROUPS, dtype=F32))
    w_blk = w_blk.reshape(F_W, F_W).astype(BF16)
    return pl.pallas_call(
        functools.partial(_fourier_kernel, n_ctx=n_ctx, n_lat=n_lat),
        grid=(nt, batch),
        in_specs=[
                  pl.BlockSpec((1, ta, F_W),
                               lambda t, b: (jnp.where(t <= ctx_tiles, b, batch - 1), 0, 0)),
                  _full((F_W, 2 * F_W)),
                  pl.BlockSpec((ROW_TILE, 2 * n_lat),
                               lambda t, b: (jnp.maximum(t - ctx_tiles, 0), 0)),
                  _full((n_ctx, 2 * n_ctx)),
                  _full((F_W, F_W)), _full((1, F_W))],
        out_specs=pl.BlockSpec((1, ROW_TILE, F_W), lambda t, b: (b, t, 0)),
        out_shape=jax.ShapeDtypeStruct((batch, ta, F_W), BF16),
        scratch_shapes=[pltpu.VMEM((2 * n_ctx, F_W), BF16),
                        pltpu.VMEM((batch, 2 * n_lat, F_W), BF16)],
        compiler_params=_cparams("arbitrary", "arbitrary"),
        name="fourier_mix",
    )(f, cs, dft_lat, dft_ctx, w_blk, fourier_b.reshape(1, F_W))


def _outproj_ln_kernel(ac_ref, al_ref, b_ref, hc_ref, hl_ref, gate_ref, w_ref, g_ref, beta_ref,
                       o_ref, wbf_ref, *, ctx_tiles):
    @pl.when(_first_step())
    def _():
        wbf_ref[...] = w_ref[...].astype(BF16)

    is_ctx = pl.program_id(1) < ctx_tiles
    a = jnp.where(is_ctx, ac_ref[0], al_ref[0])
    h = jnp.where(is_ctx, hc_ref[0], hl_ref[0])
    wa = a.shape[1]
    y = jnp.dot(a, wbf_ref[0:wa, :], preferred_element_type=F32)
    y = y + jnp.dot(b_ref[0], wbf_ref[wa:, :], preferred_element_type=F32)
    o_ref[0] = _layer_norm(ALPHA * h + gate_ref[0] * y, g_ref[...], beta_ref[...])


def _outproj_ln(a_ctx, a_lat, b2, h_ctx, h_lat, mods, layer, w_out, ln_g, ln_b, ctx_tiles):
    batch, ta, _ = b2.shape
    nt = ta // ROW_TILE
    row = lambda w: pl.BlockSpec((1, ROW_TILE, w), lambda b, t: (b, t, 0))
    return pl.pallas_call(
        functools.partial(_outproj_ln_kernel, ctx_tiles=ctx_tiles),
        grid=(batch, nt),
        in_specs=_split_specs(a_lat.shape[2], ctx_tiles) + [row(b2.shape[2])]
        + _split_specs(D_MODEL, ctx_tiles)
        + [_mod_spec(layer, 2, batch, ctx_tiles),
           _resident((D_MODEL, D_MODEL)), _full((1, D_MODEL)), _full((1, D_MODEL))],
        out_specs=row(D_MODEL),
        out_shape=jax.ShapeDtypeStruct((batch, ta, D_MODEL), F32),
        scratch_shapes=[pltpu.VMEM((D_MODEL, D_MODEL), BF16)],
        compiler_params=_cparams("arbitrary", "arbitrary"),
        name="outproj_ln",
    )(a_ctx, a_lat, b2, h_ctx, h_lat, mods, w_out, ln_g.reshape(1, D_MODEL),
      ln_b.reshape(1, D_MODEL))


def _ffn_kernel(h_ref, hp_ref, hn_ref, shift_ref, scale_ref, gate_ref,
                wup_ref, bup_ref, cw_ref, cb_ref, wdn_ref, bdn_ref, g_ref, beta_ref,
                o_ref, uext_ref, ubf_ref, act_ref, *, ctx_tiles, tile_off, nt_seq):
    t = pl.program_id(1) + tile_off
    seg_first = (t == 0) | (t == ctx_tiles)
    seg_last = (t == nt_seq - 1) | (t == ctx_tiles - 1)
    sc = 1.0 + scale_ref[0]
    sh = shift_ref[0]
    h = h_ref[0]
    tm = h.shape[0]
    uext_ref[0:HALO, :] = hp_ref[0] * sc + sh
    uext_ref[HALO:HALO + tm, :] = h * sc + sh
    uext_ref[HALO + tm:, :] = hn_ref[0] * sc + sh
    ubf_ref[...] = uext_ref[...].astype(BF16)
    row8 = lax.broadcasted_iota(jnp.int32, (8, FF_TILE), 0)

    def hidden(col0):
        cols = slice(col0, col0 + FF_TILE)
        zr = jnp.dot(ubf_ref[...], wup_ref[:, cols], preferred_element_type=F32)
        bup = bup_ref[:, cols]
        cw = cw_ref[:, cols]
        bias = cb_ref[:, cols] + (cw[0:1] + cw[1:2] + cw[2:3]) * bup
        z0 = zr[HALO:HALO + tm]
        prev = jnp.where(seg_first, -bup, zr[HALO - 1:HALO])
        nxt = jnp.where(seg_last, -bup, zr[HALO + tm:HALO + tm + 1])
        down = pltpu.roll(z0, 1, 0)
        up = pltpu.roll(z0, tm - 1, 0)
        zm1 = jnp.concatenate([jnp.where(row8 == 0, prev, down[0:8]), down[8:]], axis=0)
        zp1 = jnp.concatenate([up[:tm - 8], jnp.where(row8 == 7, nxt, up[tm - 8:])], axis=0)
        return cw[0:1] * zm1 + cw[1:2] * z0 + cw[2:3] * zp1 + bias

    for j in range(D_FF // FF_TILE):
        val = hidden(j * FF_TILE)
        gat = hidden(D_FF + j * FF_TILE)
        act_ref[:, j * FF_TILE:(j + 1) * FF_TILE] = (val * _silu(gat)).astype(BF16)
    f = jnp.dot(act_ref[...], wdn_ref[...], preferred_element_type=F32) + bdn_ref[...]
    o_ref[0] = _layer_norm(ALPHA * h + gate_ref[0] * f, g_ref[...], beta_ref[...])


def _ffn(h, mods, layer, w_up, b_up, conv_w, conv_b, w_down, b_down, ln_g, ln_b,
         ctx_tiles, tile_off, rows=ROW_TILE):
    batch, ta, _ = h.shape
    nt_seq = ta // rows
    nt = nt_seq - tile_off
    hb = rows // HALO
    n_hblk = ta // HALO
    mspec = lambda j: pl.BlockSpec(
        (1, 1, D_MODEL),
        lambda b, t: ((layer * MOD_ROWS + jnp.where(t + tile_off < ctx_tiles, batch, b)) * 6 + j, 0, 0))
    resident = lambda shape: pl.BlockSpec((None,) + shape, lambda *_: (layer, 0, 0),
                                          pipeline_mode=pl.Buffered(1))
    return pl.pallas_call(
        functools.partial(_ffn_kernel, ctx_tiles=ctx_tiles, tile_off=tile_off, nt_seq=nt_seq),
        grid=(batch, nt),
        in_specs=[pl.BlockSpec((1, rows, D_MODEL), lambda b, t: (b, t + tile_off, 0)),
                  pl.BlockSpec((1, HALO, D_MODEL),
                               lambda b, t: (b, jnp.maximum((t + tile_off) * hb - 1, 0), 0)),
                  pl.BlockSpec((1, HALO, D_MODEL),
                               lambda b, t: (b, jnp.minimum((t + tile_off + 1) * hb, n_hblk - 1), 0)),
                  mspec(3), mspec(4), mspec(5),
                  resident((D_MODEL, 2 * D_FF)), _full((1, 2 * D_FF)),
                  _full((3, 2 * D_FF)), _full((1, 2 * D_FF)),
                  resident((D_FF, D_MODEL)), _full((1, D_MODEL)),
                  _full((1, D_MODEL)), _full((1, D_MODEL))],
        out_specs=pl.BlockSpec((1, rows, D_MODEL), lambda b, t: (b, t, 0)),
        out_shape=jax.ShapeDtypeStruct((batch, nt * rows, D_MODEL), F32),
        scratch_shapes=[pltpu.VMEM((rows + 2 * HALO, D_MODEL), F32),
                        pltpu.VMEM((rows + 2 * HALO, D_MODEL), BF16),
                        pltpu.VMEM((rows, D_FF), BF16)],
        compiler_params=_cparams("parallel", "arbitrary"),
        name="conv_ffn_ln",
    )(h, h, h, mods, mods, mods, w_up, b_up.reshape(1, -1), conv_w, conv_b.reshape(1, -1),
      w_down, b_down.reshape(1, -1), ln_g.reshape(1, -1), ln_b.reshape(1, -1))


def _time_major_rows(b, batch):
    return pl.ds(b, ROW_TILE, stride=batch)


def _inproj_ssm_kernel(h_ref, shift_ref, scale_ref, w_ref, z_ref, xbc_ref, dt_ref, us_ref,
                       wbf_ref, *, batch):
    @pl.when(_first_step())
    def _():
        head = SSD_W + XBC_W
        wbf_ref[0:head, :] = w_ref[0:head, :].astype(BF16)
        wbf_ref[head:head + S5_W, :] = w_ref[head + DT_W:head + DT_W + S5_W, :].astype(BF16)
        pad = jnp.zeros((DT_PAD - DT_W, D_MODEL), F32)
        wbf_ref[head + S5_W:, :] = jnp.concatenate(
            [w_ref[head:head + DT_W, :], pad], axis=0).astype(BF16)

    u = (h_ref[0] * (1.0 + scale_ref[0]) + shift_ref[0]).astype(BF16)
    p = lax.dot_general(u, wbf_ref[...], (((1,), (1,)), ((), ())), preferred_element_type=F32)
    z_ref[0] = p[:, :SSD_W]
    xbc_ref[0] = p[:, SSD_W:SSD_W + XBC_W]
    dt_ref[0] = p[:, SSD_W + XBC_W + S5_W:]
    rows = _time_major_rows(pl.program_id(1), batch)
    for j in range(S5_W // 128):
        col0 = SSD_W + XBC_W + j * 128
        us_ref[j, rows, :] = p[:, col0:col0 + 128]


def _inproj_ssm(h, mods, layer, w_in, ctx_tiles):
    batch, ta, _ = h.shape
    nt = ta // ROW_TILE
    row = lambda w: pl.BlockSpec((1, ROW_TILE, w), lambda t, b: (b, t, 0))
    mspec = lambda j: pl.BlockSpec(
        (1, 1, D_MODEL),
        lambda t, b: ((layer * MOD_ROWS + jnp.where(t < ctx_tiles, batch, b)) * 6 + j, 0, 0))
    return pl.pallas_call(
        functools.partial(_inproj_ssm_kernel, batch=batch),
        grid=(nt, batch),
        in_specs=[row(D_MODEL), mspec(0), mspec(1), _resident((SSM_IN_W, D_MODEL))],
        out_specs=[row(SSD_W), row(XBC_W), row(DT_PAD),
                   pl.BlockSpec((S5_W // 128, ROW_TILE * batch, 128), lambda t, b: (0, t, 0))],
        out_shape=[jax.ShapeDtypeStruct((batch, ta, SSD_W), F32),
                   jax.ShapeDtypeStruct((batch, ta, XBC_W), F32),
                   jax.ShapeDtypeStruct((batch, ta, DT_PAD), F32),
                   jax.ShapeDtypeStruct((S5_W // 128, ta * batch, 128), F32)],
        scratch_shapes=[pltpu.VMEM((SSM_IN_PAD, D_MODEL), BF16)],
        compiler_params=_cparams("arbitrary", "arbitrary"),
        name="inproj_ssm",
    )(h, mods, mods, jnp.swapaxes(w_in, 0, 1))


def _cumsum_rows(v):
    n = v.shape[0]
    row = lax.broadcasted_iota(jnp.int32, v.shape, 0)
    s = 1
    while s < n:
        v = v + jnp.where(row >= s, pltpu.roll(v, s, 0), 0.0)
        s *= 2
    return v


def _expand_heads(v, e_ref):
    hi = v.astype(BF16)
    lo = (v - hi.astype(F32)).astype(BF16)
    e = e_ref[...]
    return (jnp.dot(hi, e, preferred_element_type=F32)
            + jnp.dot(lo, e, preferred_element_type=F32))


def _ssd_kernel(xbc_ref, xp_ref, xn_ref, dt_ref, z_ref, cw_ref, cb_ref, alog_ref, dtb_ref,
                dsk_ref, ng_ref, ef_ref, eb_ref, o_ref,
                xs_ref, cd_ref, st_ref, dec_ref, y_ref, *, n_tiles, ctx_tiles):
    s = pl.program_id(1)
    q = SSD_CHUNK
    cpt = ROW_TILE // q
    n_chunks = n_tiles * cpt
    ctx_chunks = ctx_tiles * cpt
    gw = SSD_W // SSD_GROUPS
    hpg = SSD_HEADS // SSD_GROUPS

    @pl.when(s < n_tiles)
    def _phase0():
        p = s
        seg_first = (p == 0) | (p == ctx_tiles)
        seg_last = (p == ctx_tiles - 1) | (p == n_tiles - 1)
        xr = xbc_ref[0]
        prev = jnp.where(seg_first, 0.0, xp_ref[0, HALO - 1:HALO, :])
        nxt = jnp.where(seg_last, 0.0, xn_ref[0, 0:1, :])
        row8 = lax.broadcasted_iota(jnp.int32, (8, XBC_W), 0)
        down = pltpu.roll(xr, 1, 0)
        up = pltpu.roll(xr, ROW_TILE - 1, 0)
        xm1 = jnp.concatenate([jnp.where(row8 == 0, prev, down[0:8]), down[8:]], axis=0)
        xp1 = jnp.concatenate([up[:ROW_TILE - 8], jnp.where(row8 == 7, nxt, up[ROW_TILE - 8:])],
                              axis=0)
        cw = cw_ref[...]
        xs_tile = _silu(cw[0:1] * xm1 + cw[1:2] * xr + cw[2:3] * xp1 + cb_ref[...])
        xs_ref[p] = xs_tile

        raw = dt_ref[0] + dtb_ref[...]
        dt_tile = jnp.maximum(raw, 0.0) + jnp.log1p(jnp.exp(-jnp.abs(raw)))
        a_row = -jnp.exp(alog_ref[...])
        for i in range(cpt):
            c = p * cpt + i
            xs = xs_tile[i * q:(i + 1) * q]
            dtv = dt_tile[i * q:(i + 1) * q]
            adt = dtv * a_row
            cum = _cumsum_rows(adt)
            tot = cum[q - 1:q, :]
            lane = lax.broadcasted_iota(jnp.int32, cum.shape, 1)
            cc = jnp.where(lane < SSD_HEADS, cum, tot - cum + adt)
            cd_ref[c, 0] = cc
            cd_ref[c, 1] = dtv
            w_end = jnp.exp(tot - cc) * dtv
            dec16 = jnp.broadcast_to(jnp.exp(tot), (16, DT_PAD))
            x = xs[:, :SSD_W]
            for d, e_ref in enumerate((ef_ref, eb_ref)):
                wx = (_expand_heads(w_end, e_ref) * x).astype(BF16)
                for g in range(SSD_GROUPS):
                    bmt = xs[:, SSD_W + g * SSD_STATE:SSD_W + (g + 1) * SSD_STATE].T.astype(BF16)
                    st_ref[c, d, :, g * gw:(g + 1) * gw] = jnp.dot(
                        bmt, wx[:, g * gw:(g + 1) * gw], preferred_element_type=F32)
                dec_ref[c, d] = _expand_heads(dec16, e_ref)[0:8]

    @pl.when(s == n_tiles)
    def _recurrence():
        fwd = list(range(n_chunks))
        bwd = list(range(ctx_chunks - 1, -1, -1)) + list(range(n_chunks - 1, ctx_chunks - 1, -1))
        for d, order in enumerate((fwd, bwd)):
            for col0 in range(0, SSD_W, 128):
                cols = slice(col0, col0 + 128)
                state = jnp.zeros((SSD_STATE, 128), F32)
                for ci in order:
                    contrib = st_ref[ci, d, :, cols]
                    st_ref[ci, d, :, cols] = state
                    state = state * dec_ref[ci, d, 0:1, cols] + contrib

    @pl.when(s >= n_tiles)
    def _phase1():
        p = s - n_tiles + ctx_tiles
        rowi = lax.broadcasted_iota(jnp.int32, (q, q), 0)
        coli = lax.broadcasted_iota(jnp.int32, (q, q), 1)
        lower = coli <= rowi
        upper = coli >= rowi
        lane = lax.broadcasted_iota(jnp.int32, (q, 128), 1)
        neg = jnp.float32(-jnp.inf)
        for i in range(cpt):
            c = p * cpt + i
            rows = slice(i * q, (i + 1) * q)
            xs = xs_ref[p, rows, :]
            x = xs[:, :SSD_W]
            cc = cd_ref[c, 0]
            dtv = cd_ref[c, 1]
            cct = cc.T
            dtt = dtv.T
            ecc = jnp.exp(cc)
            ef = _expand_heads(ecc, ef_ref)
            eb = _expand_heads(ecc, eb_ref)
            for g in range(SSD_GROUPS):
                bm = xs[:, SSD_W + g * SSD_STATE:SSD_W + (g + 1) * SSD_STATE].astype(BF16)
                cm = xs[:, SSD_W + (SSD_GROUPS + g) * SSD_STATE:
                        SSD_W + (SSD_GROUPS + g + 1) * SSD_STATE].astype(BF16)
                gmat = lax.dot_general(cm, bm, (((1,), (1,)), ((), ())),
                                       preferred_element_type=F32)
                sl = slice(g * gw, (g + 1) * gw)
                yoff = (ef[:, sl] * jnp.dot(cm, st_ref[c, 0, :, sl].astype(BF16),
                                            preferred_element_type=F32)
                        + eb[:, sl] * jnp.dot(cm, st_ref[c, 1, :, sl].astype(BF16),
                                              preferred_element_type=F32))
                for pair in range(hpg // 2):
                    col0 = g * gw + pair * 128
                    xpair = x[:, col0:col0 + 128].astype(BF16)
                    res = []
                    for hh in range(2):
                        hd = g * hpg + pair * 2 + hh
                        hb_ = SSD_HEADS + hd
                        lf = jnp.exp(jnp.where(lower, cc[:, hd:hd + 1] - cct[hd:hd + 1, :], neg))
                        lb = jnp.exp(jnp.where(upper, cc[:, hb_:hb_ + 1] - cct[hb_:hb_ + 1, :],
                                               neg))
                        mt = gmat * (lf * dtt[hd:hd + 1, :] + lb * dtt[hb_:hb_ + 1, :])
                        res.append(jnp.dot(mt.astype(BF16), xpair, preferred_element_type=F32))
                    ydiag = jnp.where(lane < SSD_HEAD_DIM, res[0], res[1])
                    y_ref[rows, col0:col0 + 128] = (
                        ydiag + yoff[:, pair * 128:(pair + 1) * 128]
                        + dsk_ref[:, col0:col0 + 128] * x[:, col0:col0 + 128])
        gated = y_ref[...] * _silu(z_ref[0])
        normed = gated * lax.rsqrt(jnp.mean(gated * gated, axis=-1, keepdims=True) + LN_EPS)
        o_ref[0] = (normed * ng_ref[...]).astype(BF16)


def _ssd(xbc, dt, z, conv_w, conv_b, a_log, dt_bias, d_skip, norm_g, n_ctx):
    batch, ta, _ = xbc.shape
    q = SSD_CHUNK
    n_chunks = ta // q
    n_tiles = ta // ROW_TILE
    ctx_tiles = n_ctx // ROW_TILE
    hb = ROW_TILE // HALO
    n_hblk = ta // HALO
    pad24 = lambda v: jnp.pad(v.reshape(1, DT_W), ((0, 0), (0, DT_PAD - DT_W)))
    heads = np.arange(SSD_HEADS)
    ef = np.zeros((DT_PAD, SSD_W), np.float32)
    eb = np.zeros((DT_PAD, SSD_W), np.float32)
    for hd in heads:
        ef[hd, hd * SSD_HEAD_DIM:(hd + 1) * SSD_HEAD_DIM] = 1.0
        eb[SSD_HEADS + hd, hd * SSD_HEAD_DIM:(hd + 1) * SSD_HEAD_DIM] = 1.0
    dsk = jnp.repeat(d_skip.astype(F32), SSD_HEAD_DIM).reshape(1, SSD_W)
    ph0 = lambda s: s < n_tiles
    tile = lambda w: pl.BlockSpec(
        (1, ROW_TILE, w), lambda b, s: (b, jnp.where(ph0(s), s, n_tiles - 1), 0))
    return pl.pallas_call(
        functools.partial(_ssd_kernel, n_tiles=n_tiles, ctx_tiles=ctx_tiles),
        grid=(batch, 2 * n_tiles - ctx_tiles),
        in_specs=[tile(XBC_W),
                  pl.BlockSpec((1, HALO, XBC_W),
                               lambda b, s: (b, jnp.where(ph0(s), jnp.maximum(s * hb - 1, 0), 0), 0)),
                  pl.BlockSpec((1, HALO, XBC_W),
                               lambda b, s: (b, jnp.where(ph0(s), jnp.minimum((s + 1) * hb, n_hblk - 1), 0), 0)),
                  tile(DT_PAD),
                  pl.BlockSpec((1, ROW_TILE, SSD_W),
                               lambda b, s: (b, jnp.where(ph0(s), 0, s - n_tiles + ctx_tiles), 0)),
                  _full((3, XBC_W)), _full((1, XBC_W)), _full((1, DT_PAD)), _full((1, DT_PAD)),
                  _full((1, SSD_W)), _full((1, SSD_W)),
                  _full((DT_PAD, SSD_W)), _full((DT_PAD, SSD_W))],
        out_specs=pl.BlockSpec(
            (1, ROW_TILE, SSD_W), lambda b, s: (b, jnp.where(ph0(s), 0, s - n_tiles), 0)),
        out_shape=jax.ShapeDtypeStruct((batch, ta - n_ctx, SSD_W), BF16),
        scratch_shapes=[pltpu.VMEM((n_tiles, ROW_TILE, XBC_W), F32),
                        pltpu.VMEM((n_chunks, 2, q, DT_PAD), F32),
                        pltpu.VMEM((n_chunks, 2, SSD_STATE, SSD_W), F32),
                        pltpu.VMEM((n_chunks, 2, 8, SSD_W), F32),
                        pltpu.VMEM((ROW_TILE, SSD_W), F32)],
        compiler_params=_cparams("parallel", "arbitrary"),
        name="ssd_bidir",
    )(xbc, xbc, xbc, dt, z, conv_w, conv_b.reshape(1, XBC_W), pad24(a_log), pad24(dt_bias),
      dsk, norm_g.reshape(1, SSD_W), jnp.asarray(ef, BF16), jnp.asarray(eb, BF16))


def _s5_disc_kernel(lr_ref, li_ref, ldt_ref, bre_ref, bim_ref, cre_ref, cim_ref,
                    a_ref, bd_ref, cd_ref):
    lr, li = lr_ref[...], li_ref[...]
    dt = jnp.exp(ldt_ref[...])
    mag = jnp.exp(dt * lr)
    ab_re, ab_im = mag * jnp.cos(dt * li), mag * jnp.sin(dt * li)
    den = lr * lr + li * li
    k_re = ((ab_re - 1.0) * lr + ab_im * li) / den
    k_im = (ab_im * lr - (ab_re - 1.0) * li) / den
    bre, bim = bre_ref[...], bim_ref[...]
    for d in range(2):
        a_ref[d, :, 0:S5_NSTATE] = jnp.broadcast_to(ab_re[d:d + 1], (8, S5_NSTATE))
        a_ref[d, :, S5_NSTATE:] = jnp.broadcast_to(ab_im[d:d + 1], (8, S5_NSTATE))
        kr, ki = k_re[d:d + 1], k_im[d:d + 1]
        bd_ref[d, :, 0:S5_NSTATE] = (kr * bre - ki * bim).astype(BF16)
        bd_ref[d, :, S5_NSTATE:] = (kr * bim + ki * bre).astype(BF16)
        cd_ref[d, 0:S5_NSTATE, :] = cre_ref[d].astype(BF16)
        cd_ref[d, S5_NSTATE:, :] = (-cim_ref[d]).astype(BF16)


def _s5_discretize(lam_re, lam_im, log_dt, b_re, b_im, c_re, c_im):
    eye = jnp.eye(S5_GROUPS, dtype=F32)
    bd = lambda b: jnp.einsum('gph,gk->ghkp', b, eye).reshape(S5_W, S5_NSTATE)
    cd = lambda cc: jnp.einsum('dghp,gk->dgpkh', cc, eye).reshape(2, S5_NSTATE, S5_W)
    ldt = jnp.repeat(log_dt, S5_STATE, axis=-1)
    return pl.pallas_call(
        _s5_disc_kernel,
        out_shape=[jax.ShapeDtypeStruct((2, 8, 2 * S5_NSTATE), F32),
                   jax.ShapeDtypeStruct((2, S5_W, 2 * S5_NSTATE), BF16),
                   jax.ShapeDtypeStruct((2, 2 * S5_NSTATE, S5_W), BF16)],
        compiler_params=pltpu.CompilerParams(vmem_limit_bytes=VMEM_LIMIT_BYTES),
        name="s5_discretize",
    )(lam_re.reshape(2, S5_NSTATE), lam_im.reshape(2, S5_NSTATE), ldt,
      bd(b_re), bd(b_im), cd(c_re), cd(c_im))


S5_TIME_CHUNK = 256
S5_UNROLL = 8


def _s5_scan_kernel(u_ref, a_ref, bd_ref, cd_ref, o_ref, hs_ref, carry_ref, *, batch):
    d = pl.program_id(0)
    j = pl.program_id(1)
    n = S5_NSTATE

    @pl.when(j == 0)
    def _():
        carry_ref[...] = jnp.zeros_like(carry_ref)

    half = hs_ref.shape[0] // 2
    n_slab = S5_W // 128
    for r in (0, half):
        u = jnp.concatenate([u_ref[s, r:r + half, :] for s in range(n_slab)], axis=1)
        hs_ref[r:r + half, :] = jnp.dot(u.astype(BF16), bd_ref[0], preferred_element_type=F32)
    ar = a_ref[0, :, 0:n]
    ai = a_ref[0, :, n:]
    if batch != 8:
        ar = jnp.broadcast_to(ar[0:1], (batch, n))
        ai = jnp.broadcast_to(ai[0:1], (batch, n))

    def body(i, carry):
        hr, hi = carry
        for s in range(S5_UNROLL):
            step = i * S5_UNROLL + s
            step = jnp.where(d == 0, step, S5_TIME_CHUNK - 1 - step)
            rows = pl.ds(pl.multiple_of(step * batch, batch), batch)
            nr = ar * hr - ai * hi + hs_ref[rows, 0:n]
            ni = ar * hi + ai * hr + hs_ref[rows, n:]
            hs_ref[rows, 0:n] = nr
            hs_ref[rows, n:] = ni
            hr, hi = nr, ni
        return hr, hi

    hr, hi = lax.fori_loop(0, S5_TIME_CHUNK // S5_UNROLL, body,
                           (carry_ref[:, 0:n], carry_ref[:, n:]))
    carry_ref[:, 0:n] = hr
    carry_ref[:, n:] = hi
    for r in (0, half):
        y = jnp.dot(hs_ref[r:r + half, :].astype(BF16), cd_ref[0], preferred_element_type=F32)
        for s in range(n_slab):
            o_ref[0, s, r:r + half, :] = y[:, s * 128:(s + 1) * 128]


def _s5_scan(us_flat, a, bd, cd, batch, n_ctx):
    n_slab, rows_total, _ = us_flat.shape
    ta = rows_total // batch
    tc = S5_TIME_CHUNK
    n_chunks = ta // tc
    ctx_chunks = n_ctx // tc
    blk = tc * batch

    def chunk_of(d, j):
        bwd = jnp.where(j < ctx_chunks, ctx_chunks - 1 - j, n_chunks - 1 - (j - ctx_chunks))
        return jnp.where(d == 0, j, bwd)

    return pl.pallas_call(
        functools.partial(_s5_scan_kernel, batch=batch),
        grid=(2, n_chunks),
        in_specs=[pl.BlockSpec((n_slab, blk, 128), lambda d, j: (0, chunk_of(d, j), 0)),
                  pl.BlockSpec((1, 8, 2 * S5_NSTATE), lambda d, j: (d, 0, 0)),
                  pl.BlockSpec((1, S5_W, 2 * S5_NSTATE), lambda d, j: (d, 0, 0)),
                  pl.BlockSpec((1, 2 * S5_NSTATE, S5_W), lambda d, j: (d, 0, 0))],
        out_specs=pl.BlockSpec((1, n_slab, blk, 128), lambda d, j: (d, 0, chunk_of(d, j), 0)),
        out_shape=jax.ShapeDtypeStruct((2, n_slab, rows_total, 128), F32),
        scratch_shapes=[pltpu.VMEM((blk, 2 * S5_NSTATE), F32),
                        pltpu.VMEM((batch, 2 * S5_NSTATE), F32)],
        compiler_params=_cparams("arbitrary", "arbitrary"),
        name="s5_scan",
    )(us_flat, a, bd, cd)


def _merge_ln_kernel(gs_ref, y5_ref, us_ref, h_ref, gate_ref, dd_ref, gw_ref, gb_ref,
                     w_ref, g_ref, beta_ref, o_ref, wbf_ref, *, batch):
    @pl.when(_first_step())
    def _():
        wbf_ref[...] = w_ref[...].astype(BF16)

    rows = _time_major_rows(pl.program_id(1), batch)
    y5 = jnp.concatenate([y5_ref[0, s, rows, :] + y5_ref[1, s, rows, :]
                          for s in range(S5_W // 128)], axis=1)
    us = jnp.concatenate([us_ref[s, rows, :] for s in range(S5_W // 128)], axis=1)
    ge = jax.nn.gelu(y5 + dd_ref[...] * us)
    s5 = ge * jax.nn.sigmoid(
        jnp.dot(ge.astype(BF16), gw_ref[...], preferred_element_type=F32) + gb_ref[...])
    y = jnp.dot(gs_ref[0], wbf_ref[0:SSD_W, :], preferred_element_type=F32)
    y = y + jnp.dot(s5.astype(BF16), wbf_ref[SSD_W:, :], preferred_element_type=F32)
    o_ref[0] = _layer_norm(ALPHA * h_ref[0] + gate_ref[0] * y, g_ref[...], beta_ref[...])


def _merge_ln(g_ssd, y5, us_t, h, mods, layer, s5_d, glu_w, glu_b, w_out, ln_g, ln_b, ctx_tiles):
    batch, ta, _ = h.shape
    nt = ta // ROW_TILE - ctx_tiles
    n_slab = S5_W // 128
    tm_rows = ROW_TILE * batch
    return pl.pallas_call(
        functools.partial(_merge_ln_kernel, batch=batch),
        grid=(nt, batch),
        in_specs=[pl.BlockSpec((1, ROW_TILE, SSD_W), lambda t, b: (b, t, 0)),
                  pl.BlockSpec((2, n_slab, tm_rows, 128), lambda t, b: (0, 0, t + ctx_tiles, 0)),
                  pl.BlockSpec((n_slab, tm_rows, 128), lambda t, b: (0, t + ctx_tiles, 0)),
                  pl.BlockSpec((1, ROW_TILE, D_MODEL), lambda t, b: (b, t + ctx_tiles, 0)),
                  pl.BlockSpec((1, 1, D_MODEL), lambda t, b: ((layer * MOD_ROWS + b) * 6 + 2, 0, 0)),
                  _full((1, S5_W)), _full((S5_W, S5_W)), _full((1, S5_W)),
                  _resident((D_MODEL, D_MODEL)), _full((1, D_MODEL)), _full((1, D_MODEL))],
        out_specs=pl.BlockSpec((1, ROW_TILE, D_MODEL), lambda t, b: (b, t, 0)),
        out_shape=jax.ShapeDtypeStruct((batch, nt * ROW_TILE, D_MODEL), F32),
        scratch_shapes=[pltpu.VMEM((D_MODEL, D_MODEL), BF16)],
        compiler_params=_cparams("arbitrary", "arbitrary"),
        name="merge_outproj_ln",
    )(g_ssd, y5, us_t, h, mods, s5_d.reshape(1, S5_W), glu_w.astype(BF16),
      glu_b.reshape(1, S5_W), w_out, ln_g.reshape(1, -1), ln_b.reshape(1, -1))


def _attn_layer(h_ctx, h_lat, mods, layer, i, p, keep_ctx):
    n_ctx = h_ctx.shape[1]
    ctx_tiles = n_ctx // ROW_TILE
    lam_init = 0.8 - 0.6 * math.exp(-0.3 * layer)
    cos, sin = _rope_tables(h_lat.shape[1], n_ctx)
    q, k, v, f = _inproj_attn(h_ctx, h_lat, mods, layer, p['attn_w_in'][i],
                              cos, sin, ctx_tiles)
    o_ctx, o_lat = _attention(q, k, v, p['attn_lambda'][i], p['attn_subln_g'][i], lam_init,
                              ctx_tiles)
    fm = _fourier(f, p['fourier_w'][i], p['fourier_b'][i], n_ctx)
    h1 = _outproj_ln(o_ctx, o_lat, fm, h_ctx, h_lat, mods, layer,
                     p['attn_w_out'][i],
                     p['ln_g'][layer, 0], p['ln_b'][layer, 0], ctx_tiles)
    return _ffn(h1, mods, layer, p['ffn_w_up_bf16'], p['ffn_b_up'][layer],
                p['ffn_conv_w'][layer], p['ffn_conv_b'][layer],
                p['ffn_w_down_bf16'], p['ffn_b_down'][layer],
                p['ln_g'][layer, 1], p['ln_b'][layer, 1], ctx_tiles, 0 if keep_ctx else ctx_tiles)


def _ssm_layer(h, mods, layer, i, n_ctx, p, keep_ctx):
    assert not keep_ctx, "an SSM layer that must also emit context rows is not implemented"
    ctx_tiles = n_ctx // ROW_TILE
    batch = h.shape[0]
    z, xbc, dt, us_t = _inproj_ssm(h, mods, layer, p['ssm_w_in'][i], ctx_tiles)
    g_ssd = _ssd(xbc, dt, z, p['ssd_conv_w'][i], p['ssd_conv_b'][i], p['ssd_a_log'][i],
                 p['ssd_dt_bias'][i], p['ssd_d'][i], p['ssd_norm_g'][i], n_ctx)
    a, bd, cd = _s5_discretize(p['s5_lambda_re'][i], p['s5_lambda_im'][i], p['s5_log_dt'][i],
                               p['s5_b_re'][i], p['s5_b_im'][i], p['s5_c_re'][i], p['s5_c_im'][i])
    y5 = _s5_scan(us_t, a, bd, cd, batch, n_ctx)
    h1 = _merge_ln(g_ssd, y5, us_t, h, mods, layer, p['s5_d'][i], p['s5_glu_w'][i],
                   p['s5_glu_b'][i], p['ssm_w_out'][i],
                   p['ln_g'][layer, 0], p['ln_b'][layer, 0], ctx_tiles)
    return _ffn(h1, mods, layer, p['ffn_w_up_bf16'], p['ffn_b_up'][layer],
                p['ffn_conv_w'][layer], p['ffn_conv_b'][layer],
                p['ffn_w_down_bf16'], p['ffn_b_down'][layer],
                p['ln_g'][layer, 1], p['ln_b'][layer, 1], 0, 0, rows=FFN_LAT_ROWS)


def kernel(x, c, ctx, c_ctx, ada_w, ada_b, ln_g, ln_b, ffn_w_up, ffn_b_up, ffn_conv_w, ffn_conv_b, ffn_w_down, ffn_b_down, attn_w_in, attn_lambda, attn_subln_g, fourier_w, fourier_b, attn_w_out, ssm_w_in, ssd_conv_w, ssd_conv_b, ssd_a_log, ssd_dt_bias, ssd_d, ssd_norm_g, s5_lambda_re, s5_lambda_im, s5_log_dt, s5_b_re, s5_b_im, s5_c_re, s5_c_im, s5_d, s5_glu_w, s5_glu_b, ssm_w_out):
    p = dict(ln_g=ln_g, ln_b=ln_b, ffn_w_up=ffn_w_up, ffn_b_up=ffn_b_up, ffn_conv_w=ffn_conv_w,
             ffn_conv_b=ffn_conv_b, ffn_w_down=ffn_w_down, ffn_b_down=ffn_b_down,
             attn_w_in=attn_w_in, attn_lambda=attn_lambda, attn_subln_g=attn_subln_g,
             fourier_w=fourier_w, fourier_b=fourier_b, attn_w_out=attn_w_out, ssm_w_in=ssm_w_in,
             ssd_conv_w=ssd_conv_w, ssd_conv_b=ssd_conv_b, ssd_a_log=ssd_a_log,
             ssd_dt_bias=ssd_dt_bias, ssd_d=ssd_d, ssd_norm_g=ssd_norm_g,
             s5_lambda_re=s5_lambda_re, s5_lambda_im=s5_lambda_im, s5_log_dt=s5_log_dt,
             s5_b_re=s5_b_re, s5_b_im=s5_b_im, s5_c_re=s5_c_re, s5_c_im=s5_c_im, s5_d=s5_d,
             s5_glu_w=s5_glu_w, s5_glu_b=s5_glu_b, ssm_w_out=ssm_w_out)
    batch, n_lat, _ = x.shape
    n_ctx = ctx.shape[1]
    assert n_ctx == ROW_TILE and n_lat % ROW_TILE == 0 and batch < MOD_ROWS
    mods = _ada_mods(c, c_ctx, ada_w, ada_b)
    p['ffn_w_up_bf16'] = ffn_w_up.astype(BF16)
    p['ffn_w_down_bf16'] = ffn_w_down.astype(BF16)
    assert DEPTH == 2
    h = _attn_layer(ctx, x, mods, 0, 0, p, keep_ctx=True)
    return _ssm_layer(h, mods, 1, 0, n_ctx, p, keep_ctx=False)
```

```python
import functools
import math

import numpy as np
import jax
import jax.numpy as jnp
from jax import lax
from jax.experimental import pallas as pl
from jax.experimental.pallas import tpu as pltpu

F32 = jnp.float32
BF16 = jnp.bfloat16

D_MODEL = 1024
DEPTH = 2
GRID_W = 64
ROPE_BASE = 10000.0
LN_EPS = 1e-5
ALPHA = (2 * DEPTH) ** 0.25
ATTN_W = 768
ATTN_HEADS = 6
ATTN_HEAD_DIM = 64
F_W = 256
F_GROUPS = 4
F_GROUP_W = 64
ATTN_IN_W = 2 * ATTN_W + ATTN_W + F_W
SSD_W = 768
SSD_HEADS = 12
SSD_HEAD_DIM = 64
SSD_GROUPS = 2
SSD_STATE = 128
SSD_CHUNK = 128
XBC_W = SSD_W + 2 * SSD_GROUPS * SSD_STATE
DT_W = 2 * SSD_HEADS
DT_PAD = 128
S5_W = 256
S5_GROUPS = 16
S5_GROUP_W = 16
S5_STATE = 64
S5_NSTATE = S5_GROUPS * S5_STATE
SSM_IN_W = SSD_W + XBC_W + DT_W + S5_W
SSM_IN_PAD = SSD_W + XBC_W + S5_W + DT_PAD
D_FF = 2816
FF_TILE = 256

ROW_TILE = 256
ATTN_Q_SUB = 128
ATTN_ITEM_TILES = 2
FFN_LAT_ROWS = 512
BATCH_GROUP = 2
HALO = 8
MOD_ROWS = 16
VMEM_LIMIT_BYTES = 56 * 1024 * 1024


def _cparams(*sem):
    return pltpu.CompilerParams(dimension_semantics=sem, vmem_limit_bytes=VMEM_LIMIT_BYTES)


def _silu(v):
    return v * jax.nn.sigmoid(v)


def _layer_norm(v, g, b):
    mu = jnp.mean(v, axis=-1, keepdims=True)
    d = v - mu
    var = jnp.mean(d * d, axis=-1, keepdims=True)
    return d * lax.rsqrt(var + LN_EPS) * g + b


def _full(shape):
    nd = len(shape)
    return pl.BlockSpec(shape, lambda *_: (0,) * nd)


def _resident(shape):
    nd = len(shape)
    return pl.BlockSpec(shape, lambda *_: (0,) * nd, pipeline_mode=pl.Buffered(1))


def _first_step():
    return (pl.program_id(0) == 0) & (pl.program_id(1) == 0)


def _mod_spec(layer, j, batch, ctx_tiles):
    def idx(b, t):
        row = jnp.where(t < ctx_tiles, batch, b)
        return ((layer * MOD_ROWS + row) * 6 + j, 0, 0)
    return pl.BlockSpec((1, 1, D_MODEL), idx)


def _mod_specs_grouped(layer, j, batch, ctx_tiles):
    def spec(i):
        def idx(bg, t):
            row = jnp.where(t < ctx_tiles, batch, bg * BATCH_GROUP + i)
            return ((layer * MOD_ROWS + row) * 6 + j, 0, 0)
        return pl.BlockSpec((1, 1, D_MODEL), idx)
    return [spec(i) for i in range(BATCH_GROUP)]


def _ada_kernel(c_ref, w_ref, b_ref, o_ref):
    s = _silu(c_ref[...])
    w = w_ref[0]
    s_hi = s.astype(BF16)
    s_lo = (s - s_hi.astype(F32)).astype(BF16)
    w_hi = w.astype(BF16)
    w_lo = (w - w_hi.astype(F32)).astype(BF16)
    dot = functools.partial(jnp.dot, preferred_element_type=F32)
    o_ref[0] = dot(s_hi, w_hi) + (dot(s_lo, w_hi) + dot(s_hi, w_lo)) + b_ref[0]


def _ada_mods(c, c_ctx, ada_w, ada_b):
    batch = c.shape[0]
    nl = ada_w.shape[0]
    c_all = jnp.concatenate(
        [c, c_ctx[None], jnp.zeros((MOD_ROWS - batch - 1, D_MODEL), F32)], axis=0)
    out = pl.pallas_call(
        _ada_kernel,
        grid=(nl, 6),
        in_specs=[_full((MOD_ROWS, D_MODEL)),
                  pl.BlockSpec((1, D_MODEL, D_MODEL), lambda l, j: (l, 0, j)),
                  pl.BlockSpec((1, 1, D_MODEL), lambda l, j: (l, 0, j))],
        out_specs=pl.BlockSpec((1, MOD_ROWS, D_MODEL), lambda l, j: (l, 0, j)),
        out_shape=jax.ShapeDtypeStruct((nl, MOD_ROWS, 6 * D_MODEL), F32),
        compiler_params=_cparams("arbitrary", "arbitrary"),
        name="ada_mods",
    )(c_all, ada_w, ada_b.reshape(nl, 1, 6 * D_MODEL))
    return out.reshape(nl * MOD_ROWS * 6, 1, D_MODEL)


def _modulated_rows(h_of, shift_refs, scale_refs):
    return jnp.concatenate(
        [(h_of(i) * (1.0 + scale_refs[i][0]) + shift_refs[i][0]).astype(BF16)
         for i in range(BATCH_GROUP)], axis=0)


def _store_rows(ref, val):
    rows = ref.shape[1]
    for i in range(BATCH_GROUP):
        ref[i] = val[i * rows:(i + 1) * rows].astype(ref.dtype)


def _inproj_attn_kernel(hc_ref, hl_ref, *rest, ctx_tiles):
    g = BATCH_GROUP
    shift_refs, scale_refs = rest[:g], rest[g:2 * g]
    w_ref, cos_ref, sin_ref, q_ref, k_ref, v_ref, f_ref, wbf_ref = rest[2 * g:]

    @pl.when(_first_step())
    def _():
        wbf_ref[...] = w_ref[...].astype(BF16)

    is_ctx = pl.program_id(1) < ctx_tiles
    u = _modulated_rows(lambda i: jnp.where(is_ctx, hc_ref[i], hl_ref[i]), shift_refs, scale_refs)
    p = jnp.dot(u, wbf_ref[...], preferred_element_type=F32)
    cos = jnp.concatenate([cos_ref[...]] * g, axis=0)
    sin = jnp.concatenate([sin_ref[...]] * g, axis=0)
    lane = lax.broadcasted_iota(jnp.int32, cos.shape, 1)
    first_half = (lane % ATTN_HEAD_DIM) < (ATTN_HEAD_DIM // 2)

    def rope(blk):
        partner = jnp.where(first_half, pltpu.roll(blk, 128 - 32, 1), pltpu.roll(blk, 32, 1))
        return blk * cos + partner * sin

    qk_scale = ATTN_HEAD_DIM ** -0.5 * math.log2(math.e)
    rows = q_ref.shape[1]
    for i in range(ATTN_HEADS):
        lo, hi = i * 128, (i + 1) * 128
        qh = (rope(p[:, lo:hi]) * qk_scale).astype(BF16)
        kh = rope(p[:, ATTN_W + lo:ATTN_W + hi]).astype(BF16)
        for b in range(g):
            q_ref[b, :, lo:hi] = qh[b * rows:(b + 1) * rows]
            k_ref[b, :, lo:hi] = kh[b * rows:(b + 1) * rows]
    _store_rows(v_ref, p[:, 2 * ATTN_W:3 * ATTN_W])
    _store_rows(f_ref, p[:, 3 * ATTN_W:])


def _rope_tables(n_lat, n_ctx):
    rows = n_lat // GRID_W
    row = jnp.repeat(jnp.arange(rows, dtype=F32), GRID_W)
    col = jnp.tile(jnp.arange(GRID_W, dtype=F32), rows)
    n_freq = ATTN_HEAD_DIM // 4
    inv_freq = ROPE_BASE ** (-jnp.arange(n_freq, dtype=F32) / n_freq)
    ang = jnp.concatenate([row[:, None] * inv_freq, col[:, None] * inv_freq], axis=-1)
    cos, sin = jnp.cos(ang), jnp.sin(ang)
    cos128 = jnp.tile(cos, (1, 4))
    sin128 = jnp.tile(jnp.concatenate([-sin, sin], axis=-1), (1, 2))
    cos_all = jnp.concatenate([jnp.ones((n_ctx, 128), F32), cos128], axis=0)
    sin_all = jnp.concatenate([jnp.zeros((n_ctx, 128), F32), sin128], axis=0)
    return cos_all, sin_all


def _split_specs(width, ctx_tiles):
    g = BATCH_GROUP
    return [pl.BlockSpec((g, ROW_TILE, width), lambda b, t: (b, jnp.minimum(t, ctx_tiles - 1), 0)),
            pl.BlockSpec((g, ROW_TILE, width), lambda b, t: (b, jnp.maximum(t - ctx_tiles, 0), 0))]


def _inproj_attn(h_ctx, h_lat, mods, layer, w_in, cos, sin, ctx_tiles):
    batch = h_lat.shape[0]
    assert batch % BATCH_GROUP == 0
    ta = h_ctx.shape[1] + h_lat.shape[1]
    nt = ta // ROW_TILE
    row = lambda w: pl.BlockSpec((BATCH_GROUP, ROW_TILE, w), lambda b, t: (b, t, 0))
    tab = pl.BlockSpec((ROW_TILE, 128), lambda b, t: (t, 0))
    n_mod = 2 * BATCH_GROUP
    return pl.pallas_call(
        functools.partial(_inproj_attn_kernel, ctx_tiles=ctx_tiles),
        grid=(batch // BATCH_GROUP, nt),
        in_specs=_split_specs(D_MODEL, ctx_tiles)
        + _mod_specs_grouped(layer, 0, batch, ctx_tiles)
        + _mod_specs_grouped(layer, 1, batch, ctx_tiles)
        + [_resident((D_MODEL, ATTN_IN_W)), tab, tab],
        out_specs=[row(ATTN_W), row(ATTN_W), row(ATTN_W), row(F_W)],
        out_shape=[jax.ShapeDtypeStruct((batch, ta, ATTN_W), BF16)] * 3
        + [jax.ShapeDtypeStruct((batch, ta, F_W), BF16)],
        scratch_shapes=[pltpu.VMEM((D_MODEL, ATTN_IN_W), BF16)],
        compiler_params=_cparams("arbitrary", "arbitrary"),
        name="inproj_attn",
    )(h_ctx, h_lat, *([mods] * n_mod), w_in, cos, sin)


def _diff_lambda(lam_ref, lam_init):
    lamv = lam_ref[...]
    l1 = jnp.sum(lamv[0:1] * lamv[1:2], axis=-1, keepdims=True)
    l2 = jnp.sum(lamv[2:3] * lamv[3:4], axis=-1, keepdims=True)
    return jnp.exp(l1) - jnp.exp(l2) + lam_init


def _stack_maps(q):
    lane = lax.broadcasted_iota(jnp.int32, q.shape, 1)
    zero = jnp.zeros_like(q)
    return jnp.concatenate([jnp.where(lane < ATTN_HEAD_DIM, q, zero),
                            jnp.where(lane >= ATTN_HEAD_DIM, q, zero)], axis=0)


def _scores(q2, k):
    return lax.dot_general(q2, k, (((1,), (1,)), ((), ())), preferred_element_type=F32)


def _diff_softmax_pv(load_s0, m0, load_s1, m1, lam, v, gain):
    e0 = jnp.exp2(load_s0() - m0)
    e1 = jnp.exp2(load_s1() - m1)
    l0 = jnp.sum(e0, axis=-1, keepdims=True)
    l1 = jnp.sum(e1, axis=-1, keepdims=True)
    w = e0 - e1 * (lam * l0 / l1)
    o = jnp.dot(w.astype(BF16), v, preferred_element_type=F32) * (1.0 / l0)
    o = o * lax.rsqrt(jnp.mean(o * o, axis=-1, keepdims=True) + LN_EPS)
    return (o * gain).astype(BF16)


def _attn_ctx_kernel(lam_ref, g_ref, q_ref, k_ref, v_ref, o_ref, *, lam_init):
    lam = _diff_lambda(lam_ref, lam_init)
    gain = g_ref[...] * (1.0 - lam_init)
    tq = q_ref.shape[1]
    for hd in range(ATTN_HEADS):
        cols = slice(hd * 128, (hd + 1) * 128)
        s = _scores(_stack_maps(q_ref[0, :, cols]), k_ref[0, :, cols])
        m = jnp.max(s, axis=-1, keepdims=True)

        def half(lo, s=s):
            return lambda: s[lo:lo + tq]

        o_ref[0, :, cols] = _diff_softmax_pv(half(0), m[:tq], half(tq), m[tq:], lam,
                                             v_ref[0, :, cols], gain)


def _attn_lat_kernel(lam_ref, g_ref, k_ref, v_ref, *rest, lam_init):
    q_refs = rest[:ATTN_ITEM_TILES]
    o_ref, sa_ref, ma_ref, sb_ref, mb_ref = rest[ATTN_ITEM_TILES:]
    t = pl.program_id(0)
    tq = ATTN_ITEM_TILES * ROW_TILE
    sub = ATTN_Q_SUB

    @pl.when(t == 0)
    def _():
        sb_ref[...] = jnp.zeros_like(sb_ref)
        mb_ref[...] = jnp.zeros_like(mb_ref)

    def step(s_new, m_new, s_old, m_old):
        lam = _diff_lambda(lam_ref, lam_init)
        gain = g_ref[...] * (1.0 - lam_init)
        q = jnp.concatenate([q_ref[0] for q_ref in q_refs], axis=0)
        s = _scores(_stack_maps(q), k_ref[0])
        s_new[...] = s
        m_new[...] = jnp.max(s, axis=-1, keepdims=True)
        v = v_ref[0]
        for i in range(tq // sub):
            r0 = slice(i * sub, (i + 1) * sub)
            r1 = slice(tq + i * sub, tq + (i + 1) * sub)
            o_ref[0, r0, :] = _diff_softmax_pv(
                functools.partial(s_old.__getitem__, (r0, slice(None))), m_old[r0, :],
                functools.partial(s_old.__getitem__, (r1, slice(None))), m_old[r1, :],
                lam, v, gain)

    @pl.when(t % 2 == 0)
    def _():
        step(sa_ref, ma_ref, sb_ref, mb_ref)

    @pl.when(t % 2 == 1)
    def _():
        step(sb_ref, mb_ref, sa_ref, ma_ref)


def _attention(q, k, v, lam_vec, subln_g, lam_init, ctx_tiles):
    batch, ta, _ = q.shape
    n_ctx = ctx_tiles * ROW_TILE
    nt = ta // ROW_TILE - ctx_tiles
    g = subln_g.reshape(1, 128)
    small = [_full((4, ATTN_HEAD_DIM)), _full((1, 128))]
    cspec = pl.BlockSpec((1, n_ctx, ATTN_W), lambda b: (b, 0, 0))
    o_ctx = pl.pallas_call(
        functools.partial(_attn_ctx_kernel, lam_init=lam_init),
        grid=(batch,),
        in_specs=small + [cspec, cspec, cspec],
        out_specs=cspec,
        out_shape=jax.ShapeDtypeStruct((batch, n_ctx, ATTN_W), BF16),
        compiler_params=_cparams("arbitrary"),
        name="diff_attention_ctx",
    )(lam_vec, g, q, k, v)

    per = ATTN_ITEM_TILES
    assert nt % per == 0
    ni = nt // per
    n_items = batch * ATTN_HEADS * ni

    def item(j):
        return j // (ATTN_HEADS * ni), (j // ni) % ATTN_HEADS, j % ni

    def score_item(j):
        return item(jnp.minimum(j, n_items - 1))

    def finish_item(j):
        return item(jnp.maximum(j - 1, 0))

    def q_spec(part):
        def idx(j):
            b, h, t = score_item(j)
            return b, ctx_tiles + t * per + part, h
        return pl.BlockSpec((1, ROW_TILE, 128), idx)

    def k_idx(j):
        b, h, _ = score_item(j)
        return b, 0, h

    def v_idx(j):
        b, h, _ = finish_item(j)
        return b, 0, h

    def o_idx(j):
        b, h, t = finish_item(j)
        return b, t, h

    o_lat = pl.pallas_call(
        functools.partial(_attn_lat_kernel, lam_init=lam_init),
        grid=(n_items + 1,),
        in_specs=small + [pl.BlockSpec((1, ta, 128), k_idx), pl.BlockSpec((1, ta, 128), v_idx)]
        + [q_spec(part) for part in range(per)],
        out_specs=pl.BlockSpec((1, per * ROW_TILE, 128), o_idx),
        out_shape=jax.ShapeDtypeStruct((batch, nt * ROW_TILE, ATTN_W), BF16),
        scratch_shapes=[pltpu.VMEM((2 * per * ROW_TILE, ta), F32),
                        pltpu.VMEM((2 * per * ROW_TILE, 1), F32)] * 2,
        compiler_params=_cparams("arbitrary"),
        name="diff_attention",
    )(lam_vec, g, k, v, *([q] * per))
    return o_ctx, o_lat


def _dft_tables(n):
    k = np.arange(n, dtype=np.int64)
    ang = 2.0 * np.pi * ((k[:, None] * k[None, :]) % n).astype(np.float64) / n
    return np.cos(ang), np.sin(ang)


def _fourier_kernel(f_ref, cs_ref, dl_ref, dc_ref, w_ref, b_ref, o_ref, ac_ref, al_ref,
                    *, n_ctx, n_lat):
    t = pl.program_id(0)
    b = pl.program_id(1)
    ctx_tiles = n_ctx // ROW_TILE

    def stage1(rows0, n):
        a = jnp.dot(f_ref[0, rows0:rows0 + n, :], cs_ref[...], preferred_element_type=F32)
        return a[:, :F_W].astype(BF16), a[:, F_W:].astype(BF16)

    def stage2(dft, a, n):
        z = jnp.dot(dft, a, preferred_element_type=F32)
        z = z * (1.0 / math.sqrt(n * F_GROUP_W))
        o = jnp.dot(z.astype(BF16), w_ref[...], preferred_element_type=F32) + b_ref[...]
        o_ref[0] = o.astype(BF16)

    @pl.when(t < ctx_tiles)
    def _():
        ac_ref[0:n_ctx, :], ac_ref[n_ctx:, :] = stage1(0, n_ctx)
        stage2(dc_ref[...], ac_ref[...], n_ctx)

    @pl.when(t == ctx_tiles)
    def _():
        al_ref[b, 0:n_lat, :], al_ref[b, n_lat:, :] = stage1(n_ctx, n_lat)

    @pl.when(t >= ctx_tiles)
    def _():
        stage2(dl_ref[...], al_ref[b], n_lat)


def _fourier(f, fourier_w, fourier_b, n_ctx):
    batch, ta, _ = f.shape
    n_lat = ta - n_ctx
    assert n_ctx == ROW_TILE
    nt = ta // ROW_TILE
    ctx_tiles = n_ctx // ROW_TILE
    cc, sc = _dft_tables(F_GROUP_W)
    eye = np.eye(F_GROUPS)
    cs = jnp.asarray(np.concatenate([np.kron(eye, cc), np.kron(eye, sc)], axis=1), BF16)
    cl, sl = _dft_tables(n_lat)
    dft_lat = jnp.asarray(np.concatenate([cl, -sl], axis=1), BF16)
    cx, sx = _dft_tables(n_ctx)
    dft_ctx = jnp.asarray(np.concatenate([cx, -sx], axis=1), BF16)
    w_blk = jnp.einsum('gce,gh->gche', fourier_w, jnp.eye(F_GROUPS, dtype=F32))
    w_blk = w_blk.reshape(F_W, F_W).astype(BF16)
    return pl.pallas_call(
        functools.partial(_fourier_kernel, n_ctx=n_ctx, n_lat=n_lat),
        grid=(nt, batch),
        in_specs=[
                  pl.BlockSpec((1, ta, F_W),
                               lambda t, b: (jnp.where(t <= ctx_tiles, b, batch - 1), 0, 0)),
                  _full((F_W, 2 * F_W)),
                  pl.BlockSpec((ROW_TILE, 2 * n_lat),
                               lambda t, b: (jnp.maximum(t - ctx_tiles, 0), 0)),
                  _full((n_ctx, 2 * n_ctx)),
                  _full((F_W, F_W)), _full((1, F_W))],
        out_specs=pl.BlockSpec((1, ROW_TILE, F_W), lambda t, b: (b, t, 0)),
        out_shape=jax.ShapeDtypeStruct((batch, ta, F_W), BF16),
        scratch_shapes=[pltpu.VMEM((2 * n_ctx, F_W), BF16),
                        pltpu.VMEM((batch, 2 * n_lat, F_W), BF16)],
        compiler_params=_cparams("arbitrary", "arbitrary"),
        name="fourier_mix",
    )(f, cs, dft_lat, dft_ctx, w_blk, fourier_b.reshape(1, F_W))


def _outproj_ln_kernel(ac_ref, al_ref, b_ref, hc_ref, hl_ref, *rest, ctx_tiles):
    g = BATCH_GROUP
    gate_refs = rest[:g]
    w_ref, g_ref, beta_ref, o_ref, wbf_ref = rest[g:]

    @pl.when(_first_step())
    def _():
        wbf_ref[...] = w_ref[...].astype(BF16)

    is_ctx = pl.program_id(1) < ctx_tiles
    a = jnp.concatenate([jnp.where(is_ctx, ac_ref[i], al_ref[i]) for i in range(g)], axis=0)
    b2 = jnp.concatenate([b_ref[i] for i in range(g)], axis=0)
    wa = a.shape[1]
    y = jnp.dot(a, wbf_ref[0:wa, :], preferred_element_type=F32)
    y = y + jnp.dot(b2, wbf_ref[wa:, :], preferred_element_type=F32)
    rows = o_ref.shape[1]
    for i in range(g):
        h = jnp.where(is_ctx, hc_ref[i], hl_ref[i])
        o_ref[i] = _layer_norm(ALPHA * h + gate_refs[i][0] * y[i * rows:(i + 1) * rows],
                               g_ref[...], beta_ref[...])


def _outproj_ln(a_ctx, a_lat, b2, h_ctx, h_lat, mods, layer, w_out, ln_g, ln_b, ctx_tiles):
    batch, ta, _ = b2.shape
    nt = ta // ROW_TILE
    row = lambda w: pl.BlockSpec((BATCH_GROUP, ROW_TILE, w), lambda b, t: (b, t, 0))
    return pl.pallas_call(
        functools.partial(_outproj_ln_kernel, ctx_tiles=ctx_tiles),
        grid=(batch // BATCH_GROUP, nt),
        in_specs=_split_specs(a_lat.shape[2], ctx_tiles) + [row(b2.shape[2])]
        + _split_specs(D_MODEL, ctx_tiles)
        + _mod_specs_grouped(layer, 2, batch, ctx_tiles)
        + [_resident((D_MODEL, D_MODEL)), _full((1, D_MODEL)), _full((1, D_MODEL))],
        out_specs=row(D_MODEL),
        out_shape=jax.ShapeDtypeStruct((batch, ta, D_MODEL), F32),
        scratch_shapes=[pltpu.VMEM((D_MODEL, D_MODEL), BF16)],
        compiler_params=_cparams("arbitrary", "arbitrary"),
        name="outproj_ln",
    )(a_ctx, a_lat, b2, h_ctx, h_lat, *([mods] * BATCH_GROUP), w_out, ln_g.reshape(1, D_MODEL),
      ln_b.reshape(1, D_MODEL))


def _ffn_kernel(h_ref, hp_ref, hn_ref, shift_ref, scale_ref, gate_ref,
                wup_ref, bup_ref, cw_ref, cb_ref, wdn_ref, bdn_ref, g_ref, beta_ref,
                o_ref, uext_ref, ubf_ref, act_ref, *, ctx_tiles, tile_off, nt_seq):
    t = pl.program_id(1) + tile_off
    seg_first = (t == 0) | (t == ctx_tiles)
    seg_last = (t == nt_seq - 1) | (t == ctx_tiles - 1)
    sc = 1.0 + scale_ref[0]
    sh = shift_ref[0]
    h = h_ref[0]
    tm = h.shape[0]
    uext_ref[0:HALO, :] = hp_ref[0] * sc + sh
    uext_ref[HALO:HALO + tm, :] = h * sc + sh
    uext_ref[HALO + tm:, :] = hn_ref[0] * sc + sh
    ubf_ref[...] = uext_ref[...].astype(BF16)
    row8 = lax.broadcasted_iota(jnp.int32, (8, FF_TILE), 0)

    def hidden(col0):
        cols = slice(col0, col0 + FF_TILE)
        zr = jnp.dot(ubf_ref[...], wup_ref[:, cols], preferred_element_type=F32)
        bup = bup_ref[:, cols]
        cw = cw_ref[:, cols]
        bias = cb_ref[:, cols] + (cw[0:1] + cw[1:2] + cw[2:3]) * bup
        z0 = zr[HALO:HALO + tm]
        prev = jnp.where(seg_first, -bup, zr[HALO - 1:HALO])
        nxt = jnp.where(seg_last, -bup, zr[HALO + tm:HALO + tm + 1])
        down = pltpu.roll(z0, 1, 0)
        up = pltpu.roll(z0, tm - 1, 0)
        zm1 = jnp.concatenate([jnp.where(row8 == 0, prev, down[0:8]), down[8:]], axis=0)
        zp1 = jnp.concatenate([up[:tm - 8], jnp.where(row8 == 7, nxt, up[tm - 8:])], axis=0)
        return cw[0:1] * zm1 + cw[1:2] * z0 + cw[2:3] * zp1 + bias

    for j in range(D_FF // FF_TILE):
        val = hidden(j * FF_TILE)
        gat = hidden(D_FF + j * FF_TILE)
        act_ref[:, j * FF_TILE:(j + 1) * FF_TILE] = (val * _silu(gat)).astype(BF16)
    f = jnp.dot(act_ref[...], wdn_ref[...], preferred_element_type=F32) + bdn_ref[...]
    o_ref[0] = _layer_norm(ALPHA * h + gate_ref[0] * f, g_ref[...], beta_ref[...])


def _ffn(h, mods, layer, w_up, b_up, conv_w, conv_b, w_down, b_down, ln_g, ln_b,
         ctx_tiles, tile_off, rows=ROW_TILE):
    batch, ta, _ = h.shape
    nt_seq = ta // rows
    nt = nt_seq - tile_off
    hb = rows // HALO
    n_hblk = ta // HALO
    mspec = lambda j: pl.BlockSpec(
        (1, 1, D_MODEL),
        lambda b, t: ((layer * MOD_ROWS + jnp.where(t + tile_off < ctx_tiles, batch, b)) * 6 + j, 0, 0))
    resident = lambda shape: pl.BlockSpec((None,) + shape, lambda *_: (layer, 0, 0),
                                          pipeline_mode=pl.Buffered(1))
    return pl.pallas_call(
        functools.partial(_ffn_kernel, ctx_tiles=ctx_tiles, tile_off=tile_off, nt_seq=nt_seq),
        grid=(batch, nt),
        in_specs=[pl.BlockSpec((1, rows, D_MODEL), lambda b, t: (b, t + tile_off, 0)),
                  pl.BlockSpec((1, HALO, D_MODEL),
                               lambda b, t: (b, jnp.maximum((t + tile_off) * hb - 1, 0), 0)),
                  pl.BlockSpec((1, HALO, D_MODEL),
                               lambda b, t: (b, jnp.minimum((t + tile_off + 1) * hb, n_hblk - 1), 0)),
                  mspec(3), mspec(4), mspec(5),
                  resident((D_MODEL, 2 * D_FF)), _full((1, 2 * D_FF)),
                  _full((3, 2 * D_FF)), _full((1, 2 * D_FF)),
                  resident((D_FF, D_MODEL)), _full((1, D_MODEL)),
                  _full((1, D_MODEL)), _full((1, D_MODEL))],
        out_specs=pl.BlockSpec((1, rows, D_MODEL), lambda b, t: (b, t, 0)),
        out_shape=jax.ShapeDtypeStruct((batch, nt * rows, D_MODEL), F32),
        scratch_shapes=[pltpu.VMEM((rows + 2 * HALO, D_MODEL), F32),
                        pltpu.VMEM((rows + 2 * HALO, D_MODEL), BF16),
                        pltpu.VMEM((rows, D_FF), BF16)],
        compiler_params=_cparams("parallel", "arbitrary"),
        name="conv_ffn_ln",
    )(h, h, h, mods, mods, mods, w_up, b_up.reshape(1, -1), conv_w, conv_b.reshape(1, -1),
      w_down, b_down.reshape(1, -1), ln_g.reshape(1, -1), ln_b.reshape(1, -1))


def _time_major_rows(b, batch):
    return pl.ds(b, ROW_TILE, stride=batch)


def _inproj_ssm_kernel(h_ref, *rest, batch):
    g = BATCH_GROUP
    shift_refs, scale_refs = rest[:g], rest[g:2 * g]
    w_ref, z_ref, xbc_ref, dt_ref, us_ref, wbf_ref = rest[2 * g:]

    @pl.when(_first_step())
    def _():
        head = SSD_W + XBC_W
        wbf_ref[0:head, :] = w_ref[0:head, :].astype(BF16)
        wbf_ref[head:head + S5_W, :] = w_ref[head + DT_W:head + DT_W + S5_W, :].astype(BF16)
        pad = jnp.zeros((DT_PAD - DT_W, D_MODEL), F32)
        wbf_ref[head + S5_W:, :] = jnp.concatenate(
            [w_ref[head:head + DT_W, :], pad], axis=0).astype(BF16)

    u = _modulated_rows(lambda i: h_ref[i], shift_refs, scale_refs)
    p = lax.dot_general(u, wbf_ref[...], (((1,), (1,)), ((), ())), preferred_element_type=F32)
    _store_rows(z_ref, p[:, :SSD_W])
    _store_rows(xbc_ref, p[:, SSD_W:SSD_W + XBC_W])
    _store_rows(dt_ref, p[:, SSD_W + XBC_W + S5_W:])
    for i in range(g):
        rows = _time_major_rows(pl.program_id(1) * g + i, batch)
        for j in range(S5_W // 128):
            col0 = SSD_W + XBC_W + j * 128
            us_ref[j, rows, :] = p[i * ROW_TILE:(i + 1) * ROW_TILE, col0:col0 + 128]


def _inproj_ssm(h, mods, layer, w_in, ctx_tiles):
    batch, ta, _ = h.shape
    assert batch % BATCH_GROUP == 0
    nt = ta // ROW_TILE
    row = lambda w: pl.BlockSpec((BATCH_GROUP, ROW_TILE, w), lambda t, b: (b, t, 0))

    def mspecs(j):
        def spec(i):
            def idx(t, bg):
                r = jnp.where(t < ctx_tiles, batch, bg * BATCH_GROUP + i)
                return ((layer * MOD_ROWS + r) * 6 + j, 0, 0)
            return pl.BlockSpec((1, 1, D_MODEL), idx)
        return [spec(i) for i in range(BATCH_GROUP)]

    return pl.pallas_call(
        functools.partial(_inproj_ssm_kernel, batch=batch),
        grid=(nt, batch // BATCH_GROUP),
        in_specs=[row(D_MODEL)] + mspecs(0) + mspecs(1) + [_resident((SSM_IN_W, D_MODEL))],
        out_specs=[row(SSD_W), row(XBC_W), row(DT_PAD),
                   pl.BlockSpec((S5_W // 128, ROW_TILE * batch, 128), lambda t, b: (0, t, 0))],
        out_shape=[jax.ShapeDtypeStruct((batch, ta, SSD_W), F32),
                   jax.ShapeDtypeStruct((batch, ta, XBC_W), F32),
                   jax.ShapeDtypeStruct((batch, ta, DT_PAD), F32),
                   jax.ShapeDtypeStruct((S5_W // 128, ta * batch, 128), F32)],
        scratch_shapes=[pltpu.VMEM((SSM_IN_PAD, D_MODEL), BF16)],
        compiler_params=_cparams("arbitrary", "arbitrary"),
        name="inproj_ssm",
    )(h, *([mods] * (2 * BATCH_GROUP)), jnp.swapaxes(w_in, 0, 1))


def _cumsum_rows(v):
    n = v.shape[0]
    row = lax.broadcasted_iota(jnp.int32, v.shape, 0)
    s = 1
    while s < n:
        v = v + jnp.where(row >= s, pltpu.roll(v, s, 0), 0.0)
        s *= 2
    return v


def _expand_heads(v, e_ref):
    hi = v.astype(BF16)
    lo = (v - hi.astype(F32)).astype(BF16)
    e = e_ref[...]
    return (jnp.dot(hi, e, preferred_element_type=F32)
            + jnp.dot(lo, e, preferred_element_type=F32))


def _ssd_kernel(xbc_ref, xp_ref, xn_ref, dt_ref, z_ref, cw_ref, cb_ref, alog_ref, dtb_ref,
                dsk_ref, ng_ref, ef_ref, eb_ref, o_ref,
                xs_ref, cd_ref, st_ref, dec_ref, y_ref, *, n_tiles, ctx_tiles):
    s = pl.program_id(1)
    q = SSD_CHUNK
    cpt = ROW_TILE // q
    n_chunks = n_tiles * cpt
    ctx_chunks = ctx_tiles * cpt
    gw = SSD_W // SSD_GROUPS
    hpg = SSD_HEADS // SSD_GROUPS

    @pl.when(s < n_tiles)
    def _phase0():
        p = s
        seg_first = (p == 0) | (p == ctx_tiles)
        seg_last = (p == ctx_tiles - 1) | (p == n_tiles - 1)
        xr = xbc_ref[0]
        prev = jnp.where(seg_first, 0.0, xp_ref[0, HALO - 1:HALO, :])
        nxt = jnp.where(seg_last, 0.0, xn_ref[0, 0:1, :])
        row8 = lax.broadcasted_iota(jnp.int32, (8, XBC_W), 0)
        down = pltpu.roll(xr, 1, 0)
        up = pltpu.roll(xr, ROW_TILE - 1, 0)
        xm1 = jnp.concatenate([jnp.where(row8 == 0, prev, down[0:8]), down[8:]], axis=0)
        xp1 = jnp.concatenate([up[:ROW_TILE - 8], jnp.where(row8 == 7, nxt, up[ROW_TILE - 8:])],
                              axis=0)
        cw = cw_ref[...]
        xs_tile = _silu(cw[0:1] * xm1 + cw[1:2] * xr + cw[2:3] * xp1 + cb_ref[...])
        xs_ref[p] = xs_tile

        raw = dt_ref[0] + dtb_ref[...]
        dt_tile = jnp.maximum(raw, 0.0) + jnp.log1p(jnp.exp(-jnp.abs(raw)))
        a_row = -jnp.exp(alog_ref[...])
        for i in range(cpt):
            c = p * cpt + i
            xs = xs_tile[i * q:(i + 1) * q]
            dtv = dt_tile[i * q:(i + 1) * q]
            adt = dtv * a_row
            cum = _cumsum_rows(adt)
            tot = cum[q - 1:q, :]
            lane = lax.broadcasted_iota(jnp.int32, cum.shape, 1)
            cc = jnp.where(lane < SSD_HEADS, cum, tot - cum + adt)
            cd_ref[c, 0] = cc
            cd_ref[c, 1] = dtv
            w_end = jnp.exp(tot - cc) * dtv
            dec16 = jnp.broadcast_to(jnp.exp(tot), (16, DT_PAD))
            x = xs[:, :SSD_W]
            for d, e_ref in enumerate((ef_ref, eb_ref)):
                wx = (_expand_heads(w_end, e_ref) * x).astype(BF16)
                for g in range(SSD_GROUPS):
                    bmt = xs[:, SSD_W + g * SSD_STATE:SSD_W + (g + 1) * SSD_STATE].T.astype(BF16)
                    st_ref[c, d, :, g * gw:(g + 1) * gw] = jnp.dot(
                        bmt, wx[:, g * gw:(g + 1) * gw], preferred_element_type=F32)
                dec_ref[c, d] = _expand_heads(dec16, e_ref)[0:8]

    @pl.when(s == n_tiles)
    def _recurrence():
        fwd = list(range(n_chunks))
        bwd = list(range(ctx_chunks - 1, -1, -1)) + list(range(n_chunks - 1, ctx_chunks - 1, -1))
        for d, order in enumerate((fwd, bwd)):
            for col0 in range(0, SSD_W, 128):
                cols = slice(col0, col0 + 128)
                state = jnp.zeros((SSD_STATE, 128), F32)
                for ci in order:
                    contrib = st_ref[ci, d, :, cols]
                    st_ref[ci, d, :, cols] = state
                    state = state * dec_ref[ci, d, 0:1, cols] + contrib

    @pl.when(s >= n_tiles)
    def _phase1():
        p = s - n_tiles + ctx_tiles
        rowi = lax.broadcasted_iota(jnp.int32, (q, q), 0)
        coli = lax.broadcasted_iota(jnp.int32, (q, q), 1)
        lower = coli <= rowi
        upper = coli >= rowi
        lane = lax.broadcasted_iota(jnp.int32, (q, 128), 1)
        neg = jnp.float32(-jnp.inf)
        for i in range(cpt):
            c = p * cpt + i
            rows = slice(i * q, (i + 1) * q)
            xs = xs_ref[p, rows, :]
            x = xs[:, :SSD_W]
            cc = cd_ref[c, 0]
            dtv = cd_ref[c, 1]
            cct = cc.T
            dtt = dtv.T
            ecc = jnp.exp(cc)
            ef = _expand_heads(ecc, ef_ref)
            eb = _expand_heads(ecc, eb_ref)
            for g in range(SSD_GROUPS):
                bm = xs[:, SSD_W + g * SSD_STATE:SSD_W + (g + 1) * SSD_STATE].astype(BF16)
                cm = xs[:, SSD_W + (SSD_GROUPS + g) * SSD_STATE:
                        SSD_W + (SSD_GROUPS + g + 1) * SSD_STATE].astype(BF16)
                gmat = lax.dot_general(cm, bm, (((1,), (1,)), ((), ())),
                                       preferred_element_type=F32)
                sl = slice(g * gw, (g + 1) * gw)
                yoff = (ef[:, sl] * jnp.dot(cm, st_ref[c, 0, :, sl].astype(BF16),
                                            preferred_element_type=F32)
                        + eb[:, sl] * jnp.dot(cm, st_ref[c, 1, :, sl].astype(BF16),
                                              preferred_element_type=F32))
                for pair in range(hpg // 2):
                    col0 = g * gw + pair * 128
                    xpair = x[:, col0:col0 + 128].astype(BF16)
                    res = []
                    for hh in range(2):
                        hd = g * hpg + pair * 2 + hh
                        hb_ = SSD_HEADS + hd
                        lf = jnp.exp(jnp.where(lower, cc[:, hd:hd + 1] - cct[hd:hd + 1, :], neg))
                        lb = jnp.exp(jnp.where(upper, cc[:, hb_:hb_ + 1] - cct[hb_:hb_ + 1, :],
                                               neg))
                        mt = gmat * (lf * dtt[hd:hd + 1, :] + lb * dtt[hb_:hb_ + 1, :])
                        res.append(jnp.dot(mt.astype(BF16), xpair, preferred_element_type=F32))
                    ydiag = jnp.where(lane < SSD_HEAD_DIM, res[0], res[1])
                    y_ref[rows, col0:col0 + 128] = (
                        ydiag + yoff[:, pair * 128:(pair + 1) * 128]
                        + dsk_ref[:, col0:col0 + 128] * x[:, col0:col0 + 128])
        gated = y_ref[...] * _silu(z_ref[0])
        normed = gated * lax.rsqrt(jnp.mean(gated * gated, axis=-1, keepdims=True) + LN_EPS)
        o_ref[0] = (normed * ng_ref[...]).astype(BF16)


def _ssd(xbc, dt, z, conv_w, conv_b, a_log, dt_bias, d_skip, norm_g, n_ctx):
    batch, ta, _ = xbc.shape
    q = SSD_CHUNK
    n_chunks = ta // q
    n_tiles = ta // ROW_TILE
    ctx_tiles = n_ctx // ROW_TILE
    hb = ROW_TILE // HALO
    n_hblk = ta // HALO
    pad24 = lambda v: jnp.pad(v.reshape(1, DT_W), ((0, 0), (0, DT_PAD - DT_W)))
    heads = np.arange(SSD_HEADS)
    ef = np.zeros((DT_PAD, SSD_W), np.float32)
    eb = np.zeros((DT_PAD, SSD_W), np.float32)
    for hd in heads:
        ef[hd, hd * SSD_HEAD_DIM:(hd + 1) * SSD_HEAD_DIM] = 1.0
        eb[SSD_HEADS + hd, hd * SSD_HEAD_DIM:(hd + 1) * SSD_HEAD_DIM] = 1.0
    dsk = jnp.repeat(d_skip.astype(F32), SSD_HEAD_DIM).reshape(1, SSD_W)
    ph0 = lambda s: s < n_tiles
    tile = lambda w: pl.BlockSpec(
        (1, ROW_TILE, w), lambda b, s: (b, jnp.where(ph0(s), s, n_tiles - 1), 0))
    return pl.pallas_call(
        functools.partial(_ssd_kernel, n_tiles=n_tiles, ctx_tiles=ctx_tiles),
        grid=(batch, 2 * n_tiles - ctx_tiles),
        in_specs=[tile(XBC_W),
                  pl.BlockSpec((1, HALO, XBC_W),
                               lambda b, s: (b, jnp.where(ph0(s), jnp.maximum(s * hb - 1, 0), 0), 0)),
                  pl.BlockSpec((1, HALO, XBC_W),
                               lambda b, s: (b, jnp.where(ph0(s), jnp.minimum((s + 1) * hb, n_hblk - 1), 0), 0)),
                  tile(DT_PAD),
                  pl.BlockSpec((1, ROW_TILE, SSD_W),
                               lambda b, s: (b, jnp.where(ph0(s), 0, s - n_tiles + ctx_tiles), 0)),
                  _full((3, XBC_W)), _full((1, XBC_W)), _full((1, DT_PAD)), _full((1, DT_PAD)),
                  _full((1, SSD_W)), _full((1, SSD_W)),
                  _full((DT_PAD, SSD_W)), _full((DT_PAD, SSD_W))],
        out_specs=pl.BlockSpec(
            (1, ROW_TILE, SSD_W), lambda b, s: (b, jnp.where(ph0(s), 0, s - n_tiles), 0)),
        out_shape=jax.ShapeDtypeStruct((batch, ta - n_ctx, SSD_W), BF16),
        scratch_shapes=[pltpu.VMEM((n_tiles, ROW_TILE, XBC_W), F32),
                        pltpu.VMEM((n_chunks, 2, q, DT_PAD), F32),
                        pltpu.VMEM((n_chunks, 2, SSD_STATE, SSD_W), F32),
                        pltpu.VMEM((n_chunks, 2, 8, SSD_W), F32),
                        pltpu.VMEM((ROW_TILE, SSD_W), F32)],
        compiler_params=_cparams("parallel", "arbitrary"),
        name="ssd_bidir",
    )(xbc, xbc, xbc, dt, z, conv_w, conv_b.reshape(1, XBC_W), pad24(a_log), pad24(dt_bias),
      dsk, norm_g.reshape(1, SSD_W), jnp.asarray(ef, BF16), jnp.asarray(eb, BF16))


def _s5_disc_kernel(lr_ref, li_ref, ldt_ref, bre_ref, bim_ref, cre_ref, cim_ref,
                    a_ref, bd_ref, cd_ref):
    lr, li = lr_ref[...], li_ref[...]
    dt = jnp.exp(ldt_ref[...])
    mag = jnp.exp(dt * lr)
    ab_re, ab_im = mag * jnp.cos(dt * li), mag * jnp.sin(dt * li)
    den = lr * lr + li * li
    k_re = ((ab_re - 1.0) * lr + ab_im * li) / den
    k_im = (ab_im * lr - (ab_re - 1.0) * li) / den
    bre, bim = bre_ref[...], bim_ref[...]
    for d in range(2):
        a_ref[d, :, 0:S5_NSTATE] = jnp.broadcast_to(ab_re[d:d + 1], (8, S5_NSTATE))
        a_ref[d, :, S5_NSTATE:] = jnp.broadcast_to(ab_im[d:d + 1], (8, S5_NSTATE))
        kr, ki = k_re[d:d + 1], k_im[d:d + 1]
        bd_ref[d, :, 0:S5_NSTATE] = (kr * bre - ki * bim).astype(BF16)
        bd_ref[d, :, S5_NSTATE:] = (kr * bim + ki * bre).astype(BF16)
        cd_ref[d, 0:S5_NSTATE, :] = cre_ref[d].astype(BF16)
        cd_ref[d, S5_NSTATE:, :] = (-cim_ref[d]).astype(BF16)


def _s5_discretize(lam_re, lam_im, log_dt, b_re, b_im, c_re, c_im):
    eye = jnp.eye(S5_GROUPS, dtype=F32)
    bd = lambda b: jnp.einsum('gph,gk->ghkp', b, eye).reshape(S5_W, S5_NSTATE)
    cd = lambda cc: jnp.einsum('dghp,gk->dgpkh', cc, eye).reshape(2, S5_NSTATE, S5_W)
    ldt = jnp.repeat(log_dt, S5_STATE, axis=-1)
    return pl.pallas_call(
        _s5_disc_kernel,
        out_shape=[jax.ShapeDtypeStruct((2, 8, 2 * S5_NSTATE), F32),
                   jax.ShapeDtypeStruct((2, S5_W, 2 * S5_NSTATE), BF16),
                   jax.ShapeDtypeStruct((2, 2 * S5_NSTATE, S5_W), BF16)],
        compiler_params=pltpu.CompilerParams(vmem_limit_bytes=VMEM_LIMIT_BYTES),
        name="s5_discretize",
    )(lam_re.reshape(2, S5_NSTATE), lam_im.reshape(2, S5_NSTATE), ldt,
      bd(b_re), bd(b_im), cd(c_re), cd(c_im))


S5_TIME_CHUNK = 256
S5_UNROLL = 8


def _s5_scan_kernel(u_ref, a_ref, bd_ref, cd_ref, o_ref, hs_ref, carry_ref, *, batch):
    d = pl.program_id(0)
    j = pl.program_id(1)
    n = S5_NSTATE

    @pl.when(j == 0)
    def _():
        carry_ref[...] = jnp.zeros_like(carry_ref)

    half = hs_ref.shape[0] // 2
    n_slab = S5_W // 128
    for r in (0, half):
        u = jnp.concatenate([u_ref[s, r:r + half, :] for s in range(n_slab)], axis=1)
        hs_ref[r:r + half, :] = jnp.dot(u.astype(BF16), bd_ref[0], preferred_element_type=F32)
    ar = a_ref[0, :, 0:n]
    ai = a_ref[0, :, n:]
    if batch != 8:
        ar = jnp.broadcast_to(ar[0:1], (batch, n))
        ai = jnp.broadcast_to(ai[0:1], (batch, n))

    def body(i, carry):
        hr, hi = carry
        for s in range(S5_UNROLL):
            step = i * S5_UNROLL + s
            step = jnp.where(d == 0, step, S5_TIME_CHUNK - 1 - step)
            rows = pl.ds(pl.multiple_of(step * batch, batch), batch)
            nr = ar * hr - ai * hi + hs_ref[rows, 0:n]
            ni = ar * hi + ai * hr + hs_ref[rows, n:]
            hs_ref[rows, 0:n] = nr
            hs_ref[rows, n:] = ni
            hr, hi = nr, ni
        return hr, hi

    hr, hi = lax.fori_loop(0, S5_TIME_CHUNK // S5_UNROLL, body,
                           (carry_ref[:, 0:n], carry_ref[:, n:]))
    carry_ref[:, 0:n] = hr
    carry_ref[:, n:] = hi
    for r in (0, half):
        y = jnp.dot(hs_ref[r:r + half, :].astype(BF16), cd_ref[0], preferred_element_type=F32)
        for s in range(n_slab):
            o_ref[0, s, r:r + half, :] = y[:, s * 128:(s + 1) * 128]


def _s5_scan(us_flat, a, bd, cd, batch, n_ctx):
    n_slab, rows_total, _ = us_flat.shape
    ta = rows_total // batch
    tc = S5_TIME_CHUNK
    n_chunks = ta // tc
    ctx_chunks = n_ctx // tc
    blk = tc * batch

    def chunk_of(d, j):
        bwd = jnp.where(j < ctx_chunks, ctx_chunks - 1 - j, n_chunks - 1 - (j - ctx_chunks))
        return jnp.where(d == 0, j, bwd)

    return pl.pallas_call(
        functools.partial(_s5_scan_kernel, batch=batch),
        grid=(2, n_chunks),
        in_specs=[pl.BlockSpec((n_slab, blk, 128), lambda d, j: (0, chunk_of(d, j), 0)),
                  pl.BlockSpec((1, 8, 2 * S5_NSTATE), lambda d, j: (d, 0, 0)),
                  pl.BlockSpec((1, S5_W, 2 * S5_NSTATE), lambda d, j: (d, 0, 0)),
                  pl.BlockSpec((1, 2 * S5_NSTATE, S5_W), lambda d, j: (d, 0, 0))],
        out_specs=pl.BlockSpec((1, n_slab, blk, 128), lambda d, j: (d, 0, chunk_of(d, j), 0)),
        out_shape=jax.ShapeDtypeStruct((2, n_slab, rows_total, 128), F32),
        scratch_shapes=[pltpu.VMEM((blk, 2 * S5_NSTATE), F32),
                        pltpu.VMEM((batch, 2 * S5_NSTATE), F32)],
        compiler_params=_cparams("arbitrary", "arbitrary"),
        name="s5_scan",
    )(us_flat, a, bd, cd)


def _merge_ln_kernel(gs_ref, y5_ref, us_ref, h_ref, *rest, batch):
    g = BATCH_GROUP
    gate_refs = rest[:g]
    dd_ref, gw_ref, gb_ref, w_ref, g_ref, beta_ref, o_ref, wbf_ref = rest[g:]

    @pl.when(_first_step())
    def _():
        wbf_ref[...] = w_ref[...].astype(BF16)

    def s5_input(i):
        rows = _time_major_rows(pl.program_id(1) * g + i, batch)
        y5 = jnp.concatenate([y5_ref[0, s, rows, :] + y5_ref[1, s, rows, :]
                              for s in range(S5_W // 128)], axis=1)
        us = jnp.concatenate([us_ref[s, rows, :] for s in range(S5_W // 128)], axis=1)
        return y5 + dd_ref[...] * us

    ge = jax.nn.gelu(jnp.concatenate([s5_input(i) for i in range(g)], axis=0))
    s5 = ge * jax.nn.sigmoid(
        jnp.dot(ge.astype(BF16), gw_ref[...], preferred_element_type=F32) + gb_ref[...])
    gs = jnp.concatenate([gs_ref[i] for i in range(g)], axis=0)
    y = jnp.dot(gs, wbf_ref[0:SSD_W, :], preferred_element_type=F32)
    y = y + jnp.dot(s5.astype(BF16), wbf_ref[SSD_W:, :], preferred_element_type=F32)
    rows = o_ref.shape[1]
    for i in range(g):
        o_ref[i] = _layer_norm(ALPHA * h_ref[i] + gate_refs[i][0] * y[i * rows:(i + 1) * rows],
                               g_ref[...], beta_ref[...])


def _merge_ln(g_ssd, y5, us_t, h, mods, layer, s5_d, glu_w, glu_b, w_out, ln_g, ln_b, ctx_tiles):
    batch, ta, _ = h.shape
    nt = ta // ROW_TILE - ctx_tiles
    n_slab = S5_W // 128
    tm_rows = ROW_TILE * batch
    g = BATCH_GROUP
    gate_specs = [
        pl.BlockSpec((1, 1, D_MODEL),
                     lambda t, bg, i=i: ((layer * MOD_ROWS + bg * g + i) * 6 + 2, 0, 0))
        for i in range(g)]
    return pl.pallas_call(
        functools.partial(_merge_ln_kernel, batch=batch),
        grid=(nt, batch // g),
        in_specs=[pl.BlockSpec((g, ROW_TILE, SSD_W), lambda t, b: (b, t, 0)),
                  pl.BlockSpec((2, n_slab, tm_rows, 128), lambda t, b: (0, 0, t + ctx_tiles, 0)),
                  pl.BlockSpec((n_slab, tm_rows, 128), lambda t, b: (0, t + ctx_tiles, 0)),
                  pl.BlockSpec((g, ROW_TILE, D_MODEL), lambda t, b: (b, t + ctx_tiles, 0))]
        + gate_specs
        + [_full((1, S5_W)), _full((S5_W, S5_W)), _full((1, S5_W)),
           _resident((D_MODEL, D_MODEL)), _full((1, D_MODEL)), _full((1, D_MODEL))],
        out_specs=pl.BlockSpec((g, ROW_TILE, D_MODEL), lambda t, b: (b, t, 0)),
        out_shape=jax.ShapeDtypeStruct((batch, nt * ROW_TILE, D_MODEL), F32),
        scratch_shapes=[pltpu.VMEM((D_MODEL, D_MODEL), BF16)],
        compiler_params=_cparams("arbitrary", "arbitrary"),
        name="merge_outproj_ln",
    )(g_ssd, y5, us_t, h, *([mods] * g), s5_d.reshape(1, S5_W), glu_w.astype(BF16),
      glu_b.reshape(1, S5_W), w_out, ln_g.reshape(1, -1), ln_b.reshape(1, -1))


def _attn_layer(h_ctx, h_lat, mods, layer, i, p, keep_ctx):
    n_ctx = h_ctx.shape[1]
    ctx_tiles = n_ctx // ROW_TILE
    lam_init = 0.8 - 0.6 * math.exp(-0.3 * layer)
    cos, sin = _rope_tables(h_lat.shape[1], n_ctx)
    q, k, v, f = _inproj_attn(h_ctx, h_lat, mods, layer, p['attn_w_in'][i],
                              cos, sin, ctx_tiles)
    o_ctx, o_lat = _attention(q, k, v, p['attn_lambda'][i], p['attn_subln_g'][i], lam_init,
                              ctx_tiles)
    fm = _fourier(f, p['fourier_w'][i], p['fourier_b'][i], n_ctx)
    h1 = _outproj_ln(o_ctx, o_lat, fm, h_ctx, h_lat, mods, layer,
                     p['attn_w_out'][i],
                     p['ln_g'][layer, 0], p['ln_b'][layer, 0], ctx_tiles)
    return _ffn(h1, mods, layer, p['ffn_w_up_bf16'], p['ffn_b_up'][layer],
                p['ffn_conv_w'][layer], p['ffn_conv_b'][layer],
                p['ffn_w_down_bf16'], p['ffn_b_down'][layer],
                p['ln_g'][layer, 1], p['ln_b'][layer, 1], ctx_tiles, 0 if keep_ctx else ctx_tiles)


def _ssm_layer(h, mods, layer, i, n_ctx, p, keep_ctx):
    assert not keep_ctx, "an SSM layer that must also emit context rows is not implemented"
    ctx_tiles = n_ctx // ROW_TILE
    batch = h.shape[0]
    z, xbc, dt, us_t = _inproj_ssm(h, mods, layer, p['ssm_w_in'][i], ctx_tiles)
    g_ssd = _ssd(xbc, dt, z, p['ssd_conv_w'][i], p['ssd_conv_b'][i], p['ssd_a_log'][i],
                 p['ssd_dt_bias'][i], p['ssd_d'][i], p['ssd_norm_g'][i], n_ctx)
    a, bd, cd = _s5_discretize(p['s5_lambda_re'][i], p['s5_lambda_im'][i], p['s5_log_dt'][i],
                               p['s5_b_re'][i], p['s5_b_im'][i], p['s5_c_re'][i], p['s5_c_im'][i])
    y5 = _s5_scan(us_t, a, bd, cd, batch, n_ctx)
    h1 = _merge_ln(g_ssd, y5, us_t, h, mods, layer, p['s5_d'][i], p['s5_glu_w'][i],
                   p['s5_glu_b'][i], p['ssm_w_out'][i],
                   p['ln_g'][layer, 0], p['ln_b'][layer, 0], ctx_tiles)
    return _ffn(h1, mods, layer, p['ffn_w_up_bf16'], p['ffn_b_up'][layer],
                p['ffn_conv_w'][layer], p['ffn_conv_b'][layer],
                p['ffn_w_down_bf16'], p['ffn_b_down'][layer],
                p['ln_g'][layer, 1], p['ln_b'][layer, 1], 0, 0, rows=FFN_LAT_ROWS)


def kernel(x, c, ctx, c_ctx, ada_w, ada_b, ln_g, ln_b, ffn_w_up, ffn_b_up, ffn_conv_w, ffn_conv_b, ffn_w_down, ffn_b_down, attn_w_in, attn_lambda, attn_subln_g, fourier_w, fourier_b, attn_w_out, ssm_w_in, ssd_conv_w, ssd_conv_b, ssd_a_log, ssd_dt_bias, ssd_d, ssd_norm_g, s5_lambda_re, s5_lambda_im, s5_log_dt, s5_b_re, s5_b_im, s5_c_re, s5_c_im, s5_d, s5_glu_w, s5_glu_b, ssm_w_out):
    p = dict(ln_g=ln_g, ln_b=ln_b, ffn_w_up=ffn_w_up, ffn_b_up=ffn_b_up, ffn_conv_w=ffn_conv_w,
             ffn_conv_b=ffn_conv_b, ffn_w_down=ffn_w_down, ffn_b_down=ffn_b_down,
             attn_w_in=attn_w_in, attn_lambda=attn_lambda, attn_subln_g=attn_subln_g,
             fourier_w=fourier_w, fourier_b=fourier_b, attn_w_out=attn_w_out, ssm_w_in=ssm_w_in,
             ssd_conv_w=ssd_conv_w, ssd_conv_b=ssd_conv_b, ssd_a_log=ssd_a_log,
             ssd_dt_bias=ssd_dt_bias, ssd_d=ssd_d, ssd_norm_g=ssd_norm_g,
             s5_lambda_re=s5_lambda_re, s5_lambda_im=s5_lambda_im, s5_log_dt=s5_log_dt,
             s5_b_re=s5_b_re, s5_b_im=s5_b_im, s5_c_re=s5_c_re, s5_c_im=s5_c_im, s5_d=s5_d,
             s5_glu_w=s5_glu_w, s5_glu_b=s5_glu_b, ssm_w_out=ssm_w_out)
    batch, n_lat, _ = x.shape
    n_ctx = ctx.shape[1]
    assert n_ctx == ROW_TILE and n_lat % ROW_TILE == 0 and batch < MOD_ROWS
    mods = _ada_mods(c, c_ctx, ada_w, ada_b)
    p['ffn_w_up_bf16'] = ffn_w_up.astype(BF16)
    p['ffn_w_down_bf16'] = ffn_w_down.astype(BF16)
    assert DEPTH == 2
    h = _attn_layer(ctx, x, mods, 0, 0, p, keep_ctx=True)
    return _ssm_layer(h, mods, 1, 0, n_ctx, p, keep_ctx=False)
```

```python
import functools
import math

import numpy as np
import jax
import jax.numpy as jnp
from jax import lax
from jax.experimental import pallas as pl
from jax.experimental.pallas import tpu as pltpu

F32 = jnp.float32
BF16 = jnp.bfloat16

D_MODEL = 1024
DEPTH = 2
GRID_W = 64
ROPE_BASE = 10000.0
LN_EPS = 1e-5
ALPHA = (2 * DEPTH) ** 0.25
ATTN_W = 768
ATTN_HEADS = 6
ATTN_HEAD_DIM = 64
F_W = 256
F_GROUPS = 4
F_GROUP_W = 64
ATTN_IN_W = 2 * ATTN_W + ATTN_W + F_W
SSD_W = 768
SSD_HEADS = 12
SSD_HEAD_DIM = 64
SSD_GROUPS = 2
SSD_STATE = 128
SSD_CHUNK = 128
XBC_W = SSD_W + 2 * SSD_GROUPS * SSD_STATE
DT_W = 2 * SSD_HEADS
DT_PAD = 128
S5_W = 256
S5_GROUPS = 16
S5_GROUP_W = 16
S5_STATE = 64
S5_NSTATE = S5_GROUPS * S5_STATE
SSM_IN_W = SSD_W + XBC_W + DT_W + S5_W
SSM_IN_PAD = SSD_W + XBC_W + S5_W + DT_PAD
D_FF = 2816
FF_TILE = 256

ROW_TILE = 256
ATTN_Q_SUB = 128
ATTN_ITEM_TILES = 2
FFN_LAT_ROWS = 512
BATCH_GROUP = 2
FOURIER_TILE = 1024
HALO = 8
MOD_ROWS = 16
VMEM_LIMIT_BYTES = 56 * 1024 * 1024


def _cparams(*sem):
    return pltpu.CompilerParams(dimension_semantics=sem, vmem_limit_bytes=VMEM_LIMIT_BYTES)


def _silu(v):
    return v * jax.nn.sigmoid(v)


def _layer_norm(v, g, b):
    mu = jnp.mean(v, axis=-1, keepdims=True)
    d = v - mu
    var = jnp.mean(d * d, axis=-1, keepdims=True)
    return d * lax.rsqrt(var + LN_EPS) * g + b


def _full(shape):
    nd = len(shape)
    return pl.BlockSpec(shape, lambda *_: (0,) * nd)


def _resident(shape):
    nd = len(shape)
    return pl.BlockSpec(shape, lambda *_: (0,) * nd, pipeline_mode=pl.Buffered(1))


def _first_step():
    return (pl.program_id(0) == 0) & (pl.program_id(1) == 0)


def _mod_spec(layer, j, batch, ctx_tiles):
    def idx(b, t):
        row = jnp.where(t < ctx_tiles, batch, b)
        return ((layer * MOD_ROWS + row) * 6 + j, 0, 0)
    return pl.BlockSpec((1, 1, D_MODEL), idx)


def _mod_specs_grouped(layer, j, batch, ctx_tiles):
    def spec(i):
        def idx(bg, t):
            row = jnp.where(t < ctx_tiles, batch, bg * BATCH_GROUP + i)
            return ((layer * MOD_ROWS + row) * 6 + j, 0, 0)
        return pl.BlockSpec((1, 1, D_MODEL), idx)
    return [spec(i) for i in range(BATCH_GROUP)]


def _ada_kernel(c_ref, w_ref, b_ref, o_ref):
    s = _silu(c_ref[...])
    w = w_ref[0]
    s_hi = s.astype(BF16)
    s_lo = (s - s_hi.astype(F32)).astype(BF16)
    w_hi = w.astype(BF16)
    w_lo = (w - w_hi.astype(F32)).astype(BF16)
    dot = functools.partial(jnp.dot, preferred_element_type=F32)
    o_ref[0] = dot(s_hi, w_hi) + (dot(s_lo, w_hi) + dot(s_hi, w_lo)) + b_ref[0]


def _ada_mods(c, c_ctx, ada_w, ada_b):
    batch = c.shape[0]
    nl = ada_w.shape[0]
    c_all = jnp.concatenate(
        [c, c_ctx[None], jnp.zeros((MOD_ROWS - batch - 1, D_MODEL), F32)], axis=0)
    out = pl.pallas_call(
        _ada_kernel,
        grid=(nl, 6),
        in_specs=[_full((MOD_ROWS, D_MODEL)),
                  pl.BlockSpec((1, D_MODEL, D_MODEL), lambda l, j: (l, 0, j)),
                  pl.BlockSpec((1, 1, D_MODEL), lambda l, j: (l, 0, j))],
        out_specs=pl.BlockSpec((1, MOD_ROWS, D_MODEL), lambda l, j: (l, 0, j)),
        out_shape=jax.ShapeDtypeStruct((nl, MOD_ROWS, 6 * D_MODEL), F32),
        compiler_params=_cparams("arbitrary", "arbitrary"),
        name="ada_mods",
    )(c_all, ada_w, ada_b.reshape(nl, 1, 6 * D_MODEL))
    return out.reshape(nl * MOD_ROWS * 6, 1, D_MODEL)


def _modulated_rows(h_of, shift_refs, scale_refs):
    return jnp.concatenate(
        [(h_of(i) * (1.0 + scale_refs[i][0]) + shift_refs[i][0]).astype(BF16)
         for i in range(BATCH_GROUP)], axis=0)


def _store_rows(ref, val):
    rows = ref.shape[1]
    for i in range(BATCH_GROUP):
        ref[i] = val[i * rows:(i + 1) * rows].astype(ref.dtype)


def _inproj_attn_kernel(hc_ref, hl_ref, *rest, ctx_tiles):
    g = BATCH_GROUP
    shift_refs, scale_refs = rest[:g], rest[g:2 * g]
    w_ref, cos_ref, sin_ref, q_ref, k_ref, v_ref, f_ref, wbf_ref = rest[2 * g:]

    @pl.when(_first_step())
    def _():
        wbf_ref[...] = w_ref[...].astype(BF16)

    is_ctx = pl.program_id(1) < ctx_tiles
    u = _modulated_rows(lambda i: jnp.where(is_ctx, hc_ref[i], hl_ref[i]), shift_refs, scale_refs)
    p = jnp.dot(u, wbf_ref[...], preferred_element_type=F32)
    cos = jnp.concatenate([cos_ref[...]] * g, axis=0)
    sin = jnp.concatenate([sin_ref[...]] * g, axis=0)
    lane = lax.broadcasted_iota(jnp.int32, cos.shape, 1)
    first_half = (lane % ATTN_HEAD_DIM) < (ATTN_HEAD_DIM // 2)

    def rope(blk):
        partner = jnp.where(first_half, pltpu.roll(blk, 128 - 32, 1), pltpu.roll(blk, 32, 1))
        return blk * cos + partner * sin

    qk_scale = ATTN_HEAD_DIM ** -0.5 * math.log2(math.e)
    rows = q_ref.shape[1]
    for i in range(ATTN_HEADS):
        lo, hi = i * 128, (i + 1) * 128
        qh = (rope(p[:, lo:hi]) * qk_scale).astype(BF16)
        kh = rope(p[:, ATTN_W + lo:ATTN_W + hi]).astype(BF16)
        for b in range(g):
            q_ref[b, :, lo:hi] = qh[b * rows:(b + 1) * rows]
            k_ref[b, :, lo:hi] = kh[b * rows:(b + 1) * rows]
    _store_rows(v_ref, p[:, 2 * ATTN_W:3 * ATTN_W])
    _store_rows(f_ref, p[:, 3 * ATTN_W:])


def _rope_tables(n_lat, n_ctx):
    rows = n_lat // GRID_W
    row = jnp.repeat(jnp.arange(rows, dtype=F32), GRID_W)
    col = jnp.tile(jnp.arange(GRID_W, dtype=F32), rows)
    n_freq = ATTN_HEAD_DIM // 4
    inv_freq = ROPE_BASE ** (-jnp.arange(n_freq, dtype=F32) / n_freq)
    ang = jnp.concatenate([row[:, None] * inv_freq, col[:, None] * inv_freq], axis=-1)
    cos, sin = jnp.cos(ang), jnp.sin(ang)
    cos128 = jnp.tile(cos, (1, 4))
    sin128 = jnp.tile(jnp.concatenate([-sin, sin], axis=-1), (1, 2))
    cos_all = jnp.concatenate([jnp.ones((n_ctx, 128), F32), cos128], axis=0)
    sin_all = jnp.concatenate([jnp.zeros((n_ctx, 128), F32), sin128], axis=0)
    return cos_all, sin_all


def _split_specs(width, ctx_tiles):
    g = BATCH_GROUP
    return [pl.BlockSpec((g, ROW_TILE, width), lambda b, t: (b, jnp.minimum(t, ctx_tiles - 1), 0)),
            pl.BlockSpec((g, ROW_TILE, width), lambda b, t: (b, jnp.maximum(t - ctx_tiles, 0), 0))]


def _inproj_attn(h_ctx, h_lat, mods, layer, w_in, cos, sin, ctx_tiles):
    batch = h_lat.shape[0]
    assert batch % BATCH_GROUP == 0
    ta = h_ctx.shape[1] + h_lat.shape[1]
    nt = ta // ROW_TILE
    row = lambda w: pl.BlockSpec((BATCH_GROUP, ROW_TILE, w), lambda b, t: (b, t, 0))
    tab = pl.BlockSpec((ROW_TILE, 128), lambda b, t: (t, 0))
    n_mod = 2 * BATCH_GROUP
    return pl.pallas_call(
        functools.partial(_inproj_attn_kernel, ctx_tiles=ctx_tiles),
        grid=(batch // BATCH_GROUP, nt),
        in_specs=_split_specs(D_MODEL, ctx_tiles)
        + _mod_specs_grouped(layer, 0, batch, ctx_tiles)
        + _mod_specs_grouped(layer, 1, batch, ctx_tiles)
        + [_resident((D_MODEL, ATTN_IN_W)), tab, tab],
        out_specs=[row(ATTN_W), row(ATTN_W), row(ATTN_W), row(F_W)],
        out_shape=[jax.ShapeDtypeStruct((batch, ta, ATTN_W), BF16)] * 3
        + [jax.ShapeDtypeStruct((batch, ta, F_W), BF16)],
        scratch_shapes=[pltpu.VMEM((D_MODEL, ATTN_IN_W), BF16)],
        compiler_params=_cparams("arbitrary", "arbitrary"),
        name="inproj_attn",
    )(h_ctx, h_lat, *([mods] * n_mod), w_in, cos, sin)


def _diff_lambda(lam_ref, lam_init):
    lamv = lam_ref[...]
    l1 = jnp.sum(lamv[0:1] * lamv[1:2], axis=-1, keepdims=True)
    l2 = jnp.sum(lamv[2:3] * lamv[3:4], axis=-1, keepdims=True)
    return jnp.exp(l1) - jnp.exp(l2) + lam_init


def _stack_maps(q):
    lane = lax.broadcasted_iota(jnp.int32, q.shape, 1)
    zero = jnp.zeros_like(q)
    return jnp.concatenate([jnp.where(lane < ATTN_HEAD_DIM, q, zero),
                            jnp.where(lane >= ATTN_HEAD_DIM, q, zero)], axis=0)


def _scores(q2, k):
    return lax.dot_general(q2, k, (((1,), (1,)), ((), ())), preferred_element_type=F32)


def _diff_softmax_pv(load_s0, m0, load_s1, m1, lam, v, gain):
    e0 = jnp.exp2(load_s0() - m0)
    e1 = jnp.exp2(load_s1() - m1)
    l0 = jnp.sum(e0, axis=-1, keepdims=True)
    l1 = jnp.sum(e1, axis=-1, keepdims=True)
    w = e0 - e1 * (lam * l0 / l1)
    o = jnp.dot(w.astype(BF16), v, preferred_element_type=F32) * (1.0 / l0)
    o = o * lax.rsqrt(jnp.mean(o * o, axis=-1, keepdims=True) + LN_EPS)
    return (o * gain).astype(BF16)


def _attn_ctx_kernel(lam_ref, g_ref, q_ref, k_ref, v_ref, o_ref, *, lam_init):
    lam = _diff_lambda(lam_ref, lam_init)
    gain = g_ref[...] * (1.0 - lam_init)
    tq = q_ref.shape[1]
    for hd in range(ATTN_HEADS):
        cols = slice(hd * 128, (hd + 1) * 128)
        s = _scores(_stack_maps(q_ref[0, :, cols]), k_ref[0, :, cols])
        m = jnp.max(s, axis=-1, keepdims=True)

        def half(lo, s=s):
            return lambda: s[lo:lo + tq]

        o_ref[0, :, cols] = _diff_softmax_pv(half(0), m[:tq], half(tq), m[tq:], lam,
                                             v_ref[0, :, cols], gain)


def _attn_lat_kernel(lam_ref, g_ref, k_ref, v_ref, *rest, lam_init):
    q_refs = rest[:ATTN_ITEM_TILES]
    o_ref, sa_ref, ma_ref, sb_ref, mb_ref = rest[ATTN_ITEM_TILES:]
    t = pl.program_id(0)
    tq = ATTN_ITEM_TILES * ROW_TILE
    sub = ATTN_Q_SUB

    @pl.when(t == 0)
    def _():
        sb_ref[...] = jnp.zeros_like(sb_ref)
        mb_ref[...] = jnp.zeros_like(mb_ref)

    def step(s_new, m_new, s_old, m_old):
        lam = _diff_lambda(lam_ref, lam_init)
        gain = g_ref[...] * (1.0 - lam_init)
        q = jnp.concatenate([q_ref[0] for q_ref in q_refs], axis=0)
        s = _scores(_stack_maps(q), k_ref[0])
        s_new[...] = s
        m_new[...] = jnp.max(s, axis=-1, keepdims=True)
        v = v_ref[0]
        for i in range(tq // sub):
            r0 = slice(i * sub, (i + 1) * sub)
            r1 = slice(tq + i * sub, tq + (i + 1) * sub)
            o_ref[0, r0, :] = _diff_softmax_pv(
                functools.partial(s_old.__getitem__, (r0, slice(None))), m_old[r0, :],
                functools.partial(s_old.__getitem__, (r1, slice(None))), m_old[r1, :],
                lam, v, gain)

    @pl.when(t % 2 == 0)
    def _():
        step(sa_ref, ma_ref, sb_ref, mb_ref)

    @pl.when(t % 2 == 1)
    def _():
        step(sb_ref, mb_ref, sa_ref, ma_ref)


def _attention(q, k, v, lam_vec, subln_g, lam_init, ctx_tiles):
    batch, ta, _ = q.shape
    n_ctx = ctx_tiles * ROW_TILE
    nt = ta // ROW_TILE - ctx_tiles
    g = subln_g.reshape(1, 128)
    small = [_full((4, ATTN_HEAD_DIM)), _full((1, 128))]
    cspec = pl.BlockSpec((1, n_ctx, ATTN_W), lambda b: (b, 0, 0))
    o_ctx = pl.pallas_call(
        functools.partial(_attn_ctx_kernel, lam_init=lam_init),
        grid=(batch,),
        in_specs=small + [cspec, cspec, cspec],
        out_specs=cspec,
        out_shape=jax.ShapeDtypeStruct((batch, n_ctx, ATTN_W), BF16),
        compiler_params=_cparams("arbitrary"),
        name="diff_attention_ctx",
    )(lam_vec, g, q, k, v)

    per = ATTN_ITEM_TILES
    assert nt % per == 0
    ni = nt // per
    n_items = batch * ATTN_HEADS * ni

    def item(j):
        return j // (ATTN_HEADS * ni), (j // ni) % ATTN_HEADS, j % ni

    def score_item(j):
        return item(jnp.minimum(j, n_items - 1))

    def finish_item(j):
        return item(jnp.maximum(j - 1, 0))

    def q_spec(part):
        def idx(j):
            b, h, t = score_item(j)
            return b, ctx_tiles + t * per + part, h
        return pl.BlockSpec((1, ROW_TILE, 128), idx)

    def k_idx(j):
        b, h, _ = score_item(j)
        return b, 0, h

    def v_idx(j):
        b, h, _ = finish_item(j)
        return b, 0, h

    def o_idx(j):
        b, h, t = finish_item(j)
        return b, t, h

    o_lat = pl.pallas_call(
        functools.partial(_attn_lat_kernel, lam_init=lam_init),
        grid=(n_items + 1,),
        in_specs=small + [pl.BlockSpec((1, ta, 128), k_idx), pl.BlockSpec((1, ta, 128), v_idx)]
        + [q_spec(part) for part in range(per)],
        out_specs=pl.BlockSpec((1, per * ROW_TILE, 128), o_idx),
        out_shape=jax.ShapeDtypeStruct((batch, nt * ROW_TILE, ATTN_W), BF16),
        scratch_shapes=[pltpu.VMEM((2 * per * ROW_TILE, ta), F32),
                        pltpu.VMEM((2 * per * ROW_TILE, 1), F32)] * 2,
        compiler_params=_cparams("arbitrary"),
        name="diff_attention",
    )(lam_vec, g, k, v, *([q] * per))
    return o_ctx, o_lat


def _dft_tables(n):
    k = np.arange(n, dtype=np.int64)
    ang = 2.0 * np.pi * ((k[:, None] * k[None, :]) % n).astype(np.float64) / n
    return np.cos(ang), np.sin(ang)


def _fourier_kernel(f_ref, cs_ref, dl_ref, dc_ref, w_ref, b_ref, oc_ref, ol_ref, ac_ref, al_ref,
                    *, n_ctx, n_lat):
    t = pl.program_id(0)
    b = pl.program_id(1)

    def stage1(rows0, n):
        a = jnp.dot(f_ref[0, rows0:rows0 + n, :], cs_ref[...], preferred_element_type=F32)
        return a[:, :F_W].astype(BF16), a[:, F_W:].astype(BF16)

    def stage2(dft, a, n):
        z = jnp.dot(dft, a, preferred_element_type=F32)
        z = z * (1.0 / math.sqrt(n * F_GROUP_W))
        o = jnp.dot(z.astype(BF16), w_ref[...], preferred_element_type=F32) + b_ref[...]
        return o.astype(BF16)

    @pl.when(t == 0)
    def _():
        ac_ref[0:n_ctx, :], ac_ref[n_ctx:, :] = stage1(0, n_ctx)
        oc_ref[0] = stage2(dc_ref[...], ac_ref[...], n_ctx)

    @pl.when(t == 1)
    def _():
        al_ref[b, 0:n_lat, :], al_ref[b, n_lat:, :] = stage1(n_ctx, n_lat)

    @pl.when(t >= 1)
    def _():
        ol_ref[0] = stage2(dl_ref[...], al_ref[b], n_lat)


def _fourier(f, fourier_w, fourier_b, n_ctx):
    batch, ta, _ = f.shape
    n_lat = ta - n_ctx
    ft = min(FOURIER_TILE, n_lat)
    assert n_lat % ft == 0
    cc, sc = _dft_tables(F_GROUP_W)
    eye = np.eye(F_GROUPS)
    cs = jnp.asarray(np.concatenate([np.kron(eye, cc), np.kron(eye, sc)], axis=1), BF16)
    cl, sl = _dft_tables(n_lat)
    dft_lat = jnp.asarray(np.concatenate([cl, -sl], axis=1), BF16)
    cx, sx = _dft_tables(n_ctx)
    dft_ctx = jnp.asarray(np.concatenate([cx, -sx], axis=1), BF16)
    w_blk = jnp.einsum('gce,gh->gche', fourier_w, jnp.eye(F_GROUPS, dtype=F32))
    w_blk = w_blk.reshape(F_W, F_W).astype(BF16)
    return pl.pallas_call(
        functools.partial(_fourier_kernel, n_ctx=n_ctx, n_lat=n_lat),
        grid=(1 + n_lat // ft, batch),
        in_specs=[
                  pl.BlockSpec((1, ta, F_W),
                               lambda t, b: (jnp.where(t <= 1, b, batch - 1), 0, 0)),
                  _full((F_W, 2 * F_W)),
                  pl.BlockSpec((ft, 2 * n_lat), lambda t, b: (jnp.maximum(t - 1, 0), 0)),
                  _full((n_ctx, 2 * n_ctx)),
                  _full((F_W, F_W)), _full((1, F_W))],
        out_specs=[pl.BlockSpec((1, n_ctx, F_W),
                                lambda t, b: (jnp.where(t == 0, b, batch - 1), 0, 0)),
                   pl.BlockSpec((1, ft, F_W),
                                lambda t, b: (jnp.where(t == 0, 0, b), jnp.maximum(t - 1, 0), 0))],
        out_shape=[jax.ShapeDtypeStruct((batch, n_ctx, F_W), BF16),
                   jax.ShapeDtypeStruct((batch, n_lat, F_W), BF16)],
        scratch_shapes=[pltpu.VMEM((2 * n_ctx, F_W), BF16),
                        pltpu.VMEM((batch, 2 * n_lat, F_W), BF16)],
        compiler_params=_cparams("arbitrary", "arbitrary"),
        name="fourier_mix",
    )(f, cs, dft_lat, dft_ctx, w_blk, fourier_b.reshape(1, F_W))


def _outproj_ln_kernel(ac_ref, al_ref, bc_ref, bl_ref, hc_ref, hl_ref, *rest, ctx_tiles):
    g = BATCH_GROUP
    gate_refs = rest[:g]
    w_ref, g_ref, beta_ref, o_ref, wbf_ref = rest[g:]

    @pl.when(_first_step())
    def _():
        wbf_ref[...] = w_ref[...].astype(BF16)

    is_ctx = pl.program_id(1) < ctx_tiles
    a = jnp.concatenate([jnp.where(is_ctx, ac_ref[i], al_ref[i]) for i in range(g)], axis=0)
    b2 = jnp.concatenate([jnp.where(is_ctx, bc_ref[i], bl_ref[i]) for i in range(g)], axis=0)
    wa = a.shape[1]
    y = jnp.dot(a, wbf_ref[0:wa, :], preferred_element_type=F32)
    y = y + jnp.dot(b2, wbf_ref[wa:, :], preferred_element_type=F32)
    rows = o_ref.shape[1]
    for i in range(g):
        h = jnp.where(is_ctx, hc_ref[i], hl_ref[i])
        o_ref[i] = _layer_norm(ALPHA * h + gate_refs[i][0] * y[i * rows:(i + 1) * rows],
                               g_ref[...], beta_ref[...])


def _outproj_ln(a_ctx, a_lat, b_ctx, b_lat, h_ctx, h_lat, mods, layer, w_out, ln_g, ln_b,
                ctx_tiles):
    batch = h_lat.shape[0]
    ta = h_ctx.shape[1] + h_lat.shape[1]
    nt = ta // ROW_TILE
    row = lambda w: pl.BlockSpec((BATCH_GROUP, ROW_TILE, w), lambda b, t: (b, t, 0))
    return pl.pallas_call(
        functools.partial(_outproj_ln_kernel, ctx_tiles=ctx_tiles),
        grid=(batch // BATCH_GROUP, nt),
        in_specs=_split_specs(a_lat.shape[2], ctx_tiles) + _split_specs(b_lat.shape[2], ctx_tiles)
        + _split_specs(D_MODEL, ctx_tiles)
        + _mod_specs_grouped(layer, 2, batch, ctx_tiles)
        + [_resident((D_MODEL, D_MODEL)), _full((1, D_MODEL)), _full((1, D_MODEL))],
        out_specs=row(D_MODEL),
        out_shape=jax.ShapeDtypeStruct((batch, ta, D_MODEL), F32),
        scratch_shapes=[pltpu.VMEM((D_MODEL, D_MODEL), BF16)],
        compiler_params=_cparams("arbitrary", "arbitrary"),
        name="outproj_ln",
    )(a_ctx, a_lat, b_ctx, b_lat, h_ctx, h_lat, *([mods] * BATCH_GROUP), w_out,
      ln_g.reshape(1, D_MODEL),
      ln_b.reshape(1, D_MODEL))


def _ffn_kernel(h_ref, hp_ref, hn_ref, *rest, group, ctx_tiles, tile_off, nt_seq):
    shift_refs, scale_refs, gate_refs = rest[:group], rest[group:2 * group], rest[2 * group:3 * group]
    (wup_ref, bup_ref, cw_ref, cb_ref, wdn_ref, bdn_ref, g_ref, beta_ref,
     o_ref, uext_ref, ubf_ref, act_ref) = rest[3 * group:]
    t = pl.program_id(1) + tile_off
    seg_first = (t == 0) | (t == ctx_tiles)
    seg_last = (t == nt_seq - 1) | (t == ctx_tiles - 1)
    tm = h_ref.shape[1]
    ext = tm + 2 * HALO
    for b in range(group):
        sc = 1.0 + scale_refs[b][0]
        sh = shift_refs[b][0]
        r0 = b * ext
        uext_ref[r0:r0 + HALO, :] = hp_ref[b] * sc + sh
        uext_ref[r0 + HALO:r0 + HALO + tm, :] = h_ref[b] * sc + sh
        uext_ref[r0 + HALO + tm:r0 + ext, :] = hn_ref[b] * sc + sh
    ubf_ref[...] = uext_ref[...].astype(BF16)
    row8 = lax.broadcasted_iota(jnp.int32, (8, FF_TILE), 0)

    def hidden(col0):
        cols = slice(col0, col0 + FF_TILE)
        zr_all = jnp.dot(ubf_ref[...], wup_ref[:, cols], preferred_element_type=F32)
        bup = bup_ref[:, cols]
        cw = cw_ref[:, cols]
        bias = cb_ref[:, cols] + (cw[0:1] + cw[1:2] + cw[2:3]) * bup
        outs = []
        for b in range(group):
            zr = zr_all[b * ext:(b + 1) * ext]
            z0 = zr[HALO:HALO + tm]
            prev = jnp.where(seg_first, -bup, zr[HALO - 1:HALO])
            nxt = jnp.where(seg_last, -bup, zr[HALO + tm:HALO + tm + 1])
            down = pltpu.roll(z0, 1, 0)
            up = pltpu.roll(z0, tm - 1, 0)
            zm1 = jnp.concatenate([jnp.where(row8 == 0, prev, down[0:8]), down[8:]], axis=0)
            zp1 = jnp.concatenate([up[:tm - 8], jnp.where(row8 == 7, nxt, up[tm - 8:])], axis=0)
            outs.append(cw[0:1] * zm1 + cw[1:2] * z0 + cw[2:3] * zp1 + bias)
        return jnp.concatenate(outs, axis=0)

    for j in range(D_FF // FF_TILE):
        val = hidden(j * FF_TILE)
        gat = hidden(D_FF + j * FF_TILE)
        act_ref[:, j * FF_TILE:(j + 1) * FF_TILE] = (val * _silu(gat)).astype(BF16)
    f = jnp.dot(act_ref[...], wdn_ref[...], preferred_element_type=F32) + bdn_ref[...]
    for b in range(group):
        o_ref[b] = _layer_norm(ALPHA * h_ref[b] + gate_refs[b][0] * f[b * tm:(b + 1) * tm],
                               g_ref[...], beta_ref[...])


def _ffn(h, mods, layer, w_up, b_up, conv_w, conv_b, w_down, b_down, ln_g, ln_b,
         ctx_tiles, tile_off, rows=ROW_TILE, group=1):
    batch, ta, _ = h.shape
    assert batch % group == 0
    nt_seq = ta // rows
    nt = nt_seq - tile_off
    hb = rows // HALO
    n_hblk = ta // HALO

    def mspecs(j):
        def spec(i):
            def idx(bg, t):
                r = jnp.where(t + tile_off < ctx_tiles, batch, bg * group + i)
                return ((layer * MOD_ROWS + r) * 6 + j, 0, 0)
            return pl.BlockSpec((1, 1, D_MODEL), idx)
        return [spec(i) for i in range(group)]

    resident = lambda shape: pl.BlockSpec((None,) + shape, lambda *_: (layer, 0, 0),
                                          pipeline_mode=pl.Buffered(1))
    ext = rows + 2 * HALO
    return pl.pallas_call(
        functools.partial(_ffn_kernel, group=group, ctx_tiles=ctx_tiles, tile_off=tile_off,
                          nt_seq=nt_seq),
        grid=(batch // group, nt),
        in_specs=[pl.BlockSpec((group, rows, D_MODEL), lambda b, t: (b, t + tile_off, 0)),
                  pl.BlockSpec((group, HALO, D_MODEL),
                               lambda b, t: (b, jnp.maximum((t + tile_off) * hb - 1, 0), 0)),
                  pl.BlockSpec((group, HALO, D_MODEL),
                               lambda b, t: (b, jnp.minimum((t + tile_off + 1) * hb, n_hblk - 1), 0))]
        + mspecs(3) + mspecs(4) + mspecs(5)
        + [resident((D_MODEL, 2 * D_FF)), _full((1, 2 * D_FF)),
           _full((3, 2 * D_FF)), _full((1, 2 * D_FF)),
           resident((D_FF, D_MODEL)), _full((1, D_MODEL)),
           _full((1, D_MODEL)), _full((1, D_MODEL))],
        out_specs=pl.BlockSpec((group, rows, D_MODEL), lambda b, t: (b, t, 0)),
        out_shape=jax.ShapeDtypeStruct((batch, nt * rows, D_MODEL), F32),
        scratch_shapes=[pltpu.VMEM((group * ext, D_MODEL), F32),
                        pltpu.VMEM((group * ext, D_MODEL), BF16),
                        pltpu.VMEM((group * rows, D_FF), BF16)],
        compiler_params=_cparams("parallel", "arbitrary"),
        name="conv_ffn_ln",
    )(h, h, h, *([mods] * (3 * group)), w_up, b_up.reshape(1, -1), conv_w, conv_b.reshape(1, -1),
      w_down, b_down.reshape(1, -1), ln_g.reshape(1, -1), ln_b.reshape(1, -1))


def _time_major_rows(b, batch):
    return pl.ds(b, ROW_TILE, stride=batch)


def _inproj_ssm_kernel(h_ref, *rest, batch):
    g = BATCH_GROUP
    shift_refs, scale_refs = rest[:g], rest[g:2 * g]
    w_ref, z_ref, xbc_ref, dt_ref, us_ref, wbf_ref = rest[2 * g:]

    @pl.when(_first_step())
    def _():
        head = SSD_W + XBC_W
        wbf_ref[0:head, :] = w_ref[0:head, :].astype(BF16)
        wbf_ref[head:head + S5_W, :] = w_ref[head + DT_W:head + DT_W + S5_W, :].astype(BF16)
        pad = jnp.zeros((DT_PAD - DT_W, D_MODEL), F32)
        wbf_ref[head + S5_W:, :] = jnp.concatenate(
            [w_ref[head:head + DT_W, :], pad], axis=0).astype(BF16)

    u = _modulated_rows(lambda i: h_ref[i], shift_refs, scale_refs)
    p = lax.dot_general(u, wbf_ref[...], (((1,), (1,)), ((), ())), preferred_element_type=F32)
    _store_rows(z_ref, p[:, :SSD_W])
    _store_rows(xbc_ref, p[:, SSD_W:SSD_W + XBC_W])
    _store_rows(dt_ref, p[:, SSD_W + XBC_W + S5_W:])
    for i in range(g):
        rows = _time_major_rows(pl.program_id(1) * g + i, batch)
        for j in range(S5_W // 128):
            col0 = SSD_W + XBC_W + j * 128
            us_ref[j, rows, :] = p[i * ROW_TILE:(i + 1) * ROW_TILE, col0:col0 + 128]


def _inproj_ssm(h, mods, layer, w_in, ctx_tiles):
    batch, ta, _ = h.shape
    assert batch % BATCH_GROUP == 0
    nt = ta // ROW_TILE
    row = lambda w: pl.BlockSpec((BATCH_GROUP, ROW_TILE, w), lambda t, b: (b, t, 0))

    def mspecs(j):
        def spec(i):
            def idx(t, bg):
                r = jnp.where(t < ctx_tiles, batch, bg * BATCH_GROUP + i)
                return ((layer * MOD_ROWS + r) * 6 + j, 0, 0)
            return pl.BlockSpec((1, 1, D_MODEL), idx)
        return [spec(i) for i in range(BATCH_GROUP)]

    return pl.pallas_call(
        functools.partial(_inproj_ssm_kernel, batch=batch),
        grid=(nt, batch // BATCH_GROUP),
        in_specs=[row(D_MODEL)] + mspecs(0) + mspecs(1) + [_resident((SSM_IN_W, D_MODEL))],
        out_specs=[row(SSD_W), row(XBC_W), row(DT_PAD),
                   pl.BlockSpec((S5_W // 128, ROW_TILE * batch, 128), lambda t, b: (0, t, 0))],
        out_shape=[jax.ShapeDtypeStruct((batch, ta, SSD_W), F32),
                   jax.ShapeDtypeStruct((batch, ta, XBC_W), F32),
                   jax.ShapeDtypeStruct((batch, ta, DT_PAD), F32),
                   jax.ShapeDtypeStruct((S5_W // 128, ta * batch, 128), F32)],
        scratch_shapes=[pltpu.VMEM((SSM_IN_PAD, D_MODEL), BF16)],
        compiler_params=_cparams("arbitrary", "arbitrary"),
        name="inproj_ssm",
    )(h, *([mods] * (2 * BATCH_GROUP)), jnp.swapaxes(w_in, 0, 1))


def _cumsum_rows(v):
    n = v.shape[0]
    row = lax.broadcasted_iota(jnp.int32, v.shape, 0)
    s = 1
    while s < n:
        v = v + jnp.where(row >= s, pltpu.roll(v, s, 0), 0.0)
        s *= 2
    return v


def _expand_heads(v, e_ref):
    hi = v.astype(BF16)
    lo = (v - hi.astype(F32)).astype(BF16)
    e = e_ref[...]
    return (jnp.dot(hi, e, preferred_element_type=F32)
            + jnp.dot(lo, e, preferred_element_type=F32))


def _ssd_kernel(xbc_ref, xp_ref, xn_ref, dt_ref, z_ref, cw_ref, cb_ref, alog_ref, dtb_ref,
                dsk_ref, ng_ref, ef_ref, eb_ref, o_ref,
                xs_ref, cd_ref, st_ref, dec_ref, y_ref, *, n_tiles, ctx_tiles):
    s = pl.program_id(1)
    q = SSD_CHUNK
    cpt = ROW_TILE // q
    n_chunks = n_tiles * cpt
    ctx_chunks = ctx_tiles * cpt
    gw = SSD_W // SSD_GROUPS
    hpg = SSD_HEADS // SSD_GROUPS

    @pl.when(s < n_tiles)
    def _phase0():
        p = s
        seg_first = (p == 0) | (p == ctx_tiles)
        seg_last = (p == ctx_tiles - 1) | (p == n_tiles - 1)
        xr = xbc_ref[0]
        prev = jnp.where(seg_first, 0.0, xp_ref[0, HALO - 1:HALO, :])
        nxt = jnp.where(seg_last, 0.0, xn_ref[0, 0:1, :])
        row8 = lax.broadcasted_iota(jnp.int32, (8, XBC_W), 0)
        down = pltpu.roll(xr, 1, 0)
        up = pltpu.roll(xr, ROW_TILE - 1, 0)
        xm1 = jnp.concatenate([jnp.where(row8 == 0, prev, down[0:8]), down[8:]], axis=0)
        xp1 = jnp.concatenate([up[:ROW_TILE - 8], jnp.where(row8 == 7, nxt, up[ROW_TILE - 8:])],
                              axis=0)
        cw = cw_ref[...]
        xs_tile = _silu(cw[0:1] * xm1 + cw[1:2] * xr + cw[2:3] * xp1 + cb_ref[...])
        xs_ref[p] = xs_tile

        raw = dt_ref[0] + dtb_ref[...]
        dt_tile = jnp.maximum(raw, 0.0) + jnp.log1p(jnp.exp(-jnp.abs(raw)))
        a_row = -jnp.exp(alog_ref[...])
        for i in range(cpt):
            c = p * cpt + i
            xs = xs_tile[i * q:(i + 1) * q]
            dtv = dt_tile[i * q:(i + 1) * q]
            adt = dtv * a_row
            cum = _cumsum_rows(adt)
            tot = cum[q - 1:q, :]
            lane = lax.broadcasted_iota(jnp.int32, cum.shape, 1)
            cc = jnp.where(lane < SSD_HEADS, cum, tot - cum + adt)
            cd_ref[c, 0] = cc
            cd_ref[c, 1] = dtv
            w_end = jnp.exp(tot - cc) * dtv
            dec16 = jnp.broadcast_to(jnp.exp(tot), (16, DT_PAD))
            x = xs[:, :SSD_W]
            for d, e_ref in enumerate((ef_ref, eb_ref)):
                wx = (_expand_heads(w_end, e_ref) * x).astype(BF16)
                for g in range(SSD_GROUPS):
                    bmt = xs[:, SSD_W + g * SSD_STATE:SSD_W + (g + 1) * SSD_STATE].T.astype(BF16)
                    st_ref[c, d, :, g * gw:(g + 1) * gw] = jnp.dot(
                        bmt, wx[:, g * gw:(g + 1) * gw], preferred_element_type=F32)
                dec_ref[c, d] = _expand_heads(dec16, e_ref)[0:8]

    @pl.when(s == n_tiles)
    def _recurrence():
        fwd = list(range(n_chunks))
        bwd = list(range(ctx_chunks - 1, -1, -1)) + list(range(n_chunks - 1, ctx_chunks - 1, -1))
        for d, order in enumerate((fwd, bwd)):
            for col0 in range(0, SSD_W, 128):
                cols = slice(col0, col0 + 128)
                state = jnp.zeros((SSD_STATE, 128), F32)
                for ci in order:
                    contrib = st_ref[ci, d, :, cols]
                    st_ref[ci, d, :, cols] = state
                    state = state * dec_ref[ci, d, 0:1, cols] + contrib

    @pl.when(s >= n_tiles)
    def _phase1():
        p = s - n_tiles + ctx_tiles
        rowi = lax.broadcasted_iota(jnp.int32, (q, q), 0)
        coli = lax.broadcasted_iota(jnp.int32, (q, q), 1)
        lower = coli <= rowi
        upper = coli >= rowi
        lane = lax.broadcasted_iota(jnp.int32, (q, 128), 1)
        neg = jnp.float32(-jnp.inf)
        for i in range(cpt):
            c = p * cpt + i
            rows = slice(i * q, (i + 1) * q)
            xs = xs_ref[p, rows, :]
            x = xs[:, :SSD_W]
            cc = cd_ref[c, 0]
            dtv = cd_ref[c, 1]
            cct = cc.T
            dtt = dtv.T
            ecc = jnp.exp(cc)
            ef = _expand_heads(ecc, ef_ref)
            eb = _expand_heads(ecc, eb_ref)
            for g in range(SSD_GROUPS):
                bm = xs[:, SSD_W + g * SSD_STATE:SSD_W + (g + 1) * SSD_STATE].astype(BF16)
                cm = xs[:, SSD_W + (SSD_GROUPS + g) * SSD_STATE:
                        SSD_W + (SSD_GROUPS + g + 1) * SSD_STATE].astype(BF16)
                gmat = lax.dot_general(cm, bm, (((1,), (1,)), ((), ())),
                                       preferred_element_type=F32)
                sl = slice(g * gw, (g + 1) * gw)
                yoff = (ef[:, sl] * jnp.dot(cm, st_ref[c, 0, :, sl].astype(BF16),
                                            preferred_element_type=F32)
                        + eb[:, sl] * jnp.dot(cm, st_ref[c, 1, :, sl].astype(BF16),
                                              preferred_element_type=F32))
                for pair in range(hpg // 2):
                    col0 = g * gw + pair * 128
                    xpair = x[:, col0:col0 + 128].astype(BF16)
                    res = []
                    for hh in range(2):
                        hd = g * hpg + pair * 2 + hh
                        hb_ = SSD_HEADS + hd
                        lf = jnp.exp(jnp.where(lower, cc[:, hd:hd + 1] - cct[hd:hd + 1, :], neg))
                        lb = jnp.exp(jnp.where(upper, cc[:, hb_:hb_ + 1] - cct[hb_:hb_ + 1, :],
                                               neg))
                        mt = gmat * (lf * dtt[hd:hd + 1, :] + lb * dtt[hb_:hb_ + 1, :])
                        res.append(jnp.dot(mt.astype(BF16), xpair, preferred_element_type=F32))
                    ydiag = jnp.where(lane < SSD_HEAD_DIM, res[0], res[1])
                    y_ref[rows, col0:col0 + 128] = (
                        ydiag + yoff[:, pair * 128:(pair + 1) * 128]
                        + dsk_ref[:, col0:col0 + 128] * x[:, col0:col0 + 128])
        gated = y_ref[...] * _silu(z_ref[0])
        normed = gated * lax.rsqrt(jnp.mean(gated * gated, axis=-1, keepdims=True) + LN_EPS)
        o_ref[0] = (normed * ng_ref[...]).astype(BF16)


def _ssd(xbc, dt, z, conv_w, conv_b, a_log, dt_bias, d_skip, norm_g, n_ctx):
    batch, ta, _ = xbc.shape
    q = SSD_CHUNK
    n_chunks = ta // q
    n_tiles = ta // ROW_TILE
    ctx_tiles = n_ctx // ROW_TILE
    hb = ROW_TILE // HALO
    n_hblk = ta // HALO
    pad24 = lambda v: jnp.pad(v.reshape(1, DT_W), ((0, 0), (0, DT_PAD - DT_W)))
    heads = np.arange(SSD_HEADS)
    ef = np.zeros((DT_PAD, SSD_W), np.float32)
    eb = np.zeros((DT_PAD, SSD_W), np.float32)
    for hd in heads:
        ef[hd, hd * SSD_HEAD_DIM:(hd + 1) * SSD_HEAD_DIM] = 1.0
        eb[SSD_HEADS + hd, hd * SSD_HEAD_DIM:(hd + 1) * SSD_HEAD_DIM] = 1.0
    dsk = jnp.repeat(d_skip.astype(F32), SSD_HEAD_DIM).reshape(1, SSD_W)
    ph0 = lambda s: s < n_tiles
    tile = lambda w: pl.BlockSpec(
        (1, ROW_TILE, w), lambda b, s: (b, jnp.where(ph0(s), s, n_tiles - 1), 0))
    return pl.pallas_call(
        functools.partial(_ssd_kernel, n_tiles=n_tiles, ctx_tiles=ctx_tiles),
        grid=(batch, 2 * n_tiles - ctx_tiles),
        in_specs=[tile(XBC_W),
                  pl.BlockSpec((1, HALO, XBC_W),
                               lambda b, s: (b, jnp.where(ph0(s), jnp.maximum(s * hb - 1, 0), 0), 0)),
                  pl.BlockSpec((1, HALO, XBC_W),
                               lambda b, s: (b, jnp.where(ph0(s), jnp.minimum((s + 1) * hb, n_hblk - 1), 0), 0)),
                  tile(DT_PAD),
                  pl.BlockSpec((1, ROW_TILE, SSD_W),
                               lambda b, s: (b, jnp.where(ph0(s), 0, s - n_tiles + ctx_tiles), 0)),
                  _full((3, XBC_W)), _full((1, XBC_W)), _full((1, DT_PAD)), _full((1, DT_PAD)),
                  _full((1, SSD_W)), _full((1, SSD_W)),
                  _full((DT_PAD, SSD_W)), _full((DT_PAD, SSD_W))],
        out_specs=pl.BlockSpec(
            (1, ROW_TILE, SSD_W), lambda b, s: (b, jnp.where(ph0(s), 0, s - n_tiles), 0)),
        out_shape=jax.ShapeDtypeStruct((batch, ta - n_ctx, SSD_W), BF16),
        scratch_shapes=[pltpu.VMEM((n_tiles, ROW_TILE, XBC_W), F32),
                        pltpu.VMEM((n_chunks, 2, q, DT_PAD), F32),
                        pltpu.VMEM((n_chunks, 2, SSD_STATE, SSD_W), F32),
                        pltpu.VMEM((n_chunks, 2, 8, SSD_W), F32),
                        pltpu.VMEM((ROW_TILE, SSD_W), F32)],
        compiler_params=_cparams("parallel", "arbitrary"),
        name="ssd_bidir",
    )(xbc, xbc, xbc, dt, z, conv_w, conv_b.reshape(1, XBC_W), pad24(a_log), pad24(dt_bias),
      dsk, norm_g.reshape(1, SSD_W), jnp.asarray(ef, BF16), jnp.asarray(eb, BF16))


def _s5_disc_kernel(lr_ref, li_ref, ldt_ref, bre_ref, bim_ref, cre_ref, cim_ref,
                    a_ref, bd_ref, cd_ref):
    lr, li = lr_ref[...], li_ref[...]
    dt = jnp.exp(ldt_ref[...])
    mag = jnp.exp(dt * lr)
    ab_re, ab_im = mag * jnp.cos(dt * li), mag * jnp.sin(dt * li)
    den = lr * lr + li * li
    k_re = ((ab_re - 1.0) * lr + ab_im * li) / den
    k_im = (ab_im * lr - (ab_re - 1.0) * li) / den
    bre, bim = bre_ref[...], bim_ref[...]
    for d in range(2):
        a_ref[d, :, 0:S5_NSTATE] = jnp.broadcast_to(ab_re[d:d + 1], (8, S5_NSTATE))
        a_ref[d, :, S5_NSTATE:] = jnp.broadcast_to(ab_im[d:d + 1], (8, S5_NSTATE))
        kr, ki = k_re[d:d + 1], k_im[d:d + 1]
        bd_ref[d, :, 0:S5_NSTATE] = (kr * bre - ki * bim).astype(BF16)
        bd_ref[d, :, S5_NSTATE:] = (kr * bim + ki * bre).astype(BF16)
        cd_ref[d, 0:S5_NSTATE, :] = cre_ref[d].astype(BF16)
        cd_ref[d, S5_NSTATE:, :] = (-cim_ref[d]).astype(BF16)


def _s5_discretize(lam_re, lam_im, log_dt, b_re, b_im, c_re, c_im):
    eye = jnp.eye(S5_GROUPS, dtype=F32)
    bd = lambda b: jnp.einsum('gph,gk->ghkp', b, eye).reshape(S5_W, S5_NSTATE)
    cd = lambda cc: jnp.einsum('dghp,gk->dgpkh', cc, eye).reshape(2, S5_NSTATE, S5_W)
    ldt = jnp.repeat(log_dt, S5_STATE, axis=-1)
    return pl.pallas_call(
        _s5_disc_kernel,
        out_shape=[jax.ShapeDtypeStruct((2, 8, 2 * S5_NSTATE), F32),
                   jax.ShapeDtypeStruct((2, S5_W, 2 * S5_NSTATE), BF16),
                   jax.ShapeDtypeStruct((2, 2 * S5_NSTATE, S5_W), BF16)],
        compiler_params=pltpu.CompilerParams(vmem_limit_bytes=VMEM_LIMIT_BYTES),
        name="s5_discretize",
    )(lam_re.reshape(2, S5_NSTATE), lam_im.reshape(2, S5_NSTATE), ldt,
      bd(b_re), bd(b_im), cd(c_re), cd(c_im))


S5_TIME_CHUNK = 256
S5_UNROLL = 8


def _s5_scan_kernel(u_ref, a_ref, bd_ref, cd_ref, o_ref, hs_ref, carry_ref, *, batch):
    d = pl.program_id(0)
    j = pl.program_id(1)
    n = S5_NSTATE

    @pl.when(j == 0)
    def _():
        carry_ref[...] = jnp.zeros_like(carry_ref)

    half = hs_ref.shape[0] // 2
    n_slab = S5_W // 128
    for r in (0, half):
        u = jnp.concatenate([u_ref[s, r:r + half, :] for s in range(n_slab)], axis=1)
        hs_ref[r:r + half, :] = jnp.dot(u.astype(BF16), bd_ref[0], preferred_element_type=F32)
    ar = a_ref[0, :, 0:n]
    ai = a_ref[0, :, n:]
    if batch != 8:
        ar = jnp.broadcast_to(ar[0:1], (batch, n))
        ai = jnp.broadcast_to(ai[0:1], (batch, n))

    def body(i, carry):
        hr, hi = carry
        for s in range(S5_UNROLL):
            step = i * S5_UNROLL + s
            step = jnp.where(d == 0, step, S5_TIME_CHUNK - 1 - step)
            rows = pl.ds(pl.multiple_of(step * batch, batch), batch)
            nr = ar * hr - ai * hi + hs_ref[rows, 0:n]
            ni = ar * hi + ai * hr + hs_ref[rows, n:]
            hs_ref[rows, 0:n] = nr
            hs_ref[rows, n:] = ni
            hr, hi = nr, ni
        return hr, hi

    hr, hi = lax.fori_loop(0, S5_TIME_CHUNK // S5_UNROLL, body,
                           (carry_ref[:, 0:n], carry_ref[:, n:]))
    carry_ref[:, 0:n] = hr
    carry_ref[:, n:] = hi
    for r in (0, half):
        y = jnp.dot(hs_ref[r:r + half, :].astype(BF16), cd_ref[0], preferred_element_type=F32)
        for s in range(n_slab):
            o_ref[0, s, r:r + half, :] = y[:, s * 128:(s + 1) * 128]


def _s5_scan(us_flat, a, bd, cd, batch, n_ctx):
    n_slab, rows_total, _ = us_flat.shape
    ta = rows_total // batch
    tc = S5_TIME_CHUNK
    n_chunks = ta // tc
    ctx_chunks = n_ctx // tc
    blk = tc * batch

    def chunk_of(d, j):
        bwd = jnp.where(j < ctx_chunks, ctx_chunks - 1 - j, n_chunks - 1 - (j - ctx_chunks))
        return jnp.where(d == 0, j, bwd)

    return pl.pallas_call(
        functools.partial(_s5_scan_kernel, batch=batch),
        grid=(2, n_chunks),
        in_specs=[pl.BlockSpec((n_slab, blk, 128), lambda d, j: (0, chunk_of(d, j), 0)),
                  pl.BlockSpec((1, 8, 2 * S5_NSTATE), lambda d, j: (d, 0, 0)),
                  pl.BlockSpec((1, S5_W, 2 * S5_NSTATE), lambda d, j: (d, 0, 0)),
                  pl.BlockSpec((1, 2 * S5_NSTATE, S5_W), lambda d, j: (d, 0, 0))],
        out_specs=pl.BlockSpec((1, n_slab, blk, 128), lambda d, j: (d, 0, chunk_of(d, j), 0)),
        out_shape=jax.ShapeDtypeStruct((2, n_slab, rows_total, 128), F32),
        scratch_shapes=[pltpu.VMEM((blk, 2 * S5_NSTATE), F32),
                        pltpu.VMEM((batch, 2 * S5_NSTATE), F32)],
        compiler_params=_cparams("arbitrary", "arbitrary"),
        name="s5_scan",
    )(us_flat, a, bd, cd)


def _merge_ln_kernel(gs_ref, y5_ref, us_ref, h_ref, *rest, batch):
    g = BATCH_GROUP
    gate_refs = rest[:g]
    dd_ref, gw_ref, gb_ref, w_ref, g_ref, beta_ref, o_ref, wbf_ref = rest[g:]

    @pl.when(_first_step())
    def _():
        wbf_ref[...] = w_ref[...].astype(BF16)

    def s5_input(i):
        rows = _time_major_rows(pl.program_id(1) * g + i, batch)
        y5 = jnp.concatenate([y5_ref[0, s, rows, :] + y5_ref[1, s, rows, :]
                              for s in range(S5_W // 128)], axis=1)
        us = jnp.concatenate([us_ref[s, rows, :] for s in range(S5_W // 128)], axis=1)
        return y5 + dd_ref[...] * us

    ge = jax.nn.gelu(jnp.concatenate([s5_input(i) for i in range(g)], axis=0))
    s5 = ge * jax.nn.sigmoid(
        jnp.dot(ge.astype(BF16), gw_ref[...], preferred_element_type=F32) + gb_ref[...])
    gs = jnp.concatenate([gs_ref[i] for i in range(g)], axis=0)
    y = jnp.dot(gs, wbf_ref[0:SSD_W, :], preferred_element_type=F32)
    y = y + jnp.dot(s5.astype(BF16), wbf_ref[SSD_W:, :], preferred_element_type=F32)
    rows = o_ref.shape[1]
    for i in range(g):
        o_ref[i] = _layer_norm(ALPHA * h_ref[i] + gate_refs[i][0] * y[i * rows:(i + 1) * rows],
                               g_ref[...], beta_ref[...])


def _merge_ln(g_ssd, y5, us_t, h, mods, layer, s5_d, glu_w, glu_b, w_out, ln_g, ln_b, ctx_tiles):
    batch, ta, _ = h.shape
    nt = ta // ROW_TILE - ctx_tiles
    n_slab = S5_W // 128
    tm_rows = ROW_TILE * batch
    g = BATCH_GROUP
    gate_specs = [
        pl.BlockSpec((1, 1, D_MODEL),
                     lambda t, bg, i=i: ((layer * MOD_ROWS + bg * g + i) * 6 + 2, 0, 0))
        for i in range(g)]
    return pl.pallas_call(
        functools.partial(_merge_ln_kernel, batch=batch),
        grid=(nt, batch // g),
        in_specs=[pl.BlockSpec((g, ROW_TILE, SSD_W), lambda t, b: (b, t, 0)),
                  pl.BlockSpec((2, n_slab, tm_rows, 128), lambda t, b: (0, 0, t + ctx_tiles, 0)),
                  pl.BlockSpec((n_slab, tm_rows, 128), lambda t, b: (0, t + ctx_tiles, 0)),
                  pl.BlockSpec((g, ROW_TILE, D_MODEL), lambda t, b: (b, t + ctx_tiles, 0))]
        + gate_specs
        + [_full((1, S5_W)), _full((S5_W, S5_W)), _full((1, S5_W)),
           _resident((D_MODEL, D_MODEL)), _full((1, D_MODEL)), _full((1, D_MODEL))],
        out_specs=pl.BlockSpec((g, ROW_TILE, D_MODEL), lambda t, b: (b, t, 0)),
        out_shape=jax.ShapeDtypeStruct((batch, nt * ROW_TILE, D_MODEL), F32),
        scratch_shapes=[pltpu.VMEM((D_MODEL, D_MODEL), BF16)],
        compiler_params=_cparams("arbitrary", "arbitrary"),
        name="merge_outproj_ln",
    )(g_ssd, y5, us_t, h, *([mods] * g), s5_d.reshape(1, S5_W), glu_w.astype(BF16),
      glu_b.reshape(1, S5_W), w_out, ln_g.reshape(1, -1), ln_b.reshape(1, -1))


def _attn_layer(h_ctx, h_lat, mods, layer, i, p, keep_ctx):
    n_ctx = h_ctx.shape[1]
    ctx_tiles = n_ctx // ROW_TILE
    lam_init = 0.8 - 0.6 * math.exp(-0.3 * layer)
    cos, sin = _rope_tables(h_lat.shape[1], n_ctx)
    q, k, v, f = _inproj_attn(h_ctx, h_lat, mods, layer, p['attn_w_in'][i],
                              cos, sin, ctx_tiles)
    o_ctx, o_lat = _attention(q, k, v, p['attn_lambda'][i], p['attn_subln_g'][i], lam_init,
                              ctx_tiles)
    fm_ctx, fm_lat = _fourier(f, p['fourier_w'][i], p['fourier_b'][i], n_ctx)
    h1 = _outproj_ln(o_ctx, o_lat, fm_ctx, fm_lat, h_ctx, h_lat, mods, layer,
                     p['attn_w_out'][i],
                     p['ln_g'][layer, 0], p['ln_b'][layer, 0], ctx_tiles)
    return _ffn(h1, mods, layer, p['ffn_w_up_bf16'], p['ffn_b_up'][layer],
                p['ffn_conv_w'][layer], p['ffn_conv_b'][layer],
                p['ffn_w_down_bf16'], p['ffn_b_down'][layer],
                p['ln_g'][layer, 1], p['ln_b'][layer, 1], ctx_tiles, 0 if keep_ctx else ctx_tiles,
                group=BATCH_GROUP)


def _ssm_layer(h, mods, layer, i, n_ctx, p, keep_ctx):
    assert not keep_ctx, "an SSM layer that must also emit context rows is not implemented"
    ctx_tiles = n_ctx // ROW_TILE
    batch = h.shape[0]
    z, xbc, dt, us_t = _inproj_ssm(h, mods, layer, p['ssm_w_in'][i], ctx_tiles)
    g_ssd = _ssd(xbc, dt, z, p['ssd_conv_w'][i], p['ssd_conv_b'][i], p['ssd_a_log'][i],
                 p['ssd_dt_bias'][i], p['ssd_d'][i], p['ssd_norm_g'][i], n_ctx)
    a, bd, cd = _s5_discretize(p['s5_lambda_re'][i], p['s5_lambda_im'][i], p['s5_log_dt'][i],
                               p['s5_b_re'][i], p['s5_b_im'][i], p['s5_c_re'][i], p['s5_c_im'][i])
    y5 = _s5_scan(us_t, a, bd, cd, batch, n_ctx)
    h1 = _merge_ln(g_ssd, y5, us_t, h, mods, layer, p['s5_d'][i], p['s5_glu_w'][i],
                   p['s5_glu_b'][i], p['ssm_w_out'][i],
                   p['ln_g'][layer, 0], p['ln_b'][layer, 0], ctx_tiles)
    return _ffn(h1, mods, layer, p['ffn_w_up_bf16'], p['ffn_b_up'][layer],
                p['ffn_conv_w'][layer], p['ffn_conv_b'][layer],
                p['ffn_w_down_bf16'], p['ffn_b_down'][layer],
                p['ln_g'][layer, 1], p['ln_b'][layer, 1], 0, 0, rows=FFN_LAT_ROWS)


def kernel(x, c, ctx, c_ctx, ada_w, ada_b, ln_g, ln_b, ffn_w_up, ffn_b_up, ffn_conv_w, ffn_conv_b, ffn_w_down, ffn_b_down, attn_w_in, attn_lambda, attn_subln_g, fourier_w, fourier_b, attn_w_out, ssm_w_in, ssd_conv_w, ssd_conv_b, ssd_a_log, ssd_dt_bias, ssd_d, ssd_norm_g, s5_lambda_re, s5_lambda_im, s5_log_dt, s5_b_re, s5_b_im, s5_c_re, s5_c_im, s5_d, s5_glu_w, s5_glu_b, ssm_w_out):
    p = dict(ln_g=ln_g, ln_b=ln_b, ffn_w_up=ffn_w_up, ffn_b_up=ffn_b_up, ffn_conv_w=ffn_conv_w,
             ffn_conv_b=ffn_conv_b, ffn_w_down=ffn_w_down, ffn_b_down=ffn_b_down,
             attn_w_in=attn_w_in, attn_lambda=attn_lambda, attn_subln_g=attn_subln_g,
             fourier_w=fourier_w, fourier_b=fourier_b, attn_w_out=attn_w_out, ssm_w_in=ssm_w_in,
             ssd_conv_w=ssd_conv_w, ssd_conv_b=ssd_conv_b, ssd_a_log=ssd_a_log,
             ssd_dt_bias=ssd_dt_bias, ssd_d=ssd_d, ssd_norm_g=ssd_norm_g,
             s5_lambda_re=s5_lambda_re, s5_lambda_im=s5_lambda_im, s5_log_dt=s5_log_dt,
             s5_b_re=s5_b_re, s5_b_im=s5_b_im, s5_c_re=s5_c_re, s5_c_im=s5_c_im, s5_d=s5_d,
             s5_glu_w=s5_glu_w, s5_glu_b=s5_glu_b, ssm_w_out=ssm_w_out)
    batch, n_lat, _ = x.shape
    n_ctx = ctx.shape[1]
    assert n_ctx == ROW_TILE and n_lat % ROW_TILE == 0 and batch < MOD_ROWS
    mods = _ada_mods(c, c_ctx, ada_w, ada_b)
    p['ffn_w_up_bf16'] = ffn_w_up.astype(BF16)
    p['ffn_w_down_bf16'] = ffn_w_down.astype(BF16)
    assert DEPTH == 2
    h = _attn_layer(ctx, x, mods, 0, 0, p, keep_ctx=True)
    return _ssm_layer(h, mods, 1, 0, n_ctx, p, keep_ctx=False)
```

```python
import functools
import math

import numpy as np
import jax
import jax.numpy as jnp
from jax import lax
from jax.experimental import pallas as pl
from jax.experimental.pallas import tpu as pltpu

F32 = jnp.float32
BF16 = jnp.bfloat16

D_MODEL = 1024
DEPTH = 2
GRID_W = 64
ROPE_BASE = 10000.0
LN_EPS = 1e-5
ALPHA = (2 * DEPTH) ** 0.25
ATTN_W = 768
ATTN_HEADS = 6
ATTN_HEAD_DIM = 64
F_W = 256
F_GROUPS = 4
F_GROUP_W = 64
ATTN_IN_W = 2 * ATTN_W + ATTN_W + F_W
SSD_W = 768
SSD_HEADS = 12
SSD_HEAD_DIM = 64
SSD_GROUPS = 2
SSD_STATE = 128
SSD_CHUNK = 128
XBC_W = SSD_W + 2 * SSD_GROUPS * SSD_STATE
DT_W = 2 * SSD_HEADS
DT_PAD = 128
S5_W = 256
S5_GROUPS = 16
S5_GROUP_W = 16
S5_STATE = 64
S5_NSTATE = S5_GROUPS * S5_STATE
SSM_IN_W = SSD_W + XBC_W + DT_W + S5_W
SSM_IN_PAD = SSD_W + XBC_W + S5_W + DT_PAD
D_FF = 2816
FF_TILE = 256

ROW_TILE = 256
ATTN_Q_SUB = 128
ATTN_ITEM_TILES = 2
FFN_LAT_ROWS = 512
BATCH_GROUP = 2
FOURIER_TILE = 1024
HALO = 8
MOD_ROWS = 16
VMEM_LIMIT_BYTES = 56 * 1024 * 1024


def _cparams(*sem):
    return pltpu.CompilerParams(dimension_semantics=sem, vmem_limit_bytes=VMEM_LIMIT_BYTES)


def _silu(v):
    return v * jax.nn.sigmoid(v)


def _layer_norm(v, g, b):
    mu = jnp.mean(v, axis=-1, keepdims=True)
    d = v - mu
    var = jnp.mean(d * d, axis=-1, keepdims=True)
    return d * lax.rsqrt(var + LN_EPS) * g + b


def _full(shape):
    nd = len(shape)
    return pl.BlockSpec(shape, lambda *_: (0,) * nd)


def _resident(shape):
    nd = len(shape)
    return pl.BlockSpec(shape, lambda *_: (0,) * nd, pipeline_mode=pl.Buffered(1))


def _first_step():
    return (pl.program_id(0) == 0) & (pl.program_id(1) == 0)


def _mod_spec(layer, j, batch, ctx_tiles):
    def idx(b, t):
        row = jnp.where(t < ctx_tiles, batch, b)
        return ((layer * MOD_ROWS + row) * 6 + j, 0, 0)
    return pl.BlockSpec((1, 1, D_MODEL), idx)


def _mod_specs_grouped(layer, j, batch, ctx_tiles):
    def spec(i):
        def idx(bg, t):
            row = jnp.where(t < ctx_tiles, batch, bg * BATCH_GROUP + i)
            return ((layer * MOD_ROWS + row) * 6 + j, 0, 0)
        return pl.BlockSpec((1, 1, D_MODEL), idx)
    return [spec(i) for i in range(BATCH_GROUP)]


def _ada_kernel(c_ref, w_ref, b_ref, o_ref):
    s = _silu(c_ref[...])
    w = w_ref[0]
    s_hi = s.astype(BF16)
    s_lo = (s - s_hi.astype(F32)).astype(BF16)
    w_hi = w.astype(BF16)
    w_lo = (w - w_hi.astype(F32)).astype(BF16)
    dot = functools.partial(jnp.dot, preferred_element_type=F32)
    o_ref[0] = dot(s_hi, w_hi) + (dot(s_lo, w_hi) + dot(s_hi, w_lo)) + b_ref[0]


def _ada_mods(c, c_ctx, ada_w, ada_b):
    batch = c.shape[0]
    nl = ada_w.shape[0]
    c_all = jnp.concatenate(
        [c, c_ctx[None], jnp.zeros((MOD_ROWS - batch - 1, D_MODEL), F32)], axis=0)
    out = pl.pallas_call(
        _ada_kernel,
        grid=(nl, 6),
        in_specs=[_full((MOD_ROWS, D_MODEL)),
                  pl.BlockSpec((1, D_MODEL, D_MODEL), lambda l, j: (l, 0, j)),
                  pl.BlockSpec((1, 1, D_MODEL), lambda l, j: (l, 0, j))],
        out_specs=pl.BlockSpec((1, MOD_ROWS, D_MODEL), lambda l, j: (l, 0, j)),
        out_shape=jax.ShapeDtypeStruct((nl, MOD_ROWS, 6 * D_MODEL), F32),
        compiler_params=_cparams("arbitrary", "arbitrary"),
        name="ada_mods",
    )(c_all, ada_w, ada_b.reshape(nl, 1, 6 * D_MODEL))
    return out.reshape(nl * MOD_ROWS * 6, 1, D_MODEL)


def _modulated_rows(h_of, shift_refs, scale_refs):
    return jnp.concatenate(
        [(h_of(i) * (1.0 + scale_refs[i][0]) + shift_refs[i][0]).astype(BF16)
         for i in range(BATCH_GROUP)], axis=0)


def _store_rows(ref, val):
    rows = ref.shape[1]
    for i in range(BATCH_GROUP):
        ref[i] = val[i * rows:(i + 1) * rows].astype(ref.dtype)


def _inproj_attn_kernel(hc_ref, hl_ref, *rest, ctx_tiles):
    g = BATCH_GROUP
    shift_refs, scale_refs = rest[:g], rest[g:2 * g]
    w_ref, cos_ref, sin_ref, q_ref, k_ref, v_ref, f_ref, wbf_ref = rest[2 * g:]

    @pl.when(_first_step())
    def _():
        wbf_ref[...] = w_ref[...].astype(BF16)

    is_ctx = pl.program_id(1) < ctx_tiles
    u = _modulated_rows(lambda i: jnp.where(is_ctx, hc_ref[i], hl_ref[i]), shift_refs, scale_refs)
    p = jnp.dot(u, wbf_ref[...], preferred_element_type=F32)
    cos = jnp.concatenate([cos_ref[...]] * g, axis=0)
    sin = jnp.concatenate([sin_ref[...]] * g, axis=0)
    lane = lax.broadcasted_iota(jnp.int32, cos.shape, 1)
    first_half = (lane % ATTN_HEAD_DIM) < (ATTN_HEAD_DIM // 2)

    def rope(blk):
        partner = jnp.where(first_half, pltpu.roll(blk, 128 - 32, 1), pltpu.roll(blk, 32, 1))
        return blk * cos + partner * sin

    qk_scale = ATTN_HEAD_DIM ** -0.5 * math.log2(math.e)
    rows = q_ref.shape[1]
    for i in range(ATTN_HEADS):
        lo, hi = i * 128, (i + 1) * 128
        qh = (rope(p[:, lo:hi]) * qk_scale).astype(BF16)
        kh = rope(p[:, ATTN_W + lo:ATTN_W + hi]).astype(BF16)
        for b in range(g):
            q_ref[b, :, lo:hi] = qh[b * rows:(b + 1) * rows]
            k_ref[b, :, lo:hi] = kh[b * rows:(b + 1) * rows]
    _store_rows(v_ref, p[:, 2 * ATTN_W:3 * ATTN_W])
    _store_rows(f_ref, p[:, 3 * ATTN_W:])


def _rope_tables(n_lat, n_ctx):
    rows = n_lat // GRID_W
    row = jnp.repeat(jnp.arange(rows, dtype=F32), GRID_W)
    col = jnp.tile(jnp.arange(GRID_W, dtype=F32), rows)
    n_freq = ATTN_HEAD_DIM // 4
    inv_freq = ROPE_BASE ** (-jnp.arange(n_freq, dtype=F32) / n_freq)
    ang = jnp.concatenate([row[:, None] * inv_freq, col[:, None] * inv_freq], axis=-1)
    cos, sin = jnp.cos(ang), jnp.sin(ang)
    cos128 = jnp.tile(cos, (1, 4))
    sin128 = jnp.tile(jnp.concatenate([-sin, sin], axis=-1), (1, 2))
    cos_all = jnp.concatenate([jnp.ones((n_ctx, 128), F32), cos128], axis=0)
    sin_all = jnp.concatenate([jnp.zeros((n_ctx, 128), F32), sin128], axis=0)
    return cos_all, sin_all


def _split_specs(width, ctx_tiles):
    g = BATCH_GROUP
    return [pl.BlockSpec((g, ROW_TILE, width), lambda b, t: (b, jnp.minimum(t, ctx_tiles - 1), 0)),
            pl.BlockSpec((g, ROW_TILE, width), lambda b, t: (b, jnp.maximum(t - ctx_tiles, 0), 0))]


def _inproj_attn(h_ctx, h_lat, mods, layer, w_in, cos, sin, ctx_tiles):
    batch = h_lat.shape[0]
    assert batch % BATCH_GROUP == 0
    ta = h_ctx.shape[1] + h_lat.shape[1]
    nt = ta // ROW_TILE
    row = lambda w: pl.BlockSpec((BATCH_GROUP, ROW_TILE, w), lambda b, t: (b, t, 0))
    tab = pl.BlockSpec((ROW_TILE, 128), lambda b, t: (t, 0))
    n_mod = 2 * BATCH_GROUP
    return pl.pallas_call(
        functools.partial(_inproj_attn_kernel, ctx_tiles=ctx_tiles),
        grid=(batch // BATCH_GROUP, nt),
        in_specs=_split_specs(D_MODEL, ctx_tiles)
        + _mod_specs_grouped(layer, 0, batch, ctx_tiles)
        + _mod_specs_grouped(layer, 1, batch, ctx_tiles)
        + [_resident((D_MODEL, ATTN_IN_W)), tab, tab],
        out_specs=[row(ATTN_W), row(ATTN_W), row(ATTN_W), row(F_W)],
        out_shape=[jax.ShapeDtypeStruct((batch, ta, ATTN_W), BF16)] * 3
        + [jax.ShapeDtypeStruct((batch, ta, F_W), BF16)],
        scratch_shapes=[pltpu.VMEM((D_MODEL, ATTN_IN_W), BF16)],
        compiler_params=_cparams("arbitrary", "arbitrary"),
        name="inproj_attn",
    )(h_ctx, h_lat, *([mods] * n_mod), w_in, cos, sin)


def _diff_lambda(lam_ref, lam_init):
    lamv = lam_ref[...]
    l1 = jnp.sum(lamv[0:1] * lamv[1:2], axis=-1, keepdims=True)
    l2 = jnp.sum(lamv[2:3] * lamv[3:4], axis=-1, keepdims=True)
    return jnp.exp(l1) - jnp.exp(l2) + lam_init


def _stack_maps(q):
    lane = lax.broadcasted_iota(jnp.int32, q.shape, 1)
    zero = jnp.zeros_like(q)
    return jnp.concatenate([jnp.where(lane < ATTN_HEAD_DIM, q, zero),
                            jnp.where(lane >= ATTN_HEAD_DIM, q, zero)], axis=0)


def _scores(q2, k):
    return lax.dot_general(q2, k, (((1,), (1,)), ((), ())), preferred_element_type=F32)


def _diff_softmax_pv(load_s0, m0, load_s1, m1, lam, v, gain):
    e0 = jnp.exp2(load_s0() - m0)
    e1 = jnp.exp2(load_s1() - m1)
    l0 = jnp.sum(e0, axis=-1, keepdims=True)
    l1 = jnp.sum(e1, axis=-1, keepdims=True)
    w = e0 - e1 * (lam * l0 / l1)
    o = jnp.dot(w.astype(BF16), v, preferred_element_type=F32) * (1.0 / l0)
    o = o * lax.rsqrt(jnp.mean(o * o, axis=-1, keepdims=True) + LN_EPS)
    return (o * gain).astype(BF16)


def _attn_ctx_kernel(lam_ref, g_ref, q_ref, k_ref, v_ref, o_ref, *, lam_init):
    lam = _diff_lambda(lam_ref, lam_init)
    gain = g_ref[...] * (1.0 - lam_init)
    tq = q_ref.shape[1]
    for hd in range(ATTN_HEADS):
        cols = slice(hd * 128, (hd + 1) * 128)
        s = _scores(_stack_maps(q_ref[0, :, cols]), k_ref[0, :, cols])
        m = jnp.max(s, axis=-1, keepdims=True)

        def half(lo, s=s):
            return lambda: s[lo:lo + tq]

        o_ref[0, :, cols] = _diff_softmax_pv(half(0), m[:tq], half(tq), m[tq:], lam,
                                             v_ref[0, :, cols], gain)


def _attn_lat_kernel(lam_ref, g_ref, k_ref, v_ref, *rest, lam_init):
    q_refs = rest[:ATTN_ITEM_TILES]
    o_ref, sa_ref, ma_ref, sb_ref, mb_ref = rest[ATTN_ITEM_TILES:]
    t = pl.program_id(0)
    tq = ATTN_ITEM_TILES * ROW_TILE
    sub = ATTN_Q_SUB

    @pl.when(t == 0)
    def _():
        sb_ref[...] = jnp.zeros_like(sb_ref)
        mb_ref[...] = jnp.zeros_like(mb_ref)

    def step(s_new, m_new, s_old, m_old):
        lam = _diff_lambda(lam_ref, lam_init)
        gain = g_ref[...] * (1.0 - lam_init)
        q = jnp.concatenate([q_ref[0] for q_ref in q_refs], axis=0)
        s = _scores(_stack_maps(q), k_ref[0])
        s_new[...] = s
        m_new[...] = jnp.max(s, axis=-1, keepdims=True)
        v = v_ref[0]
        for i in range(tq // sub):
            r0 = slice(i * sub, (i + 1) * sub)
            r1 = slice(tq + i * sub, tq + (i + 1) * sub)
            o_ref[0, r0, :] = _diff_softmax_pv(
                functools.partial(s_old.__getitem__, (r0, slice(None))), m_old[r0, :],
                functools.partial(s_old.__getitem__, (r1, slice(None))), m_old[r1, :],
                lam, v, gain)

    @pl.when(t % 2 == 0)
    def _():
        step(sa_ref, ma_ref, sb_ref, mb_ref)

    @pl.when(t % 2 == 1)
    def _():
        step(sb_ref, mb_ref, sa_ref, ma_ref)


def _attention(q, k, v, lam_vec, subln_g, lam_init, ctx_tiles):
    batch, ta, _ = q.shape
    n_ctx = ctx_tiles * ROW_TILE
    nt = ta // ROW_TILE - ctx_tiles
    g = subln_g.reshape(1, 128)
    small = [_full((4, ATTN_HEAD_DIM)), _full((1, 128))]
    cspec = pl.BlockSpec((1, n_ctx, ATTN_W), lambda b: (b, 0, 0))
    o_ctx = pl.pallas_call(
        functools.partial(_attn_ctx_kernel, lam_init=lam_init),
        grid=(batch,),
        in_specs=small + [cspec, cspec, cspec],
        out_specs=cspec,
        out_shape=jax.ShapeDtypeStruct((batch, n_ctx, ATTN_W), BF16),
        compiler_params=_cparams("arbitrary"),
        name="diff_attention_ctx",
    )(lam_vec, g, q, k, v)

    per = ATTN_ITEM_TILES
    assert nt % per == 0
    ni = nt // per
    n_items = batch * ATTN_HEADS * ni

    def item(j):
        return j // (ATTN_HEADS * ni), (j // ni) % ATTN_HEADS, j % ni

    def score_item(j):
        return item(jnp.minimum(j, n_items - 1))

    def finish_item(j):
        return item(jnp.maximum(j - 1, 0))

    def q_spec(part):
        def idx(j):
            b, h, t = score_item(j)
            return b, ctx_tiles + t * per + part, h
        return pl.BlockSpec((1, ROW_TILE, 128), idx)

    def k_idx(j):
        b, h, _ = score_item(j)
        return b, 0, h

    def v_idx(j):
        b, h, _ = finish_item(j)
        return b, 0, h

    def o_idx(j):
        b, h, t = finish_item(j)
        return b, t, h

    o_lat = pl.pallas_call(
        functools.partial(_attn_lat_kernel, lam_init=lam_init),
        grid=(n_items + 1,),
        in_specs=small + [pl.BlockSpec((1, ta, 128), k_idx), pl.BlockSpec((1, ta, 128), v_idx)]
        + [q_spec(part) for part in range(per)],
        out_specs=pl.BlockSpec((1, per * ROW_TILE, 128), o_idx),
        out_shape=jax.ShapeDtypeStruct((batch, nt * ROW_TILE, ATTN_W), BF16),
        scratch_shapes=[pltpu.VMEM((2 * per * ROW_TILE, ta), F32),
                        pltpu.VMEM((2 * per * ROW_TILE, 1), F32)] * 2,
        compiler_params=_cparams("arbitrary"),
        name="diff_attention",
    )(lam_vec, g, k, v, *([q] * per))
    return o_ctx, o_lat


def _dft_tables(n):
    k = np.arange(n, dtype=np.int64)
    ang = 2.0 * np.pi * ((k[:, None] * k[None, :]) % n).astype(np.float64) / n
    return np.cos(ang), np.sin(ang)


def _fourier_kernel(f_ref, cs_ref, dl_ref, dc_ref, w_ref, b_ref, oc_ref, ol_ref, ac_ref, al_ref,
                    *, n_ctx, n_lat):
    t = pl.program_id(0)
    b = pl.program_id(1)

    def stage1(rows0, n):
        a = jnp.dot(f_ref[0, rows0:rows0 + n, :], cs_ref[...], preferred_element_type=F32)
        return a[:, :F_W].astype(BF16), a[:, F_W:].astype(BF16)

    def stage2(dft, a, n):
        z = jnp.dot(dft, a, preferred_element_type=F32)
        z = z * (1.0 / math.sqrt(n * F_GROUP_W))
        o = jnp.dot(z.astype(BF16), w_ref[...], preferred_element_type=F32) + b_ref[...]
        return o.astype(BF16)

    @pl.when(t == 0)
    def _():
        ac_ref[0:n_ctx, :], ac_ref[n_ctx:, :] = stage1(0, n_ctx)
        oc_ref[0] = stage2(dc_ref[...], ac_ref[...], n_ctx)

    @pl.when(t == 1)
    def _():
        al_ref[b, 0:n_lat, :], al_ref[b, n_lat:, :] = stage1(n_ctx, n_lat)

    @pl.when(t >= 1)
    def _():
        ol_ref[0] = stage2(dl_ref[...], al_ref[b], n_lat)


def _fourier(f, fourier_w, fourier_b, n_ctx):
    batch, ta, _ = f.shape
    n_lat = ta - n_ctx
    ft = min(FOURIER_TILE, n_lat)
    assert n_lat % ft == 0
    cc, sc = _dft_tables(F_GROUP_W)
    eye = np.eye(F_GROUPS)
    cs = jnp.asarray(np.concatenate([np.kron(eye, cc), np.kron(eye, sc)], axis=1), BF16)
    cl, sl = _dft_tables(n_lat)
    dft_lat = jnp.asarray(np.concatenate([cl, -sl], axis=1), BF16)
    cx, sx = _dft_tables(n_ctx)
    dft_ctx = jnp.asarray(np.concatenate([cx, -sx], axis=1), BF16)
    w_blk = jnp.einsum('gce,gh->gche', fourier_w, jnp.eye(F_GROUPS, dtype=F32))
    w_blk = w_blk.reshape(F_W, F_W).astype(BF16)
    return pl.pallas_call(
        functools.partial(_fourier_kernel, n_ctx=n_ctx, n_lat=n_lat),
        grid=(1 + n_lat // ft, batch),
        in_specs=[
                  pl.BlockSpec((1, ta, F_W),
                               lambda t, b: (jnp.where(t <= 1, b, batch - 1), 0, 0)),
                  _full((F_W, 2 * F_W)),
                  pl.BlockSpec((ft, 2 * n_lat), lambda t, b: (jnp.maximum(t - 1, 0), 0)),
                  _full((n_ctx, 2 * n_ctx)),
                  _full((F_W, F_W)), _full((1, F_W))],
        out_specs=[pl.BlockSpec((1, n_ctx, F_W),
                                lambda t, b: (jnp.where(t == 0, b, batch - 1), 0, 0)),
                   pl.BlockSpec((1, ft, F_W),
                                lambda t, b: (jnp.where(t == 0, 0, b), jnp.maximum(t - 1, 0), 0))],
        out_shape=[jax.ShapeDtypeStruct((batch, n_ctx, F_W), BF16),
                   jax.ShapeDtypeStruct((batch, n_lat, F_W), BF16)],
        scratch_shapes=[pltpu.VMEM((2 * n_ctx, F_W), BF16),
                        pltpu.VMEM((batch, 2 * n_lat, F_W), BF16)],
        compiler_params=_cparams("arbitrary", "arbitrary"),
        name="fourier_mix",
    )(f, cs, dft_lat, dft_ctx, w_blk, fourier_b.reshape(1, F_W))


def _outproj_ln_kernel(ac_ref, al_ref, bc_ref, bl_ref, hc_ref, hl_ref, *rest, ctx_tiles):
    g = BATCH_GROUP
    gate_refs = rest[:g]
    w_ref, g_ref, beta_ref, o_ref, wbf_ref = rest[g:]

    @pl.when(_first_step())
    def _():
        wbf_ref[...] = w_ref[...].astype(BF16)

    is_ctx = pl.program_id(1) < ctx_tiles
    a = jnp.concatenate([jnp.where(is_ctx, ac_ref[i], al_ref[i]) for i in range(g)], axis=0)
    b2 = jnp.concatenate([jnp.where(is_ctx, bc_ref[i], bl_ref[i]) for i in range(g)], axis=0)
    wa = a.shape[1]
    y = jnp.dot(a, wbf_ref[0:wa, :], preferred_element_type=F32)
    y = y + jnp.dot(b2, wbf_ref[wa:, :], preferred_element_type=F32)
    rows = o_ref.shape[1]
    for i in range(g):
        h = jnp.where(is_ctx, hc_ref[i], hl_ref[i])
        o_ref[i] = _layer_norm(ALPHA * h + gate_refs[i][0] * y[i * rows:(i + 1) * rows],
                               g_ref[...], beta_ref[...])


def _outproj_ln(a_ctx, a_lat, b_ctx, b_lat, h_ctx, h_lat, mods, layer, w_out, ln_g, ln_b,
                ctx_tiles):
    batch = h_lat.shape[0]
    ta = h_ctx.shape[1] + h_lat.shape[1]
    nt = ta // ROW_TILE
    row = lambda w: pl.BlockSpec((BATCH_GROUP, ROW_TILE, w), lambda b, t: (b, t, 0))
    return pl.pallas_call(
        functools.partial(_outproj_ln_kernel, ctx_tiles=ctx_tiles),
        grid=(batch // BATCH_GROUP, nt),
        in_specs=_split_specs(a_lat.shape[2], ctx_tiles) + _split_specs(b_lat.shape[2], ctx_tiles)
        + _split_specs(D_MODEL, ctx_tiles)
        + _mod_specs_grouped(layer, 2, batch, ctx_tiles)
        + [_resident((D_MODEL, D_MODEL)), _full((1, D_MODEL)), _full((1, D_MODEL))],
        out_specs=row(D_MODEL),
        out_shape=jax.ShapeDtypeStruct((batch, ta, D_MODEL), F32),
        scratch_shapes=[pltpu.VMEM((D_MODEL, D_MODEL), BF16)],
        compiler_params=_cparams("arbitrary", "arbitrary"),
        name="outproj_ln",
    )(a_ctx, a_lat, b_ctx, b_lat, h_ctx, h_lat, *([mods] * BATCH_GROUP), w_out,
      ln_g.reshape(1, D_MODEL),
      ln_b.reshape(1, D_MODEL))


def _ffn_kernel(h_ref, hp_ref, hn_ref, *rest, group, ctx_tiles, tile_off, nt_seq):
    shift_refs, scale_refs, gate_refs = rest[:group], rest[group:2 * group], rest[2 * group:3 * group]
    (wup_ref, bup_ref, cw_ref, cb_ref, wdn_ref, bdn_ref, g_ref, beta_ref,
     o_ref, uext_ref, ubf_ref, act_ref) = rest[3 * group:]
    t = pl.program_id(1) + tile_off
    seg_first = (t == 0) | (t == ctx_tiles)
    seg_last = (t == nt_seq - 1) | (t == ctx_tiles - 1)
    tm = h_ref.shape[1]
    ext = tm + 2 * HALO
    for b in range(group):
        sc = 1.0 + scale_refs[b][0]
        sh = shift_refs[b][0]
        r0 = b * ext
        uext_ref[r0:r0 + HALO, :] = hp_ref[b] * sc + sh
        uext_ref[r0 + HALO:r0 + HALO + tm, :] = h_ref[b] * sc + sh
        uext_ref[r0 + HALO + tm:r0 + ext, :] = hn_ref[b] * sc + sh
    ubf_ref[...] = uext_ref[...].astype(BF16)
    row8 = lax.broadcasted_iota(jnp.int32, (8, FF_TILE), 0)

    def hidden(col0):
        cols = slice(col0, col0 + FF_TILE)
        zr_all = jnp.dot(ubf_ref[...], wup_ref[:, cols], preferred_element_type=F32)
        bup = bup_ref[:, cols]
        cw = cw_ref[:, cols]
        bias = cb_ref[:, cols] + (cw[0:1] + cw[1:2] + cw[2:3]) * bup
        outs = []
        for b in range(group):
            zr = zr_all[b * ext:(b + 1) * ext]
            z0 = zr[HALO:HALO + tm]
            prev = jnp.where(seg_first, -bup, zr[HALO - 1:HALO])
            nxt = jnp.where(seg_last, -bup, zr[HALO + tm:HALO + tm + 1])
            down = pltpu.roll(z0, 1, 0)
            up = pltpu.roll(z0, tm - 1, 0)
            zm1 = jnp.concatenate([jnp.where(row8 == 0, prev, down[0:8]), down[8:]], axis=0)
            zp1 = jnp.concatenate([up[:tm - 8], jnp.where(row8 == 7, nxt, up[tm - 8:])], axis=0)
            outs.append(cw[0:1] * zm1 + cw[1:2] * z0 + cw[2:3] * zp1 + bias)
        return jnp.concatenate(outs, axis=0)

    for j in range(D_FF // FF_TILE):
        val = hidden(j * FF_TILE)
        gat = hidden(D_FF + j * FF_TILE)
        act_ref[:, j * FF_TILE:(j + 1) * FF_TILE] = (val * _silu(gat)).astype(BF16)
    f = jnp.dot(act_ref[...], wdn_ref[...], preferred_element_type=F32) + bdn_ref[...]
    for b in range(group):
        o_ref[b] = _layer_norm(ALPHA * h_ref[b] + gate_refs[b][0] * f[b * tm:(b + 1) * tm],
                               g_ref[...], beta_ref[...])


def _ffn(h, mods, layer, w_up, b_up, conv_w, conv_b, w_down, b_down, ln_g, ln_b,
         ctx_tiles, tile_off, rows=ROW_TILE, group=1):
    batch, ta, _ = h.shape
    assert batch % group == 0
    nt_seq = ta // rows
    nt = nt_seq - tile_off
    hb = rows // HALO
    n_hblk = ta // HALO

    def mspecs(j):
        def spec(i):
            def idx(bg, t):
                r = jnp.where(t + tile_off < ctx_tiles, batch, bg * group + i)
                return ((layer * MOD_ROWS + r) * 6 + j, 0, 0)
            return pl.BlockSpec((1, 1, D_MODEL), idx)
        return [spec(i) for i in range(group)]

    resident = lambda shape: pl.BlockSpec((None,) + shape, lambda *_: (layer, 0, 0),
                                          pipeline_mode=pl.Buffered(1))
    ext = rows + 2 * HALO
    return pl.pallas_call(
        functools.partial(_ffn_kernel, group=group, ctx_tiles=ctx_tiles, tile_off=tile_off,
                          nt_seq=nt_seq),
        grid=(batch // group, nt),
        in_specs=[pl.BlockSpec((group, rows, D_MODEL), lambda b, t: (b, t + tile_off, 0)),
                  pl.BlockSpec((group, HALO, D_MODEL),
                               lambda b, t: (b, jnp.maximum((t + tile_off) * hb - 1, 0), 0)),
                  pl.BlockSpec((group, HALO, D_MODEL),
                               lambda b, t: (b, jnp.minimum((t + tile_off + 1) * hb, n_hblk - 1), 0))]
        + mspecs(3) + mspecs(4) + mspecs(5)
        + [resident((D_MODEL, 2 * D_FF)), _full((1, 2 * D_FF)),
           _full((3, 2 * D_FF)), _full((1, 2 * D_FF)),
           resident((D_FF, D_MODEL)), _full((1, D_MODEL)),
           _full((1, D_MODEL)), _full((1, D_MODEL))],
        out_specs=pl.BlockSpec((group, rows, D_MODEL), lambda b, t: (b, t, 0)),
        out_shape=jax.ShapeDtypeStruct((batch, nt * rows, D_MODEL), F32),
        scratch_shapes=[pltpu.VMEM((group * ext, D_MODEL), F32),
                        pltpu.VMEM((group * ext, D_MODEL), BF16),
                        pltpu.VMEM((group * rows, D_FF), BF16)],
        compiler_params=_cparams("parallel", "arbitrary"),
        name="conv_ffn_ln",
    )(h, h, h, *([mods] * (3 * group)), w_up, b_up.reshape(1, -1), conv_w, conv_b.reshape(1, -1),
      w_down, b_down.reshape(1, -1), ln_g.reshape(1, -1), ln_b.reshape(1, -1))


def _time_major_rows(b, batch):
    return pl.ds(b, ROW_TILE, stride=batch)


def _inproj_ssm_kernel(h_ref, *rest, batch):
    g = BATCH_GROUP
    shift_refs, scale_refs = rest[:g], rest[g:2 * g]
    w_ref, z_ref, xbc_ref, dt_ref, us_ref, wbf_ref = rest[2 * g:]

    @pl.when(_first_step())
    def _():
        head = SSD_W + XBC_W
        wbf_ref[0:head, :] = w_ref[0:head, :].astype(BF16)
        wbf_ref[head:head + S5_W, :] = w_ref[head + DT_W:head + DT_W + S5_W, :].astype(BF16)
        pad = jnp.zeros((DT_PAD - DT_W, D_MODEL), F32)
        wbf_ref[head + S5_W:, :] = jnp.concatenate(
            [w_ref[head:head + DT_W, :], pad], axis=0).astype(BF16)

    u = _modulated_rows(lambda i: h_ref[i], shift_refs, scale_refs)
    p = lax.dot_general(u, wbf_ref[...], (((1,), (1,)), ((), ())), preferred_element_type=F32)
    _store_rows(z_ref, p[:, :SSD_W])
    _store_rows(xbc_ref, p[:, SSD_W:SSD_W + XBC_W])
    _store_rows(dt_ref, p[:, SSD_W + XBC_W + S5_W:])
    for i in range(g):
        rows = _time_major_rows(pl.program_id(1) * g + i, batch)
        for j in range(S5_W // 128):
            col0 = SSD_W + XBC_W + j * 128
            us_ref[j, rows, :] = p[i * ROW_TILE:(i + 1) * ROW_TILE, col0:col0 + 128]


def _inproj_ssm(h, mods, layer, w_in, ctx_tiles):
    batch, ta, _ = h.shape
    assert batch % BATCH_GROUP == 0
    nt = ta // ROW_TILE
    row = lambda w: pl.BlockSpec((BATCH_GROUP, ROW_TILE, w), lambda t, b: (b, t, 0))

    def mspecs(j):
        def spec(i):
            def idx(t, bg):
                r = jnp.where(t < ctx_tiles, batch, bg * BATCH_GROUP + i)
                return ((layer * MOD_ROWS + r) * 6 + j, 0, 0)
            return pl.BlockSpec((1, 1, D_MODEL), idx)
        return [spec(i) for i in range(BATCH_GROUP)]

    return pl.pallas_call(
        functools.partial(_inproj_ssm_kernel, batch=batch),
        grid=(nt, batch // BATCH_GROUP),
        in_specs=[row(D_MODEL)] + mspecs(0) + mspecs(1) + [_resident((SSM_IN_W, D_MODEL))],
        out_specs=[row(SSD_W), row(XBC_W), row(DT_PAD),
                   pl.BlockSpec((S5_W // 128, ROW_TILE * batch, 128), lambda t, b: (0, t, 0))],
        out_shape=[jax.ShapeDtypeStruct((batch, ta, SSD_W), F32),
                   jax.ShapeDtypeStruct((batch, ta, XBC_W), F32),
                   jax.ShapeDtypeStruct((batch, ta, DT_PAD), F32),
                   jax.ShapeDtypeStruct((S5_W // 128, ta * batch, 128), F32)],
        scratch_shapes=[pltpu.VMEM((SSM_IN_PAD, D_MODEL), BF16)],
        compiler_params=_cparams("arbitrary", "arbitrary"),
        name="inproj_ssm",
    )(h, *([mods] * (2 * BATCH_GROUP)), jnp.swapaxes(w_in, 0, 1))


def _cumsum_rows(v):
    n = v.shape[0]
    row = lax.broadcasted_iota(jnp.int32, v.shape, 0)
    s = 1
    while s < n:
        v = v + jnp.where(row >= s, pltpu.roll(v, s, 0), 0.0)
        s *= 2
    return v


def _expand_heads(v, e_ref):
    hi = v.astype(BF16)
    lo = (v - hi.astype(F32)).astype(BF16)
    e = e_ref[...]
    return (jnp.dot(hi, e, preferred_element_type=F32)
            + jnp.dot(lo, e, preferred_element_type=F32))


def _ssd_kernel(xbc_ref, xp_ref, xn_ref, dt_ref, z_ref, cw_ref, cb_ref, alog_ref, dtb_ref,
                dsk_ref, ng_ref, ef_ref, eb_ref, o_ref,
                xs_ref, cd_ref, st_ref, dec_ref, y_ref, *, n_tiles, ctx_tiles):
    s = pl.program_id(1)
    q = SSD_CHUNK
    cpt = ROW_TILE // q
    n_chunks = n_tiles * cpt
    ctx_chunks = ctx_tiles * cpt
    gw = SSD_W // SSD_GROUPS
    hpg = SSD_HEADS // SSD_GROUPS

    @pl.when(s < n_tiles)
    def _phase0():
        p = s
        seg_first = (p == 0) | (p == ctx_tiles)
        seg_last = (p == ctx_tiles - 1) | (p == n_tiles - 1)
        xr = xbc_ref[0]
        prev = jnp.where(seg_first, 0.0, xp_ref[0, HALO - 1:HALO, :])
        nxt = jnp.where(seg_last, 0.0, xn_ref[0, 0:1, :])
        row8 = lax.broadcasted_iota(jnp.int32, (8, XBC_W), 0)
        down = pltpu.roll(xr, 1, 0)
        up = pltpu.roll(xr, ROW_TILE - 1, 0)
        xm1 = jnp.concatenate([jnp.where(row8 == 0, prev, down[0:8]), down[8:]], axis=0)
        xp1 = jnp.concatenate([up[:ROW_TILE - 8], jnp.where(row8 == 7, nxt, up[ROW_TILE - 8:])],
                              axis=0)
        cw = cw_ref[...]
        xs_tile = _silu(cw[0:1] * xm1 + cw[1:2] * xr + cw[2:3] * xp1 + cb_ref[...])
        xs_ref[p] = xs_tile

        raw = dt_ref[0] + dtb_ref[...]
        dt_tile = jnp.maximum(raw, 0.0) + jnp.log1p(jnp.exp(-jnp.abs(raw)))
        a_row = -jnp.exp(alog_ref[...])
        for i in range(cpt):
            c = p * cpt + i
            xs = xs_tile[i * q:(i + 1) * q]
            dtv = dt_tile[i * q:(i + 1) * q]
            adt = dtv * a_row
            cum = _cumsum_rows(adt)
            tot = cum[q - 1:q, :]
            lane = lax.broadcasted_iota(jnp.int32, cum.shape, 1)
            cc = jnp.where(lane < SSD_HEADS, cum, tot - cum + adt)
            cd_ref[c, 0] = cc
            cd_ref[c, 1] = dtv
            w_end = jnp.exp(tot - cc) * dtv
            dec16 = jnp.broadcast_to(jnp.exp(tot), (16, DT_PAD))
            x = xs[:, :SSD_W]
            for d, e_ref in enumerate((ef_ref, eb_ref)):
                wx = (_expand_heads(w_end, e_ref) * x).astype(BF16)
                for g in range(SSD_GROUPS):
                    bmt = xs[:, SSD_W + g * SSD_STATE:SSD_W + (g + 1) * SSD_STATE].T.astype(BF16)
                    st_ref[c, d, :, g * gw:(g + 1) * gw] = jnp.dot(
                        bmt, wx[:, g * gw:(g + 1) * gw], preferred_element_type=F32)
                dec_ref[c, d] = _expand_heads(dec16, e_ref)[0:8]

    @pl.when(s == n_tiles)
    def _recurrence():
        fwd = list(range(n_chunks))
        bwd = list(range(ctx_chunks - 1, -1, -1)) + list(range(n_chunks - 1, ctx_chunks - 1, -1))
        for d, order in enumerate((fwd, bwd)):
            for col0 in range(0, SSD_W, 128):
                cols = slice(col0, col0 + 128)
                state = jnp.zeros((SSD_STATE, 128), F32)
                for ci in order:
                    contrib = st_ref[ci, d, :, cols]
                    st_ref[ci, d, :, cols] = state
                    state = state * dec_ref[ci, d, 0:1, cols] + contrib

    @pl.when(s >= n_tiles)
    def _phase1():
        p = s - n_tiles + ctx_tiles
        rowi = lax.broadcasted_iota(jnp.int32, (q, q), 0)
        coli = lax.broadcasted_iota(jnp.int32, (q, q), 1)
        lower = coli <= rowi
        upper = coli >= rowi
        lane = lax.broadcasted_iota(jnp.int32, (q, 128), 1)
        neg = jnp.float32(-jnp.inf)
        for i in range(cpt):
            c = p * cpt + i
            rows = slice(i * q, (i + 1) * q)
            xs = xs_ref[p, rows, :]
            x = xs[:, :SSD_W]
            cc = cd_ref[c, 0]
            dtv = cd_ref[c, 1]
            cct = cc.T
            dtt = dtv.T
            ecc = jnp.exp(cc)
            ef = _expand_heads(ecc, ef_ref)
            eb = _expand_heads(ecc, eb_ref)
            for g in range(SSD_GROUPS):
                bm = xs[:, SSD_W + g * SSD_STATE:SSD_W + (g + 1) * SSD_STATE].astype(BF16)
                cm = xs[:, SSD_W + (SSD_GROUPS + g) * SSD_STATE:
                        SSD_W + (SSD_GROUPS + g + 1) * SSD_STATE].astype(BF16)
                gmat = lax.dot_general(cm, bm, (((1,), (1,)), ((), ())),
                                       preferred_element_type=F32)
                sl = slice(g * gw, (g + 1) * gw)
                yoff = (ef[:, sl] * jnp.dot(cm, st_ref[c, 0, :, sl].astype(BF16),
                                            preferred_element_type=F32)
                        + eb[:, sl] * jnp.dot(cm, st_ref[c, 1, :, sl].astype(BF16),
                                              preferred_element_type=F32))
                for pair in range(hpg // 2):
                    col0 = g * gw + pair * 128
                    xpair = x[:, col0:col0 + 128].astype(BF16)
                    res = []
                    for hh in range(2):
                        hd = g * hpg + pair * 2 + hh
                        hb_ = SSD_HEADS + hd
                        lf = jnp.exp(jnp.where(lower, cc[:, hd:hd + 1] - cct[hd:hd + 1, :], neg))
                        lb = jnp.exp(jnp.where(upper, cc[:, hb_:hb_ + 1] - cct[hb_:hb_ + 1, :],
                                               neg))
                        mt = gmat * (lf * dtt[hd:hd + 1, :] + lb * dtt[hb_:hb_ + 1, :])
                        res.append(jnp.dot(mt.astype(BF16), xpair, preferred_element_type=F32))
                    ydiag = jnp.where(lane < SSD_HEAD_DIM, res[0], res[1])
                    y_ref[rows, col0:col0 + 128] = (
                        ydiag + yoff[:, pair * 128:(pair + 1) * 128]
                        + dsk_ref[:, col0:col0 + 128] * x[:, col0:col0 + 128])
        gated = y_ref[...] * _silu(z_ref[0])
        normed = gated * lax.rsqrt(jnp.mean(gated * gated, axis=-1, keepdims=True) + LN_EPS)
        o_ref[0] = (normed * ng_ref[...]).astype(BF16)


def _ssd(xbc, dt, z, conv_w, conv_b, a_log, dt_bias, d_skip, norm_g, n_ctx):
    batch, ta, _ = xbc.shape
    q = SSD_CHUNK
    n_chunks = ta // q
    n_tiles = ta // ROW_TILE
    ctx_tiles = n_ctx // ROW_TILE
    hb = ROW_TILE // HALO
    n_hblk = ta // HALO
    pad24 = lambda v: jnp.pad(v.reshape(1, DT_W), ((0, 0), (0, DT_PAD - DT_W)))
    heads = np.arange(SSD_HEADS)
    ef = np.zeros((DT_PAD, SSD_W), np.float32)
    eb = np.zeros((DT_PAD, SSD_W), np.float32)
    for hd in heads:
        ef[hd, hd * SSD_HEAD_DIM:(hd + 1) * SSD_HEAD_DIM] = 1.0
        eb[SSD_HEADS + hd, hd * SSD_HEAD_DIM:(hd + 1) * SSD_HEAD_DIM] = 1.0
    dsk = jnp.repeat(d_skip.astype(F32), SSD_HEAD_DIM).reshape(1, SSD_W)
    ph0 = lambda s: s < n_tiles
    tile = lambda w: pl.BlockSpec(
        (1, ROW_TILE, w), lambda b, s: (b, jnp.where(ph0(s), s, n_tiles - 1), 0))
    return pl.pallas_call(
        functools.partial(_ssd_kernel, n_tiles=n_tiles, ctx_tiles=ctx_tiles),
        grid=(batch, 2 * n_tiles - ctx_tiles),
        in_specs=[tile(XBC_W),
                  pl.BlockSpec((1, HALO, XBC_W),
                               lambda b, s: (b, jnp.where(ph0(s), jnp.maximum(s * hb - 1, 0), 0), 0)),
                  pl.BlockSpec((1, HALO, XBC_W),
                               lambda b, s: (b, jnp.where(ph0(s), jnp.minimum((s + 1) * hb, n_hblk - 1), 0), 0)),
                  tile(DT_PAD),
                  pl.BlockSpec((1, ROW_TILE, SSD_W),
                               lambda b, s: (b, jnp.where(ph0(s), 0, s - n_tiles + ctx_tiles), 0)),
                  _full((3, XBC_W)), _full((1, XBC_W)), _full((1, DT_PAD)), _full((1, DT_PAD)),
                  _full((1, SSD_W)), _full((1, SSD_W)),
                  _full((DT_PAD, SSD_W)), _full((DT_PAD, SSD_W))],
        out_specs=pl.BlockSpec(
            (1, ROW_TILE, SSD_W), lambda b, s: (b, jnp.where(ph0(s), 0, s - n_tiles), 0)),
        out_shape=jax.ShapeDtypeStruct((batch, ta - n_ctx, SSD_W), BF16),
        scratch_shapes=[pltpu.VMEM((n_tiles, ROW_TILE, XBC_W), F32),
                        pltpu.VMEM((n_chunks, 2, q, DT_PAD), F32),
                        pltpu.VMEM((n_chunks, 2, SSD_STATE, SSD_W), F32),
                        pltpu.VMEM((n_chunks, 2, 8, SSD_W), F32),
                        pltpu.VMEM((ROW_TILE, SSD_W), F32)],
        compiler_params=_cparams("parallel", "arbitrary"),
        name="ssd_bidir",
    )(xbc, xbc, xbc, dt, z, conv_w, conv_b.reshape(1, XBC_W), pad24(a_log), pad24(dt_bias),
      dsk, norm_g.reshape(1, SSD_W), jnp.asarray(ef, BF16), jnp.asarray(eb, BF16))


def _s5_disc_kernel(lr_ref, li_ref, ldt_ref, bre_ref, bim_ref, cre_ref, cim_ref,
                    a_ref, bd_ref, cd_ref):
    lr, li = lr_ref[...], li_ref[...]
    dt = jnp.exp(ldt_ref[...])
    mag = jnp.exp(dt * lr)
    ab_re, ab_im = mag * jnp.cos(dt * li), mag * jnp.sin(dt * li)
    den = lr * lr + li * li
    k_re = ((ab_re - 1.0) * lr + ab_im * li) / den
    k_im = (ab_im * lr - (ab_re - 1.0) * li) / den
    bre, bim = bre_ref[...], bim_ref[...]
    for d in range(2):
        a_ref[d, :, 0:S5_NSTATE] = jnp.broadcast_to(ab_re[d:d + 1], (8, S5_NSTATE))
        a_ref[d, :, S5_NSTATE:] = jnp.broadcast_to(ab_im[d:d + 1], (8, S5_NSTATE))
        kr, ki = k_re[d:d + 1], k_im[d:d + 1]
        bd_ref[d, :, 0:S5_NSTATE] = (kr * bre - ki * bim).astype(BF16)
        bd_ref[d, :, S5_NSTATE:] = (kr * bim + ki * bre).astype(BF16)
        cd_ref[d, 0:S5_NSTATE, :] = cre_ref[d].astype(BF16)
        cd_ref[d, S5_NSTATE:, :] = (-cim_ref[d]).astype(BF16)


def _s5_discretize(lam_re, lam_im, log_dt, b_re, b_im, c_re, c_im):
    eye = jnp.eye(S5_GROUPS, dtype=F32)
    bd = lambda b: jnp.einsum('gph,gk->ghkp', b, eye).reshape(S5_W, S5_NSTATE)
    cd = lambda cc: jnp.einsum('dghp,gk->dgpkh', cc, eye).reshape(2, S5_NSTATE, S5_W)
    ldt = jnp.repeat(log_dt, S5_STATE, axis=-1)
    return pl.pallas_call(
        _s5_disc_kernel,
        out_shape=[jax.ShapeDtypeStruct((2, 8, 2 * S5_NSTATE), F32),
                   jax.ShapeDtypeStruct((2, S5_W, 2 * S5_NSTATE), BF16),
                   jax.ShapeDtypeStruct((2, 2 * S5_NSTATE, S5_W), BF16)],
        compiler_params=pltpu.CompilerParams(vmem_limit_bytes=VMEM_LIMIT_BYTES),
        name="s5_discretize",
    )(lam_re.reshape(2, S5_NSTATE), lam_im.reshape(2, S5_NSTATE), ldt,
      bd(b_re), bd(b_im), cd(c_re), cd(c_im))


S5_TIME_CHUNK = 128


def _s5_scan_kernel(uf_ref, ub_ref, a_ref, bd_ref, cd_ref, yf_ref, yb_ref,
                    hsf_ref, hsb_ref, carry_ref, *, batch):
    j = pl.program_id(0)
    n = S5_NSTATE
    tc = S5_TIME_CHUNK
    n_slab = S5_W // 128
    half = hsf_ref.shape[0] // 2
    chains = ((uf_ref, hsf_ref, yf_ref), (ub_ref, hsb_ref, yb_ref))

    @pl.when(j == 0)
    def _():
        carry_ref[...] = jnp.zeros_like(carry_ref)

    for d, (u_ref, hs_ref, _) in enumerate(chains):
        for r in (0, half):
            u = jnp.concatenate([u_ref[s, r:r + half, :] for s in range(n_slab)], axis=1)
            hs_ref[r:r + half, :] = jnp.dot(u.astype(BF16), bd_ref[d], preferred_element_type=F32)

    for d, (_, hs_ref, _) in enumerate(chains):
        ar = jnp.broadcast_to(a_ref[d, 0:1, 0:n], (batch, n))
        ai = jnp.broadcast_to(a_ref[d, 0:1, n:], (batch, n))
        hr = carry_ref[d, :, 0:n]
        hi = carry_ref[d, :, n:]
        for step in (range(tc) if d == 0 else range(tc - 1, -1, -1)):
            rows = slice(step * batch, (step + 1) * batch)
            hr, hi = (ar * hr - ai * hi + hs_ref[rows, 0:n],
                      ar * hi + ai * hr + hs_ref[rows, n:])
            hs_ref[rows, 0:n] = hr
            hs_ref[rows, n:] = hi
        carry_ref[d, :, 0:n] = hr
        carry_ref[d, :, n:] = hi

    for d, (_, hs_ref, y_ref) in enumerate(chains):
        for r in (0, half):
            y = jnp.dot(hs_ref[r:r + half, :].astype(BF16), cd_ref[d], preferred_element_type=F32)
            for s in range(n_slab):
                y_ref[s, r:r + half, :] = y[:, s * 128:(s + 1) * 128]


def _s5_scan(us_flat, a, bd, cd, batch, n_ctx):
    n_slab, rows_total, _ = us_flat.shape
    ta = rows_total // batch
    tc = S5_TIME_CHUNK
    n_chunks = ta // tc
    ctx_chunks = n_ctx // tc
    blk = tc * batch

    def bwd_chunk(j):
        return jnp.where(j < ctx_chunks, ctx_chunks - 1 - j, n_chunks - 1 - (j - ctx_chunks))

    fwd_spec = pl.BlockSpec((n_slab, blk, 128), lambda j: (0, j, 0))
    bwd_spec = pl.BlockSpec((n_slab, blk, 128), lambda j: (0, bwd_chunk(j), 0))
    return pl.pallas_call(
        functools.partial(_s5_scan_kernel, batch=batch),
        grid=(n_chunks,),
        in_specs=[fwd_spec, bwd_spec, _full((2, 8, 2 * S5_NSTATE)),
                  _full((2, S5_W, 2 * S5_NSTATE)), _full((2, 2 * S5_NSTATE, S5_W))],
        out_specs=[fwd_spec, bwd_spec],
        out_shape=[jax.ShapeDtypeStruct((n_slab, rows_total, 128), F32)] * 2,
        scratch_shapes=[pltpu.VMEM((blk, 2 * S5_NSTATE), F32),
                        pltpu.VMEM((blk, 2 * S5_NSTATE), F32),
                        pltpu.VMEM((2, batch, 2 * S5_NSTATE), F32)],
        compiler_params=_cparams("arbitrary"),
        name="s5_scan",
    )(us_flat, us_flat, a, bd, cd)


def _merge_ln_kernel(gs_ref, yf_ref, yb_ref, us_ref, h_ref, *rest, batch):
    g = BATCH_GROUP
    gate_refs = rest[:g]
    dd_ref, gw_ref, gb_ref, w_ref, g_ref, beta_ref, o_ref, wbf_ref = rest[g:]

    @pl.when(_first_step())
    def _():
        wbf_ref[...] = w_ref[...].astype(BF16)

    def s5_input(i):
        rows = _time_major_rows(pl.program_id(1) * g + i, batch)
        y5 = jnp.concatenate([yf_ref[s, rows, :] + yb_ref[s, rows, :]
                              for s in range(S5_W // 128)], axis=1)
        us = jnp.concatenate([us_ref[s, rows, :] for s in range(S5_W // 128)], axis=1)
        return y5 + dd_ref[...] * us

    ge = jax.nn.gelu(jnp.concatenate([s5_input(i) for i in range(g)], axis=0))
    s5 = ge * jax.nn.sigmoid(
        jnp.dot(ge.astype(BF16), gw_ref[...], preferred_element_type=F32) + gb_ref[...])
    gs = jnp.concatenate([gs_ref[i] for i in range(g)], axis=0)
    y = jnp.dot(gs, wbf_ref[0:SSD_W, :], preferred_element_type=F32)
    y = y + jnp.dot(s5.astype(BF16), wbf_ref[SSD_W:, :], preferred_element_type=F32)
    rows = o_ref.shape[1]
    for i in range(g):
        o_ref[i] = _layer_norm(ALPHA * h_ref[i] + gate_refs[i][0] * y[i * rows:(i + 1) * rows],
                               g_ref[...], beta_ref[...])


def _merge_ln(g_ssd, y5_fwd, y5_bwd, us_t, h, mods, layer, s5_d, glu_w, glu_b, w_out, ln_g, ln_b,
              ctx_tiles):
    batch, ta, _ = h.shape
    nt = ta // ROW_TILE - ctx_tiles
    n_slab = S5_W // 128
    tm_rows = ROW_TILE * batch
    g = BATCH_GROUP
    gate_specs = [
        pl.BlockSpec((1, 1, D_MODEL),
                     lambda t, bg, i=i: ((layer * MOD_ROWS + bg * g + i) * 6 + 2, 0, 0))
        for i in range(g)]
    return pl.pallas_call(
        functools.partial(_merge_ln_kernel, batch=batch),
        grid=(nt, batch // g),
        in_specs=[pl.BlockSpec((g, ROW_TILE, SSD_W), lambda t, b: (b, t, 0)),
                  pl.BlockSpec((n_slab, tm_rows, 128), lambda t, b: (0, t + ctx_tiles, 0)),
                  pl.BlockSpec((n_slab, tm_rows, 128), lambda t, b: (0, t + ctx_tiles, 0)),
                  pl.BlockSpec((n_slab, tm_rows, 128), lambda t, b: (0, t + ctx_tiles, 0)),
                  pl.BlockSpec((g, ROW_TILE, D_MODEL), lambda t, b: (b, t + ctx_tiles, 0))]
        + gate_specs
        + [_full((1, S5_W)), _full((S5_W, S5_W)), _full((1, S5_W)),
           _resident((D_MODEL, D_MODEL)), _full((1, D_MODEL)), _full((1, D_MODEL))],
        out_specs=pl.BlockSpec((g, ROW_TILE, D_MODEL), lambda t, b: (b, t, 0)),
        out_shape=jax.ShapeDtypeStruct((batch, nt * ROW_TILE, D_MODEL), F32),
        scratch_shapes=[pltpu.VMEM((D_MODEL, D_MODEL), BF16)],
        compiler_params=_cparams("arbitrary", "arbitrary"),
        name="merge_outproj_ln",
    )(g_ssd, y5_fwd, y5_bwd, us_t, h, *([mods] * g), s5_d.reshape(1, S5_W), glu_w.astype(BF16),
      glu_b.reshape(1, S5_W), w_out, ln_g.reshape(1, -1), ln_b.reshape(1, -1))


def _attn_layer(h_ctx, h_lat, mods, layer, i, p, keep_ctx):
    n_ctx = h_ctx.shape[1]
    ctx_tiles = n_ctx // ROW_TILE
    lam_init = 0.8 - 0.6 * math.exp(-0.3 * layer)
    cos, sin = _rope_tables(h_lat.shape[1], n_ctx)
    q, k, v, f = _inproj_attn(h_ctx, h_lat, mods, layer, p['attn_w_in'][i],
                              cos, sin, ctx_tiles)
    o_ctx, o_lat = _attention(q, k, v, p['attn_lambda'][i], p['attn_subln_g'][i], lam_init,
                              ctx_tiles)
    fm_ctx, fm_lat = _fourier(f, p['fourier_w'][i], p['fourier_b'][i], n_ctx)
    h1 = _outproj_ln(o_ctx, o_lat, fm_ctx, fm_lat, h_ctx, h_lat, mods, layer,
                     p['attn_w_out'][i],
                     p['ln_g'][layer, 0], p['ln_b'][layer, 0], ctx_tiles)
    return _ffn(h1, mods, layer, p['ffn_w_up_bf16'], p['ffn_b_up'][layer],
                p['ffn_conv_w'][layer], p['ffn_conv_b'][layer],
                p['ffn_w_down_bf16'], p['ffn_b_down'][layer],
                p['ln_g'][layer, 1], p['ln_b'][layer, 1], ctx_tiles, 0 if keep_ctx else ctx_tiles,
                group=BATCH_GROUP)


def _ssm_layer(h, mods, layer, i, n_ctx, p, keep_ctx):
    assert not keep_ctx, "an SSM layer that must also emit context rows is not implemented"
    ctx_tiles = n_ctx // ROW_TILE
    batch = h.shape[0]
    z, xbc, dt, us_t = _inproj_ssm(h, mods, layer, p['ssm_w_in'][i], ctx_tiles)
    g_ssd = _ssd(xbc, dt, z, p['ssd_conv_w'][i], p['ssd_conv_b'][i], p['ssd_a_log'][i],
                 p['ssd_dt_bias'][i], p['ssd_d'][i], p['ssd_norm_g'][i], n_ctx)
    a, bd, cd = _s5_discretize(p['s5_lambda_re'][i], p['s5_lambda_im'][i], p['s5_log_dt'][i],
                               p['s5_b_re'][i], p['s5_b_im'][i], p['s5_c_re'][i], p['s5_c_im'][i])
    y5_fwd, y5_bwd = _s5_scan(us_t, a, bd, cd, batch, n_ctx)
    h1 = _merge_ln(g_ssd, y5_fwd, y5_bwd, us_t, h, mods, layer, p['s5_d'][i], p['s5_glu_w'][i],
                   p['s5_glu_b'][i], p['ssm_w_out'][i],
                   p['ln_g'][layer, 0], p['ln_b'][layer, 0], ctx_tiles)
    return _ffn(h1, mods, layer, p['ffn_w_up_bf16'], p['ffn_b_up'][layer],
                p['ffn_conv_w'][layer], p['ffn_conv_b'][layer],
                p['ffn_w_down_bf16'], p['ffn_b_down'][layer],
                p['ln_g'][layer, 1], p['ln_b'][layer, 1], 0, 0, rows=FFN_LAT_ROWS)


def kernel(x, c, ctx, c_ctx, ada_w, ada_b, ln_g, ln_b, ffn_w_up, ffn_b_up, ffn_conv_w, ffn_conv_b, ffn_w_down, ffn_b_down, attn_w_in, attn_lambda, attn_subln_g, fourier_w, fourier_b, attn_w_out, ssm_w_in, ssd_conv_w, ssd_conv_b, ssd_a_log, ssd_dt_bias, ssd_d, ssd_norm_g, s5_lambda_re, s5_lambda_im, s5_log_dt, s5_b_re, s5_b_im, s5_c_re, s5_c_im, s5_d, s5_glu_w, s5_glu_b, ssm_w_out):
    p = dict(ln_g=ln_g, ln_b=ln_b, ffn_w_up=ffn_w_up, ffn_b_up=ffn_b_up, ffn_conv_w=ffn_conv_w,
             ffn_conv_b=ffn_conv_b, ffn_w_down=ffn_w_down, ffn_b_down=ffn_b_down,
             attn_w_in=attn_w_in, attn_lambda=attn_lambda, attn_subln_g=attn_subln_g,
             fourier_w=fourier_w, fourier_b=fourier_b, attn_w_out=attn_w_out, ssm_w_in=ssm_w_in,
             ssd_conv_w=ssd_conv_w, ssd_conv_b=ssd_conv_b, ssd_a_log=ssd_a_log,
             ssd_dt_bias=ssd_dt_bias, ssd_d=ssd_d, ssd_norm_g=ssd_norm_g,
             s5_lambda_re=s5_lambda_re, s5_lambda_im=s5_lambda_im, s5_log_dt=s5_log_dt,
             s5_b_re=s5_b_re, s5_b_im=s5_b_im, s5_c_re=s5_c_re, s5_c_im=s5_c_im, s5_d=s5_d,
             s5_glu_w=s5_glu_w, s5_glu_b=s5_glu_b, ssm_w_out=ssm_w_out)
    batch, n_lat, _ = x.shape
    n_ctx = ctx.shape[1]
    assert n_ctx == ROW_TILE and n_lat % ROW_TILE == 0 and batch < MOD_ROWS
    mods = _ada_mods(c, c_ctx, ada_w, ada_b)
    p['ffn_w_up_bf16'] = ffn_w_up.astype(BF16)
    p['ffn_w_down_bf16'] = ffn_w_down.astype(BF16)
    assert DEPTH == 2
    h = _attn_layer(ctx, x, mods, 0, 0, p, keep_ctx=True)
    return _ssm_layer(h, mods, 1, 0, n_ctx, p, keep_ctx=False)
```

```python
import functools
import math

import numpy as np
import jax
import jax.numpy as jnp
from jax import lax
from jax.experimental import pallas as pl
from jax.experimental.pallas import tpu as pltpu

F32 = jnp.float32
BF16 = jnp.bfloat16

D_MODEL = 1024
DEPTH = 2
GRID_W = 64
ROPE_BASE = 10000.0
LN_EPS = 1e-5
ALPHA = (2 * DEPTH) ** 0.25
ATTN_W = 768
ATTN_HEADS = 6
ATTN_HEAD_DIM = 64
F_W = 256
F_GROUPS = 4
F_GROUP_W = 64
ATTN_IN_W = 2 * ATTN_W + ATTN_W + F_W
SSD_W = 768
SSD_HEADS = 12
SSD_HEAD_DIM = 64
SSD_GROUPS = 2
SSD_STATE = 128
SSD_CHUNK = 128
XBC_W = SSD_W + 2 * SSD_GROUPS * SSD_STATE
DT_W = 2 * SSD_HEADS
DT_PAD = 128
S5_W = 256
S5_GROUPS = 16
S5_STATE = 64
S5_NSTATE = S5_GROUPS * S5_STATE
SSM_IN_W = SSD_W + XBC_W + DT_W + S5_W
SSM_IN_PAD = SSD_W + XBC_W + S5_W + DT_PAD
D_FF = 2816
FF_TILE = 256

ROW_TILE = 256
ATTN_Q_SUB = 256
ATTN_ITEM_TILES = 2
FFN_LAT_ROWS = 512
BATCH_GROUP = 2
FOURIER_TILE = 1024
HALO = 8
MOD_ROWS = 16
VMEM_LIMIT_BYTES = 56 * 1024 * 1024


def _cparams(*sem):
    return pltpu.CompilerParams(dimension_semantics=sem, vmem_limit_bytes=VMEM_LIMIT_BYTES)


def _silu(v):
    return v * jax.nn.sigmoid(v)


def _layer_norm(v, g, b):
    mu = jnp.mean(v, axis=-1, keepdims=True)
    d = v - mu
    var = jnp.mean(d * d, axis=-1, keepdims=True)
    return d * lax.rsqrt(var + LN_EPS) * g + b


def _full(shape):
    nd = len(shape)
    return pl.BlockSpec(shape, lambda *_: (0,) * nd)


def _resident(shape):
    nd = len(shape)
    return pl.BlockSpec(shape, lambda *_: (0,) * nd, pipeline_mode=pl.Buffered(1))


def _first_step():
    return (pl.program_id(0) == 0) & (pl.program_id(1) == 0)


def _mod_specs_grouped(layer, j, batch, ctx_tiles):
    def spec(i):
        def idx(bg, t):
            row = jnp.where(t < ctx_tiles, batch, bg * BATCH_GROUP + i)
            return ((layer * MOD_ROWS + row) * 6 + j, 0, 0)
        return pl.BlockSpec((1, 1, D_MODEL), idx)
    return [spec(i) for i in range(BATCH_GROUP)]


def _ada_kernel(c_ref, w_ref, b_ref, o_ref):
    s = _silu(c_ref[...])
    w = w_ref[0]
    s_hi = s.astype(BF16)
    s_lo = (s - s_hi.astype(F32)).astype(BF16)
    w_hi = w.astype(BF16)
    w_lo = (w - w_hi.astype(F32)).astype(BF16)
    dot = functools.partial(jnp.dot, preferred_element_type=F32)
    o_ref[0] = dot(s_hi, w_hi) + (dot(s_lo, w_hi) + dot(s_hi, w_lo)) + b_ref[0]


def _ada_mods(c, c_ctx, ada_w, ada_b):
    batch = c.shape[0]
    nl = ada_w.shape[0]
    c_all = jnp.concatenate(
        [c, c_ctx[None], jnp.zeros((MOD_ROWS - batch - 1, D_MODEL), F32)], axis=0)
    out = pl.pallas_call(
        _ada_kernel,
        grid=(nl, 6),
        in_specs=[_full((MOD_ROWS, D_MODEL)),
                  pl.BlockSpec((1, D_MODEL, D_MODEL), lambda l, j: (l, 0, j)),
                  pl.BlockSpec((1, 1, D_MODEL), lambda l, j: (l, 0, j))],
        out_specs=pl.BlockSpec((1, MOD_ROWS, D_MODEL), lambda l, j: (l, 0, j)),
        out_shape=jax.ShapeDtypeStruct((nl, MOD_ROWS, 6 * D_MODEL), F32),
        compiler_params=_cparams("arbitrary", "arbitrary"),
        name="ada_mods",
    )(c_all, ada_w, ada_b.reshape(nl, 1, 6 * D_MODEL))
    return out.reshape(nl * MOD_ROWS * 6, 1, D_MODEL)


def _modulated_rows(h_of, shift_refs, scale_refs):
    return jnp.concatenate(
        [(h_of(i) * (1.0 + scale_refs[i][0]) + shift_refs[i][0]).astype(BF16)
         for i in range(BATCH_GROUP)], axis=0)


def _store_rows(ref, val):
    rows = ref.shape[1]
    for i in range(BATCH_GROUP):
        ref[i] = val[i * rows:(i + 1) * rows].astype(ref.dtype)


def _inproj_attn_kernel(hc_ref, hl_ref, *rest, ctx_tiles):
    g = BATCH_GROUP
    shift_refs, scale_refs = rest[:g], rest[g:2 * g]
    w_ref, cos_ref, sin_ref, q_ref, k_ref, v_ref, f_ref, wbf_ref = rest[2 * g:]

    @pl.when(_first_step())
    def _():
        wbf_ref[...] = w_ref[...].astype(BF16)

    is_ctx = pl.program_id(1) < ctx_tiles
    u = _modulated_rows(lambda i: jnp.where(is_ctx, hc_ref[i], hl_ref[i]), shift_refs, scale_refs)
    p = jnp.dot(u, wbf_ref[...], preferred_element_type=F32)
    cos = jnp.concatenate([cos_ref[...]] * g, axis=0)
    sin = jnp.concatenate([sin_ref[...]] * g, axis=0)
    lane = lax.broadcasted_iota(jnp.int32, cos.shape, 1)
    first_half = (lane % ATTN_HEAD_DIM) < (ATTN_HEAD_DIM // 2)

    def rope(blk):
        partner = jnp.where(first_half, pltpu.roll(blk, 128 - 32, 1), pltpu.roll(blk, 32, 1))
        return blk * cos + partner * sin

    qk_scale = ATTN_HEAD_DIM ** -0.5 * math.log2(math.e)
    rows = q_ref.shape[1]
    for i in range(ATTN_HEADS):
        lo, hi = i * 128, (i + 1) * 128
        qh = (rope(p[:, lo:hi]) * qk_scale).astype(BF16)
        kh = rope(p[:, ATTN_W + lo:ATTN_W + hi]).astype(BF16)
        for b in range(g):
            q_ref[b, :, lo:hi] = qh[b * rows:(b + 1) * rows]
            k_ref[b, :, lo:hi] = kh[b * rows:(b + 1) * rows]
    _store_rows(v_ref, p[:, 2 * ATTN_W:3 * ATTN_W])
    _store_rows(f_ref, p[:, 3 * ATTN_W:])


def _rope_tables(n_lat, n_ctx):
    rows = n_lat // GRID_W
    row = jnp.repeat(jnp.arange(rows, dtype=F32), GRID_W)
    col = jnp.tile(jnp.arange(GRID_W, dtype=F32), rows)
    n_freq = ATTN_HEAD_DIM // 4
    inv_freq = ROPE_BASE ** (-jnp.arange(n_freq, dtype=F32) / n_freq)
    ang = jnp.concatenate([row[:, None] * inv_freq, col[:, None] * inv_freq], axis=-1)
    cos, sin = jnp.cos(ang), jnp.sin(ang)
    cos128 = jnp.tile(cos, (1, 4))
    sin128 = jnp.tile(jnp.concatenate([-sin, sin], axis=-1), (1, 2))
    cos_all = jnp.concatenate([jnp.ones((n_ctx, 128), F32), cos128], axis=0)
    sin_all = jnp.concatenate([jnp.zeros((n_ctx, 128), F32), sin128], axis=0)
    return cos_all, sin_all


def _split_specs(width, ctx_tiles):
    g = BATCH_GROUP
    return [pl.BlockSpec((g, ROW_TILE, width), lambda b, t: (b, jnp.minimum(t, ctx_tiles - 1), 0)),
            pl.BlockSpec((g, ROW_TILE, width), lambda b, t: (b, jnp.maximum(t - ctx_tiles, 0), 0))]


def _inproj_attn(h_ctx, h_lat, mods, layer, w_in, cos, sin, ctx_tiles):
    batch = h_lat.shape[0]
    assert batch % BATCH_GROUP == 0
    ta = h_ctx.shape[1] + h_lat.shape[1]
    nt = ta // ROW_TILE
    row = lambda w: pl.BlockSpec((BATCH_GROUP, ROW_TILE, w), lambda b, t: (b, t, 0))
    tab = pl.BlockSpec((ROW_TILE, 128), lambda b, t: (t, 0))
    n_mod = 2 * BATCH_GROUP
    return pl.pallas_call(
        functools.partial(_inproj_attn_kernel, ctx_tiles=ctx_tiles),
        grid=(batch // BATCH_GROUP, nt),
        in_specs=_split_specs(D_MODEL, ctx_tiles)
        + _mod_specs_grouped(layer, 0, batch, ctx_tiles)
        + _mod_specs_grouped(layer, 1, batch, ctx_tiles)
        + [_resident((D_MODEL, ATTN_IN_W)), tab, tab],
        out_specs=[row(ATTN_W), row(ATTN_W), row(ATTN_W), row(F_W)],
        out_shape=[jax.ShapeDtypeStruct((batch, ta, ATTN_W), BF16)] * 3
        + [jax.ShapeDtypeStruct((batch, ta, F_W), BF16)],
        scratch_shapes=[pltpu.VMEM((D_MODEL, ATTN_IN_W), BF16)],
        compiler_params=_cparams("arbitrary", "arbitrary"),
        name="inproj_attn",
    )(h_ctx, h_lat, *([mods] * n_mod), w_in, cos, sin)


def _diff_lambda(lam_ref, lam_init):
    lamv = lam_ref[...]
    l1 = jnp.sum(lamv[0:1] * lamv[1:2], axis=-1, keepdims=True)
    l2 = jnp.sum(lamv[2:3] * lamv[3:4], axis=-1, keepdims=True)
    return jnp.exp(l1) - jnp.exp(l2) + lam_init


def _stack_maps(q):
    lane = lax.broadcasted_iota(jnp.int32, q.shape, 1)
    zero = jnp.zeros_like(q)
    return jnp.concatenate([jnp.where(lane < ATTN_HEAD_DIM, q, zero),
                            jnp.where(lane >= ATTN_HEAD_DIM, q, zero)], axis=0)


def _scores(q2, k):
    return lax.dot_general(q2, k, (((1,), (1,)), ((), ())), preferred_element_type=F32)


def _diff_softmax_pv(load_s0, m0, load_s1, m1, lam, v, gain):
    e0 = jnp.exp2(load_s0() - m0)
    e1 = jnp.exp2(load_s1() - m1)
    l0 = jnp.sum(e0, axis=-1, keepdims=True)
    l1 = jnp.sum(e1, axis=-1, keepdims=True)
    w = e0 - e1 * (lam * l0 / l1)
    o = jnp.dot(w.astype(BF16), v, preferred_element_type=F32) * (1.0 / l0)
    o = o * lax.rsqrt(jnp.mean(o * o, axis=-1, keepdims=True) + LN_EPS)
    return (o * gain).astype(BF16)


def _attn_ctx_kernel(lam_ref, g_ref, q_ref, k_ref, v_ref, o_ref, *, lam_init):
    lam = _diff_lambda(lam_ref, lam_init)
    gain = g_ref[...] * (1.0 - lam_init)
    tq = q_ref.shape[1]
    for hd in range(ATTN_HEADS):
        cols = slice(hd * 128, (hd + 1) * 128)
        s = _scores(_stack_maps(q_ref[0, :, cols]), k_ref[0, :, cols])
        m = jnp.max(s, axis=-1, keepdims=True)

        def half(lo, s=s):
            return lambda: s[lo:lo + tq]

        o_ref[0, :, cols] = _diff_softmax_pv(half(0), m[:tq], half(tq), m[tq:], lam,
                                             v_ref[0, :, cols], gain)


def _attn_lat_kernel(lam_ref, g_ref, k_ref, v_ref, *rest, lam_init):
    q_refs = rest[:ATTN_ITEM_TILES]
    o_ref, sa_ref, ma_ref, sb_ref, mb_ref = rest[ATTN_ITEM_TILES:]
    t = pl.program_id(0)
    tq = ATTN_ITEM_TILES * ROW_TILE
    sub = ATTN_Q_SUB

    @pl.when(t == 0)
    def _():
        sb_ref[...] = jnp.zeros_like(sb_ref)
        mb_ref[...] = jnp.zeros_like(mb_ref)

    def step(s_new, m_new, s_old, m_old):
        lam = _diff_lambda(lam_ref, lam_init)
        gain = g_ref[...] * (1.0 - lam_init)
        q = jnp.concatenate([q_ref[0] for q_ref in q_refs], axis=0)
        s = _scores(_stack_maps(q), k_ref[0])
        s_new[...] = s
        m_new[...] = jnp.max(s, axis=-1, keepdims=True)
        v = v_ref[0]
        for i in range(tq // sub):
            r0 = slice(i * sub, (i + 1) * sub)
            r1 = slice(tq + i * sub, tq + (i + 1) * sub)
            o_ref[0, r0, :] = _diff_softmax_pv(
                functools.partial(s_old.__getitem__, (r0, slice(None))), m_old[r0, :],
                functools.partial(s_old.__getitem__, (r1, slice(None))), m_old[r1, :],
                lam, v, gain)

    @pl.when(t % 2 == 0)
    def _():
        step(sa_ref, ma_ref, sb_ref, mb_ref)

    @pl.when(t % 2 == 1)
    def _():
        step(sb_ref, mb_ref, sa_ref, ma_ref)


def _attention(q, k, v, lam_vec, subln_g, lam_init, ctx_tiles):
    batch, ta, _ = q.shape
    n_ctx = ctx_tiles * ROW_TILE
    nt = ta // ROW_TILE - ctx_tiles
    g = subln_g.reshape(1, 128)
    small = [_full((4, ATTN_HEAD_DIM)), _full((1, 128))]
    cspec = pl.BlockSpec((1, n_ctx, ATTN_W), lambda b: (b, 0, 0))
    o_ctx = pl.pallas_call(
        functools.partial(_attn_ctx_kernel, lam_init=lam_init),
        grid=(batch,),
        in_specs=small + [cspec, cspec, cspec],
        out_specs=cspec,
        out_shape=jax.ShapeDtypeStruct((batch, n_ctx, ATTN_W), BF16),
        compiler_params=_cparams("arbitrary"),
        name="diff_attention_ctx",
    )(lam_vec, g, q, k, v)

    per = ATTN_ITEM_TILES
    assert nt % per == 0
    ni = nt // per
    n_items = batch * ATTN_HEADS * ni

    def item(j):
        return j // (ATTN_HEADS * ni), (j // ni) % ATTN_HEADS, j % ni

    def score_item(j):
        return item(jnp.minimum(j, n_items - 1))

    def finish_item(j):
        return item(jnp.maximum(j - 1, 0))

    def q_spec(part):
        def idx(j):
            b, h, t = score_item(j)
            return b, ctx_tiles + t * per + part, h
        return pl.BlockSpec((1, ROW_TILE, 128), idx)

    def k_idx(j):
        b, h, _ = score_item(j)
        return b, 0, h

    def v_idx(j):
        b, h, _ = finish_item(j)
        return b, 0, h

    def o_idx(j):
        b, h, t = finish_item(j)
        return b, t, h

    o_lat = pl.pallas_call(
        functools.partial(_attn_lat_kernel, lam_init=lam_init),
        grid=(n_items + 1,),
        in_specs=small + [pl.BlockSpec((1, ta, 128), k_idx), pl.BlockSpec((1, ta, 128), v_idx)]
        + [q_spec(part) for part in range(per)],
        out_specs=pl.BlockSpec((1, per * ROW_TILE, 128), o_idx),
        out_shape=jax.ShapeDtypeStruct((batch, nt * ROW_TILE, ATTN_W), BF16),
        scratch_shapes=[pltpu.VMEM((2 * per * ROW_TILE, ta), F32),
                        pltpu.VMEM((2 * per * ROW_TILE, 1), F32)] * 2,
        compiler_params=_cparams("arbitrary"),
        name="diff_attention",
    )(lam_vec, g, k, v, *([q] * per))
    return o_ctx, o_lat


def _dft_tables(n):
    k = np.arange(n, dtype=np.int64)
    ang = 2.0 * np.pi * ((k[:, None] * k[None, :]) % n).astype(np.float64) / n
    return np.cos(ang), np.sin(ang)


def _fourier_kernel(f_ref, cs_ref, dl_ref, dc_ref, w_ref, b_ref, oc_ref, ol_ref, ac_ref, al_ref,
                    *, n_ctx, n_lat):
    t = pl.program_id(0)
    b = pl.program_id(1)

    def stage1(rows0, n):
        a = jnp.dot(f_ref[0, rows0:rows0 + n, :], cs_ref[...], preferred_element_type=F32)
        return a[:, :F_W].astype(BF16), a[:, F_W:].astype(BF16)

    def stage2(dft, a, n):
        z = jnp.dot(dft, a, preferred_element_type=F32)
        z = z * (1.0 / math.sqrt(n * F_GROUP_W))
        o = jnp.dot(z.astype(BF16), w_ref[...], preferred_element_type=F32) + b_ref[...]
        return o.astype(BF16)

    @pl.when(t == 0)
    def _():
        ac_ref[0:n_ctx, :], ac_ref[n_ctx:, :] = stage1(0, n_ctx)
        oc_ref[0] = stage2(dc_ref[...], ac_ref[...], n_ctx)

    @pl.when(t == 1)
    def _():
        al_ref[b, 0:n_lat, :], al_ref[b, n_lat:, :] = stage1(n_ctx, n_lat)

    @pl.when(t >= 1)
    def _():
        ol_ref[0] = stage2(dl_ref[...], al_ref[b], n_lat)


def _fourier(f, fourier_w, fourier_b, n_ctx):
    batch, ta, _ = f.shape
    n_lat = ta - n_ctx
    ft = min(FOURIER_TILE, n_lat)
    assert n_lat % ft == 0
    cc, sc = _dft_tables(F_GROUP_W)
    eye = np.eye(F_GROUPS)
    cs = jnp.asarray(np.concatenate([np.kron(eye, cc), np.kron(eye, sc)], axis=1), BF16)
    cl, sl = _dft_tables(n_lat)
    dft_lat = jnp.asarray(np.concatenate([cl, -sl], axis=1), BF16)
    cx, sx = _dft_tables(n_ctx)
    dft_ctx = jnp.asarray(np.concatenate([cx, -sx], axis=1), BF16)
    w_blk = jnp.einsum('gce,gh->gche', fourier_w, jnp.eye(F_GROUPS, dtype=F32))
    w_blk = w_blk.reshape(F_W, F_W).astype(BF16)
    return pl.pallas_call(
        functools.partial(_fourier_kernel, n_ctx=n_ctx, n_lat=n_lat),
        grid=(1 + n_lat // ft, batch),
        in_specs=[
                  pl.BlockSpec((1, ta, F_W),
                               lambda t, b: (jnp.where(t <= 1, b, batch - 1), 0, 0)),
                  _full((F_W, 2 * F_W)),
                  pl.BlockSpec((ft, 2 * n_lat), lambda t, b: (jnp.maximum(t - 1, 0), 0)),
                  _full((n_ctx, 2 * n_ctx)),
                  _full((F_W, F_W)), _full((1, F_W))],
        out_specs=[pl.BlockSpec((1, n_ctx, F_W),
                                lambda t, b: (jnp.where(t == 0, b, batch - 1), 0, 0)),
                   pl.BlockSpec((1, ft, F_W),
                                lambda t, b: (jnp.where(t == 0, 0, b), jnp.maximum(t - 1, 0), 0))],
        out_shape=[jax.ShapeDtypeStruct((batch, n_ctx, F_W), BF16),
                   jax.ShapeDtypeStruct((batch, n_lat, F_W), BF16)],
        scratch_shapes=[pltpu.VMEM((2 * n_ctx, F_W), BF16),
                        pltpu.VMEM((batch, 2 * n_lat, F_W), BF16)],
        compiler_params=_cparams("arbitrary", "arbitrary"),
        name="fourier_mix",
    )(f, cs, dft_lat, dft_ctx, w_blk, fourier_b.reshape(1, F_W))


def _outproj_ln_kernel(ac_ref, al_ref, bc_ref, bl_ref, hc_ref, hl_ref, *rest, ctx_tiles):
    g = BATCH_GROUP
    gate_refs = rest[:g]
    w_ref, g_ref, beta_ref, o_ref, wbf_ref = rest[g:]

    @pl.when(_first_step())
    def _():
        wbf_ref[...] = w_ref[...].astype(BF16)

    is_ctx = pl.program_id(1) < ctx_tiles
    a = jnp.concatenate([jnp.where(is_ctx, ac_ref[i], al_ref[i]) for i in range(g)], axis=0)
    b2 = jnp.concatenate([jnp.where(is_ctx, bc_ref[i], bl_ref[i]) for i in range(g)], axis=0)
    wa = a.shape[1]
    y = jnp.dot(a, wbf_ref[0:wa, :], preferred_element_type=F32)
    y = y + jnp.dot(b2, wbf_ref[wa:, :], preferred_element_type=F32)
    rows = o_ref.shape[1]
    for i in range(g):
        h = jnp.where(is_ctx, hc_ref[i], hl_ref[i])
        o_ref[i] = _layer_norm(ALPHA * h + gate_refs[i][0] * y[i * rows:(i + 1) * rows],
                               g_ref[...], beta_ref[...])


def _outproj_ln(a_ctx, a_lat, b_ctx, b_lat, h_ctx, h_lat, mods, layer, w_out, ln_g, ln_b,
                ctx_tiles):
    batch = h_lat.shape[0]
    ta = h_ctx.shape[1] + h_lat.shape[1]
    nt = ta // ROW_TILE
    row = lambda w: pl.BlockSpec((BATCH_GROUP, ROW_TILE, w), lambda b, t: (b, t, 0))
    return pl.pallas_call(
        functools.partial(_outproj_ln_kernel, ctx_tiles=ctx_tiles),
        grid=(batch // BATCH_GROUP, nt),
        in_specs=_split_specs(a_lat.shape[2], ctx_tiles) + _split_specs(b_lat.shape[2], ctx_tiles)
        + _split_specs(D_MODEL, ctx_tiles)
        + _mod_specs_grouped(layer, 2, batch, ctx_tiles)
        + [_resident((D_MODEL, D_MODEL)), _full((1, D_MODEL)), _full((1, D_MODEL))],
        out_specs=row(D_MODEL),
        out_shape=jax.ShapeDtypeStruct((batch, ta, D_MODEL), F32),
        scratch_shapes=[pltpu.VMEM((D_MODEL, D_MODEL), BF16)],
        compiler_params=_cparams("arbitrary", "arbitrary"),
        name="outproj_ln",
    )(a_ctx, a_lat, b_ctx, b_lat, h_ctx, h_lat, *([mods] * BATCH_GROUP), w_out,
      ln_g.reshape(1, D_MODEL),
      ln_b.reshape(1, D_MODEL))


def _ffn_kernel(h_ref, hp_ref, hn_ref, *rest, group, ctx_tiles, tile_off, nt_seq):
    shift_refs, scale_refs, gate_refs = rest[:group], rest[group:2 * group], rest[2 * group:3 * group]
    (wup_ref, bup_ref, cw_ref, cb_ref, wdn_ref, bdn_ref, g_ref, beta_ref,
     o_ref, uext_ref, ubf_ref, act_ref) = rest[3 * group:]
    t = pl.program_id(1) + tile_off
    seg_first = (t == 0) | (t == ctx_tiles)
    seg_last = (t == nt_seq - 1) | (t == ctx_tiles - 1)
    tm = h_ref.shape[1]
    ext = tm + 2 * HALO
    for b in range(group):
        sc = 1.0 + scale_refs[b][0]
        sh = shift_refs[b][0]
        r0 = b * ext
        uext_ref[r0:r0 + HALO, :] = hp_ref[b] * sc + sh
        uext_ref[r0 + HALO:r0 + HALO + tm, :] = h_ref[b] * sc + sh
        uext_ref[r0 + HALO + tm:r0 + ext, :] = hn_ref[b] * sc + sh
    ubf_ref[...] = uext_ref[...].astype(BF16)
    row8 = lax.broadcasted_iota(jnp.int32, (8, FF_TILE), 0)

    def hidden(col0):
        cols = slice(col0, col0 + FF_TILE)
        zr_all = jnp.dot(ubf_ref[...], wup_ref[:, cols], preferred_element_type=F32)
        bup = bup_ref[:, cols]
        cw = cw_ref[:, cols]
        bias = cb_ref[:, cols] + (cw[0:1] + cw[1:2] + cw[2:3]) * bup
        outs = []
        for b in range(group):
            zr = zr_all[b * ext:(b + 1) * ext]
            z0 = zr[HALO:HALO + tm]
            prev = jnp.where(seg_first, -bup, zr[HALO - 1:HALO])
            nxt = jnp.where(seg_last, -bup, zr[HALO + tm:HALO + tm + 1])
            down = pltpu.roll(z0, 1, 0)
            up = pltpu.roll(z0, tm - 1, 0)
            zm1 = jnp.concatenate([jnp.where(row8 == 0, prev, down[0:8]), down[8:]], axis=0)
            zp1 = jnp.concatenate([up[:tm - 8], jnp.where(row8 == 7, nxt, up[tm - 8:])], axis=0)
            outs.append(cw[0:1] * zm1 + cw[1:2] * z0 + cw[2:3] * zp1 + bias)
        return jnp.concatenate(outs, axis=0)

    for j in range(D_FF // FF_TILE):
        val = hidden(j * FF_TILE)
        gat = hidden(D_FF + j * FF_TILE)
        act_ref[:, j * FF_TILE:(j + 1) * FF_TILE] = (val * _silu(gat)).astype(BF16)
    f = jnp.dot(act_ref[...], wdn_ref[...], preferred_element_type=F32) + bdn_ref[...]
    for b in range(group):
        o_ref[b] = _layer_norm(ALPHA * h_ref[b] + gate_refs[b][0] * f[b * tm:(b + 1) * tm],
                               g_ref[...], beta_ref[...])


def _ffn(h, mods, layer, w_up, b_up, conv_w, conv_b, w_down, b_down, ln_g, ln_b,
         ctx_tiles, tile_off, rows=ROW_TILE, group=1):
    batch, ta, _ = h.shape
    assert batch % group == 0
    nt_seq = ta // rows
    nt = nt_seq - tile_off
    hb = rows // HALO
    n_hblk = ta // HALO

    def mspecs(j):
        def spec(i):
            def idx(bg, t):
                r = jnp.where(t + tile_off < ctx_tiles, batch, bg * group + i)
                return ((layer * MOD_ROWS + r) * 6 + j, 0, 0)
            return pl.BlockSpec((1, 1, D_MODEL), idx)
        return [spec(i) for i in range(group)]

    resident = lambda shape: pl.BlockSpec((None,) + shape, lambda *_: (layer, 0, 0),
                                          pipeline_mode=pl.Buffered(1))
    ext = rows + 2 * HALO
    return pl.pallas_call(
        functools.partial(_ffn_kernel, group=group, ctx_tiles=ctx_tiles, tile_off=tile_off,
                          nt_seq=nt_seq),
        grid=(batch // group, nt),
        in_specs=[pl.BlockSpec((group, rows, D_MODEL), lambda b, t: (b, t + tile_off, 0)),
                  pl.BlockSpec((group, HALO, D_MODEL),
                               lambda b, t: (b, jnp.maximum((t + tile_off) * hb - 1, 0), 0)),
                  pl.BlockSpec((group, HALO, D_MODEL),
                               lambda b, t: (b, jnp.minimum((t + tile_off + 1) * hb, n_hblk - 1), 0))]
        + mspecs(3) + mspecs(4) + mspecs(5)
        + [resident((D_MODEL, 2 * D_FF)), _full((1, 2 * D_FF)),
           _full((3, 2 * D_FF)), _full((1, 2 * D_FF)),
           resident((D_FF, D_MODEL)), _full((1, D_MODEL)),
           _full((1, D_MODEL)), _full((1, D_MODEL))],
        out_specs=pl.BlockSpec((group, rows, D_MODEL), lambda b, t: (b, t, 0)),
        out_shape=jax.ShapeDtypeStruct((batch, nt * rows, D_MODEL), F32),
        scratch_shapes=[pltpu.VMEM((group * ext, D_MODEL), F32),
                        pltpu.VMEM((group * ext, D_MODEL), BF16),
                        pltpu.VMEM((group * rows, D_FF), BF16)],
        compiler_params=_cparams("parallel", "arbitrary"),
        name="conv_ffn_ln",
    )(h, h, h, *([mods] * (3 * group)), w_up, b_up.reshape(1, -1), conv_w, conv_b.reshape(1, -1),
      w_down, b_down.reshape(1, -1), ln_g.reshape(1, -1), ln_b.reshape(1, -1))


def _time_major_rows(b, batch):
    return pl.ds(b, ROW_TILE, stride=batch)


def _inproj_ssm_kernel(h_ref, *rest, batch):
    g = BATCH_GROUP
    shift_refs, scale_refs = rest[:g], rest[g:2 * g]
    w_ref, z_ref, xbc_ref, dt_ref, us_ref, wbf_ref = rest[2 * g:]

    @pl.when(_first_step())
    def _():
        head = SSD_W + XBC_W
        wbf_ref[0:head, :] = w_ref[0:head, :].astype(BF16)
        wbf_ref[head:head + S5_W, :] = w_ref[head + DT_W:head + DT_W + S5_W, :].astype(BF16)
        pad = jnp.zeros((DT_PAD - DT_W, D_MODEL), F32)
        wbf_ref[head + S5_W:, :] = jnp.concatenate(
            [w_ref[head:head + DT_W, :], pad], axis=0).astype(BF16)

    u = _modulated_rows(lambda i: h_ref[i], shift_refs, scale_refs)
    p = lax.dot_general(u, wbf_ref[...], (((1,), (1,)), ((), ())), preferred_element_type=F32)
    _store_rows(z_ref, p[:, :SSD_W])
    _store_rows(xbc_ref, p[:, SSD_W:SSD_W + XBC_W])
    _store_rows(dt_ref, p[:, SSD_W + XBC_W + S5_W:])
    for i in range(g):
        rows = _time_major_rows(pl.program_id(1) * g + i, batch)
        for j in range(S5_W // 128):
            col0 = SSD_W + XBC_W + j * 128
            us_ref[j, rows, :] = p[i * ROW_TILE:(i + 1) * ROW_TILE, col0:col0 + 128]


def _inproj_ssm(h, mods, layer, w_in, ctx_tiles):
    batch, ta, _ = h.shape
    assert batch % BATCH_GROUP == 0
    nt = ta // ROW_TILE
    row = lambda w: pl.BlockSpec((BATCH_GROUP, ROW_TILE, w), lambda t, b: (b, t, 0))

    def mspecs(j):
        def spec(i):
            def idx(t, bg):
                r = jnp.where(t < ctx_tiles, batch, bg * BATCH_GROUP + i)
                return ((layer * MOD_ROWS + r) * 6 + j, 0, 0)
            return pl.BlockSpec((1, 1, D_MODEL), idx)
        return [spec(i) for i in range(BATCH_GROUP)]

    return pl.pallas_call(
        functools.partial(_inproj_ssm_kernel, batch=batch),
        grid=(nt, batch // BATCH_GROUP),
        in_specs=[row(D_MODEL)] + mspecs(0) + mspecs(1) + [_resident((SSM_IN_W, D_MODEL))],
        out_specs=[row(SSD_W), row(XBC_W), row(DT_PAD),
                   pl.BlockSpec((S5_W // 128, ROW_TILE * batch, 128), lambda t, b: (0, t, 0))],
        out_shape=[jax.ShapeDtypeStruct((batch, ta, SSD_W), F32),
                   jax.ShapeDtypeStruct((batch, ta, XBC_W), F32),
                   jax.ShapeDtypeStruct((batch, ta, DT_PAD), F32),
                   jax.ShapeDtypeStruct((S5_W // 128, ta * batch, 128), F32)],
        scratch_shapes=[pltpu.VMEM((SSM_IN_PAD, D_MODEL), BF16)],
        compiler_params=_cparams("arbitrary", "arbitrary"),
        name="inproj_ssm",
    )(h, *([mods] * (2 * BATCH_GROUP)), jnp.swapaxes(w_in, 0, 1))


def _cumsum_rows(v):
    n = v.shape[0]
    row = lax.broadcasted_iota(jnp.int32, v.shape, 0)
    s = 1
    while s < n:
        v = v + jnp.where(row >= s, pltpu.roll(v, s, 0), 0.0)
        s *= 2
    return v


def _expand_heads(v, e_ref):
    hi = v.astype(BF16)
    lo = (v - hi.astype(F32)).astype(BF16)
    e = e_ref[...]
    return (jnp.dot(hi, e, preferred_element_type=F32)
            + jnp.dot(lo, e, preferred_element_type=F32))


def _ssd_kernel(xbc_ref, xp_ref, xn_ref, dt_ref, z_ref, cw_ref, cb_ref, alog_ref, dtb_ref,
                dsk_ref, ng_ref, ef_ref, eb_ref, o_ref,
                xs_ref, cd_ref, st_ref, dec_ref, y_ref, *, n_tiles, ctx_tiles):
    s = pl.program_id(1)
    q = SSD_CHUNK
    cpt = ROW_TILE // q
    n_chunks = n_tiles * cpt
    ctx_chunks = ctx_tiles * cpt
    gw = SSD_W // SSD_GROUPS
    hpg = SSD_HEADS // SSD_GROUPS

    @pl.when(s < n_tiles)
    def _phase0():
        p = s
        seg_first = (p == 0) | (p == ctx_tiles)
        seg_last = (p == ctx_tiles - 1) | (p == n_tiles - 1)
        xr = xbc_ref[0]
        prev = jnp.where(seg_first, 0.0, xp_ref[0, HALO - 1:HALO, :])
        nxt = jnp.where(seg_last, 0.0, xn_ref[0, 0:1, :])
        row8 = lax.broadcasted_iota(jnp.int32, (8, XBC_W), 0)
        down = pltpu.roll(xr, 1, 0)
        up = pltpu.roll(xr, ROW_TILE - 1, 0)
        xm1 = jnp.concatenate([jnp.where(row8 == 0, prev, down[0:8]), down[8:]], axis=0)
        xp1 = jnp.concatenate([up[:ROW_TILE - 8], jnp.where(row8 == 7, nxt, up[ROW_TILE - 8:])],
                              axis=0)
        cw = cw_ref[...]
        xs_tile = _silu(cw[0:1] * xm1 + cw[1:2] * xr + cw[2:3] * xp1 + cb_ref[...])
        xs_ref[p] = xs_tile

        raw = dt_ref[0] + dtb_ref[...]
        dt_tile = jnp.maximum(raw, 0.0) + jnp.log1p(jnp.exp(-jnp.abs(raw)))
        a_row = -jnp.exp(alog_ref[...])
        for i in range(cpt):
            c = p * cpt + i
            xs = xs_tile[i * q:(i + 1) * q]
            dtv = dt_tile[i * q:(i + 1) * q]
            adt = dtv * a_row
            cum = _cumsum_rows(adt)
            tot = cum[q - 1:q, :]
            lane = lax.broadcasted_iota(jnp.int32, cum.shape, 1)
            cc = jnp.where(lane < SSD_HEADS, cum, tot - cum + adt)
            cd_ref[c, 0] = cc
            cd_ref[c, 1] = dtv
            w_end = jnp.exp(tot - cc) * dtv
            dec16 = jnp.broadcast_to(jnp.exp(tot), (16, DT_PAD))
            x = xs[:, :SSD_W]
            for d, e_ref in enumerate((ef_ref, eb_ref)):
                wx = (_expand_heads(w_end, e_ref) * x).astype(BF16)
                for g in range(SSD_GROUPS):
                    bmt = xs[:, SSD_W + g * SSD_STATE:SSD_W + (g + 1) * SSD_STATE].T.astype(BF16)
                    st_ref[c, d, :, g * gw:(g + 1) * gw] = jnp.dot(
                        bmt, wx[:, g * gw:(g + 1) * gw], preferred_element_type=F32)
                dec_ref[c, d] = _expand_heads(dec16, e_ref)[0:8]

    @pl.when(s == n_tiles)
    def _recurrence():
        fwd = list(range(n_chunks))
        bwd = list(range(ctx_chunks - 1, -1, -1)) + list(range(n_chunks - 1, ctx_chunks - 1, -1))
        for d, order in enumerate((fwd, bwd)):
            for col0 in range(0, SSD_W, 128):
                cols = slice(col0, col0 + 128)
                state = jnp.zeros((SSD_STATE, 128), F32)
                for ci in order:
                    contrib = st_ref[ci, d, :, cols]
                    st_ref[ci, d, :, cols] = state
                    state = state * dec_ref[ci, d, 0:1, cols] + contrib

    @pl.when(s >= n_tiles)
    def _phase1():
        p = s - n_tiles + ctx_tiles
        rowi = lax.broadcasted_iota(jnp.int32, (q, q), 0)
        coli = lax.broadcasted_iota(jnp.int32, (q, q), 1)
        lower = coli <= rowi
        upper = coli >= rowi
        lane = lax.broadcasted_iota(jnp.int32, (q, 128), 1)
        neg = jnp.float32(-jnp.inf)
        for i in range(cpt):
            c = p * cpt + i
            rows = slice(i * q, (i + 1) * q)
            xs = xs_ref[p, rows, :]
            x = xs[:, :SSD_W]
            cc = cd_ref[c, 0]
            dtv = cd_ref[c, 1]
            cct = cc.T
            dtt = dtv.T
            ecc = jnp.exp(cc)
            ef = _expand_heads(ecc, ef_ref)
            eb = _expand_heads(ecc, eb_ref)
            for g in range(SSD_GROUPS):
                bm = xs[:, SSD_W + g * SSD_STATE:SSD_W + (g + 1) * SSD_STATE].astype(BF16)
                cm = xs[:, SSD_W + (SSD_GROUPS + g) * SSD_STATE:
                        SSD_W + (SSD_GROUPS + g + 1) * SSD_STATE].astype(BF16)
                gmat = lax.dot_general(cm, bm, (((1,), (1,)), ((), ())),
                                       preferred_element_type=F32)
                sl = slice(g * gw, (g + 1) * gw)
                yoff = (ef[:, sl] * jnp.dot(cm, st_ref[c, 0, :, sl].astype(BF16),
                                            preferred_element_type=F32)
                        + eb[:, sl] * jnp.dot(cm, st_ref[c, 1, :, sl].astype(BF16),
                                              preferred_element_type=F32))
                for pair in range(hpg // 2):
                    col0 = g * gw + pair * 128
                    xpair = x[:, col0:col0 + 128].astype(BF16)
                    res = []
                    for hh in range(2):
                        hd = g * hpg + pair * 2 + hh
                        hb_ = SSD_HEADS + hd
                        lf = jnp.exp(jnp.where(lower, cc[:, hd:hd + 1] - cct[hd:hd + 1, :], neg))
                        lb = jnp.exp(jnp.where(upper, cc[:, hb_:hb_ + 1] - cct[hb_:hb_ + 1, :],
                                               neg))
                        mt = gmat * (lf * dtt[hd:hd + 1, :] + lb * dtt[hb_:hb_ + 1, :])
                        res.append(jnp.dot(mt.astype(BF16), xpair, preferred_element_type=F32))
                    ydiag = jnp.where(lane < SSD_HEAD_DIM, res[0], res[1])
                    y_ref[rows, col0:col0 + 128] = (
                        ydiag + yoff[:, pair * 128:(pair + 1) * 128]
                        + dsk_ref[:, col0:col0 + 128] * x[:, col0:col0 + 128])
        gated = y_ref[...] * _silu(z_ref[0])
        normed = gated * lax.rsqrt(jnp.mean(gated * gated, axis=-1, keepdims=True) + LN_EPS)
        o_ref[0] = (normed * ng_ref[...]).astype(BF16)


def _ssd(xbc, dt, z, conv_w, conv_b, a_log, dt_bias, d_skip, norm_g, n_ctx):
    batch, ta, _ = xbc.shape
    q = SSD_CHUNK
    n_chunks = ta // q
    n_tiles = ta // ROW_TILE
    ctx_tiles = n_ctx // ROW_TILE
    hb = ROW_TILE // HALO
    n_hblk = ta // HALO
    pad24 = lambda v: jnp.pad(v.reshape(1, DT_W), ((0, 0), (0, DT_PAD - DT_W)))
    heads = np.arange(SSD_HEADS)
    ef = np.zeros((DT_PAD, SSD_W), np.float32)
    eb = np.zeros((DT_PAD, SSD_W), np.float32)
    for hd in heads:
        ef[hd, hd * SSD_HEAD_DIM:(hd + 1) * SSD_HEAD_DIM] = 1.0
        eb[SSD_HEADS + hd, hd * SSD_HEAD_DIM:(hd + 1) * SSD_HEAD_DIM] = 1.0
    dsk = jnp.repeat(d_skip.astype(F32), SSD_HEAD_DIM).reshape(1, SSD_W)
    ph0 = lambda s: s < n_tiles
    tile = lambda w: pl.BlockSpec(
        (1, ROW_TILE, w), lambda b, s: (b, jnp.where(ph0(s), s, n_tiles - 1), 0))
    return pl.pallas_call(
        functools.partial(_ssd_kernel, n_tiles=n_tiles, ctx_tiles=ctx_tiles),
        grid=(batch, 2 * n_tiles - ctx_tiles),
        in_specs=[tile(XBC_W),
                  pl.BlockSpec((1, HALO, XBC_W),
                               lambda b, s: (b, jnp.where(ph0(s), jnp.maximum(s * hb - 1, 0), 0), 0)),
                  pl.BlockSpec((1, HALO, XBC_W),
                               lambda b, s: (b, jnp.where(ph0(s), jnp.minimum((s + 1) * hb, n_hblk - 1), 0), 0)),
                  tile(DT_PAD),
                  pl.BlockSpec((1, ROW_TILE, SSD_W),
                               lambda b, s: (b, jnp.where(ph0(s), 0, s - n_tiles + ctx_tiles), 0)),
                  _full((3, XBC_W)), _full((1, XBC_W)), _full((1, DT_PAD)), _full((1, DT_PAD)),
                  _full((1, SSD_W)), _full((1, SSD_W)),
                  _full((DT_PAD, SSD_W)), _full((DT_PAD, SSD_W))],
        out_specs=pl.BlockSpec(
            (1, ROW_TILE, SSD_W), lambda b, s: (b, jnp.where(ph0(s), 0, s - n_tiles), 0)),
        out_shape=jax.ShapeDtypeStruct((batch, ta - n_ctx, SSD_W), BF16),
        scratch_shapes=[pltpu.VMEM((n_tiles, ROW_TILE, XBC_W), F32),
                        pltpu.VMEM((n_chunks, 2, q, DT_PAD), F32),
                        pltpu.VMEM((n_chunks, 2, SSD_STATE, SSD_W), F32),
                        pltpu.VMEM((n_chunks, 2, 8, SSD_W), F32),
                        pltpu.VMEM((ROW_TILE, SSD_W), F32)],
        compiler_params=_cparams("parallel", "arbitrary"),
        name="ssd_bidir",
    )(xbc, xbc, xbc, dt, z, conv_w, conv_b.reshape(1, XBC_W), pad24(a_log), pad24(dt_bias),
      dsk, norm_g.reshape(1, SSD_W), jnp.asarray(ef, BF16), jnp.asarray(eb, BF16))


def _s5_disc_kernel(lr_ref, li_ref, ldt_ref, bre_ref, bim_ref, cre_ref, cim_ref,
                    a_ref, bd_ref, cd_ref):
    lr, li = lr_ref[...], li_ref[...]
    dt = jnp.exp(ldt_ref[...])
    mag = jnp.exp(dt * lr)
    ab_re, ab_im = mag * jnp.cos(dt * li), mag * jnp.sin(dt * li)
    den = lr * lr + li * li
    k_re = ((ab_re - 1.0) * lr + ab_im * li) / den
    k_im = (ab_im * lr - (ab_re - 1.0) * li) / den
    bre, bim = bre_ref[...], bim_ref[...]
    for d in range(2):
        a_ref[d, :, 0:S5_NSTATE] = jnp.broadcast_to(ab_re[d:d + 1], (8, S5_NSTATE))
        a_ref[d, :, S5_NSTATE:] = jnp.broadcast_to(ab_im[d:d + 1], (8, S5_NSTATE))
        kr, ki = k_re[d:d + 1], k_im[d:d + 1]
        bd_ref[d, :, 0:S5_NSTATE] = (kr * bre - ki * bim).astype(BF16)
        bd_ref[d, :, S5_NSTATE:] = (kr * bim + ki * bre).astype(BF16)
        cd_ref[d, 0:S5_NSTATE, :] = cre_ref[d].astype(BF16)
        cd_ref[d, S5_NSTATE:, :] = (-cim_ref[d]).astype(BF16)


def _s5_discretize(lam_re, lam_im, log_dt, b_re, b_im, c_re, c_im):
    eye = jnp.eye(S5_GROUPS, dtype=F32)
    bd = lambda b: jnp.einsum('gph,gk->ghkp', b, eye).reshape(S5_W, S5_NSTATE)
    cd = lambda cc: jnp.einsum('dghp,gk->dgpkh', cc, eye).reshape(2, S5_NSTATE, S5_W)
    ldt = jnp.repeat(log_dt, S5_STATE, axis=-1)
    return pl.pallas_call(
        _s5_disc_kernel,
        out_shape=[jax.ShapeDtypeStruct((2, 8, 2 * S5_NSTATE), F32),
                   jax.ShapeDtypeStruct((2, S5_W, 2 * S5_NSTATE), BF16),
                   jax.ShapeDtypeStruct((2, 2 * S5_NSTATE, S5_W), BF16)],
        compiler_params=pltpu.CompilerParams(vmem_limit_bytes=VMEM_LIMIT_BYTES),
        name="s5_discretize",
    )(lam_re.reshape(2, S5_NSTATE), lam_im.reshape(2, S5_NSTATE), ldt,
      bd(b_re), bd(b_im), cd(c_re), cd(c_im))


S5_TIME_CHUNK = 128


def _s5_scan_kernel(uf_ref, ub_ref, a_ref, bd_ref, cd_ref, yf_ref, yb_ref,
                    hsf_ref, hsb_ref, carry_ref, *, batch):
    j = pl.program_id(0)
    n = S5_NSTATE
    tc = S5_TIME_CHUNK
    n_slab = S5_W // 128
    half = hsf_ref.shape[0] // 2
    chains = ((uf_ref, hsf_ref, yf_ref), (ub_ref, hsb_ref, yb_ref))

    @pl.when(j == 0)
    def _():
        carry_ref[...] = jnp.zeros_like(carry_ref)

    for d, (u_ref, hs_ref, _) in enumerate(chains):
        for r in (0, half):
            u = jnp.concatenate([u_ref[s, r:r + half, :] for s in range(n_slab)], axis=1)
            hs_ref[r:r + half, :] = jnp.dot(u.astype(BF16), bd_ref[d], preferred_element_type=F32)

    for d, (_, hs_ref, _) in enumerate(chains):
        ar = jnp.broadcast_to(a_ref[d, 0:1, 0:n], (batch, n))
        ai = jnp.broadcast_to(a_ref[d, 0:1, n:], (batch, n))
        hr = carry_ref[d, :, 0:n]
        hi = carry_ref[d, :, n:]
        for step in (range(tc) if d == 0 else range(tc - 1, -1, -1)):
            rows = slice(step * batch, (step + 1) * batch)
            hr, hi = (ar * hr - ai * hi + hs_ref[rows, 0:n],
                      ar * hi + ai * hr + hs_ref[rows, n:])
            hs_ref[rows, 0:n] = hr
            hs_ref[rows, n:] = hi
        carry_ref[d, :, 0:n] = hr
        carry_ref[d, :, n:] = hi

    for d, (_, hs_ref, y_ref) in enumerate(chains):
        for r in (0, half):
            y = jnp.dot(hs_ref[r:r + half, :].astype(BF16), cd_ref[d], preferred_element_type=F32)
            for s in range(n_slab):
                y_ref[s, r:r + half, :] = y[:, s * 128:(s + 1) * 128]


def _s5_scan(us_flat, a, bd, cd, batch, n_ctx):
    n_slab, rows_total, _ = us_flat.shape
    ta = rows_total // batch
    tc = S5_TIME_CHUNK
    n_chunks = ta // tc
    ctx_chunks = n_ctx // tc
    blk = tc * batch

    def bwd_chunk(j):
        return jnp.where(j < ctx_chunks, ctx_chunks - 1 - j, n_chunks - 1 - (j - ctx_chunks))

    fwd_spec = pl.BlockSpec((n_slab, blk, 128), lambda j: (0, j, 0))
    bwd_spec = pl.BlockSpec((n_slab, blk, 128), lambda j: (0, bwd_chunk(j), 0))
    return pl.pallas_call(
        functools.partial(_s5_scan_kernel, batch=batch),
        grid=(n_chunks,),
        in_specs=[fwd_spec, bwd_spec, _full((2, 8, 2 * S5_NSTATE)),
                  _full((2, S5_W, 2 * S5_NSTATE)), _full((2, 2 * S5_NSTATE, S5_W))],
        out_specs=[fwd_spec, bwd_spec],
        out_shape=[jax.ShapeDtypeStruct((n_slab, rows_total, 128), F32)] * 2,
        scratch_shapes=[pltpu.VMEM((blk, 2 * S5_NSTATE), F32),
                        pltpu.VMEM((blk, 2 * S5_NSTATE), F32),
                        pltpu.VMEM((2, batch, 2 * S5_NSTATE), F32)],
        compiler_params=_cparams("arbitrary"),
        name="s5_scan",
    )(us_flat, us_flat, a, bd, cd)


def _merge_ln_kernel(gs_ref, yf_ref, yb_ref, us_ref, h_ref, *rest, batch):
    g = BATCH_GROUP
    gate_refs = rest[:g]
    dd_ref, gw_ref, gb_ref, w_ref, g_ref, beta_ref, o_ref, wbf_ref = rest[g:]

    @pl.when(_first_step())
    def _():
        wbf_ref[...] = w_ref[...].astype(BF16)

    def s5_input(i):
        rows = _time_major_rows(pl.program_id(1) * g + i, batch)
        y5 = jnp.concatenate([yf_ref[s, rows, :] + yb_ref[s, rows, :]
                              for s in range(S5_W // 128)], axis=1)
        us = jnp.concatenate([us_ref[s, rows, :] for s in range(S5_W // 128)], axis=1)
        return y5 + dd_ref[...] * us

    ge = jax.nn.gelu(jnp.concatenate([s5_input(i) for i in range(g)], axis=0))
    s5 = ge * jax.nn.sigmoid(
        jnp.dot(ge.astype(BF16), gw_ref[...], preferred_element_type=F32) + gb_ref[...])
    gs = jnp.concatenate([gs_ref[i] for i in range(g)], axis=0)
    y = jnp.dot(gs, wbf_ref[0:SSD_W, :], preferred_element_type=F32)
    y = y + jnp.dot(s5.astype(BF16), wbf_ref[SSD_W:, :], preferred_element_type=F32)
    rows = o_ref.shape[1]
    for i in range(g):
        o_ref[i] = _layer_norm(ALPHA * h_ref[i] + gate_refs[i][0] * y[i * rows:(i + 1) * rows],
                               g_ref[...], beta_ref[...])


def _merge_ln(g_ssd, y5_fwd, y5_bwd, us_t, h, mods, layer, s5_d, glu_w, glu_b, w_out, ln_g, ln_b,
              ctx_tiles):
    batch, ta, _ = h.shape
    nt = ta // ROW_TILE - ctx_tiles
    n_slab = S5_W // 128
    tm_rows = ROW_TILE * batch
    g = BATCH_GROUP
    gate_specs = [
        pl.BlockSpec((1, 1, D_MODEL),
                     lambda t, bg, i=i: ((layer * MOD_ROWS + bg * g + i) * 6 + 2, 0, 0))
        for i in range(g)]
    return pl.pallas_call(
        functools.partial(_merge_ln_kernel, batch=batch),
        grid=(nt, batch // g),
        in_specs=[pl.BlockSpec((g, ROW_TILE, SSD_W), lambda t, b: (b, t, 0)),
                  pl.BlockSpec((n_slab, tm_rows, 128), lambda t, b: (0, t + ctx_tiles, 0)),
                  pl.BlockSpec((n_slab, tm_rows, 128), lambda t, b: (0, t + ctx_tiles, 0)),
                  pl.BlockSpec((n_slab, tm_rows, 128), lambda t, b: (0, t + ctx_tiles, 0)),
                  pl.BlockSpec((g, ROW_TILE, D_MODEL), lambda t, b: (b, t + ctx_tiles, 0))]
        + gate_specs
        + [_full((1, S5_W)), _full((S5_W, S5_W)), _full((1, S5_W)),
           _resident((D_MODEL, D_MODEL)), _full((1, D_MODEL)), _full((1, D_MODEL))],
        out_specs=pl.BlockSpec((g, ROW_TILE, D_MODEL), lambda t, b: (b, t, 0)),
        out_shape=jax.ShapeDtypeStruct((batch, nt * ROW_TILE, D_MODEL), F32),
        scratch_shapes=[pltpu.VMEM((D_MODEL, D_MODEL), BF16)],
        compiler_params=_cparams("arbitrary", "arbitrary"),
        name="merge_outproj_ln",
    )(g_ssd, y5_fwd, y5_bwd, us_t, h, *([mods] * g), s5_d.reshape(1, S5_W), glu_w.astype(BF16),
      glu_b.reshape(1, S5_W), w_out, ln_g.reshape(1, -1), ln_b.reshape(1, -1))


def _attn_layer(h_ctx, h_lat, mods, layer, i, p, keep_ctx):
    n_ctx = h_ctx.shape[1]
    ctx_tiles = n_ctx // ROW_TILE
    lam_init = 0.8 - 0.6 * math.exp(-0.3 * layer)
    cos, sin = _rope_tables(h_lat.shape[1], n_ctx)
    q, k, v, f = _inproj_attn(h_ctx, h_lat, mods, layer, p['attn_w_in'][i],
                              cos, sin, ctx_tiles)
    o_ctx, o_lat = _attention(q, k, v, p['attn_lambda'][i], p['attn_subln_g'][i], lam_init,
                              ctx_tiles)
    fm_ctx, fm_lat = _fourier(f, p['fourier_w'][i], p['fourier_b'][i], n_ctx)
    h1 = _outproj_ln(o_ctx, o_lat, fm_ctx, fm_lat, h_ctx, h_lat, mods, layer,
                     p['attn_w_out'][i],
                     p['ln_g'][layer, 0], p['ln_b'][layer, 0], ctx_tiles)
    return _ffn(h1, mods, layer, p['ffn_w_up_bf16'], p['ffn_b_up'][layer],
                p['ffn_conv_w'][layer], p['ffn_conv_b'][layer],
                p['ffn_w_down_bf16'], p['ffn_b_down'][layer],
                p['ln_g'][layer, 1], p['ln_b'][layer, 1], ctx_tiles, 0 if keep_ctx else ctx_tiles,
                group=BATCH_GROUP)


def _ssm_layer(h, mods, layer, i, n_ctx, p, keep_ctx):
    assert not keep_ctx, "an SSM layer that must also emit context rows is not implemented"
    ctx_tiles = n_ctx // ROW_TILE
    batch = h.shape[0]
    z, xbc, dt, us_t = _inproj_ssm(h, mods, layer, p['ssm_w_in'][i], ctx_tiles)
    g_ssd = _ssd(xbc, dt, z, p['ssd_conv_w'][i], p['ssd_conv_b'][i], p['ssd_a_log'][i],
                 p['ssd_dt_bias'][i], p['ssd_d'][i], p['ssd_norm_g'][i], n_ctx)
    a, bd, cd = _s5_discretize(p['s5_lambda_re'][i], p['s5_lambda_im'][i], p['s5_log_dt'][i],
                               p['s5_b_re'][i], p['s5_b_im'][i], p['s5_c_re'][i], p['s5_c_im'][i])
    y5_fwd, y5_bwd = _s5_scan(us_t, a, bd, cd, batch, n_ctx)
    h1 = _merge_ln(g_ssd, y5_fwd, y5_bwd, us_t, h, mods, layer, p['s5_d'][i], p['s5_glu_w'][i],
                   p['s5_glu_b'][i], p['ssm_w_out'][i],
                   p['ln_g'][layer, 0], p['ln_b'][layer, 0], ctx_tiles)
    return _ffn(h1, mods, layer, p['ffn_w_up_bf16'], p['ffn_b_up'][layer],
                p['ffn_conv_w'][layer], p['ffn_conv_b'][layer],
                p['ffn_w_down_bf16'], p['ffn_b_down'][layer],
                p['ln_g'][layer, 1], p['ln_b'][layer, 1], 0, 0, rows=FFN_LAT_ROWS)


def kernel(x, c, ctx, c_ctx, ada_w, ada_b, ln_g, ln_b, ffn_w_up, ffn_b_up, ffn_conv_w, ffn_conv_b, ffn_w_down, ffn_b_down, attn_w_in, attn_lambda, attn_subln_g, fourier_w, fourier_b, attn_w_out, ssm_w_in, ssd_conv_w, ssd_conv_b, ssd_a_log, ssd_dt_bias, ssd_d, ssd_norm_g, s5_lambda_re, s5_lambda_im, s5_log_dt, s5_b_re, s5_b_im, s5_c_re, s5_c_im, s5_d, s5_glu_w, s5_glu_b, ssm_w_out):
    p = dict(ln_g=ln_g, ln_b=ln_b, ffn_w_up=ffn_w_up, ffn_b_up=ffn_b_up, ffn_conv_w=ffn_conv_w,
             ffn_conv_b=ffn_conv_b, ffn_w_down=ffn_w_down, ffn_b_down=ffn_b_down,
             attn_w_in=attn_w_in, attn_lambda=attn_lambda, attn_subln_g=attn_subln_g,
             fourier_w=fourier_w, fourier_b=fourier_b, attn_w_out=attn_w_out, ssm_w_in=ssm_w_in,
             ssd_conv_w=ssd_conv_w, ssd_conv_b=ssd_conv_b, ssd_a_log=ssd_a_log,
             ssd_dt_bias=ssd_dt_bias, ssd_d=ssd_d, ssd_norm_g=ssd_norm_g,
             s5_lambda_re=s5_lambda_re, s5_lambda_im=s5_lambda_im, s5_log_dt=s5_log_dt,
             s5_b_re=s5_b_re, s5_b_im=s5_b_im, s5_c_re=s5_c_re, s5_c_im=s5_c_im, s5_d=s5_d,
             s5_glu_w=s5_glu_w, s5_glu_b=s5_glu_b, ssm_w_out=ssm_w_out)
    batch, n_lat, _ = x.shape
    n_ctx = ctx.shape[1]
    assert n_ctx == ROW_TILE and n_lat % ROW_TILE == 0 and batch < MOD_ROWS
    mods = _ada_mods(c, c_ctx, ada_w, ada_b)
    p['ffn_w_up_bf16'] = ffn_w_up.astype(BF16)
    p['ffn_w_down_bf16'] = ffn_w_down.astype(BF16)
    assert DEPTH == 2
    h = _attn_layer(ctx, x, mods, 0, 0, p, keep_ctx=True)
    return _ssm_layer(h, mods, 1, 0, n_ctx, p, keep_ctx=False)
```

```python
import functools
import math

import numpy as np
import jax
import jax.numpy as jnp
from jax import lax
from jax.experimental import pallas as pl
from jax.experimental.pallas import tpu as pltpu

F32 = jnp.float32
BF16 = jnp.bfloat16

D_MODEL = 1024
DEPTH = 2
GRID_W = 64
ROPE_BASE = 10000.0
LN_EPS = 1e-5
ALPHA = (2 * DEPTH) ** 0.25
ATTN_W = 768
ATTN_HEADS = 6
ATTN_HEAD_DIM = 64
F_W = 256
F_GROUPS = 4
F_GROUP_W = 64
ATTN_IN_W = 2 * ATTN_W + ATTN_W + F_W
SSD_W = 768
SSD_HEADS = 12
SSD_HEAD_DIM = 64
SSD_GROUPS = 2
SSD_STATE = 128
SSD_CHUNK = 128
XBC_W = SSD_W + 2 * SSD_GROUPS * SSD_STATE
DT_W = 2 * SSD_HEADS
DT_PAD = 128
S5_W = 256
S5_GROUPS = 16
S5_STATE = 64
S5_NSTATE = S5_GROUPS * S5_STATE
SSM_IN_W = SSD_W + XBC_W + DT_W + S5_W
SSM_IN_PAD = SSD_W + XBC_W + S5_W + DT_PAD
D_FF = 2816
FF_TILE = 256

ROW_TILE = 256
ATTN_Q_SUB = 256
ATTN_ITEM_TILES = 2
FFN_LAT_ROWS = 512
BATCH_GROUP = 2
FOURIER_TILE = 1024
HALO = 8
MOD_ROWS = 16
VMEM_LIMIT_BYTES = 56 * 1024 * 1024


def _cparams(*sem):
    return pltpu.CompilerParams(dimension_semantics=sem, vmem_limit_bytes=VMEM_LIMIT_BYTES)


def _silu(v):
    return v * jax.nn.sigmoid(v)


def _layer_norm(v, g, b):
    mu = jnp.mean(v, axis=-1, keepdims=True)
    d = v - mu
    var = jnp.mean(d * d, axis=-1, keepdims=True)
    return d * lax.rsqrt(var + LN_EPS) * g + b


def _full(shape):
    nd = len(shape)
    return pl.BlockSpec(shape, lambda *_: (0,) * nd)


def _resident(shape):
    nd = len(shape)
    return pl.BlockSpec(shape, lambda *_: (0,) * nd, pipeline_mode=pl.Buffered(1))


def _first_step():
    return (pl.program_id(0) == 0) & (pl.program_id(1) == 0)


def _mod_specs_grouped(layer, j, batch, ctx_tiles):
    def spec(i):
        def idx(bg, t):
            row = jnp.where(t < ctx_tiles, batch, bg * BATCH_GROUP + i)
            return ((layer * MOD_ROWS + row) * 6 + j, 0, 0)
        return pl.BlockSpec((1, 1, D_MODEL), idx)
    return [spec(i) for i in range(BATCH_GROUP)]


def _ada_kernel(c_ref, w_ref, b_ref, o_ref):
    s = _silu(c_ref[...])
    w = w_ref[0]
    s_hi = s.astype(BF16)
    s_lo = (s - s_hi.astype(F32)).astype(BF16)
    w_hi = w.astype(BF16)
    w_lo = (w - w_hi.astype(F32)).astype(BF16)
    dot = functools.partial(jnp.dot, preferred_element_type=F32)
    o_ref[0] = dot(s_hi, w_hi) + (dot(s_lo, w_hi) + dot(s_hi, w_lo)) + b_ref[0]


def _ada_mods(c, c_ctx, ada_w, ada_b):
    batch = c.shape[0]
    nl = ada_w.shape[0]
    c_all = jnp.concatenate(
        [c, c_ctx[None], jnp.zeros((MOD_ROWS - batch - 1, D_MODEL), F32)], axis=0)
    out = pl.pallas_call(
        _ada_kernel,
        grid=(nl, 6),
        in_specs=[_full((MOD_ROWS, D_MODEL)),
                  pl.BlockSpec((1, D_MODEL, D_MODEL), lambda l, j: (l, 0, j)),
                  pl.BlockSpec((1, 1, D_MODEL), lambda l, j: (l, 0, j))],
        out_specs=pl.BlockSpec((1, MOD_ROWS, D_MODEL), lambda l, j: (l, 0, j)),
        out_shape=jax.ShapeDtypeStruct((nl, MOD_ROWS, 6 * D_MODEL), F32),
        compiler_params=_cparams("arbitrary", "arbitrary"),
        name="ada_mods",
    )(c_all, ada_w, ada_b.reshape(nl, 1, 6 * D_MODEL))
    return out.reshape(nl * MOD_ROWS * 6, 1, D_MODEL)


def _modulated_rows(h_of, shift_refs, scale_refs):
    return jnp.concatenate(
        [(h_of(i) * (1.0 + scale_refs[i][0]) + shift_refs[i][0]).astype(BF16)
         for i in range(BATCH_GROUP)], axis=0)


def _store_rows(ref, val):
    rows = ref.shape[1]
    for i in range(BATCH_GROUP):
        ref[i] = val[i * rows:(i + 1) * rows].astype(ref.dtype)


def _inproj_attn_kernel(hc_ref, hl_ref, *rest, ctx_tiles):
    g = BATCH_GROUP
    shift_refs, scale_refs = rest[:g], rest[g:2 * g]
    w_ref, cos_ref, sin_ref, q_ref, k_ref, v_ref, f_ref, wbf_ref = rest[2 * g:]

    @pl.when(_first_step())
    def _():
        wbf_ref[...] = w_ref[...].astype(BF16)

    is_ctx = pl.program_id(1) < ctx_tiles
    u = _modulated_rows(lambda i: jnp.where(is_ctx, hc_ref[i], hl_ref[i]), shift_refs, scale_refs)
    p = jnp.dot(u, wbf_ref[...], preferred_element_type=F32)
    cos = jnp.concatenate([cos_ref[...]] * g, axis=0)
    sin = jnp.concatenate([sin_ref[...]] * g, axis=0)
    lane = lax.broadcasted_iota(jnp.int32, cos.shape, 1)
    first_half = (lane % ATTN_HEAD_DIM) < (ATTN_HEAD_DIM // 2)

    def rope(blk):
        partner = jnp.where(first_half, pltpu.roll(blk, 128 - 32, 1), pltpu.roll(blk, 32, 1))
        return blk * cos + partner * sin

    qk_scale = ATTN_HEAD_DIM ** -0.5 * math.log2(math.e)
    rows = q_ref.shape[1]
    for i in range(ATTN_HEADS):
        lo, hi = i * 128, (i + 1) * 128
        qh = (rope(p[:, lo:hi]) * qk_scale).astype(BF16)
        kh = rope(p[:, ATTN_W + lo:ATTN_W + hi]).astype(BF16)
        for b in range(g):
            q_ref[b, :, lo:hi] = qh[b * rows:(b + 1) * rows]
            k_ref[b, :, lo:hi] = kh[b * rows:(b + 1) * rows]
    _store_rows(v_ref, p[:, 2 * ATTN_W:3 * ATTN_W])
    _store_rows(f_ref, p[:, 3 * ATTN_W:])


def _rope_tables(n_lat, n_ctx):
    rows = n_lat // GRID_W
    row = jnp.repeat(jnp.arange(rows, dtype=F32), GRID_W)
    col = jnp.tile(jnp.arange(GRID_W, dtype=F32), rows)
    n_freq = ATTN_HEAD_DIM // 4
    inv_freq = ROPE_BASE ** (-jnp.arange(n_freq, dtype=F32) / n_freq)
    ang = jnp.concatenate([row[:, None] * inv_freq, col[:, None] * inv_freq], axis=-1)
    cos, sin = jnp.cos(ang), jnp.sin(ang)
    cos128 = jnp.tile(cos, (1, 4))
    sin128 = jnp.tile(jnp.concatenate([-sin, sin], axis=-1), (1, 2))
    cos_all = jnp.concatenate([jnp.ones((n_ctx, 128), F32), cos128], axis=0)
    sin_all = jnp.concatenate([jnp.zeros((n_ctx, 128), F32), sin128], axis=0)
    return cos_all, sin_all


def _split_specs(width, ctx_tiles):
    g = BATCH_GROUP
    return [pl.BlockSpec((g, ROW_TILE, width), lambda b, t: (b, jnp.minimum(t, ctx_tiles - 1), 0)),
            pl.BlockSpec((g, ROW_TILE, width), lambda b, t: (b, jnp.maximum(t - ctx_tiles, 0), 0))]


def _inproj_attn(h_ctx, h_lat, mods, layer, w_in, cos, sin, ctx_tiles):
    batch = h_lat.shape[0]
    assert batch % BATCH_GROUP == 0
    ta = h_ctx.shape[1] + h_lat.shape[1]
    nt = ta // ROW_TILE
    row = lambda w: pl.BlockSpec((BATCH_GROUP, ROW_TILE, w), lambda b, t: (b, t, 0))
    tab = pl.BlockSpec((ROW_TILE, 128), lambda b, t: (t, 0))
    n_mod = 2 * BATCH_GROUP
    return pl.pallas_call(
        functools.partial(_inproj_attn_kernel, ctx_tiles=ctx_tiles),
        grid=(batch // BATCH_GROUP, nt),
        in_specs=_split_specs(D_MODEL, ctx_tiles)
        + _mod_specs_grouped(layer, 0, batch, ctx_tiles)
        + _mod_specs_grouped(layer, 1, batch, ctx_tiles)
        + [_resident((D_MODEL, ATTN_IN_W)), tab, tab],
        out_specs=[row(ATTN_W), row(ATTN_W), row(ATTN_W), row(F_W)],
        out_shape=[jax.ShapeDtypeStruct((batch, ta, ATTN_W), BF16)] * 3
        + [jax.ShapeDtypeStruct((batch, ta, F_W), BF16)],
        scratch_shapes=[pltpu.VMEM((D_MODEL, ATTN_IN_W), BF16)],
        compiler_params=_cparams("arbitrary", "arbitrary"),
        name="inproj_attn",
    )(h_ctx, h_lat, *([mods] * n_mod), w_in, cos, sin)


def _diff_lambda(lam_ref, lam_init):
    lamv = lam_ref[...]
    l1 = jnp.sum(lamv[0:1] * lamv[1:2], axis=-1, keepdims=True)
    l2 = jnp.sum(lamv[2:3] * lamv[3:4], axis=-1, keepdims=True)
    return jnp.exp(l1) - jnp.exp(l2) + lam_init


def _stack_maps(q):
    lane = lax.broadcasted_iota(jnp.int32, q.shape, 1)
    zero = jnp.zeros_like(q)
    return jnp.concatenate([jnp.where(lane < ATTN_HEAD_DIM, q, zero),
                            jnp.where(lane >= ATTN_HEAD_DIM, q, zero)], axis=0)


def _scores(q2, k):
    return lax.dot_general(q2, k, (((1,), (1,)), ((), ())), preferred_element_type=F32)


def _diff_softmax_pv(load_s0, m0, load_s1, m1, lam, v, gain):
    e0 = jnp.exp2(load_s0() - m0)
    e1 = jnp.exp2(load_s1() - m1)
    l0 = jnp.sum(e0, axis=-1, keepdims=True)
    l1 = jnp.sum(e1, axis=-1, keepdims=True)
    w = e0 - e1 * (lam * l0 / l1)
    o = jnp.dot(w.astype(BF16), v, preferred_element_type=F32) * (1.0 / l0)
    o = o * lax.rsqrt(jnp.mean(o * o, axis=-1, keepdims=True) + LN_EPS)
    return (o * gain).astype(BF16)


def _attn_ctx_kernel(lam_ref, g_ref, q_ref, k_ref, v_ref, o_ref, *, lam_init):
    lam = _diff_lambda(lam_ref, lam_init)
    gain = g_ref[...] * (1.0 - lam_init)
    tq = q_ref.shape[1]
    for b in range(q_ref.shape[0]):
        for hd in range(ATTN_HEADS):
            cols = slice(hd * 128, (hd + 1) * 128)
            s = _scores(_stack_maps(q_ref[b, :, cols]), k_ref[b, :, cols])
            m = jnp.max(s, axis=-1, keepdims=True)

            def half(lo, s=s):
                return lambda: s[lo:lo + tq]

            o_ref[b, :, cols] = _diff_softmax_pv(half(0), m[:tq], half(tq), m[tq:], lam,
                                                 v_ref[b, :, cols], gain)


def _attn_lat_kernel(lam_ref, g_ref, k_ref, v_ref, *rest, lam_init):
    q_refs = rest[:ATTN_ITEM_TILES]
    o_ref, sa_ref, ma_ref, sb_ref, mb_ref = rest[ATTN_ITEM_TILES:]
    t = pl.program_id(0)
    tq = ATTN_ITEM_TILES * ROW_TILE
    sub = ATTN_Q_SUB

    @pl.when(t == 0)
    def _():
        sb_ref[...] = jnp.zeros_like(sb_ref)
        mb_ref[...] = jnp.zeros_like(mb_ref)

    def step(s_new, m_new, s_old, m_old):
        lam = _diff_lambda(lam_ref, lam_init)
        gain = g_ref[...] * (1.0 - lam_init)
        q = jnp.concatenate([q_ref[0] for q_ref in q_refs], axis=0)
        s = _scores(_stack_maps(q), k_ref[0])
        s_new[...] = s
        m_new[...] = jnp.max(s, axis=-1, keepdims=True)
        v = v_ref[0]
        for i in range(tq // sub):
            r0 = slice(i * sub, (i + 1) * sub)
            r1 = slice(tq + i * sub, tq + (i + 1) * sub)
            o_ref[0, r0, :] = _diff_softmax_pv(
                functools.partial(s_old.__getitem__, (r0, slice(None))), m_old[r0, :],
                functools.partial(s_old.__getitem__, (r1, slice(None))), m_old[r1, :],
                lam, v, gain)

    @pl.when(t % 2 == 0)
    def _():
        step(sa_ref, ma_ref, sb_ref, mb_ref)

    @pl.when(t % 2 == 1)
    def _():
        step(sb_ref, mb_ref, sa_ref, ma_ref)


def _attention(q, k, v, lam_vec, subln_g, lam_init, ctx_tiles):
    batch, ta, _ = q.shape
    n_ctx = ctx_tiles * ROW_TILE
    nt = ta // ROW_TILE - ctx_tiles
    g = subln_g.reshape(1, 128)
    small = [_full((4, ATTN_HEAD_DIM)), _full((1, 128))]
    cspec = pl.BlockSpec((BATCH_GROUP, n_ctx, ATTN_W), lambda b: (b, 0, 0))
    o_ctx = pl.pallas_call(
        functools.partial(_attn_ctx_kernel, lam_init=lam_init),
        grid=(batch // BATCH_GROUP,),
        in_specs=small + [cspec, cspec, cspec],
        out_specs=cspec,
        out_shape=jax.ShapeDtypeStruct((batch, n_ctx, ATTN_W), BF16),
        compiler_params=_cparams("arbitrary"),
        name="diff_attention_ctx",
    )(lam_vec, g, q, k, v)

    per = ATTN_ITEM_TILES
    assert nt % per == 0
    ni = nt // per
    n_items = batch * ATTN_HEADS * ni

    def item(j):
        return j // (ATTN_HEADS * ni), (j // ni) % ATTN_HEADS, j % ni

    def score_item(j):
        return item(jnp.minimum(j, n_items - 1))

    def finish_item(j):
        return item(jnp.maximum(j - 1, 0))

    def q_spec(part):
        def idx(j):
            b, h, t = score_item(j)
            return b, ctx_tiles + t * per + part, h
        return pl.BlockSpec((1, ROW_TILE, 128), idx)

    def k_idx(j):
        b, h, _ = score_item(j)
        return b, 0, h

    def v_idx(j):
        b, h, _ = finish_item(j)
        return b, 0, h

    def o_idx(j):
        b, h, t = finish_item(j)
        return b, t, h

    o_lat = pl.pallas_call(
        functools.partial(_attn_lat_kernel, lam_init=lam_init),
        grid=(n_items + 1,),
        in_specs=small + [pl.BlockSpec((1, ta, 128), k_idx), pl.BlockSpec((1, ta, 128), v_idx)]
        + [q_spec(part) for part in range(per)],
        out_specs=pl.BlockSpec((1, per * ROW_TILE, 128), o_idx),
        out_shape=jax.ShapeDtypeStruct((batch, nt * ROW_TILE, ATTN_W), BF16),
        scratch_shapes=[pltpu.VMEM((2 * per * ROW_TILE, ta), F32),
                        pltpu.VMEM((2 * per * ROW_TILE, 1), F32)] * 2,
        compiler_params=_cparams("arbitrary"),
        name="diff_attention",
    )(lam_vec, g, k, v, *([q] * per))
    return o_ctx, o_lat


def _dft_tables(n):
    k = np.arange(n, dtype=np.int64)
    ang = 2.0 * np.pi * ((k[:, None] * k[None, :]) % n).astype(np.float64) / n
    return np.cos(ang), np.sin(ang)


def _fourier_kernel(f_ref, cs_ref, dl_ref, dc_ref, w_ref, b_ref, oc_ref, ol_ref, ac_ref, al_ref,
                    *, n_ctx, n_lat):
    t = pl.program_id(0)
    b = pl.program_id(1)

    def stage1(rows0, n):
        a = jnp.dot(f_ref[0, rows0:rows0 + n, :], cs_ref[...], preferred_element_type=F32)
        return a[:, :F_W].astype(BF16), a[:, F_W:].astype(BF16)

    def stage2(dft, a, n):
        z = jnp.dot(dft, a, preferred_element_type=F32)
        z = z * (1.0 / math.sqrt(n * F_GROUP_W))
        o = jnp.dot(z.astype(BF16), w_ref[...], preferred_element_type=F32) + b_ref[...]
        return o.astype(BF16)

    @pl.when(t == 0)
    def _():
        ac_ref[0:n_ctx, :], ac_ref[n_ctx:, :] = stage1(0, n_ctx)
        oc_ref[0] = stage2(dc_ref[...], ac_ref[...], n_ctx)

    @pl.when(t == 1)
    def _():
        al_ref[b, 0:n_lat, :], al_ref[b, n_lat:, :] = stage1(n_ctx, n_lat)

    @pl.when(t >= 1)
    def _():
        ol_ref[0] = stage2(dl_ref[...], al_ref[b], n_lat)


def _fourier(f, fourier_w, fourier_b, n_ctx):
    batch, ta, _ = f.shape
    n_lat = ta - n_ctx
    ft = min(FOURIER_TILE, n_lat)
    assert n_lat % ft == 0
    cc, sc = _dft_tables(F_GROUP_W)
    eye = np.eye(F_GROUPS)
    cs = jnp.asarray(np.concatenate([np.kron(eye, cc), np.kron(eye, sc)], axis=1), BF16)
    cl, sl = _dft_tables(n_lat)
    dft_lat = jnp.asarray(np.concatenate([cl, -sl], axis=1), BF16)
    cx, sx = _dft_tables(n_ctx)
    dft_ctx = jnp.asarray(np.concatenate([cx, -sx], axis=1), BF16)
    w_blk = jnp.einsum('gce,gh->gche', fourier_w, jnp.eye(F_GROUPS, dtype=F32))
    w_blk = w_blk.reshape(F_W, F_W).astype(BF16)
    return pl.pallas_call(
        functools.partial(_fourier_kernel, n_ctx=n_ctx, n_lat=n_lat),
        grid=(1 + n_lat // ft, batch),
        in_specs=[
                  pl.BlockSpec((1, ta, F_W),
                               lambda t, b: (jnp.where(t <= 1, b, batch - 1), 0, 0)),
                  _full((F_W, 2 * F_W)),
                  pl.BlockSpec((ft, 2 * n_lat), lambda t, b: (jnp.maximum(t - 1, 0), 0)),
                  _full((n_ctx, 2 * n_ctx)),
                  _full((F_W, F_W)), _full((1, F_W))],
        out_specs=[pl.BlockSpec((1, n_ctx, F_W),
                                lambda t, b: (jnp.where(t == 0, b, batch - 1), 0, 0)),
                   pl.BlockSpec((1, ft, F_W),
                                lambda t, b: (jnp.where(t == 0, 0, b), jnp.maximum(t - 1, 0), 0))],
        out_shape=[jax.ShapeDtypeStruct((batch, n_ctx, F_W), BF16),
                   jax.ShapeDtypeStruct((batch, n_lat, F_W), BF16)],
        scratch_shapes=[pltpu.VMEM((2 * n_ctx, F_W), BF16),
                        pltpu.VMEM((batch, 2 * n_lat, F_W), BF16)],
        compiler_params=_cparams("arbitrary", "arbitrary"),
        name="fourier_mix",
    )(f, cs, dft_lat, dft_ctx, w_blk, fourier_b.reshape(1, F_W))


def _outproj_ln_kernel(ac_ref, al_ref, bc_ref, bl_ref, hc_ref, hl_ref, *rest, ctx_tiles):
    g = BATCH_GROUP
    gate_refs = rest[:g]
    w_ref, g_ref, beta_ref, o_ref, wbf_ref = rest[g:]

    @pl.when(_first_step())
    def _():
        wbf_ref[...] = w_ref[...].astype(BF16)

    is_ctx = pl.program_id(1) < ctx_tiles
    a = jnp.concatenate([jnp.where(is_ctx, ac_ref[i], al_ref[i]) for i in range(g)], axis=0)
    b2 = jnp.concatenate([jnp.where(is_ctx, bc_ref[i], bl_ref[i]) for i in range(g)], axis=0)
    wa = a.shape[1]
    y = jnp.dot(a, wbf_ref[0:wa, :], preferred_element_type=F32)
    y = y + jnp.dot(b2, wbf_ref[wa:, :], preferred_element_type=F32)
    rows = o_ref.shape[1]
    for i in range(g):
        h = jnp.where(is_ctx, hc_ref[i], hl_ref[i])
        o_ref[i] = _layer_norm(ALPHA * h + gate_refs[i][0] * y[i * rows:(i + 1) * rows],
                               g_ref[...], beta_ref[...])


def _outproj_ln(a_ctx, a_lat, b_ctx, b_lat, h_ctx, h_lat, mods, layer, w_out, ln_g, ln_b,
                ctx_tiles):
    batch = h_lat.shape[0]
    ta = h_ctx.shape[1] + h_lat.shape[1]
    nt = ta // ROW_TILE
    row = lambda w: pl.BlockSpec((BATCH_GROUP, ROW_TILE, w), lambda b, t: (b, t, 0))
    return pl.pallas_call(
        functools.partial(_outproj_ln_kernel, ctx_tiles=ctx_tiles),
        grid=(batch // BATCH_GROUP, nt),
        in_specs=_split_specs(a_lat.shape[2], ctx_tiles) + _split_specs(b_lat.shape[2], ctx_tiles)
        + _split_specs(D_MODEL, ctx_tiles)
        + _mod_specs_grouped(layer, 2, batch, ctx_tiles)
        + [_resident((D_MODEL, D_MODEL)), _full((1, D_MODEL)), _full((1, D_MODEL))],
        out_specs=row(D_MODEL),
        out_shape=jax.ShapeDtypeStruct((batch, ta, D_MODEL), F32),
        scratch_shapes=[pltpu.VMEM((D_MODEL, D_MODEL), BF16)],
        compiler_params=_cparams("arbitrary", "arbitrary"),
        name="outproj_ln",
    )(a_ctx, a_lat, b_ctx, b_lat, h_ctx, h_lat, *([mods] * BATCH_GROUP), w_out,
      ln_g.reshape(1, D_MODEL),
      ln_b.reshape(1, D_MODEL))


def _ffn_kernel(h_ref, hp_ref, hn_ref, *rest, group, ctx_tiles, tile_off, nt_seq):
    shift_refs, scale_refs, gate_refs = rest[:group], rest[group:2 * group], rest[2 * group:3 * group]
    (wup_ref, bup_ref, cw_ref, cb_ref, wdn_ref, bdn_ref, g_ref, beta_ref,
     o_ref, uext_ref, ubf_ref, act_ref, wdnbf_ref) = rest[3 * group:]

    @pl.when(_first_step())
    def _():
        wdnbf_ref[...] = wdn_ref[...].astype(BF16)

    t = pl.program_id(1) + tile_off
    seg_first = (t == 0) | (t == ctx_tiles)
    seg_last = (t == nt_seq - 1) | (t == ctx_tiles - 1)
    tm = h_ref.shape[1]
    ext = tm + 2 * HALO
    for b in range(group):
        sc = 1.0 + scale_refs[b][0]
        sh = shift_refs[b][0]
        r0 = b * ext
        uext_ref[r0:r0 + HALO, :] = hp_ref[b] * sc + sh
        uext_ref[r0 + HALO:r0 + HALO + tm, :] = h_ref[b] * sc + sh
        uext_ref[r0 + HALO + tm:r0 + ext, :] = hn_ref[b] * sc + sh
    ubf_ref[...] = uext_ref[...].astype(BF16)
    row8 = lax.broadcasted_iota(jnp.int32, (8, FF_TILE), 0)

    def hidden(col0):
        cols = slice(col0, col0 + FF_TILE)
        zr_all = jnp.dot(ubf_ref[...], wup_ref[:, cols], preferred_element_type=F32)
        bup = bup_ref[:, cols]
        cw = cw_ref[:, cols]
        bias = cb_ref[:, cols] + (cw[0:1] + cw[1:2] + cw[2:3]) * bup
        outs = []
        for b in range(group):
            zr = zr_all[b * ext:(b + 1) * ext]
            z0 = zr[HALO:HALO + tm]
            prev = jnp.where(seg_first, -bup, zr[HALO - 1:HALO])
            nxt = jnp.where(seg_last, -bup, zr[HALO + tm:HALO + tm + 1])
            down = pltpu.roll(z0, 1, 0)
            up = pltpu.roll(z0, tm - 1, 0)
            zm1 = jnp.concatenate([jnp.where(row8 == 0, prev, down[0:8]), down[8:]], axis=0)
            zp1 = jnp.concatenate([up[:tm - 8], jnp.where(row8 == 7, nxt, up[tm - 8:])], axis=0)
            outs.append(cw[0:1] * zm1 + cw[1:2] * z0 + cw[2:3] * zp1 + bias)
        return jnp.concatenate(outs, axis=0)

    for j in range(D_FF // FF_TILE):
        val = hidden(j * FF_TILE)
        gat = hidden(D_FF + j * FF_TILE)
        act_ref[:, j * FF_TILE:(j + 1) * FF_TILE] = (val * _silu(gat)).astype(BF16)
    f = jnp.dot(act_ref[...], wdnbf_ref[...], preferred_element_type=F32) + bdn_ref[...]
    for b in range(group):
        o_ref[b] = _layer_norm(ALPHA * h_ref[b] + gate_refs[b][0] * f[b * tm:(b + 1) * tm],
                               g_ref[...], beta_ref[...])


def _ffn(h, mods, layer, w_up, b_up, conv_w, conv_b, w_down, b_down, ln_g, ln_b,
         ctx_tiles, tile_off, rows=ROW_TILE, group=1):
    batch, ta, _ = h.shape
    assert batch % group == 0
    nt_seq = ta // rows
    nt = nt_seq - tile_off
    hb = rows // HALO
    n_hblk = ta // HALO

    def mspecs(j):
        def spec(i):
            def idx(bg, t):
                r = jnp.where(t + tile_off < ctx_tiles, batch, bg * group + i)
                return ((layer * MOD_ROWS + r) * 6 + j, 0, 0)
            return pl.BlockSpec((1, 1, D_MODEL), idx)
        return [spec(i) for i in range(group)]

    resident = lambda shape: pl.BlockSpec((None,) + shape, lambda *_: (layer, 0, 0),
                                          pipeline_mode=pl.Buffered(1))
    ext = rows + 2 * HALO
    return pl.pallas_call(
        functools.partial(_ffn_kernel, group=group, ctx_tiles=ctx_tiles, tile_off=tile_off,
                          nt_seq=nt_seq),
        grid=(batch // group, nt),
        in_specs=[pl.BlockSpec((group, rows, D_MODEL), lambda b, t: (b, t + tile_off, 0)),
                  pl.BlockSpec((group, HALO, D_MODEL),
                               lambda b, t: (b, jnp.maximum((t + tile_off) * hb - 1, 0), 0)),
                  pl.BlockSpec((group, HALO, D_MODEL),
                               lambda b, t: (b, jnp.minimum((t + tile_off + 1) * hb, n_hblk - 1), 0))]
        + mspecs(3) + mspecs(4) + mspecs(5)
        + [resident((D_MODEL, 2 * D_FF)), _full((1, 2 * D_FF)),
           _full((3, 2 * D_FF)), _full((1, 2 * D_FF)),
           resident((D_FF, D_MODEL)), _full((1, D_MODEL)),
           _full((1, D_MODEL)), _full((1, D_MODEL))],
        out_specs=pl.BlockSpec((group, rows, D_MODEL), lambda b, t: (b, t, 0)),
        out_shape=jax.ShapeDtypeStruct((batch, nt * rows, D_MODEL), F32),
        scratch_shapes=[pltpu.VMEM((group * ext, D_MODEL), F32),
                        pltpu.VMEM((group * ext, D_MODEL), BF16),
                        pltpu.VMEM((group * rows, D_FF), BF16),
                        pltpu.VMEM((D_FF, D_MODEL), BF16)],
        compiler_params=_cparams("arbitrary", "arbitrary"),
        name="conv_ffn_ln",
    )(h, h, h, *([mods] * (3 * group)), w_up, b_up.reshape(1, -1), conv_w, conv_b.reshape(1, -1),
      w_down, b_down.reshape(1, -1), ln_g.reshape(1, -1), ln_b.reshape(1, -1))


def _time_major_rows(b, batch):
    return pl.ds(b, ROW_TILE, stride=batch)


def _inproj_ssm_kernel(h_ref, *rest, batch):
    g = BATCH_GROUP
    shift_refs, scale_refs = rest[:g], rest[g:2 * g]
    w_ref, z_ref, xbc_ref, dt_ref, us_ref, wbf_ref = rest[2 * g:]

    @pl.when(_first_step())
    def _():
        head = SSD_W + XBC_W
        wbf_ref[0:head, :] = w_ref[0:head, :].astype(BF16)
        wbf_ref[head:head + S5_W, :] = w_ref[head + DT_W:head + DT_W + S5_W, :].astype(BF16)
        pad = jnp.zeros((DT_PAD - DT_W, D_MODEL), F32)
        wbf_ref[head + S5_W:, :] = jnp.concatenate(
            [w_ref[head:head + DT_W, :], pad], axis=0).astype(BF16)

    u = _modulated_rows(lambda i: h_ref[i], shift_refs, scale_refs)
    p = lax.dot_general(u, wbf_ref[...], (((1,), (1,)), ((), ())), preferred_element_type=F32)
    _store_rows(z_ref, p[:, :SSD_W])
    _store_rows(xbc_ref, p[:, SSD_W:SSD_W + XBC_W])
    _store_rows(dt_ref, p[:, SSD_W + XBC_W + S5_W:])
    for i in range(g):
        rows = _time_major_rows(pl.program_id(1) * g + i, batch)
        for j in range(S5_W // 128):
            col0 = SSD_W + XBC_W + j * 128
            us_ref[j, rows, :] = p[i * ROW_TILE:(i + 1) * ROW_TILE, col0:col0 + 128]


def _inproj_ssm(h, mods, layer, w_in, ctx_tiles):
    batch, ta, _ = h.shape
    assert batch % BATCH_GROUP == 0
    nt = ta // ROW_TILE
    row = lambda w: pl.BlockSpec((BATCH_GROUP, ROW_TILE, w), lambda t, b: (b, t, 0))

    def mspecs(j):
        def spec(i):
            def idx(t, bg):
                r = jnp.where(t < ctx_tiles, batch, bg * BATCH_GROUP + i)
                return ((layer * MOD_ROWS + r) * 6 + j, 0, 0)
            return pl.BlockSpec((1, 1, D_MODEL), idx)
        return [spec(i) for i in range(BATCH_GROUP)]

    return pl.pallas_call(
        functools.partial(_inproj_ssm_kernel, batch=batch),
        grid=(nt, batch // BATCH_GROUP),
        in_specs=[row(D_MODEL)] + mspecs(0) + mspecs(1) + [_resident((SSM_IN_W, D_MODEL))],
        out_specs=[row(SSD_W), row(XBC_W), row(DT_PAD),
                   pl.BlockSpec((S5_W // 128, ROW_TILE * batch, 128), lambda t, b: (0, t, 0))],
        out_shape=[jax.ShapeDtypeStruct((batch, ta, SSD_W), F32),
                   jax.ShapeDtypeStruct((batch, ta, XBC_W), F32),
                   jax.ShapeDtypeStruct((batch, ta, DT_PAD), F32),
                   jax.ShapeDtypeStruct((S5_W // 128, ta * batch, 128), F32)],
        scratch_shapes=[pltpu.VMEM((SSM_IN_PAD, D_MODEL), BF16)],
        compiler_params=_cparams("arbitrary", "arbitrary"),
        name="inproj_ssm",
    )(h, *([mods] * (2 * BATCH_GROUP)), jnp.swapaxes(w_in, 0, 1))


def _cumsum_rows(v):
    n = v.shape[0]
    row = lax.broadcasted_iota(jnp.int32, v.shape, 0)
    s = 1
    while s < n:
        v = v + jnp.where(row >= s, pltpu.roll(v, s, 0), 0.0)
        s *= 2
    return v


def _expand_heads(v, e_ref):
    hi = v.astype(BF16)
    lo = (v - hi.astype(F32)).astype(BF16)
    e = e_ref[...]
    return (jnp.dot(hi, e, preferred_element_type=F32)
            + jnp.dot(lo, e, preferred_element_type=F32))


def _ssd_kernel(xbc_ref, xp_ref, xn_ref, dt_ref, z_ref, cw_ref, cb_ref, alog_ref, dtb_ref,
                dsk_ref, ng_ref, ef_ref, eb_ref, o_ref,
                xs_ref, cd_ref, st_ref, dec_ref, y_ref, *, n_tiles, ctx_tiles):
    s = pl.program_id(1)
    q = SSD_CHUNK
    cpt = ROW_TILE // q
    n_chunks = n_tiles * cpt
    ctx_chunks = ctx_tiles * cpt
    gw = SSD_W // SSD_GROUPS
    hpg = SSD_HEADS // SSD_GROUPS

    @pl.when(s < n_tiles)
    def _phase0():
        p = s
        seg_first = (p == 0) | (p == ctx_tiles)
        seg_last = (p == ctx_tiles - 1) | (p == n_tiles - 1)
        xr = xbc_ref[0]
        prev = jnp.where(seg_first, 0.0, xp_ref[0, HALO - 1:HALO, :])
        nxt = jnp.where(seg_last, 0.0, xn_ref[0, 0:1, :])
        row8 = lax.broadcasted_iota(jnp.int32, (8, XBC_W), 0)
        down = pltpu.roll(xr, 1, 0)
        up = pltpu.roll(xr, ROW_TILE - 1, 0)
        xm1 = jnp.concatenate([jnp.where(row8 == 0, prev, down[0:8]), down[8:]], axis=0)
        xp1 = jnp.concatenate([up[:ROW_TILE - 8], jnp.where(row8 == 7, nxt, up[ROW_TILE - 8:])],
                              axis=0)
        cw = cw_ref[...]
        xs_tile = _silu(cw[0:1] * xm1 + cw[1:2] * xr + cw[2:3] * xp1 + cb_ref[...])
        xs_ref[p] = xs_tile

        raw = dt_ref[0] + dtb_ref[...]
        dt_tile = jnp.maximum(raw, 0.0) + jnp.log1p(jnp.exp(-jnp.abs(raw)))
        a_row = -jnp.exp(alog_ref[...])
        for i in range(cpt):
            c = p * cpt + i
            xs = xs_tile[i * q:(i + 1) * q]
            dtv = dt_tile[i * q:(i + 1) * q]
            adt = dtv * a_row
            cum = _cumsum_rows(adt)
            tot = cum[q - 1:q, :]
            lane = lax.broadcasted_iota(jnp.int32, cum.shape, 1)
            cc = jnp.where(lane < SSD_HEADS, cum, tot - cum + adt)
            cd_ref[c, 0] = cc
            cd_ref[c, 1] = dtv
            w_end = jnp.exp(tot - cc) * dtv
            dec16 = jnp.broadcast_to(jnp.exp(tot), (16, DT_PAD))
            x = xs[:, :SSD_W]
            for d, e_ref in enumerate((ef_ref, eb_ref)):
                wx = (_expand_heads(w_end, e_ref) * x).astype(BF16)
                for g in range(SSD_GROUPS):
                    bmt = xs[:, SSD_W + g * SSD_STATE:SSD_W + (g + 1) * SSD_STATE].T.astype(BF16)
                    st_ref[c, d, :, g * gw:(g + 1) * gw] = jnp.dot(
                        bmt, wx[:, g * gw:(g + 1) * gw], preferred_element_type=F32)
                dec_ref[c, d] = _expand_heads(dec16, e_ref)[0:8]

    @pl.when(s == n_tiles)
    def _recurrence():
        fwd = list(range(n_chunks))
        bwd = list(range(ctx_chunks - 1, -1, -1)) + list(range(n_chunks - 1, ctx_chunks - 1, -1))
        for d, order in enumerate((fwd, bwd)):
            for col0 in range(0, SSD_W, 128):
                cols = slice(col0, col0 + 128)
                state = jnp.zeros((SSD_STATE, 128), F32)
                for ci in order:
                    contrib = st_ref[ci, d, :, cols]
                    st_ref[ci, d, :, cols] = state
                    state = state * dec_ref[ci, d, 0:1, cols] + contrib

    @pl.when(s >= n_tiles)
    def _phase1():
        p = s - n_tiles + ctx_tiles
        rowi = lax.broadcasted_iota(jnp.int32, (q, q), 0)
        coli = lax.broadcasted_iota(jnp.int32, (q, q), 1)
        lower = coli <= rowi
        upper = coli >= rowi
        lane = lax.broadcasted_iota(jnp.int32, (q, 128), 1)
        neg = jnp.float32(-jnp.inf)
        for i in range(cpt):
            c = p * cpt + i
            rows = slice(i * q, (i + 1) * q)
            xs = xs_ref[p, rows, :]
            x = xs[:, :SSD_W]
            cc = cd_ref[c, 0]
            dtv = cd_ref[c, 1]
            cct = cc.T
            dtt = dtv.T
            ecc = jnp.exp(cc)
            ef = _expand_heads(ecc, ef_ref)
            eb = _expand_heads(ecc, eb_ref)
            for g in range(SSD_GROUPS):
                bm = xs[:, SSD_W + g * SSD_STATE:SSD_W + (g + 1) * SSD_STATE].astype(BF16)
                cm = xs[:, SSD_W + (SSD_GROUPS + g) * SSD_STATE:
                        SSD_W + (SSD_GROUPS + g + 1) * SSD_STATE].astype(BF16)
                gmat = lax.dot_general(cm, bm, (((1,), (1,)), ((), ())),
                                       preferred_element_type=F32)
                sl = slice(g * gw, (g + 1) * gw)
                yoff = (ef[:, sl] * jnp.dot(cm, st_ref[c, 0, :, sl].astype(BF16),
                                            preferred_element_type=F32)
                        + eb[:, sl] * jnp.dot(cm, st_ref[c, 1, :, sl].astype(BF16),
                                              preferred_element_type=F32))
                for pair in range(hpg // 2):
                    col0 = g * gw + pair * 128
                    xpair = x[:, col0:col0 + 128].astype(BF16)
                    res = []
                    for hh in range(2):
                        hd = g * hpg + pair * 2 + hh
                        hb_ = SSD_HEADS + hd
                        lf = jnp.exp(jnp.where(lower, cc[:, hd:hd + 1] - cct[hd:hd + 1, :], neg))
                        lb = jnp.exp(jnp.where(upper, cc[:, hb_:hb_ + 1] - cct[hb_:hb_ + 1, :],
                                               neg))
                        mt = gmat * (lf * dtt[hd:hd + 1, :] + lb * dtt[hb_:hb_ + 1, :])
                        res.append(jnp.dot(mt.astype(BF16), xpair, preferred_element_type=F32))
                    ydiag = jnp.where(lane < SSD_HEAD_DIM, res[0], res[1])
                    y_ref[rows, col0:col0 + 128] = (
                        ydiag + yoff[:, pair * 128:(pair + 1) * 128]
                        + dsk_ref[:, col0:col0 + 128] * x[:, col0:col0 + 128])
        gated = y_ref[...] * _silu(z_ref[0])
        normed = gated * lax.rsqrt(jnp.mean(gated * gated, axis=-1, keepdims=True) + LN_EPS)
        o_ref[0] = (normed * ng_ref[...]).astype(BF16)


def _ssd(xbc, dt, z, conv_w, conv_b, a_log, dt_bias, d_skip, norm_g, n_ctx):
    batch, ta, _ = xbc.shape
    q = SSD_CHUNK
    n_chunks = ta // q
    n_tiles = ta // ROW_TILE
    ctx_tiles = n_ctx // ROW_TILE
    hb = ROW_TILE // HALO
    n_hblk = ta // HALO
    pad24 = lambda v: jnp.pad(v.reshape(1, DT_W), ((0, 0), (0, DT_PAD - DT_W)))
    heads = np.arange(SSD_HEADS)
    ef = np.zeros((DT_PAD, SSD_W), np.float32)
    eb = np.zeros((DT_PAD, SSD_W), np.float32)
    for hd in heads:
        ef[hd, hd * SSD_HEAD_DIM:(hd + 1) * SSD_HEAD_DIM] = 1.0
        eb[SSD_HEADS + hd, hd * SSD_HEAD_DIM:(hd + 1) * SSD_HEAD_DIM] = 1.0
    dsk = jnp.repeat(d_skip.astype(F32), SSD_HEAD_DIM).reshape(1, SSD_W)
    ph0 = lambda s: s < n_tiles
    tile = lambda w: pl.BlockSpec(
        (1, ROW_TILE, w), lambda b, s: (b, jnp.where(ph0(s), s, n_tiles - 1), 0))
    return pl.pallas_call(
        functools.partial(_ssd_kernel, n_tiles=n_tiles, ctx_tiles=ctx_tiles),
        grid=(batch, 2 * n_tiles - ctx_tiles),
        in_specs=[tile(XBC_W),
                  pl.BlockSpec((1, HALO, XBC_W),
                               lambda b, s: (b, jnp.where(ph0(s), jnp.maximum(s * hb - 1, 0), 0), 0)),
                  pl.BlockSpec((1, HALO, XBC_W),
                               lambda b, s: (b, jnp.where(ph0(s), jnp.minimum((s + 1) * hb, n_hblk - 1), 0), 0)),
                  tile(DT_PAD),
                  pl.BlockSpec((1, ROW_TILE, SSD_W),
                               lambda b, s: (b, jnp.where(ph0(s), 0, s - n_tiles + ctx_tiles), 0)),
                  _full((3, XBC_W)), _full((1, XBC_W)), _full((1, DT_PAD)), _full((1, DT_PAD)),
                  _full((1, SSD_W)), _full((1, SSD_W)),
                  _full((DT_PAD, SSD_W)), _full((DT_PAD, SSD_W))],
        out_specs=pl.BlockSpec(
            (1, ROW_TILE, SSD_W), lambda b, s: (b, jnp.where(ph0(s), 0, s - n_tiles), 0)),
        out_shape=jax.ShapeDtypeStruct((batch, ta - n_ctx, SSD_W), BF16),
        scratch_shapes=[pltpu.VMEM((n_tiles, ROW_TILE, XBC_W), F32),
                        pltpu.VMEM((n_chunks, 2, q, DT_PAD), F32),
                        pltpu.VMEM((n_chunks, 2, SSD_STATE, SSD_W), F32),
                        pltpu.VMEM((n_chunks, 2, 8, SSD_W), F32),
                        pltpu.VMEM((ROW_TILE, SSD_W), F32)],
        compiler_params=_cparams("parallel", "arbitrary"),
        name="ssd_bidir",
    )(xbc, xbc, xbc, dt, z, conv_w, conv_b.reshape(1, XBC_W), pad24(a_log), pad24(dt_bias),
      dsk, norm_g.reshape(1, SSD_W), jnp.asarray(ef, BF16), jnp.asarray(eb, BF16))


def _s5_disc_kernel(lr_ref, li_ref, ldt_ref, bre_ref, bim_ref, cre_ref, cim_ref,
                    a_ref, bd_ref, cd_ref):
    lr, li = lr_ref[...], li_ref[...]
    dt = jnp.exp(ldt_ref[...])
    mag = jnp.exp(dt * lr)
    ab_re, ab_im = mag * jnp.cos(dt * li), mag * jnp.sin(dt * li)
    den = lr * lr + li * li
    k_re = ((ab_re - 1.0) * lr + ab_im * li) / den
    k_im = (ab_im * lr - (ab_re - 1.0) * li) / den
    bre, bim = bre_ref[...], bim_ref[...]
    for d in range(2):
        a_ref[d, :, 0:S5_NSTATE] = jnp.broadcast_to(ab_re[d:d + 1], (8, S5_NSTATE))
        a_ref[d, :, S5_NSTATE:] = jnp.broadcast_to(ab_im[d:d + 1], (8, S5_NSTATE))
        kr, ki = k_re[d:d + 1], k_im[d:d + 1]
        bd_ref[d, :, 0:S5_NSTATE] = (kr * bre - ki * bim).astype(BF16)
        bd_ref[d, :, S5_NSTATE:] = (kr * bim + ki * bre).astype(BF16)
        cd_ref[d, 0:S5_NSTATE, :] = cre_ref[d].astype(BF16)
        cd_ref[d, S5_NSTATE:, :] = (-cim_ref[d]).astype(BF16)


def _s5_discretize(lam_re, lam_im, log_dt, b_re, b_im, c_re, c_im):
    eye = jnp.eye(S5_GROUPS, dtype=F32)
    bd = lambda b: jnp.einsum('gph,gk->ghkp', b, eye).reshape(S5_W, S5_NSTATE)
    cd = lambda cc: jnp.einsum('dghp,gk->dgpkh', cc, eye).reshape(2, S5_NSTATE, S5_W)
    ldt = jnp.repeat(log_dt, S5_STATE, axis=-1)
    return pl.pallas_call(
        _s5_disc_kernel,
        out_shape=[jax.ShapeDtypeStruct((2, 8, 2 * S5_NSTATE), F32),
                   jax.ShapeDtypeStruct((2, S5_W, 2 * S5_NSTATE), BF16),
                   jax.ShapeDtypeStruct((2, 2 * S5_NSTATE, S5_W), BF16)],
        compiler_params=pltpu.CompilerParams(vmem_limit_bytes=VMEM_LIMIT_BYTES),
        name="s5_discretize",
    )(lam_re.reshape(2, S5_NSTATE), lam_im.reshape(2, S5_NSTATE), ldt,
      bd(b_re), bd(b_im), cd(c_re), cd(c_im))


S5_TIME_CHUNK = 128


def _s5_scan_kernel(uf_ref, ub_ref, a_ref, bd_ref, cd_ref, yf_ref, yb_ref,
                    hsf_ref, hsb_ref, carry_ref, *, batch):
    j = pl.program_id(0)
    n = S5_NSTATE
    tc = S5_TIME_CHUNK
    n_slab = S5_W // 128
    half = hsf_ref.shape[0] // 2
    chains = ((uf_ref, hsf_ref, yf_ref), (ub_ref, hsb_ref, yb_ref))

    @pl.when(j == 0)
    def _():
        carry_ref[...] = jnp.zeros_like(carry_ref)

    for d, (u_ref, hs_ref, _) in enumerate(chains):
        for r in (0, half):
            u = jnp.concatenate([u_ref[s, r:r + half, :] for s in range(n_slab)], axis=1)
            hs_ref[r:r + half, :] = jnp.dot(u.astype(BF16), bd_ref[d], preferred_element_type=F32)

    for d, (_, hs_ref, _) in enumerate(chains):
        ar = jnp.broadcast_to(a_ref[d, 0:1, 0:n], (batch, n))
        ai = jnp.broadcast_to(a_ref[d, 0:1, n:], (batch, n))
        hr = carry_ref[d, :, 0:n]
        hi = carry_ref[d, :, n:]
        for step in (range(tc) if d == 0 else range(tc - 1, -1, -1)):
            rows = slice(step * batch, (step + 1) * batch)
            hr, hi = (ar * hr - ai * hi + hs_ref[rows, 0:n],
                      ar * hi + ai * hr + hs_ref[rows, n:])
            hs_ref[rows, 0:n] = hr
            hs_ref[rows, n:] = hi
        carry_ref[d, :, 0:n] = hr
        carry_ref[d, :, n:] = hi

    for d, (_, hs_ref, y_ref) in enumerate(chains):
        for r in (0, half):
            y = jnp.dot(hs_ref[r:r + half, :].astype(BF16), cd_ref[d], preferred_element_type=F32)
            for s in range(n_slab):
                y_ref[s, r:r + half, :] = y[:, s * 128:(s + 1) * 128]


def _s5_scan(us_flat, a, bd, cd, batch, n_ctx):
    n_slab, rows_total, _ = us_flat.shape
    ta = rows_total // batch
    tc = S5_TIME_CHUNK
    n_chunks = ta // tc
    ctx_chunks = n_ctx // tc
    blk = tc * batch

    def bwd_chunk(j):
        return jnp.where(j < ctx_chunks, ctx_chunks - 1 - j, n_chunks - 1 - (j - ctx_chunks))

    fwd_spec = pl.BlockSpec((n_slab, blk, 128), lambda j: (0, j, 0))
    bwd_spec = pl.BlockSpec((n_slab, blk, 128), lambda j: (0, bwd_chunk(j), 0))
    return pl.pallas_call(
        functools.partial(_s5_scan_kernel, batch=batch),
        grid=(n_chunks,),
        in_specs=[fwd_spec, bwd_spec, _full((2, 8, 2 * S5_NSTATE)),
                  _full((2, S5_W, 2 * S5_NSTATE)), _full((2, 2 * S5_NSTATE, S5_W))],
        out_specs=[fwd_spec, bwd_spec],
        out_shape=[jax.ShapeDtypeStruct((n_slab, rows_total, 128), F32)] * 2,
        scratch_shapes=[pltpu.VMEM((blk, 2 * S5_NSTATE), F32),
                        pltpu.VMEM((blk, 2 * S5_NSTATE), F32),
                        pltpu.VMEM((2, batch, 2 * S5_NSTATE), F32)],
        compiler_params=_cparams("arbitrary"),
        name="s5_scan",
    )(us_flat, us_flat, a, bd, cd)


def _merge_ln_kernel(gs_ref, yf_ref, yb_ref, us_ref, h_ref, *rest, batch):
    g = BATCH_GROUP
    gate_refs = rest[:g]
    dd_ref, gw_ref, gb_ref, w_ref, g_ref, beta_ref, o_ref, wbf_ref = rest[g:]

    @pl.when(_first_step())
    def _():
        wbf_ref[...] = w_ref[...].astype(BF16)

    def s5_input(i):
        rows = _time_major_rows(pl.program_id(1) * g + i, batch)
        y5 = jnp.concatenate([yf_ref[s, rows, :] + yb_ref[s, rows, :]
                              for s in range(S5_W // 128)], axis=1)
        us = jnp.concatenate([us_ref[s, rows, :] for s in range(S5_W // 128)], axis=1)
        return y5 + dd_ref[...] * us

    ge = jax.nn.gelu(jnp.concatenate([s5_input(i) for i in range(g)], axis=0))
    s5 = ge * jax.nn.sigmoid(
        jnp.dot(ge.astype(BF16), gw_ref[...], preferred_element_type=F32) + gb_ref[...])
    gs = jnp.concatenate([gs_ref[i] for i in range(g)], axis=0)
    y = jnp.dot(gs, wbf_ref[0:SSD_W, :], preferred_element_type=F32)
    y = y + jnp.dot(s5.astype(BF16), wbf_ref[SSD_W:, :], preferred_element_type=F32)
    rows = o_ref.shape[1]
    for i in range(g):
        o_ref[i] = _layer_norm(ALPHA * h_ref[i] + gate_refs[i][0] * y[i * rows:(i + 1) * rows],
                               g_ref[...], beta_ref[...])


def _merge_ln(g_ssd, y5_fwd, y5_bwd, us_t, h, mods, layer, s5_d, glu_w, glu_b, w_out, ln_g, ln_b,
              ctx_tiles):
    batch, ta, _ = h.shape
    nt = ta // ROW_TILE - ctx_tiles
    n_slab = S5_W // 128
    tm_rows = ROW_TILE * batch
    g = BATCH_GROUP
    gate_specs = [
        pl.BlockSpec((1, 1, D_MODEL),
                     lambda t, bg, i=i: ((layer * MOD_ROWS + bg * g + i) * 6 + 2, 0, 0))
        for i in range(g)]
    return pl.pallas_call(
        functools.partial(_merge_ln_kernel, batch=batch),
        grid=(nt, batch // g),
        in_specs=[pl.BlockSpec((g, ROW_TILE, SSD_W), lambda t, b: (b, t, 0)),
                  pl.BlockSpec((n_slab, tm_rows, 128), lambda t, b: (0, t + ctx_tiles, 0)),
                  pl.BlockSpec((n_slab, tm_rows, 128), lambda t, b: (0, t + ctx_tiles, 0)),
                  pl.BlockSpec((n_slab, tm_rows, 128), lambda t, b: (0, t + ctx_tiles, 0)),
                  pl.BlockSpec((g, ROW_TILE, D_MODEL), lambda t, b: (b, t + ctx_tiles, 0))]
        + gate_specs
        + [_full((1, S5_W)), _full((S5_W, S5_W)), _full((1, S5_W)),
           _resident((D_MODEL, D_MODEL)), _full((1, D_MODEL)), _full((1, D_MODEL))],
        out_specs=pl.BlockSpec((g, ROW_TILE, D_MODEL), lambda t, b: (b, t, 0)),
        out_shape=jax.ShapeDtypeStruct((batch, nt * ROW_TILE, D_MODEL), F32),
        scratch_shapes=[pltpu.VMEM((D_MODEL, D_MODEL), BF16)],
        compiler_params=_cparams("arbitrary", "arbitrary"),
        name="merge_outproj_ln",
    )(g_ssd, y5_fwd, y5_bwd, us_t, h, *([mods] * g), s5_d.reshape(1, S5_W), glu_w.astype(BF16),
      glu_b.reshape(1, S5_W), w_out, ln_g.reshape(1, -1), ln_b.reshape(1, -1))


def _attn_layer(h_ctx, h_lat, mods, layer, i, p, keep_ctx):
    n_ctx = h_ctx.shape[1]
    ctx_tiles = n_ctx // ROW_TILE
    lam_init = 0.8 - 0.6 * math.exp(-0.3 * layer)
    cos, sin = _rope_tables(h_lat.shape[1], n_ctx)
    q, k, v, f = _inproj_attn(h_ctx, h_lat, mods, layer, p['attn_w_in'][i],
                              cos, sin, ctx_tiles)
    o_ctx, o_lat = _attention(q, k, v, p['attn_lambda'][i], p['attn_subln_g'][i], lam_init,
                              ctx_tiles)
    fm_ctx, fm_lat = _fourier(f, p['fourier_w'][i], p['fourier_b'][i], n_ctx)
    h1 = _outproj_ln(o_ctx, o_lat, fm_ctx, fm_lat, h_ctx, h_lat, mods, layer,
                     p['attn_w_out'][i],
                     p['ln_g'][layer, 0], p['ln_b'][layer, 0], ctx_tiles)
    return _ffn(h1, mods, layer, p['ffn_w_up_bf16'], p['ffn_b_up'][layer],
                p['ffn_conv_w'][layer], p['ffn_conv_b'][layer],
                p['ffn_w_down'], p['ffn_b_down'][layer],
                p['ln_g'][layer, 1], p['ln_b'][layer, 1], ctx_tiles, 0 if keep_ctx else ctx_tiles,
                group=BATCH_GROUP)


def _ssm_layer(h, mods, layer, i, n_ctx, p, keep_ctx):
    assert not keep_ctx, "an SSM layer that must also emit context rows is not implemented"
    ctx_tiles = n_ctx // ROW_TILE
    batch = h.shape[0]
    z, xbc, dt, us_t = _inproj_ssm(h, mods, layer, p['ssm_w_in'][i], ctx_tiles)
    g_ssd = _ssd(xbc, dt, z, p['ssd_conv_w'][i], p['ssd_conv_b'][i], p['ssd_a_log'][i],
                 p['ssd_dt_bias'][i], p['ssd_d'][i], p['ssd_norm_g'][i], n_ctx)
    a, bd, cd = _s5_discretize(p['s5_lambda_re'][i], p['s5_lambda_im'][i], p['s5_log_dt'][i],
                               p['s5_b_re'][i], p['s5_b_im'][i], p['s5_c_re'][i], p['s5_c_im'][i])
    y5_fwd, y5_bwd = _s5_scan(us_t, a, bd, cd, batch, n_ctx)
    h1 = _merge_ln(g_ssd, y5_fwd, y5_bwd, us_t, h, mods, layer, p['s5_d'][i], p['s5_glu_w'][i],
                   p['s5_glu_b'][i], p['ssm_w_out'][i],
                   p['ln_g'][layer, 0], p['ln_b'][layer, 0], ctx_tiles)
    return _ffn(h1, mods, layer, p['ffn_w_up_bf16'], p['ffn_b_up'][layer],
                p['ffn_conv_w'][layer], p['ffn_conv_b'][layer],
                p['ffn_w_down'], p['ffn_b_down'][layer],
                p['ln_g'][layer, 1], p['ln_b'][layer, 1], 0, 0, rows=FFN_LAT_ROWS)


def kernel(x, c, ctx, c_ctx, ada_w, ada_b, ln_g, ln_b, ffn_w_up, ffn_b_up, ffn_conv_w, ffn_conv_b, ffn_w_down, ffn_b_down, attn_w_in, attn_lambda, attn_subln_g, fourier_w, fourier_b, attn_w_out, ssm_w_in, ssd_conv_w, ssd_conv_b, ssd_a_log, ssd_dt_bias, ssd_d, ssd_norm_g, s5_lambda_re, s5_lambda_im, s5_log_dt, s5_b_re, s5_b_im, s5_c_re, s5_c_im, s5_d, s5_glu_w, s5_glu_b, ssm_w_out):
    p = dict(ln_g=ln_g, ln_b=ln_b, ffn_w_up=ffn_w_up, ffn_b_up=ffn_b_up, ffn_conv_w=ffn_conv_w,
             ffn_conv_b=ffn_conv_b, ffn_w_down=ffn_w_down, ffn_b_down=ffn_b_down,
             attn_w_in=attn_w_in, attn_lambda=attn_lambda, attn_subln_g=attn_subln_g,
             fourier_w=fourier_w, fourier_b=fourier_b, attn_w_out=attn_w_out, ssm_w_in=ssm_w_in,
             ssd_conv_w=ssd_conv_w, ssd_conv_b=ssd_conv_b, ssd_a_log=ssd_a_log,
             ssd_dt_bias=ssd_dt_bias, ssd_d=ssd_d, ssd_norm_g=ssd_norm_g,
             s5_lambda_re=s5_lambda_re, s5_lambda_im=s5_lambda_im, s5_log_dt=s5_log_dt,
             s5_b_re=s5_b_re, s5_b_im=s5_b_im, s5_c_re=s5_c_re, s5_c_im=s5_c_im, s5_d=s5_d,
             s5_glu_w=s5_glu_w, s5_glu_b=s5_glu_b, ssm_w_out=ssm_w_out)
    batch, n_lat, _ = x.shape
    n_ctx = ctx.shape[1]
    assert n_ctx == ROW_TILE and n_lat % ROW_TILE == 0 and batch < MOD_ROWS
    mods = _ada_mods(c, c_ctx, ada_w, ada_b)
    p['ffn_w_up_bf16'] = ffn_w_up.astype(BF16)
    assert DEPTH == 2
    h = _attn_layer(ctx, x, mods, 0, 0, p, keep_ctx=True)
    return _ssm_layer(h, mods, 1, 0, n_ctx, p, keep_ctx=False)
```

```python
import functools
import math

import numpy as np
import jax
import jax.numpy as jnp
from jax import lax
from jax.experimental import pallas as pl
from jax.experimental.pallas import tpu as pltpu

F32 = jnp.float32
BF16 = jnp.bfloat16

D_MODEL = 1024
DEPTH = 2
GRID_W = 64
ROPE_BASE = 10000.0
LN_EPS = 1e-5
ALPHA = (2 * DEPTH) ** 0.25
ATTN_W = 768
ATTN_HEADS = 6
ATTN_HEAD_DIM = 64
F_W = 256
F_GROUPS = 4
F_GROUP_W = 64
ATTN_IN_W = 2 * ATTN_W + ATTN_W + F_W
SSD_W = 768
SSD_HEADS = 12
SSD_HEAD_DIM = 64
SSD_GROUPS = 2
SSD_STATE = 128
SSD_CHUNK = 128
XBC_W = SSD_W + 2 * SSD_GROUPS * SSD_STATE
DT_W = 2 * SSD_HEADS
DT_PAD = 128
S5_W = 256
S5_GROUPS = 16
S5_STATE = 64
S5_NSTATE = S5_GROUPS * S5_STATE
SSM_IN_W = SSD_W + XBC_W + DT_W + S5_W
SSM_IN_PAD = SSD_W + XBC_W + S5_W + DT_PAD
D_FF = 2816
FF_TILE = 256

ROW_TILE = 256
ATTN_Q_SUB = 256
ATTN_ITEM_TILES = 2
FFN_LAT_ROWS = 512
BATCH_GROUP = 4
FFN_BATCH_GROUP = 2
FOURIER_TILE = 1024
HALO = 8
MOD_ROWS = 16
ADA_CHUNKS_PER_STEP = 2
VMEM_LIMIT_BYTES = 56 * 1024 * 1024


def _cparams(*sem):
    return pltpu.CompilerParams(dimension_semantics=sem, vmem_limit_bytes=VMEM_LIMIT_BYTES)


def _silu(v):
    return v * jax.nn.sigmoid(v)


def _layer_norm(v, g, b):
    mu = jnp.mean(v, axis=-1, keepdims=True)
    d = v - mu
    var = jnp.mean(d * d, axis=-1, keepdims=True)
    return d * lax.rsqrt(var + LN_EPS) * g + b


def _full(shape):
    nd = len(shape)
    return pl.BlockSpec(shape, lambda *_: (0,) * nd)


def _resident(shape):
    nd = len(shape)
    return pl.BlockSpec(shape, lambda *_: (0,) * nd, pipeline_mode=pl.Buffered(1))


def _first_step():
    return (pl.program_id(0) == 0) & (pl.program_id(1) == 0)


def _mod_specs_grouped(layer, j, batch, ctx_tiles):
    def spec(i):
        def idx(bg, t):
            row = jnp.where(t < ctx_tiles, batch, bg * BATCH_GROUP + i)
            return ((layer * MOD_ROWS + row) * 6 + j, 0, 0)
        return pl.BlockSpec((1, 1, D_MODEL), idx)
    return [spec(i) for i in range(BATCH_GROUP)]


def _ada_kernel(c_ref, w_ref, b_ref, o_ref):
    s = _silu(c_ref[...])
    w = w_ref[0]
    s_hi = s.astype(BF16)
    s_lo = (s - s_hi.astype(F32)).astype(BF16)
    w_hi = w.astype(BF16)
    w_lo = (w - w_hi.astype(F32)).astype(BF16)
    dot = functools.partial(jnp.dot, preferred_element_type=F32)
    o_ref[0] = dot(s_hi, w_hi) + (dot(s_lo, w_hi) + dot(s_hi, w_lo)) + b_ref[0]


def _ada_mods(c, c_ctx, ada_w, ada_b):
    batch = c.shape[0]
    nl = ada_w.shape[0]
    c_all = jnp.concatenate(
        [c, c_ctx[None], jnp.zeros((MOD_ROWS - batch - 1, D_MODEL), F32)], axis=0)
    out = pl.pallas_call(
        _ada_kernel,
        grid=(nl, 6 // ADA_CHUNKS_PER_STEP),
        in_specs=[_full((MOD_ROWS, D_MODEL)),
                  pl.BlockSpec((1, D_MODEL, ADA_CHUNKS_PER_STEP * D_MODEL), lambda l, j: (l, 0, j)),
                  pl.BlockSpec((1, 1, ADA_CHUNKS_PER_STEP * D_MODEL), lambda l, j: (l, 0, j))],
        out_specs=pl.BlockSpec((1, MOD_ROWS, ADA_CHUNKS_PER_STEP * D_MODEL),
                               lambda l, j: (l, 0, j)),
        out_shape=jax.ShapeDtypeStruct((nl, MOD_ROWS, 6 * D_MODEL), F32),
        compiler_params=_cparams("arbitrary", "arbitrary"),
        name="ada_mods",
    )(c_all, ada_w, ada_b.reshape(nl, 1, 6 * D_MODEL))
    return out.reshape(nl * MOD_ROWS * 6, 1, D_MODEL)


def _modulated_rows(h_of, shift_refs, scale_refs):
    return jnp.concatenate(
        [(h_of(i) * (1.0 + scale_refs[i][0]) + shift_refs[i][0]).astype(BF16)
         for i in range(BATCH_GROUP)], axis=0)


def _store_rows(ref, val):
    rows = ref.shape[1]
    for i in range(BATCH_GROUP):
        ref[i] = val[i * rows:(i + 1) * rows].astype(ref.dtype)


def _inproj_attn_kernel(hc_ref, hl_ref, *rest, ctx_tiles):
    g = BATCH_GROUP
    shift_refs, scale_refs = rest[:g], rest[g:2 * g]
    w_ref, cos_ref, sin_ref, q_ref, k_ref, v_ref, f_ref, wbf_ref = rest[2 * g:]

    @pl.when(_first_step())
    def _():
        wbf_ref[...] = w_ref[...].astype(BF16)

    is_ctx = pl.program_id(1) < ctx_tiles
    u = _modulated_rows(lambda i: jnp.where(is_ctx, hc_ref[i], hl_ref[i]), shift_refs, scale_refs)
    p = jnp.dot(u, wbf_ref[...], preferred_element_type=F32)
    cos = jnp.concatenate([cos_ref[...]] * g, axis=0)
    sin = jnp.concatenate([sin_ref[...]] * g, axis=0)
    lane = lax.broadcasted_iota(jnp.int32, cos.shape, 1)
    first_half = (lane % ATTN_HEAD_DIM) < (ATTN_HEAD_DIM // 2)

    def rope(blk):
        partner = jnp.where(first_half, pltpu.roll(blk, 128 - 32, 1), pltpu.roll(blk, 32, 1))
        return blk * cos + partner * sin

    qk_scale = ATTN_HEAD_DIM ** -0.5 * math.log2(math.e)
    rows = q_ref.shape[1]
    for i in range(ATTN_HEADS):
        lo, hi = i * 128, (i + 1) * 128
        qh = (rope(p[:, lo:hi]) * qk_scale).astype(BF16)
        kh = rope(p[:, ATTN_W + lo:ATTN_W + hi]).astype(BF16)
        for b in range(g):
            q_ref[b, :, lo:hi] = qh[b * rows:(b + 1) * rows]
            k_ref[b, :, lo:hi] = kh[b * rows:(b + 1) * rows]
    _store_rows(v_ref, p[:, 2 * ATTN_W:3 * ATTN_W])
    _store_rows(f_ref, p[:, 3 * ATTN_W:])


def _rope_tables(n_lat, n_ctx):
    rows = n_lat // GRID_W
    row = jnp.repeat(jnp.arange(rows, dtype=F32), GRID_W)
    col = jnp.tile(jnp.arange(GRID_W, dtype=F32), rows)
    n_freq = ATTN_HEAD_DIM // 4
    inv_freq = ROPE_BASE ** (-jnp.arange(n_freq, dtype=F32) / n_freq)
    ang = jnp.concatenate([row[:, None] * inv_freq, col[:, None] * inv_freq], axis=-1)
    cos, sin = jnp.cos(ang), jnp.sin(ang)
    cos128 = jnp.tile(cos, (1, 4))
    sin128 = jnp.tile(jnp.concatenate([-sin, sin], axis=-1), (1, 2))
    cos_all = jnp.concatenate([jnp.ones((n_ctx, 128), F32), cos128], axis=0)
    sin_all = jnp.concatenate([jnp.zeros((n_ctx, 128), F32), sin128], axis=0)
    return cos_all, sin_all


def _split_specs(width, ctx_tiles):
    g = BATCH_GROUP
    return [pl.BlockSpec((g, ROW_TILE, width), lambda b, t: (b, jnp.minimum(t, ctx_tiles - 1), 0)),
            pl.BlockSpec((g, ROW_TILE, width), lambda b, t: (b, jnp.maximum(t - ctx_tiles, 0), 0))]


def _inproj_attn(h_ctx, h_lat, mods, layer, w_in, cos, sin, ctx_tiles):
    batch = h_lat.shape[0]
    assert batch % BATCH_GROUP == 0
    ta = h_ctx.shape[1] + h_lat.shape[1]
    nt = ta // ROW_TILE
    row = lambda w: pl.BlockSpec((BATCH_GROUP, ROW_TILE, w), lambda b, t: (b, t, 0))
    tab = pl.BlockSpec((ROW_TILE, 128), lambda b, t: (t, 0))
    n_mod = 2 * BATCH_GROUP
    return pl.pallas_call(
        functools.partial(_inproj_attn_kernel, ctx_tiles=ctx_tiles),
        grid=(batch // BATCH_GROUP, nt),
        in_specs=_split_specs(D_MODEL, ctx_tiles)
        + _mod_specs_grouped(layer, 0, batch, ctx_tiles)
        + _mod_specs_grouped(layer, 1, batch, ctx_tiles)
        + [_resident((D_MODEL, ATTN_IN_W)), tab, tab],
        out_specs=[row(ATTN_W), row(ATTN_W), row(ATTN_W), row(F_W)],
        out_shape=[jax.ShapeDtypeStruct((batch, ta, ATTN_W), BF16)] * 3
        + [jax.ShapeDtypeStruct((batch, ta, F_W), BF16)],
        scratch_shapes=[pltpu.VMEM((D_MODEL, ATTN_IN_W), BF16)],
        compiler_params=_cparams("arbitrary", "arbitrary"),
        name="inproj_attn",
    )(h_ctx, h_lat, *([mods] * n_mod), w_in, cos, sin)


def _diff_lambda(lam_ref, lam_init):
    lamv = lam_ref[...]
    l1 = jnp.sum(lamv[0:1] * lamv[1:2], axis=-1, keepdims=True)
    l2 = jnp.sum(lamv[2:3] * lamv[3:4], axis=-1, keepdims=True)
    return jnp.exp(l1) - jnp.exp(l2) + lam_init


def _stack_maps(q):
    lane = lax.broadcasted_iota(jnp.int32, q.shape, 1)
    zero = jnp.zeros_like(q)
    return jnp.concatenate([jnp.where(lane < ATTN_HEAD_DIM, q, zero),
                            jnp.where(lane >= ATTN_HEAD_DIM, q, zero)], axis=0)


def _scores(q2, k):
    return lax.dot_general(q2, k, (((1,), (1,)), ((), ())), preferred_element_type=F32)


def _diff_softmax_pv(load_s0, m0, load_s1, m1, lam, v, gain):
    e0 = jnp.exp2(load_s0() - m0)
    e1 = jnp.exp2(load_s1() - m1)
    l0 = jnp.sum(e0, axis=-1, keepdims=True)
    l1 = jnp.sum(e1, axis=-1, keepdims=True)
    w = e0 - e1 * (lam * l0 / l1)
    o = jnp.dot(w.astype(BF16), v, preferred_element_type=F32) * (1.0 / l0)
    o = o * lax.rsqrt(jnp.mean(o * o, axis=-1, keepdims=True) + LN_EPS)
    return (o * gain).astype(BF16)


def _attn_ctx_kernel(lam_ref, g_ref, q_ref, k_ref, v_ref, o_ref, *, lam_init):
    lam = _diff_lambda(lam_ref, lam_init)
    gain = g_ref[...] * (1.0 - lam_init)
    tq = q_ref.shape[1]
    for b in range(q_ref.shape[0]):
        for hd in range(ATTN_HEADS):
            cols = slice(hd * 128, (hd + 1) * 128)
            s = _scores(_stack_maps(q_ref[b, :, cols]), k_ref[b, :, cols])
            m = jnp.max(s, axis=-1, keepdims=True)

            def half(lo, s=s):
                return lambda: s[lo:lo + tq]

            o_ref[b, :, cols] = _diff_softmax_pv(half(0), m[:tq], half(tq), m[tq:], lam,
                                                 v_ref[b, :, cols], gain)


def _attn_lat_kernel(lam_ref, g_ref, k_ref, v_ref, *rest, lam_init):
    q_refs = rest[:ATTN_ITEM_TILES]
    o_ref, sa_ref, ma_ref, sb_ref, mb_ref = rest[ATTN_ITEM_TILES:]
    t = pl.program_id(0)
    tq = ATTN_ITEM_TILES * ROW_TILE
    sub = ATTN_Q_SUB

    @pl.when(t == 0)
    def _():
        sb_ref[...] = jnp.zeros_like(sb_ref)
        mb_ref[...] = jnp.zeros_like(mb_ref)

    def step(s_new, m_new, s_old, m_old):
        lam = _diff_lambda(lam_ref, lam_init)
        gain = g_ref[...] * (1.0 - lam_init)
        q = jnp.concatenate([q_ref[0] for q_ref in q_refs], axis=0)
        s = _scores(_stack_maps(q), k_ref[0])
        s_new[...] = s
        m_new[...] = jnp.max(s, axis=-1, keepdims=True)
        v = v_ref[0]
        for i in range(tq // sub):
            r0 = slice(i * sub, (i + 1) * sub)
            r1 = slice(tq + i * sub, tq + (i + 1) * sub)
            o_ref[0, r0, :] = _diff_softmax_pv(
                functools.partial(s_old.__getitem__, (r0, slice(None))), m_old[r0, :],
                functools.partial(s_old.__getitem__, (r1, slice(None))), m_old[r1, :],
                lam, v, gain)

    @pl.when(t % 2 == 0)
    def _():
        step(sa_ref, ma_ref, sb_ref, mb_ref)

    @pl.when(t % 2 == 1)
    def _():
        step(sb_ref, mb_ref, sa_ref, ma_ref)


def _attention(q, k, v, lam_vec, subln_g, lam_init, ctx_tiles):
    batch, ta, _ = q.shape
    n_ctx = ctx_tiles * ROW_TILE
    nt = ta // ROW_TILE - ctx_tiles
    g = subln_g.reshape(1, 128)
    small = [_full((4, ATTN_HEAD_DIM)), _full((1, 128))]
    cspec = pl.BlockSpec((BATCH_GROUP, n_ctx, ATTN_W), lambda b: (b, 0, 0))
    o_ctx = pl.pallas_call(
        functools.partial(_attn_ctx_kernel, lam_init=lam_init),
        grid=(batch // BATCH_GROUP,),
        in_specs=small + [cspec, cspec, cspec],
        out_specs=cspec,
        out_shape=jax.ShapeDtypeStruct((batch, n_ctx, ATTN_W), BF16),
        compiler_params=_cparams("arbitrary"),
        name="diff_attention_ctx",
    )(lam_vec, g, q, k, v)

    per = ATTN_ITEM_TILES
    assert nt % per == 0
    ni = nt // per
    n_items = batch * ATTN_HEADS * ni

    def item(j):
        return j // (ATTN_HEADS * ni), (j // ni) % ATTN_HEADS, j % ni

    def score_item(j):
        return item(jnp.minimum(j, n_items - 1))

    def finish_item(j):
        return item(jnp.maximum(j - 1, 0))

    def q_spec(part):
        def idx(j):
            b, h, t = score_item(j)
            return b, ctx_tiles + t * per + part, h
        return pl.BlockSpec((1, ROW_TILE, 128), idx)

    def k_idx(j):
        b, h, _ = score_item(j)
        return b, 0, h

    def v_idx(j):
        b, h, _ = finish_item(j)
        return b, 0, h

    def o_idx(j):
        b, h, t = finish_item(j)
        return b, t, h

    o_lat = pl.pallas_call(
        functools.partial(_attn_lat_kernel, lam_init=lam_init),
        grid=(n_items + 1,),
        in_specs=small + [pl.BlockSpec((1, ta, 128), k_idx), pl.BlockSpec((1, ta, 128), v_idx)]
        + [q_spec(part) for part in range(per)],
        out_specs=pl.BlockSpec((1, per * ROW_TILE, 128), o_idx),
        out_shape=jax.ShapeDtypeStruct((batch, nt * ROW_TILE, ATTN_W), BF16),
        scratch_shapes=[pltpu.VMEM((2 * per * ROW_TILE, ta), F32),
                        pltpu.VMEM((2 * per * ROW_TILE, 1), F32)] * 2,
        compiler_params=_cparams("arbitrary"),
        name="diff_attention",
    )(lam_vec, g, k, v, *([q] * per))
    return o_ctx, o_lat


def _dft_tables(n):
    k = np.arange(n, dtype=np.int64)
    ang = 2.0 * np.pi * ((k[:, None] * k[None, :]) % n).astype(np.float64) / n
    return np.cos(ang), np.sin(ang)


def _fourier_kernel(f_ref, cs_ref, dl_ref, dc_ref, w_ref, b_ref, oc_ref, ol_ref, ac_ref, al_ref,
                    *, n_ctx, n_lat):
    t = pl.program_id(0)
    b = pl.program_id(1)

    def stage1(rows0, n):
        a = jnp.dot(f_ref[0, rows0:rows0 + n, :], cs_ref[...], preferred_element_type=F32)
        return a[:, :F_W].astype(BF16), a[:, F_W:].astype(BF16)

    def stage2(dft, a, n):
        z = jnp.dot(dft, a, preferred_element_type=F32)
        z = z * (1.0 / math.sqrt(n * F_GROUP_W))
        o = jnp.dot(z.astype(BF16), w_ref[...], preferred_element_type=F32) + b_ref[...]
        return o.astype(BF16)

    @pl.when(t == 0)
    def _():
        ac_ref[0:n_ctx, :], ac_ref[n_ctx:, :] = stage1(0, n_ctx)
        oc_ref[0] = stage2(dc_ref[...], ac_ref[...], n_ctx)

    @pl.when(t == 1)
    def _():
        al_ref[b, 0:n_lat, :], al_ref[b, n_lat:, :] = stage1(n_ctx, n_lat)

    @pl.when(t >= 1)
    def _():
        ol_ref[0] = stage2(dl_ref[...], al_ref[b], n_lat)


def _fourier(f, fourier_w, fourier_b, n_ctx):
    batch, ta, _ = f.shape
    n_lat = ta - n_ctx
    ft = min(FOURIER_TILE, n_lat)
    assert n_lat % ft == 0
    cc, sc = _dft_tables(F_GROUP_W)
    eye = np.eye(F_GROUPS)
    cs = jnp.asarray(np.concatenate([np.kron(eye, cc), np.kron(eye, sc)], axis=1), BF16)
    cl, sl = _dft_tables(n_lat)
    dft_lat = jnp.asarray(np.concatenate([cl, -sl], axis=1), BF16)
    cx, sx = _dft_tables(n_ctx)
    dft_ctx = jnp.asarray(np.concatenate([cx, -sx], axis=1), BF16)
    w_blk = jnp.einsum('gce,gh->gche', fourier_w, jnp.eye(F_GROUPS, dtype=F32))
    w_blk = w_blk.reshape(F_W, F_W).astype(BF16)
    return pl.pallas_call(
        functools.partial(_fourier_kernel, n_ctx=n_ctx, n_lat=n_lat),
        grid=(1 + n_lat // ft, batch),
        in_specs=[
                  pl.BlockSpec((1, ta, F_W),
                               lambda t, b: (jnp.where(t <= 1, b, batch - 1), 0, 0)),
                  _full((F_W, 2 * F_W)),
                  pl.BlockSpec((ft, 2 * n_lat), lambda t, b: (jnp.maximum(t - 1, 0), 0)),
                  _full((n_ctx, 2 * n_ctx)),
                  _full((F_W, F_W)), _full((1, F_W))],
        out_specs=[pl.BlockSpec((1, n_ctx, F_W),
                                lambda t, b: (jnp.where(t == 0, b, batch - 1), 0, 0)),
                   pl.BlockSpec((1, ft, F_W),
                                lambda t, b: (jnp.where(t == 0, 0, b), jnp.maximum(t - 1, 0), 0))],
        out_shape=[jax.ShapeDtypeStruct((batch, n_ctx, F_W), BF16),
                   jax.ShapeDtypeStruct((batch, n_lat, F_W), BF16)],
        scratch_shapes=[pltpu.VMEM((2 * n_ctx, F_W), BF16),
                        pltpu.VMEM((batch, 2 * n_lat, F_W), BF16)],
        compiler_params=_cparams("arbitrary", "arbitrary"),
        name="fourier_mix",
    )(f, cs, dft_lat, dft_ctx, w_blk, fourier_b.reshape(1, F_W))


def _outproj_ln_kernel(ac_ref, al_ref, bc_ref, bl_ref, hc_ref, hl_ref, *rest, ctx_tiles):
    g = BATCH_GROUP
    gate_refs = rest[:g]
    w_ref, g_ref, beta_ref, o_ref, wbf_ref = rest[g:]

    @pl.when(_first_step())
    def _():
        wbf_ref[...] = w_ref[...].astype(BF16)

    is_ctx = pl.program_id(1) < ctx_tiles
    a = jnp.concatenate([jnp.where(is_ctx, ac_ref[i], al_ref[i]) for i in range(g)], axis=0)
    b2 = jnp.concatenate([jnp.where(is_ctx, bc_ref[i], bl_ref[i]) for i in range(g)], axis=0)
    wa = a.shape[1]
    y = jnp.dot(a, wbf_ref[0:wa, :], preferred_element_type=F32)
    y = y + jnp.dot(b2, wbf_ref[wa:, :], preferred_element_type=F32)
    rows = o_ref.shape[1]
    for i in range(g):
        h = jnp.where(is_ctx, hc_ref[i], hl_ref[i])
        o_ref[i] = _layer_norm(ALPHA * h + gate_refs[i][0] * y[i * rows:(i + 1) * rows],
                               g_ref[...], beta_ref[...])


def _outproj_ln(a_ctx, a_lat, b_ctx, b_lat, h_ctx, h_lat, mods, layer, w_out, ln_g, ln_b,
                ctx_tiles):
    batch = h_lat.shape[0]
    ta = h_ctx.shape[1] + h_lat.shape[1]
    nt = ta // ROW_TILE
    row = lambda w: pl.BlockSpec((BATCH_GROUP, ROW_TILE, w), lambda b, t: (b, t, 0))
    return pl.pallas_call(
        functools.partial(_outproj_ln_kernel, ctx_tiles=ctx_tiles),
        grid=(batch // BATCH_GROUP, nt),
        in_specs=_split_specs(a_lat.shape[2], ctx_tiles) + _split_specs(b_lat.shape[2], ctx_tiles)
        + _split_specs(D_MODEL, ctx_tiles)
        + _mod_specs_grouped(layer, 2, batch, ctx_tiles)
        + [_resident((D_MODEL, D_MODEL)), _full((1, D_MODEL)), _full((1, D_MODEL))],
        out_specs=row(D_MODEL),
        out_shape=jax.ShapeDtypeStruct((batch, ta, D_MODEL), F32),
        scratch_shapes=[pltpu.VMEM((D_MODEL, D_MODEL), BF16)],
        compiler_params=_cparams("arbitrary", "arbitrary"),
        name="outproj_ln",
    )(a_ctx, a_lat, b_ctx, b_lat, h_ctx, h_lat, *([mods] * BATCH_GROUP), w_out,
      ln_g.reshape(1, D_MODEL),
      ln_b.reshape(1, D_MODEL))


def _ffn_kernel(h_ref, hp_ref, hn_ref, *rest, group, ctx_tiles, tile_off, nt_seq):
    shift_refs, scale_refs, gate_refs = rest[:group], rest[group:2 * group], rest[2 * group:3 * group]
    (wup_ref, bup_ref, cw_ref, cb_ref, wdn_ref, bdn_ref, g_ref, beta_ref,
     o_ref, uext_ref, ubf_ref, act_ref, wdnbf_ref) = rest[3 * group:]

    @pl.when(_first_step())
    def _():
        wdnbf_ref[...] = wdn_ref[...].astype(BF16)

    t = pl.program_id(1) + tile_off
    seg_first = (t == 0) | (t == ctx_tiles)
    seg_last = (t == nt_seq - 1) | (t == ctx_tiles - 1)
    tm = h_ref.shape[1]
    ext = tm + 2 * HALO
    for b in range(group):
        sc = 1.0 + scale_refs[b][0]
        sh = shift_refs[b][0]
        r0 = b * ext
        uext_ref[r0:r0 + HALO, :] = hp_ref[b] * sc + sh
        uext_ref[r0 + HALO:r0 + HALO + tm, :] = h_ref[b] * sc + sh
        uext_ref[r0 + HALO + tm:r0 + ext, :] = hn_ref[b] * sc + sh
    ubf_ref[...] = uext_ref[...].astype(BF16)
    row8 = lax.broadcasted_iota(jnp.int32, (8, FF_TILE), 0)

    def hidden(col0):
        cols = slice(col0, col0 + FF_TILE)
        zr_all = jnp.dot(ubf_ref[...], wup_ref[:, cols], preferred_element_type=F32)
        bup = bup_ref[:, cols]
        cw = cw_ref[:, cols]
        bias = cb_ref[:, cols] + (cw[0:1] + cw[1:2] + cw[2:3]) * bup
        outs = []
        for b in range(group):
            zr = zr_all[b * ext:(b + 1) * ext]
            z0 = zr[HALO:HALO + tm]
            prev = jnp.where(seg_first, -bup, zr[HALO - 1:HALO])
            nxt = jnp.where(seg_last, -bup, zr[HALO + tm:HALO + tm + 1])
            down = pltpu.roll(z0, 1, 0)
            up = pltpu.roll(z0, tm - 1, 0)
            zm1 = jnp.concatenate([jnp.where(row8 == 0, prev, down[0:8]), down[8:]], axis=0)
            zp1 = jnp.concatenate([up[:tm - 8], jnp.where(row8 == 7, nxt, up[tm - 8:])], axis=0)
            outs.append(cw[0:1] * zm1 + cw[1:2] * z0 + cw[2:3] * zp1 + bias)
        return jnp.concatenate(outs, axis=0)

    for j in range(D_FF // FF_TILE):
        val = hidden(j * FF_TILE)
        gat = hidden(D_FF + j * FF_TILE)
        act_ref[:, j * FF_TILE:(j + 1) * FF_TILE] = (val * _silu(gat)).astype(BF16)
    f = jnp.dot(act_ref[...], wdnbf_ref[...], preferred_element_type=F32) + bdn_ref[...]
    for b in range(group):
        o_ref[b] = _layer_norm(ALPHA * h_ref[b] + gate_refs[b][0] * f[b * tm:(b + 1) * tm],
                               g_ref[...], beta_ref[...])


def _ffn(h, mods, layer, w_up, b_up, conv_w, conv_b, w_down, b_down, ln_g, ln_b,
         ctx_tiles, tile_off, rows=ROW_TILE, group=1):
    batch, ta, _ = h.shape
    assert batch % group == 0
    nt_seq = ta // rows
    nt = nt_seq - tile_off
    hb = rows // HALO
    n_hblk = ta // HALO

    def mspecs(j):
        def spec(i):
            def idx(bg, t):
                r = jnp.where(t + tile_off < ctx_tiles, batch, bg * group + i)
                return ((layer * MOD_ROWS + r) * 6 + j, 0, 0)
            return pl.BlockSpec((1, 1, D_MODEL), idx)
        return [spec(i) for i in range(group)]

    resident = lambda shape: pl.BlockSpec((None,) + shape, lambda *_: (layer, 0, 0),
                                          pipeline_mode=pl.Buffered(1))
    ext = rows + 2 * HALO
    return pl.pallas_call(
        functools.partial(_ffn_kernel, group=group, ctx_tiles=ctx_tiles, tile_off=tile_off,
                          nt_seq=nt_seq),
        grid=(batch // group, nt),
        in_specs=[pl.BlockSpec((group, rows, D_MODEL), lambda b, t: (b, t + tile_off, 0)),
                  pl.BlockSpec((group, HALO, D_MODEL),
                               lambda b, t: (b, jnp.maximum((t + tile_off) * hb - 1, 0), 0)),
                  pl.BlockSpec((group, HALO, D_MODEL),
                               lambda b, t: (b, jnp.minimum((t + tile_off + 1) * hb, n_hblk - 1), 0))]
        + mspecs(3) + mspecs(4) + mspecs(5)
        + [resident((D_MODEL, 2 * D_FF)), _full((1, 2 * D_FF)),
           _full((3, 2 * D_FF)), _full((1, 2 * D_FF)),
           resident((D_FF, D_MODEL)), _full((1, D_MODEL)),
           _full((1, D_MODEL)), _full((1, D_MODEL))],
        out_specs=pl.BlockSpec((group, rows, D_MODEL), lambda b, t: (b, t, 0)),
        out_shape=jax.ShapeDtypeStruct((batch, nt * rows, D_MODEL), F32),
        scratch_shapes=[pltpu.VMEM((group * ext, D_MODEL), F32),
                        pltpu.VMEM((group * ext, D_MODEL), BF16),
                        pltpu.VMEM((group * rows, D_FF), BF16),
                        pltpu.VMEM((D_FF, D_MODEL), BF16)],
        compiler_params=_cparams("arbitrary", "arbitrary"),
        name="conv_ffn_ln",
    )(h, h, h, *([mods] * (3 * group)), w_up, b_up.reshape(1, -1), conv_w, conv_b.reshape(1, -1),
      w_down, b_down.reshape(1, -1), ln_g.reshape(1, -1), ln_b.reshape(1, -1))


def _time_major_rows(b, batch):
    return pl.ds(b, ROW_TILE, stride=batch)


def _inproj_ssm_kernel(h_ref, *rest, batch):
    g = BATCH_GROUP
    shift_refs, scale_refs = rest[:g], rest[g:2 * g]
    w_ref, z_ref, xbc_ref, dt_ref, us_ref, wbf_ref = rest[2 * g:]

    @pl.when(_first_step())
    def _():
        head = SSD_W + XBC_W
        wbf_ref[0:head, :] = w_ref[0:head, :].astype(BF16)
        wbf_ref[head:head + S5_W, :] = w_ref[head + DT_W:head + DT_W + S5_W, :].astype(BF16)
        pad = jnp.zeros((DT_PAD - DT_W, D_MODEL), F32)
        wbf_ref[head + S5_W:, :] = jnp.concatenate(
            [w_ref[head:head + DT_W, :], pad], axis=0).astype(BF16)

    u = _modulated_rows(lambda i: h_ref[i], shift_refs, scale_refs)
    p = lax.dot_general(u, wbf_ref[...], (((1,), (1,)), ((), ())), preferred_element_type=F32)
    _store_rows(z_ref, p[:, :SSD_W])
    _store_rows(xbc_ref, p[:, SSD_W:SSD_W + XBC_W])
    _store_rows(dt_ref, p[:, SSD_W + XBC_W + S5_W:])
    for i in range(g):
        rows = _time_major_rows(pl.program_id(1) * g + i, batch)
        for j in range(S5_W // 128):
            col0 = SSD_W + XBC_W + j * 128
            us_ref[j, rows, :] = p[i * ROW_TILE:(i + 1) * ROW_TILE, col0:col0 + 128]


def _inproj_ssm(h, mods, layer, w_in, ctx_tiles):
    batch, ta, _ = h.shape
    assert batch % BATCH_GROUP == 0
    nt = ta // ROW_TILE
    row = lambda w: pl.BlockSpec((BATCH_GROUP, ROW_TILE, w), lambda t, b: (b, t, 0))

    def mspecs(j):
        def spec(i):
            def idx(t, bg):
                r = jnp.where(t < ctx_tiles, batch, bg * BATCH_GROUP + i)
                return ((layer * MOD_ROWS + r) * 6 + j, 0, 0)
            return pl.BlockSpec((1, 1, D_MODEL), idx)
        return [spec(i) for i in range(BATCH_GROUP)]

    return pl.pallas_call(
        functools.partial(_inproj_ssm_kernel, batch=batch),
        grid=(nt, batch // BATCH_GROUP),
        in_specs=[row(D_MODEL)] + mspecs(0) + mspecs(1) + [_resident((SSM_IN_W, D_MODEL))],
        out_specs=[row(SSD_W), row(XBC_W), row(DT_PAD),
                   pl.BlockSpec((S5_W // 128, ROW_TILE * batch, 128), lambda t, b: (0, t, 0))],
        out_shape=[jax.ShapeDtypeStruct((batch, ta, SSD_W), F32),
                   jax.ShapeDtypeStruct((batch, ta, XBC_W), F32),
                   jax.ShapeDtypeStruct((batch, ta, DT_PAD), F32),
                   jax.ShapeDtypeStruct((S5_W // 128, ta * batch, 128), F32)],
        scratch_shapes=[pltpu.VMEM((SSM_IN_PAD, D_MODEL), BF16)],
        compiler_params=_cparams("arbitrary", "arbitrary"),
        name="inproj_ssm",
    )(h, *([mods] * (2 * BATCH_GROUP)), jnp.swapaxes(w_in, 0, 1))


def _cumsum_rows(v):
    n = v.shape[0]
    row = lax.broadcasted_iota(jnp.int32, v.shape, 0)
    s = 1
    while s < n:
        v = v + jnp.where(row >= s, pltpu.roll(v, s, 0), 0.0)
        s *= 2
    return v


def _expand_heads(v, e_ref):
    hi = v.astype(BF16)
    lo = (v - hi.astype(F32)).astype(BF16)
    e = e_ref[...]
    return (jnp.dot(hi, e, preferred_element_type=F32)
            + jnp.dot(lo, e, preferred_element_type=F32))


def _ssd_kernel(xbc_ref, xp_ref, xn_ref, dt_ref, z_ref, cw_ref, cb_ref, alog_ref, dtb_ref,
                dsk_ref, ng_ref, ef_ref, eb_ref, o_ref,
                xs_ref, cd_ref, st_ref, dec_ref, y_ref, *, n_tiles, ctx_tiles):
    s = pl.program_id(1)
    q = SSD_CHUNK
    cpt = ROW_TILE // q
    n_chunks = n_tiles * cpt
    ctx_chunks = ctx_tiles * cpt
    gw = SSD_W // SSD_GROUPS
    hpg = SSD_HEADS // SSD_GROUPS

    @pl.when(s < n_tiles)
    def _phase0():
        p = s
        seg_first = (p == 0) | (p == ctx_tiles)
        seg_last = (p == ctx_tiles - 1) | (p == n_tiles - 1)
        xr = xbc_ref[0]
        prev = jnp.where(seg_first, 0.0, xp_ref[0, HALO - 1:HALO, :])
        nxt = jnp.where(seg_last, 0.0, xn_ref[0, 0:1, :])
        row8 = lax.broadcasted_iota(jnp.int32, (8, XBC_W), 0)
        down = pltpu.roll(xr, 1, 0)
        up = pltpu.roll(xr, ROW_TILE - 1, 0)
        xm1 = jnp.concatenate([jnp.where(row8 == 0, prev, down[0:8]), down[8:]], axis=0)
        xp1 = jnp.concatenate([up[:ROW_TILE - 8], jnp.where(row8 == 7, nxt, up[ROW_TILE - 8:])],
                              axis=0)
        cw = cw_ref[...]
        xs_tile = _silu(cw[0:1] * xm1 + cw[1:2] * xr + cw[2:3] * xp1 + cb_ref[...])
        xs_ref[p] = xs_tile

        raw = dt_ref[0] + dtb_ref[...]
        dt_tile = jnp.maximum(raw, 0.0) + jnp.log1p(jnp.exp(-jnp.abs(raw)))
        a_row = -jnp.exp(alog_ref[...])
        for i in range(cpt):
            c = p * cpt + i
            xs = xs_tile[i * q:(i + 1) * q]
            dtv = dt_tile[i * q:(i + 1) * q]
            adt = dtv * a_row
            cum = _cumsum_rows(adt)
            tot = cum[q - 1:q, :]
            lane = lax.broadcasted_iota(jnp.int32, cum.shape, 1)
            cc = jnp.where(lane < SSD_HEADS, cum, tot - cum + adt)
            cd_ref[c, 0] = cc
            cd_ref[c, 1] = dtv
            w_end = jnp.exp(tot - cc) * dtv
            dec16 = jnp.broadcast_to(jnp.exp(tot), (16, DT_PAD))
            x = xs[:, :SSD_W]
            for d, e_ref in enumerate((ef_ref, eb_ref)):
                wx = (_expand_heads(w_end, e_ref) * x).astype(BF16)
                for g in range(SSD_GROUPS):
                    bmt = xs[:, SSD_W + g * SSD_STATE:SSD_W + (g + 1) * SSD_STATE].T.astype(BF16)
                    st_ref[c, d, :, g * gw:(g + 1) * gw] = jnp.dot(
                        bmt, wx[:, g * gw:(g + 1) * gw], preferred_element_type=F32)
                dec_ref[c, d] = _expand_heads(dec16, e_ref)[0:8]

    @pl.when(s == n_tiles)
    def _recurrence():
        fwd = list(range(n_chunks))
        bwd = list(range(ctx_chunks - 1, -1, -1)) + list(range(n_chunks - 1, ctx_chunks - 1, -1))
        for d, order in enumerate((fwd, bwd)):
            for col0 in range(0, SSD_W, 128):
                cols = slice(col0, col0 + 128)
                state = jnp.zeros((SSD_STATE, 128), F32)
                for ci in order:
                    contrib = st_ref[ci, d, :, cols]
                    st_ref[ci, d, :, cols] = state
                    state = state * dec_ref[ci, d, 0:1, cols] + contrib

    @pl.when(s >= n_tiles)
    def _phase1():
        p = s - n_tiles + ctx_tiles
        rowi = lax.broadcasted_iota(jnp.int32, (q, q), 0)
        coli = lax.broadcasted_iota(jnp.int32, (q, q), 1)
        lower = coli <= rowi
        upper = coli >= rowi
        lane = lax.broadcasted_iota(jnp.int32, (q, 128), 1)
        neg = jnp.float32(-jnp.inf)
        for i in range(cpt):
            c = p * cpt + i
            rows = slice(i * q, (i + 1) * q)
            xs = xs_ref[p, rows, :]
            x = xs[:, :SSD_W]
            cc = cd_ref[c, 0]
            dtv = cd_ref[c, 1]
            cct = cc.T
            dtt = dtv.T
            ecc = jnp.exp(cc)
            ef = _expand_heads(ecc, ef_ref)
            eb = _expand_heads(ecc, eb_ref)
            for g in range(SSD_GROUPS):
                bm = xs[:, SSD_W + g * SSD_STATE:SSD_W + (g + 1) * SSD_STATE].astype(BF16)
                cm = xs[:, SSD_W + (SSD_GROUPS + g) * SSD_STATE:
                        SSD_W + (SSD_GROUPS + g + 1) * SSD_STATE].astype(BF16)
                gmat = lax.dot_general(cm, bm, (((1,), (1,)), ((), ())),
                                       preferred_element_type=F32)
                sl = slice(g * gw, (g + 1) * gw)
                yoff = (ef[:, sl] * jnp.dot(cm, st_ref[c, 0, :, sl].astype(BF16),
                                            preferred_element_type=F32)
                        + eb[:, sl] * jnp.dot(cm, st_ref[c, 1, :, sl].astype(BF16),
                                              preferred_element_type=F32))
                for pair in range(hpg // 2):
                    col0 = g * gw + pair * 128
                    xpair = x[:, col0:col0 + 128].astype(BF16)
                    res = []
                    for hh in range(2):
                        hd = g * hpg + pair * 2 + hh
                        hb_ = SSD_HEADS + hd
                        lf = jnp.exp(jnp.where(lower, cc[:, hd:hd + 1] - cct[hd:hd + 1, :], neg))
                        lb = jnp.exp(jnp.where(upper, cc[:, hb_:hb_ + 1] - cct[hb_:hb_ + 1, :],
                                               neg))
                        mt = gmat * (lf * dtt[hd:hd + 1, :] + lb * dtt[hb_:hb_ + 1, :])
                        res.append(jnp.dot(mt.astype(BF16), xpair, preferred_element_type=F32))
                    ydiag = jnp.where(lane < SSD_HEAD_DIM, res[0], res[1])
                    y_ref[rows, col0:col0 + 128] = (
                        ydiag + yoff[:, pair * 128:(pair + 1) * 128]
                        + dsk_ref[:, col0:col0 + 128] * x[:, col0:col0 + 128])
        gated = y_ref[...] * _silu(z_ref[0])
        normed = gated * lax.rsqrt(jnp.mean(gated * gated, axis=-1, keepdims=True) + LN_EPS)
        o_ref[0] = (normed * ng_ref[...]).astype(BF16)


def _ssd(xbc, dt, z, conv_w, conv_b, a_log, dt_bias, d_skip, norm_g, n_ctx):
    batch, ta, _ = xbc.shape
    q = SSD_CHUNK
    n_chunks = ta // q
    n_tiles = ta // ROW_TILE
    ctx_tiles = n_ctx // ROW_TILE
    hb = ROW_TILE // HALO
    n_hblk = ta // HALO
    pad24 = lambda v: jnp.pad(v.reshape(1, DT_W), ((0, 0), (0, DT_PAD - DT_W)))
    heads = np.arange(SSD_HEADS)
    ef = np.zeros((DT_PAD, SSD_W), np.float32)
    eb = np.zeros((DT_PAD, SSD_W), np.float32)
    for hd in heads:
        ef[hd, hd * SSD_HEAD_DIM:(hd + 1) * SSD_HEAD_DIM] = 1.0
        eb[SSD_HEADS + hd, hd * SSD_HEAD_DIM:(hd + 1) * SSD_HEAD_DIM] = 1.0
    dsk = jnp.repeat(d_skip.astype(F32), SSD_HEAD_DIM).reshape(1, SSD_W)
    ph0 = lambda s: s < n_tiles
    tile = lambda w: pl.BlockSpec(
        (1, ROW_TILE, w), lambda b, s: (b, jnp.where(ph0(s), s, n_tiles - 1), 0))
    return pl.pallas_call(
        functools.partial(_ssd_kernel, n_tiles=n_tiles, ctx_tiles=ctx_tiles),
        grid=(batch, 2 * n_tiles - ctx_tiles),
        in_specs=[tile(XBC_W),
                  pl.BlockSpec((1, HALO, XBC_W),
                               lambda b, s: (b, jnp.where(ph0(s), jnp.maximum(s * hb - 1, 0), 0), 0)),
                  pl.BlockSpec((1, HALO, XBC_W),
                               lambda b, s: (b, jnp.where(ph0(s), jnp.minimum((s + 1) * hb, n_hblk - 1), 0), 0)),
                  tile(DT_PAD),
                  pl.BlockSpec((1, ROW_TILE, SSD_W),
                               lambda b, s: (b, jnp.where(ph0(s), 0, s - n_tiles + ctx_tiles), 0)),
                  _full((3, XBC_W)), _full((1, XBC_W)), _full((1, DT_PAD)), _full((1, DT_PAD)),
                  _full((1, SSD_W)), _full((1, SSD_W)),
                  _full((DT_PAD, SSD_W)), _full((DT_PAD, SSD_W))],
        out_specs=pl.BlockSpec(
            (1, ROW_TILE, SSD_W), lambda b, s: (b, jnp.where(ph0(s), 0, s - n_tiles), 0)),
        out_shape=jax.ShapeDtypeStruct((batch, ta - n_ctx, SSD_W), BF16),
        scratch_shapes=[pltpu.VMEM((n_tiles, ROW_TILE, XBC_W), F32),
                        pltpu.VMEM((n_chunks, 2, q, DT_PAD), F32),
                        pltpu.VMEM((n_chunks, 2, SSD_STATE, SSD_W), F32),
                        pltpu.VMEM((n_chunks, 2, 8, SSD_W), F32),
                        pltpu.VMEM((ROW_TILE, SSD_W), F32)],
        compiler_params=_cparams("parallel", "arbitrary"),
        name="ssd_bidir",
    )(xbc, xbc, xbc, dt, z, conv_w, conv_b.reshape(1, XBC_W), pad24(a_log), pad24(dt_bias),
      dsk, norm_g.reshape(1, SSD_W), jnp.asarray(ef, BF16), jnp.asarray(eb, BF16))


def _s5_disc_kernel(lr_ref, li_ref, ldt_ref, bre_ref, bim_ref, cre_ref, cim_ref,
                    a_ref, bd_ref, cd_ref):
    lr, li = lr_ref[...], li_ref[...]
    dt = jnp.exp(ldt_ref[...])
    mag = jnp.exp(dt * lr)
    ab_re, ab_im = mag * jnp.cos(dt * li), mag * jnp.sin(dt * li)
    den = lr * lr + li * li
    k_re = ((ab_re - 1.0) * lr + ab_im * li) / den
    k_im = (ab_im * lr - (ab_re - 1.0) * li) / den
    bre, bim = bre_ref[...], bim_ref[...]
    for d in range(2):
        a_ref[d, :, 0:S5_NSTATE] = jnp.broadcast_to(ab_re[d:d + 1], (8, S5_NSTATE))
        a_ref[d, :, S5_NSTATE:] = jnp.broadcast_to(ab_im[d:d + 1], (8, S5_NSTATE))
        kr, ki = k_re[d:d + 1], k_im[d:d + 1]
        bd_ref[d, :, 0:S5_NSTATE] = (kr * bre - ki * bim).astype(BF16)
        bd_ref[d, :, S5_NSTATE:] = (kr * bim + ki * bre).astype(BF16)
        cd_ref[d, 0:S5_NSTATE, :] = cre_ref[d].astype(BF16)
        cd_ref[d, S5_NSTATE:, :] = (-cim_ref[d]).astype(BF16)


def _s5_discretize(lam_re, lam_im, log_dt, b_re, b_im, c_re, c_im):
    eye = jnp.eye(S5_GROUPS, dtype=F32)
    bd = lambda b: jnp.einsum('gph,gk->ghkp', b, eye).reshape(S5_W, S5_NSTATE)
    cd = lambda cc: jnp.einsum('dghp,gk->dgpkh', cc, eye).reshape(2, S5_NSTATE, S5_W)
    ldt = jnp.repeat(log_dt, S5_STATE, axis=-1)
    return pl.pallas_call(
        _s5_disc_kernel,
        out_shape=[jax.ShapeDtypeStruct((2, 8, 2 * S5_NSTATE), F32),
                   jax.ShapeDtypeStruct((2, S5_W, 2 * S5_NSTATE), BF16),
                   jax.ShapeDtypeStruct((2, 2 * S5_NSTATE, S5_W), BF16)],
        compiler_params=pltpu.CompilerParams(vmem_limit_bytes=VMEM_LIMIT_BYTES),
        name="s5_discretize",
    )(lam_re.reshape(2, S5_NSTATE), lam_im.reshape(2, S5_NSTATE), ldt,
      bd(b_re), bd(b_im), cd(c_re), cd(c_im))


S5_TIME_CHUNK = 128


def _s5_scan_kernel(uf_ref, ub_ref, a_ref, bd_ref, cd_ref, yf_ref, yb_ref,
                    hsf_ref, hsb_ref, carry_ref, *, batch):
    j = pl.program_id(0)
    n = S5_NSTATE
    tc = S5_TIME_CHUNK
    n_slab = S5_W // 128
    half = hsf_ref.shape[0] // 2
    chains = ((uf_ref, hsf_ref, yf_ref), (ub_ref, hsb_ref, yb_ref))

    @pl.when(j == 0)
    def _():
        carry_ref[...] = jnp.zeros_like(carry_ref)

    for d, (u_ref, hs_ref, _) in enumerate(chains):
        for r in (0, half):
            u = jnp.concatenate([u_ref[s, r:r + half, :] for s in range(n_slab)], axis=1)
            hs_ref[r:r + half, :] = jnp.dot(u.astype(BF16), bd_ref[d], preferred_element_type=F32)

    for d, (_, hs_ref, _) in enumerate(chains):
        ar = jnp.broadcast_to(a_ref[d, 0:1, 0:n], (batch, n))
        ai = jnp.broadcast_to(a_ref[d, 0:1, n:], (batch, n))
        hr = carry_ref[d, :, 0:n]
        hi = carry_ref[d, :, n:]
        for step in (range(tc) if d == 0 else range(tc - 1, -1, -1)):
            rows = slice(step * batch, (step + 1) * batch)
            hr, hi = (ar * hr - ai * hi + hs_ref[rows, 0:n],
                      ar * hi + ai * hr + hs_ref[rows, n:])
            hs_ref[rows, 0:n] = hr
            hs_ref[rows, n:] = hi
        carry_ref[d, :, 0:n] = hr
        carry_ref[d, :, n:] = hi

    for d, (_, hs_ref, y_ref) in enumerate(chains):
        for r in (0, half):
            y = jnp.dot(hs_ref[r:r + half, :].astype(BF16), cd_ref[d], preferred_element_type=F32)
            for s in range(n_slab):
                y_ref[s, r:r + half, :] = y[:, s * 128:(s + 1) * 128]


def _s5_scan(us_flat, a, bd, cd, batch, n_ctx):
    n_slab, rows_total, _ = us_flat.shape
    ta = rows_total // batch
    tc = S5_TIME_CHUNK
    n_chunks = ta // tc
    ctx_chunks = n_ctx // tc
    blk = tc * batch

    def bwd_chunk(j):
        return jnp.where(j < ctx_chunks, ctx_chunks - 1 - j, n_chunks - 1 - (j - ctx_chunks))

    fwd_spec = pl.BlockSpec((n_slab, blk, 128), lambda j: (0, j, 0))
    bwd_spec = pl.BlockSpec((n_slab, blk, 128), lambda j: (0, bwd_chunk(j), 0))
    return pl.pallas_call(
        functools.partial(_s5_scan_kernel, batch=batch),
        grid=(n_chunks,),
        in_specs=[fwd_spec, bwd_spec, _full((2, 8, 2 * S5_NSTATE)),
                  _full((2, S5_W, 2 * S5_NSTATE)), _full((2, 2 * S5_NSTATE, S5_W))],
        out_specs=[fwd_spec, bwd_spec],
        out_shape=[jax.ShapeDtypeStruct((n_slab, rows_total, 128), F32)] * 2,
        scratch_shapes=[pltpu.VMEM((blk, 2 * S5_NSTATE), F32),
                        pltpu.VMEM((blk, 2 * S5_NSTATE), F32),
                        pltpu.VMEM((2, batch, 2 * S5_NSTATE), F32)],
        compiler_params=_cparams("arbitrary"),
        name="s5_scan",
    )(us_flat, us_flat, a, bd, cd)


def _merge_ln_kernel(gs_ref, yf_ref, yb_ref, us_ref, h_ref, *rest, batch):
    g = BATCH_GROUP
    gate_refs = rest[:g]
    dd_ref, gw_ref, gb_ref, w_ref, g_ref, beta_ref, o_ref, wbf_ref = rest[g:]

    @pl.when(_first_step())
    def _():
        wbf_ref[...] = w_ref[...].astype(BF16)

    def s5_input(i):
        rows = _time_major_rows(pl.program_id(1) * g + i, batch)
        y5 = jnp.concatenate([yf_ref[s, rows, :] + yb_ref[s, rows, :]
                              for s in range(S5_W // 128)], axis=1)
        us = jnp.concatenate([us_ref[s, rows, :] for s in range(S5_W // 128)], axis=1)
        return y5 + dd_ref[...] * us

    ge = jax.nn.gelu(jnp.concatenate([s5_input(i) for i in range(g)], axis=0))
    s5 = ge * jax.nn.sigmoid(
        jnp.dot(ge.astype(BF16), gw_ref[...], preferred_element_type=F32) + gb_ref[...])
    gs = jnp.concatenate([gs_ref[i] for i in range(g)], axis=0)
    y = jnp.dot(gs, wbf_ref[0:SSD_W, :], preferred_element_type=F32)
    y = y + jnp.dot(s5.astype(BF16), wbf_ref[SSD_W:, :], preferred_element_type=F32)
    rows = o_ref.shape[1]
    for i in range(g):
        o_ref[i] = _layer_norm(ALPHA * h_ref[i] + gate_refs[i][0] * y[i * rows:(i + 1) * rows],
                               g_ref[...], beta_ref[...])


def _merge_ln(g_ssd, y5_fwd, y5_bwd, us_t, h, mods, layer, s5_d, glu_w, glu_b, w_out, ln_g, ln_b,
              ctx_tiles):
    batch, ta, _ = h.shape
    nt = ta // ROW_TILE - ctx_tiles
    n_slab = S5_W // 128
    tm_rows = ROW_TILE * batch
    g = BATCH_GROUP
    gate_specs = [
        pl.BlockSpec((1, 1, D_MODEL),
                     lambda t, bg, i=i: ((layer * MOD_ROWS + bg * g + i) * 6 + 2, 0, 0))
        for i in range(g)]
    return pl.pallas_call(
        functools.partial(_merge_ln_kernel, batch=batch),
        grid=(nt, batch // g),
        in_specs=[pl.BlockSpec((g, ROW_TILE, SSD_W), lambda t, b: (b, t, 0)),
                  pl.BlockSpec((n_slab, tm_rows, 128), lambda t, b: (0, t + ctx_tiles, 0)),
                  pl.BlockSpec((n_slab, tm_rows, 128), lambda t, b: (0, t + ctx_tiles, 0)),
                  pl.BlockSpec((n_slab, tm_rows, 128), lambda t, b: (0, t + ctx_tiles, 0)),
                  pl.BlockSpec((g, ROW_TILE, D_MODEL), lambda t, b: (b, t + ctx_tiles, 0))]
        + gate_specs
        + [_full((1, S5_W)), _full((S5_W, S5_W)), _full((1, S5_W)),
           _resident((D_MODEL, D_MODEL)), _full((1, D_MODEL)), _full((1, D_MODEL))],
        out_specs=pl.BlockSpec((g, ROW_TILE, D_MODEL), lambda t, b: (b, t, 0)),
        out_shape=jax.ShapeDtypeStruct((batch, nt * ROW_TILE, D_MODEL), F32),
        scratch_shapes=[pltpu.VMEM((D_MODEL, D_MODEL), BF16)],
        compiler_params=_cparams("arbitrary", "arbitrary"),
        name="merge_outproj_ln",
    )(g_ssd, y5_fwd, y5_bwd, us_t, h, *([mods] * g), s5_d.reshape(1, S5_W), glu_w.astype(BF16),
      glu_b.reshape(1, S5_W), w_out, ln_g.reshape(1, -1), ln_b.reshape(1, -1))


def _attn_layer(h_ctx, h_lat, mods, layer, i, p, keep_ctx):
    n_ctx = h_ctx.shape[1]
    ctx_tiles = n_ctx // ROW_TILE
    lam_init = 0.8 - 0.6 * math.exp(-0.3 * layer)
    cos, sin = _rope_tables(h_lat.shape[1], n_ctx)
    q, k, v, f = _inproj_attn(h_ctx, h_lat, mods, layer, p['attn_w_in'][i],
                              cos, sin, ctx_tiles)
    o_ctx, o_lat = _attention(q, k, v, p['attn_lambda'][i], p['attn_subln_g'][i], lam_init,
                              ctx_tiles)
    fm_ctx, fm_lat = _fourier(f, p['fourier_w'][i], p['fourier_b'][i], n_ctx)
    h1 = _outproj_ln(o_ctx, o_lat, fm_ctx, fm_lat, h_ctx, h_lat, mods, layer,
                     p['attn_w_out'][i],
                     p['ln_g'][layer, 0], p['ln_b'][layer, 0], ctx_tiles)
    return _ffn(h1, mods, layer, p['ffn_w_up_bf16'], p['ffn_b_up'][layer],
                p['ffn_conv_w'][layer], p['ffn_conv_b'][layer],
                p['ffn_w_down'], p['ffn_b_down'][layer],
                p['ln_g'][layer, 1], p['ln_b'][layer, 1], ctx_tiles, 0 if keep_ctx else ctx_tiles,
                group=FFN_BATCH_GROUP)


def _ssm_layer(h, mods, layer, i, n_ctx, p, keep_ctx):
    assert not keep_ctx, "an SSM layer that must also emit context rows is not implemented"
    ctx_tiles = n_ctx // ROW_TILE
    batch = h.shape[0]
    z, xbc, dt, us_t = _inproj_ssm(h, mods, layer, p['ssm_w_in'][i], ctx_tiles)
    g_ssd = _ssd(xbc, dt, z, p['ssd_conv_w'][i], p['ssd_conv_b'][i], p['ssd_a_log'][i],
                 p['ssd_dt_bias'][i], p['ssd_d'][i], p['ssd_norm_g'][i], n_ctx)
    a, bd, cd = _s5_discretize(p['s5_lambda_re'][i], p['s5_lambda_im'][i], p['s5_log_dt'][i],
                               p['s5_b_re'][i], p['s5_b_im'][i], p['s5_c_re'][i], p['s5_c_im'][i])
    y5_fwd, y5_bwd = _s5_scan(us_t, a, bd, cd, batch, n_ctx)
    h1 = _merge_ln(g_ssd, y5_fwd, y5_bwd, us_t, h, mods, layer, p['s5_d'][i], p['s5_glu_w'][i],
                   p['s5_glu_b'][i], p['ssm_w_out'][i],
                   p['ln_g'][layer, 0], p['ln_b'][layer, 0], ctx_tiles)
    return _ffn(h1, mods, layer, p['ffn_w_up_bf16'], p['ffn_b_up'][layer],
                p['ffn_conv_w'][layer], p['ffn_conv_b'][layer],
                p['ffn_w_down'], p['ffn_b_down'][layer],
                p['ln_g'][layer, 1], p['ln_b'][layer, 1], 0, 0, rows=FFN_LAT_ROWS)


def kernel(x, c, ctx, c_ctx, ada_w, ada_b, ln_g, ln_b, ffn_w_up, ffn_b_up, ffn_conv_w, ffn_conv_b, ffn_w_down, ffn_b_down, attn_w_in, attn_lambda, attn_subln_g, fourier_w, fourier_b, attn_w_out, ssm_w_in, ssd_conv_w, ssd_conv_b, ssd_a_log, ssd_dt_bias, ssd_d, ssd_norm_g, s5_lambda_re, s5_lambda_im, s5_log_dt, s5_b_re, s5_b_im, s5_c_re, s5_c_im, s5_d, s5_glu_w, s5_glu_b, ssm_w_out):
    p = dict(ln_g=ln_g, ln_b=ln_b, ffn_w_up=ffn_w_up, ffn_b_up=ffn_b_up, ffn_conv_w=ffn_conv_w,
             ffn_conv_b=ffn_conv_b, ffn_w_down=ffn_w_down, ffn_b_down=ffn_b_down,
             attn_w_in=attn_w_in, attn_lambda=attn_lambda, attn_subln_g=attn_subln_g,
             fourier_w=fourier_w, fourier_b=fourier_b, attn_w_out=attn_w_out, ssm_w_in=ssm_w_in,
             ssd_conv_w=ssd_conv_w, ssd_conv_b=ssd_conv_b, ssd_a_log=ssd_a_log,
             ssd_dt_bias=ssd_dt_bias, ssd_d=ssd_d, ssd_norm_g=ssd_norm_g,
             s5_lambda_re=s5_lambda_re, s5_lambda_im=s5_lambda_im, s5_log_dt=s5_log_dt,
             s5_b_re=s5_b_re, s5_b_im=s5_b_im, s5_c_re=s5_c_re, s5_c_im=s5_c_im, s5_d=s5_d,
             s5_glu_w=s5_glu_w, s5_glu_b=s5_glu_b, ssm_w_out=ssm_w_out)
    batch, n_lat, _ = x.shape
    n_ctx = ctx.shape[1]
    assert n_ctx == ROW_TILE and n_lat % ROW_TILE == 0 and batch < MOD_ROWS
    mods = _ada_mods(c, c_ctx, ada_w, ada_b)
    p['ffn_w_up_bf16'] = ffn_w_up.astype(BF16)
    assert DEPTH == 2
    h = _attn_layer(ctx, x, mods, 0, 0, p, keep_ctx=True)
    return _ssm_layer(h, mods, 1, 0, n_ctx, p, keep_ctx=False)
```

```python
import functools
import math

import numpy as np
import jax
import jax.numpy as jnp
from jax import lax
from jax.experimental import pallas as pl
from jax.experimental.pallas import tpu as pltpu

F32 = jnp.float32
BF16 = jnp.bfloat16

D_MODEL = 1024
DEPTH = 2
GRID_W = 64
ROPE_BASE = 10000.0
LN_EPS = 1e-5
ALPHA = (2 * DEPTH) ** 0.25
ATTN_W = 768
ATTN_HEADS = 6
ATTN_HEAD_DIM = 64
F_W = 256
F_GROUPS = 4
F_GROUP_W = 64
ATTN_IN_W = 2 * ATTN_W + ATTN_W + F_W
SSD_W = 768
SSD_HEADS = 12
SSD_HEAD_DIM = 64
SSD_GROUPS = 2
SSD_STATE = 128
SSD_CHUNK = 128
XBC_W = SSD_W + 2 * SSD_GROUPS * SSD_STATE
DT_W = 2 * SSD_HEADS
DT_PAD = 128
S5_W = 256
S5_GROUPS = 16
S5_STATE = 64
S5_NSTATE = S5_GROUPS * S5_STATE
SSM_IN_W = SSD_W + XBC_W + DT_W + S5_W
SSM_IN_PAD = SSD_W + XBC_W + S5_W + DT_PAD
D_FF = 2816
FF_TILE = 256

ROW_TILE = 256
ATTN_Q_SUB = 256
ATTN_ITEM_TILES = 2
FFN_LAT_ROWS = 512
BATCH_GROUP = 4
FFN_BATCH_GROUP = 2
FOURIER_TILE = 1024
HALO = 8
MOD_ROWS = 16
ADA_CHUNKS_PER_STEP = 2
VMEM_LIMIT_BYTES = 56 * 1024 * 1024


def _cparams(*sem):
    return pltpu.CompilerParams(dimension_semantics=sem, vmem_limit_bytes=VMEM_LIMIT_BYTES)


def _silu(v):
    return v * jax.nn.sigmoid(v)


def _layer_norm(v, g, b):
    mu = jnp.mean(v, axis=-1, keepdims=True)
    d = v - mu
    var = jnp.mean(d * d, axis=-1, keepdims=True)
    return d * lax.rsqrt(var + LN_EPS) * g + b


def _full(shape):
    nd = len(shape)
    return pl.BlockSpec(shape, lambda *_: (0,) * nd)


def _resident(shape):
    nd = len(shape)
    return pl.BlockSpec(shape, lambda *_: (0,) * nd, pipeline_mode=pl.Buffered(1))


def _first_step():
    return (pl.program_id(0) == 0) & (pl.program_id(1) == 0)


def _mod_specs_grouped(layer, j, batch, ctx_tiles):
    def spec(i):
        def idx(bg, t):
            row = jnp.where(t < ctx_tiles, batch, bg * BATCH_GROUP + i)
            return ((layer * MOD_ROWS + row) * 6 + j, 0, 0)
        return pl.BlockSpec((1, 1, D_MODEL), idx)
    return [spec(i) for i in range(BATCH_GROUP)]


def _ada_kernel(c_ref, w_ref, b_ref, o_ref):
    s = _silu(c_ref[...])
    w = w_ref[0]
    s_hi = s.astype(BF16)
    s_lo = (s - s_hi.astype(F32)).astype(BF16)
    w_hi = w.astype(BF16)
    w_lo = (w - w_hi.astype(F32)).astype(BF16)
    dot = functools.partial(jnp.dot, preferred_element_type=F32)
    o_ref[0] = dot(s_hi, w_hi) + (dot(s_lo, w_hi) + dot(s_hi, w_lo)) + b_ref[0]


def _ada_mods(c, c_ctx, ada_w, ada_b):
    batch = c.shape[0]
    nl = ada_w.shape[0]
    c_all = jnp.concatenate(
        [c, c_ctx[None], jnp.zeros((MOD_ROWS - batch - 1, D_MODEL), F32)], axis=0)
    out = pl.pallas_call(
        _ada_kernel,
        grid=(nl, 6 // ADA_CHUNKS_PER_STEP),
        in_specs=[_full((MOD_ROWS, D_MODEL)),
                  pl.BlockSpec((1, D_MODEL, ADA_CHUNKS_PER_STEP * D_MODEL), lambda l, j: (l, 0, j)),
                  pl.BlockSpec((1, 1, ADA_CHUNKS_PER_STEP * D_MODEL), lambda l, j: (l, 0, j))],
        out_specs=pl.BlockSpec((1, MOD_ROWS, ADA_CHUNKS_PER_STEP * D_MODEL),
                               lambda l, j: (l, 0, j)),
        out_shape=jax.ShapeDtypeStruct((nl, MOD_ROWS, 6 * D_MODEL), F32),
        compiler_params=_cparams("arbitrary", "arbitrary"),
        name="ada_mods",
    )(c_all, ada_w, ada_b.reshape(nl, 1, 6 * D_MODEL))
    return out.reshape(nl * MOD_ROWS * 6, 1, D_MODEL)


def _modulated_rows(h_of, shift_refs, scale_refs):
    return jnp.concatenate(
        [(h_of(i) * (1.0 + scale_refs[i][0]) + shift_refs[i][0]).astype(BF16)
         for i in range(BATCH_GROUP)], axis=0)


def _store_rows(ref, val):
    rows = ref.shape[1]
    for i in range(BATCH_GROUP):
        ref[i] = val[i * rows:(i + 1) * rows].astype(ref.dtype)


def _inproj_attn_kernel(hc_ref, hl_ref, *rest, ctx_tiles):
    g = BATCH_GROUP
    shift_refs, scale_refs = rest[:g], rest[g:2 * g]
    w_ref, cos_ref, sin_ref, q_ref, k_ref, v_ref, f_ref, wbf_ref = rest[2 * g:]

    @pl.when(_first_step())
    def _():
        wbf_ref[...] = w_ref[...].astype(BF16)

    is_ctx = pl.program_id(1) < ctx_tiles
    u = _modulated_rows(lambda i: jnp.where(is_ctx, hc_ref[i], hl_ref[i]), shift_refs, scale_refs)
    p = jnp.dot(u, wbf_ref[...], preferred_element_type=F32)
    cos = jnp.concatenate([cos_ref[...]] * g, axis=0)
    sin = jnp.concatenate([sin_ref[...]] * g, axis=0)
    lane = lax.broadcasted_iota(jnp.int32, cos.shape, 1)
    first_half = (lane % ATTN_HEAD_DIM) < (ATTN_HEAD_DIM // 2)

    def rope(blk):
        partner = jnp.where(first_half, pltpu.roll(blk, 128 - 32, 1), pltpu.roll(blk, 32, 1))
        return blk * cos + partner * sin

    qk_scale = ATTN_HEAD_DIM ** -0.5 * math.log2(math.e)
    rows = q_ref.shape[1]
    for i in range(ATTN_HEADS):
        lo, hi = i * 128, (i + 1) * 128
        qh = (rope(p[:, lo:hi]) * qk_scale).astype(BF16)
        kh = rope(p[:, ATTN_W + lo:ATTN_W + hi]).astype(BF16)
        for b in range(g):
            q_ref[b, :, lo:hi] = qh[b * rows:(b + 1) * rows]
            k_ref[b, :, lo:hi] = kh[b * rows:(b + 1) * rows]
    _store_rows(v_ref, p[:, 2 * ATTN_W:3 * ATTN_W])
    _store_rows(f_ref, p[:, 3 * ATTN_W:])


def _rope_tables(n_lat, n_ctx):
    rows = n_lat // GRID_W
    row = jnp.repeat(jnp.arange(rows, dtype=F32), GRID_W)
    col = jnp.tile(jnp.arange(GRID_W, dtype=F32), rows)
    n_freq = ATTN_HEAD_DIM // 4
    inv_freq = ROPE_BASE ** (-jnp.arange(n_freq, dtype=F32) / n_freq)
    ang = jnp.concatenate([row[:, None] * inv_freq, col[:, None] * inv_freq], axis=-1)
    cos, sin = jnp.cos(ang), jnp.sin(ang)
    cos128 = jnp.tile(cos, (1, 4))
    sin128 = jnp.tile(jnp.concatenate([-sin, sin], axis=-1), (1, 2))
    cos_all = jnp.concatenate([jnp.ones((n_ctx, 128), F32), cos128], axis=0)
    sin_all = jnp.concatenate([jnp.zeros((n_ctx, 128), F32), sin128], axis=0)
    return cos_all, sin_all


def _split_specs(width, ctx_tiles):
    g = BATCH_GROUP
    return [pl.BlockSpec((g, ROW_TILE, width), lambda b, t: (b, jnp.minimum(t, ctx_tiles - 1), 0)),
            pl.BlockSpec((g, ROW_TILE, width), lambda b, t: (b, jnp.maximum(t - ctx_tiles, 0), 0))]


def _inproj_attn(h_ctx, h_lat, mods, layer, w_in, cos, sin, ctx_tiles):
    batch = h_lat.shape[0]
    assert batch % BATCH_GROUP == 0
    ta = h_ctx.shape[1] + h_lat.shape[1]
    nt = ta // ROW_TILE
    row = lambda w: pl.BlockSpec((BATCH_GROUP, ROW_TILE, w), lambda b, t: (b, t, 0))
    tab = pl.BlockSpec((ROW_TILE, 128), lambda b, t: (t, 0))
    n_mod = 2 * BATCH_GROUP
    return pl.pallas_call(
        functools.partial(_inproj_attn_kernel, ctx_tiles=ctx_tiles),
        grid=(batch // BATCH_GROUP, nt),
        in_specs=_split_specs(D_MODEL, ctx_tiles)
        + _mod_specs_grouped(layer, 0, batch, ctx_tiles)
        + _mod_specs_grouped(layer, 1, batch, ctx_tiles)
        + [_resident((D_MODEL, ATTN_IN_W)), tab, tab],
        out_specs=[row(ATTN_W), row(ATTN_W), row(ATTN_W), row(F_W)],
        out_shape=[jax.ShapeDtypeStruct((batch, ta, ATTN_W), BF16)] * 3
        + [jax.ShapeDtypeStruct((batch, ta, F_W), BF16)],
        scratch_shapes=[pltpu.VMEM((D_MODEL, ATTN_IN_W), BF16)],
        compiler_params=_cparams("arbitrary", "arbitrary"),
        name="inproj_attn",
    )(h_ctx, h_lat, *([mods] * n_mod), w_in, cos, sin)


def _diff_lambda(lam_ref, lam_init):
    lamv = lam_ref[...]
    l1 = jnp.sum(lamv[0:1] * lamv[1:2], axis=-1, keepdims=True)
    l2 = jnp.sum(lamv[2:3] * lamv[3:4], axis=-1, keepdims=True)
    return jnp.exp(l1) - jnp.exp(l2) + lam_init


def _stack_maps(q):
    lane = lax.broadcasted_iota(jnp.int32, q.shape, 1)
    zero = jnp.zeros_like(q)
    return jnp.concatenate([jnp.where(lane < ATTN_HEAD_DIM, q, zero),
                            jnp.where(lane >= ATTN_HEAD_DIM, q, zero)], axis=0)


def _scores(q2, k):
    return lax.dot_general(q2, k, (((1,), (1,)), ((), ())), preferred_element_type=F32)


def _diff_softmax_pv(load_s0, m0, load_s1, m1, lam, v, gain):
    e0 = jnp.exp2(load_s0() - m0)
    e1 = jnp.exp2(load_s1() - m1)
    l0 = jnp.sum(e0, axis=-1, keepdims=True)
    l1 = jnp.sum(e1, axis=-1, keepdims=True)
    w = e0 - e1 * (lam * l0 / l1)
    o = jnp.dot(w.astype(BF16), v, preferred_element_type=F32) * (1.0 / l0)
    o = o * lax.rsqrt(jnp.mean(o * o, axis=-1, keepdims=True) + LN_EPS)
    return (o * gain).astype(BF16)


def _attn_ctx_kernel(lam_ref, g_ref, q_ref, k_ref, v_ref, o_ref, *, lam_init):
    lam = _diff_lambda(lam_ref, lam_init)
    gain = g_ref[...] * (1.0 - lam_init)
    tq = q_ref.shape[1]
    for b in range(q_ref.shape[0]):
        for hd in range(ATTN_HEADS):
            cols = slice(hd * 128, (hd + 1) * 128)
            s = _scores(_stack_maps(q_ref[b, :, cols]), k_ref[b, :, cols])
            m = jnp.max(s, axis=-1, keepdims=True)

            def half(lo, s=s):
                return lambda: s[lo:lo + tq]

            o_ref[b, :, cols] = _diff_softmax_pv(half(0), m[:tq], half(tq), m[tq:], lam,
                                                 v_ref[b, :, cols], gain)


def _attn_lat_kernel(lam_ref, g_ref, k_ref, v_ref, *rest, lam_init):
    q_refs = rest[:ATTN_ITEM_TILES]
    o_ref, sa_ref, ma_ref, sb_ref, mb_ref = rest[ATTN_ITEM_TILES:]
    t = pl.program_id(0)
    tq = ATTN_ITEM_TILES * ROW_TILE
    sub = ATTN_Q_SUB

    @pl.when(t == 0)
    def _():
        sb_ref[...] = jnp.zeros_like(sb_ref)
        mb_ref[...] = jnp.zeros_like(mb_ref)

    def step(s_new, m_new, s_old, m_old):
        lam = _diff_lambda(lam_ref, lam_init)
        gain = g_ref[...] * (1.0 - lam_init)
        k = k_ref[0]
        v = v_ref[0]
        for i, q_ref in enumerate(q_refs):
            s = _scores(_stack_maps(q_ref[0]), k)
            m = jnp.max(s, axis=-1, keepdims=True)
            r0 = slice(i * ROW_TILE, (i + 1) * ROW_TILE)
            r1 = slice(tq + i * ROW_TILE, tq + (i + 1) * ROW_TILE)
            s_new[r0, :] = s[:ROW_TILE]
            s_new[r1, :] = s[ROW_TILE:]
            m_new[r0, :] = m[:ROW_TILE]
            m_new[r1, :] = m[ROW_TILE:]
        for i in range(tq // sub):
            r0 = slice(i * sub, (i + 1) * sub)
            r1 = slice(tq + i * sub, tq + (i + 1) * sub)
            o_ref[0, r0, :] = _diff_softmax_pv(
                functools.partial(s_old.__getitem__, (r0, slice(None))), m_old[r0, :],
                functools.partial(s_old.__getitem__, (r1, slice(None))), m_old[r1, :],
                lam, v, gain)

    @pl.when(t % 2 == 0)
    def _():
        step(sa_ref, ma_ref, sb_ref, mb_ref)

    @pl.when(t % 2 == 1)
    def _():
        step(sb_ref, mb_ref, sa_ref, ma_ref)


def _attention(q, k, v, lam_vec, subln_g, lam_init, ctx_tiles):
    batch, ta, _ = q.shape
    n_ctx = ctx_tiles * ROW_TILE
    nt = ta // ROW_TILE - ctx_tiles
    g = subln_g.reshape(1, 128)
    small = [_full((4, ATTN_HEAD_DIM)), _full((1, 128))]
    cspec = pl.BlockSpec((BATCH_GROUP, n_ctx, ATTN_W), lambda b: (b, 0, 0))
    o_ctx = pl.pallas_call(
        functools.partial(_attn_ctx_kernel, lam_init=lam_init),
        grid=(batch // BATCH_GROUP,),
        in_specs=small + [cspec, cspec, cspec],
        out_specs=cspec,
        out_shape=jax.ShapeDtypeStruct((batch, n_ctx, ATTN_W), BF16),
        compiler_params=_cparams("arbitrary"),
        name="diff_attention_ctx",
    )(lam_vec, g, q, k, v)

    per = ATTN_ITEM_TILES
    assert nt % per == 0
    ni = nt // per
    n_items = batch * ATTN_HEADS * ni

    def item(j):
        return j // (ATTN_HEADS * ni), (j // ni) % ATTN_HEADS, j % ni

    def score_item(j):
        return item(jnp.minimum(j, n_items - 1))

    def finish_item(j):
        return item(jnp.maximum(j - 1, 0))

    def q_spec(part):
        def idx(j):
            b, h, t = score_item(j)
            return b, ctx_tiles + t * per + part, h
        return pl.BlockSpec((1, ROW_TILE, 128), idx)

    def k_idx(j):
        b, h, _ = score_item(j)
        return b, 0, h

    def v_idx(j):
        b, h, _ = finish_item(j)
        return b, 0, h

    def o_idx(j):
        b, h, t = finish_item(j)
        return b, t, h

    o_lat = pl.pallas_call(
        functools.partial(_attn_lat_kernel, lam_init=lam_init),
        grid=(n_items + 1,),
        in_specs=small + [pl.BlockSpec((1, ta, 128), k_idx), pl.BlockSpec((1, ta, 128), v_idx)]
        + [q_spec(part) for part in range(per)],
        out_specs=pl.BlockSpec((1, per * ROW_TILE, 128), o_idx),
        out_shape=jax.ShapeDtypeStruct((batch, nt * ROW_TILE, ATTN_W), BF16),
        scratch_shapes=[pltpu.VMEM((2 * per * ROW_TILE, ta), F32),
                        pltpu.VMEM((2 * per * ROW_TILE, 1), F32)] * 2,
        compiler_params=_cparams("arbitrary"),
        name="diff_attention",
    )(lam_vec, g, k, v, *([q] * per))
    return o_ctx, o_lat


def _dft_tables(n):
    k = np.arange(n, dtype=np.int64)
    ang = 2.0 * np.pi * ((k[:, None] * k[None, :]) % n).astype(np.float64) / n
    return np.cos(ang), np.sin(ang)


def _fourier_kernel(f_ref, cs_ref, dl_ref, dc_ref, w_ref, b_ref, oc_ref, ol_ref, ac_ref, al_ref,
                    *, n_ctx, n_lat):
    t = pl.program_id(0)
    b = pl.program_id(1)

    def stage1(rows0, n):
        a = jnp.dot(f_ref[0, rows0:rows0 + n, :], cs_ref[...], preferred_element_type=F32)
        return a[:, :F_W].astype(BF16), a[:, F_W:].astype(BF16)

    def stage2(dft, a, n):
        z = jnp.dot(dft, a, preferred_element_type=F32)
        z = z * (1.0 / math.sqrt(n * F_GROUP_W))
        o = jnp.dot(z.astype(BF16), w_ref[...], preferred_element_type=F32) + b_ref[...]
        return o.astype(BF16)

    @pl.when(t == 0)
    def _():
        ac_ref[0:n_ctx, :], ac_ref[n_ctx:, :] = stage1(0, n_ctx)
        oc_ref[0] = stage2(dc_ref[...], ac_ref[...], n_ctx)

    @pl.when(t == 1)
    def _():
        al_ref[b, 0:n_lat, :], al_ref[b, n_lat:, :] = stage1(n_ctx, n_lat)

    @pl.when(t >= 1)
    def _():
        ol_ref[0] = stage2(dl_ref[...], al_ref[b], n_lat)


def _fourier(f, fourier_w, fourier_b, n_ctx):
    batch, ta, _ = f.shape
    n_lat = ta - n_ctx
    ft = min(FOURIER_TILE, n_lat)
    assert n_lat % ft == 0
    cc, sc = _dft_tables(F_GROUP_W)
    eye = np.eye(F_GROUPS)
    cs = jnp.asarray(np.concatenate([np.kron(eye, cc), np.kron(eye, sc)], axis=1), BF16)
    cl, sl = _dft_tables(n_lat)
    dft_lat = jnp.asarray(np.concatenate([cl, -sl], axis=1), BF16)
    cx, sx = _dft_tables(n_ctx)
    dft_ctx = jnp.asarray(np.concatenate([cx, -sx], axis=1), BF16)
    w_blk = jnp.einsum('gce,gh->gche', fourier_w, jnp.eye(F_GROUPS, dtype=F32))
    w_blk = w_blk.reshape(F_W, F_W).astype(BF16)
    return pl.pallas_call(
        functools.partial(_fourier_kernel, n_ctx=n_ctx, n_lat=n_lat),
        grid=(1 + n_lat // ft, batch),
        in_specs=[
                  pl.BlockSpec((1, ta, F_W),
                               lambda t, b: (jnp.where(t <= 1, b, batch - 1), 0, 0)),
                  _full((F_W, 2 * F_W)),
                  pl.BlockSpec((ft, 2 * n_lat), lambda t, b: (jnp.maximum(t - 1, 0), 0)),
                  _full((n_ctx, 2 * n_ctx)),
                  _full((F_W, F_W)), _full((1, F_W))],
        out_specs=[pl.BlockSpec((1, n_ctx, F_W),
                                lambda t, b: (jnp.where(t == 0, b, batch - 1), 0, 0)),
                   pl.BlockSpec((1, ft, F_W),
                                lambda t, b: (jnp.where(t == 0, 0, b), jnp.maximum(t - 1, 0), 0))],
        out_shape=[jax.ShapeDtypeStruct((batch, n_ctx, F_W), BF16),
                   jax.ShapeDtypeStruct((batch, n_lat, F_W), BF16)],
        scratch_shapes=[pltpu.VMEM((2 * n_ctx, F_W), BF16),
                        pltpu.VMEM((batch, 2 * n_lat, F_W), BF16)],
        compiler_params=_cparams("arbitrary", "arbitrary"),
        name="fourier_mix",
    )(f, cs, dft_lat, dft_ctx, w_blk, fourier_b.reshape(1, F_W))


def _outproj_ln_kernel(ac_ref, al_ref, bc_ref, bl_ref, hc_ref, hl_ref, *rest, ctx_tiles):
    g = BATCH_GROUP
    gate_refs = rest[:g]
    w_ref, g_ref, beta_ref, o_ref, wbf_ref = rest[g:]

    @pl.when(_first_step())
    def _():
        wbf_ref[...] = w_ref[...].astype(BF16)

    is_ctx = pl.program_id(1) < ctx_tiles
    a = jnp.concatenate([jnp.where(is_ctx, ac_ref[i], al_ref[i]) for i in range(g)], axis=0)
    b2 = jnp.concatenate([jnp.where(is_ctx, bc_ref[i], bl_ref[i]) for i in range(g)], axis=0)
    wa = a.shape[1]
    y = jnp.dot(a, wbf_ref[0:wa, :], preferred_element_type=F32)
    y = y + jnp.dot(b2, wbf_ref[wa:, :], preferred_element_type=F32)
    rows = o_ref.shape[1]
    for i in range(g):
        h = jnp.where(is_ctx, hc_ref[i], hl_ref[i])
        o_ref[i] = _layer_norm(ALPHA * h + gate_refs[i][0] * y[i * rows:(i + 1) * rows],
                               g_ref[...], beta_ref[...])


def _outproj_ln(a_ctx, a_lat, b_ctx, b_lat, h_ctx, h_lat, mods, layer, w_out, ln_g, ln_b,
                ctx_tiles):
    batch = h_lat.shape[0]
    ta = h_ctx.shape[1] + h_lat.shape[1]
    nt = ta // ROW_TILE
    row = lambda w: pl.BlockSpec((BATCH_GROUP, ROW_TILE, w), lambda b, t: (b, t, 0))
    return pl.pallas_call(
        functools.partial(_outproj_ln_kernel, ctx_tiles=ctx_tiles),
        grid=(batch // BATCH_GROUP, nt),
        in_specs=_split_specs(a_lat.shape[2], ctx_tiles) + _split_specs(b_lat.shape[2], ctx_tiles)
        + _split_specs(D_MODEL, ctx_tiles)
        + _mod_specs_grouped(layer, 2, batch, ctx_tiles)
        + [_resident((D_MODEL, D_MODEL)), _full((1, D_MODEL)), _full((1, D_MODEL))],
        out_specs=row(D_MODEL),
        out_shape=jax.ShapeDtypeStruct((batch, ta, D_MODEL), F32),
        scratch_shapes=[pltpu.VMEM((D_MODEL, D_MODEL), BF16)],
        compiler_params=_cparams("arbitrary", "arbitrary"),
        name="outproj_ln",
    )(a_ctx, a_lat, b_ctx, b_lat, h_ctx, h_lat, *([mods] * BATCH_GROUP), w_out,
      ln_g.reshape(1, D_MODEL),
      ln_b.reshape(1, D_MODEL))


def _ffn_kernel(h_ref, hp_ref, hn_ref, *rest, group, ctx_tiles, tile_off, nt_seq):
    shift_refs, scale_refs, gate_refs = rest[:group], rest[group:2 * group], rest[2 * group:3 * group]
    (wup_ref, bup_ref, cw_ref, cb_ref, wdn_ref, bdn_ref, g_ref, beta_ref,
     o_ref, uext_ref, ubf_ref, act_ref, wdnbf_ref) = rest[3 * group:]

    @pl.when(_first_step())
    def _():
        wdnbf_ref[...] = wdn_ref[...].astype(BF16)

    t = pl.program_id(1) + tile_off
    seg_first = (t == 0) | (t == ctx_tiles)
    seg_last = (t == nt_seq - 1) | (t == ctx_tiles - 1)
    tm = h_ref.shape[1]
    ext = tm + 2 * HALO
    for b in range(group):
        sc = 1.0 + scale_refs[b][0]
        sh = shift_refs[b][0]
        r0 = b * ext
        uext_ref[r0:r0 + HALO, :] = hp_ref[b] * sc + sh
        uext_ref[r0 + HALO:r0 + HALO + tm, :] = h_ref[b] * sc + sh
        uext_ref[r0 + HALO + tm:r0 + ext, :] = hn_ref[b] * sc + sh
    ubf_ref[...] = uext_ref[...].astype(BF16)
    row8 = lax.broadcasted_iota(jnp.int32, (8, FF_TILE), 0)

    def hidden(col0):
        cols = slice(col0, col0 + FF_TILE)
        zr_all = jnp.dot(ubf_ref[...], wup_ref[:, cols], preferred_element_type=F32)
        bup = bup_ref[:, cols]
        cw = cw_ref[:, cols]
        bias = cb_ref[:, cols] + (cw[0:1] + cw[1:2] + cw[2:3]) * bup
        outs = []
        for b in range(group):
            zr = zr_all[b * ext:(b + 1) * ext]
            z0 = zr[HALO:HALO + tm]
            prev = jnp.where(seg_first, -bup, zr[HALO - 1:HALO])
            nxt = jnp.where(seg_last, -bup, zr[HALO + tm:HALO + tm + 1])
            down = pltpu.roll(z0, 1, 0)
            up = pltpu.roll(z0, tm - 1, 0)
            zm1 = jnp.concatenate([jnp.where(row8 == 0, prev, down[0:8]), down[8:]], axis=0)
            zp1 = jnp.concatenate([up[:tm - 8], jnp.where(row8 == 7, nxt, up[tm - 8:])], axis=0)
            outs.append(cw[0:1] * zm1 + cw[1:2] * z0 + cw[2:3] * zp1 + bias)
        return jnp.concatenate(outs, axis=0)

    for j in range(D_FF // FF_TILE):
        val = hidden(j * FF_TILE)
        gat = hidden(D_FF + j * FF_TILE)
        act_ref[:, j * FF_TILE:(j + 1) * FF_TILE] = (val * _silu(gat)).astype(BF16)
    f = jnp.dot(act_ref[...], wdnbf_ref[...], preferred_element_type=F32) + bdn_ref[...]
    for b in range(group):
        o_ref[b] = _layer_norm(ALPHA * h_ref[b] + gate_refs[b][0] * f[b * tm:(b + 1) * tm],
                               g_ref[...], beta_ref[...])


def _ffn(h, mods, layer, w_up, b_up, conv_w, conv_b, w_down, b_down, ln_g, ln_b,
         ctx_tiles, tile_off, rows=ROW_TILE, group=1):
    batch, ta, _ = h.shape
    assert batch % group == 0
    nt_seq = ta // rows
    nt = nt_seq - tile_off
    hb = rows // HALO
    n_hblk = ta // HALO

    def mspecs(j):
        def spec(i):
            def idx(bg, t):
                r = jnp.where(t + tile_off < ctx_tiles, batch, bg * group + i)
                return ((layer * MOD_ROWS + r) * 6 + j, 0, 0)
            return pl.BlockSpec((1, 1, D_MODEL), idx)
        return [spec(i) for i in range(group)]

    resident = lambda shape: pl.BlockSpec((None,) + shape, lambda *_: (layer, 0, 0),
                                          pipeline_mode=pl.Buffered(1))
    ext = rows + 2 * HALO
    return pl.pallas_call(
        functools.partial(_ffn_kernel, group=group, ctx_tiles=ctx_tiles, tile_off=tile_off,
                          nt_seq=nt_seq),
        grid=(batch // group, nt),
        in_specs=[pl.BlockSpec((group, rows, D_MODEL), lambda b, t: (b, t + tile_off, 0)),
                  pl.BlockSpec((group, HALO, D_MODEL),
                               lambda b, t: (b, jnp.maximum((t + tile_off) * hb - 1, 0), 0)),
                  pl.BlockSpec((group, HALO, D_MODEL),
                               lambda b, t: (b, jnp.minimum((t + tile_off + 1) * hb, n_hblk - 1), 0))]
        + mspecs(3) + mspecs(4) + mspecs(5)
        + [resident((D_MODEL, 2 * D_FF)), _full((1, 2 * D_FF)),
           _full((3, 2 * D_FF)), _full((1, 2 * D_FF)),
           resident((D_FF, D_MODEL)), _full((1, D_MODEL)),
           _full((1, D_MODEL)), _full((1, D_MODEL))],
        out_specs=pl.BlockSpec((group, rows, D_MODEL), lambda b, t: (b, t, 0)),
        out_shape=jax.ShapeDtypeStruct((batch, nt * rows, D_MODEL), F32),
        scratch_shapes=[pltpu.VMEM((group * ext, D_MODEL), F32),
                        pltpu.VMEM((group * ext, D_MODEL), BF16),
                        pltpu.VMEM((group * rows, D_FF), BF16),
                        pltpu.VMEM((D_FF, D_MODEL), BF16)],
        compiler_params=_cparams("arbitrary", "arbitrary"),
        name="conv_ffn_ln",
    )(h, h, h, *([mods] * (3 * group)), w_up, b_up.reshape(1, -1), conv_w, conv_b.reshape(1, -1),
      w_down, b_down.reshape(1, -1), ln_g.reshape(1, -1), ln_b.reshape(1, -1))


def _time_major_rows(b, batch):
    return pl.ds(b, ROW_TILE, stride=batch)


def _inproj_ssm_kernel(h_ref, *rest, batch):
    g = BATCH_GROUP
    shift_refs, scale_refs = rest[:g], rest[g:2 * g]
    w_ref, z_ref, xbc_ref, dt_ref, us_ref, wbf_ref = rest[2 * g:]

    @pl.when(_first_step())
    def _():
        head = SSD_W + XBC_W
        wbf_ref[0:head, :] = w_ref[0:head, :].astype(BF16)
        wbf_ref[head:head + S5_W, :] = w_ref[head + DT_W:head + DT_W + S5_W, :].astype(BF16)
        pad = jnp.zeros((DT_PAD - DT_W, D_MODEL), F32)
        wbf_ref[head + S5_W:, :] = jnp.concatenate(
            [w_ref[head:head + DT_W, :], pad], axis=0).astype(BF16)

    u = _modulated_rows(lambda i: h_ref[i], shift_refs, scale_refs)
    p = lax.dot_general(u, wbf_ref[...], (((1,), (1,)), ((), ())), preferred_element_type=F32)
    _store_rows(z_ref, p[:, :SSD_W])
    _store_rows(xbc_ref, p[:, SSD_W:SSD_W + XBC_W])
    _store_rows(dt_ref, p[:, SSD_W + XBC_W + S5_W:])
    for i in range(g):
        rows = _time_major_rows(pl.program_id(1) * g + i, batch)
        for j in range(S5_W // 128):
            col0 = SSD_W + XBC_W + j * 128
            us_ref[j, rows, :] = p[i * ROW_TILE:(i + 1) * ROW_TILE, col0:col0 + 128]


def _inproj_ssm(h, mods, layer, w_in, ctx_tiles):
    batch, ta, _ = h.shape
    assert batch % BATCH_GROUP == 0
    nt = ta // ROW_TILE
    row = lambda w: pl.BlockSpec((BATCH_GROUP, ROW_TILE, w), lambda t, b: (b, t, 0))

    def mspecs(j):
        def spec(i):
            def idx(t, bg):
                r = jnp.where(t < ctx_tiles, batch, bg * BATCH_GROUP + i)
                return ((layer * MOD_ROWS + r) * 6 + j, 0, 0)
            return pl.BlockSpec((1, 1, D_MODEL), idx)
        return [spec(i) for i in range(BATCH_GROUP)]

    return pl.pallas_call(
        functools.partial(_inproj_ssm_kernel, batch=batch),
        grid=(nt, batch // BATCH_GROUP),
        in_specs=[row(D_MODEL)] + mspecs(0) + mspecs(1) + [_resident((SSM_IN_W, D_MODEL))],
        out_specs=[row(SSD_W), row(XBC_W), row(DT_PAD),
                   pl.BlockSpec((S5_W // 128, ROW_TILE * batch, 128), lambda t, b: (0, t, 0))],
        out_shape=[jax.ShapeDtypeStruct((batch, ta, SSD_W), F32),
                   jax.ShapeDtypeStruct((batch, ta, XBC_W), F32),
                   jax.ShapeDtypeStruct((batch, ta, DT_PAD), F32),
                   jax.ShapeDtypeStruct((S5_W // 128, ta * batch, 128), F32)],
        scratch_shapes=[pltpu.VMEM((SSM_IN_PAD, D_MODEL), BF16)],
        compiler_params=_cparams("arbitrary", "arbitrary"),
        name="inproj_ssm",
    )(h, *([mods] * (2 * BATCH_GROUP)), jnp.swapaxes(w_in, 0, 1))


def _cumsum_rows(v):
    n = v.shape[0]
    row = lax.broadcasted_iota(jnp.int32, v.shape, 0)
    s = 1
    while s < n:
        v = v + jnp.where(row >= s, pltpu.roll(v, s, 0), 0.0)
        s *= 2
    return v


def _expand_heads(v, e_ref):
    hi = v.astype(BF16)
    lo = (v - hi.astype(F32)).astype(BF16)
    e = e_ref[...]
    return (jnp.dot(hi, e, preferred_element_type=F32)
            + jnp.dot(lo, e, preferred_element_type=F32))


def _ssd_kernel(xbc_ref, xp_ref, xn_ref, dt_ref, z_ref, cw_ref, cb_ref, alog_ref, dtb_ref,
                dsk_ref, ng_ref, ef_ref, eb_ref, o_ref,
                xs_ref, cd_ref, st_ref, dec_ref, y_ref, *, n_tiles, ctx_tiles):
    s = pl.program_id(1)
    q = SSD_CHUNK
    cpt = ROW_TILE // q
    n_chunks = n_tiles * cpt
    ctx_chunks = ctx_tiles * cpt
    gw = SSD_W // SSD_GROUPS
    hpg = SSD_HEADS // SSD_GROUPS

    @pl.when(s < n_tiles)
    def _phase0():
        p = s
        seg_first = (p == 0) | (p == ctx_tiles)
        seg_last = (p == ctx_tiles - 1) | (p == n_tiles - 1)
        xr = xbc_ref[0]
        prev = jnp.where(seg_first, 0.0, xp_ref[0, HALO - 1:HALO, :])
        nxt = jnp.where(seg_last, 0.0, xn_ref[0, 0:1, :])
        row8 = lax.broadcasted_iota(jnp.int32, (8, XBC_W), 0)
        down = pltpu.roll(xr, 1, 0)
        up = pltpu.roll(xr, ROW_TILE - 1, 0)
        xm1 = jnp.concatenate([jnp.where(row8 == 0, prev, down[0:8]), down[8:]], axis=0)
        xp1 = jnp.concatenate([up[:ROW_TILE - 8], jnp.where(row8 == 7, nxt, up[ROW_TILE - 8:])],
                              axis=0)
        cw = cw_ref[...]
        xs_tile = _silu(cw[0:1] * xm1 + cw[1:2] * xr + cw[2:3] * xp1 + cb_ref[...])
        xs_ref[p] = xs_tile

        raw = dt_ref[0] + dtb_ref[...]
        dt_tile = jnp.maximum(raw, 0.0) + jnp.log1p(jnp.exp(-jnp.abs(raw)))
        a_row = -jnp.exp(alog_ref[...])
        for i in range(cpt):
            c = p * cpt + i
            xs = xs_tile[i * q:(i + 1) * q]
            dtv = dt_tile[i * q:(i + 1) * q]
            adt = dtv * a_row
            cum = _cumsum_rows(adt)
            tot = cum[q - 1:q, :]
            lane = lax.broadcasted_iota(jnp.int32, cum.shape, 1)
            cc = jnp.where(lane < SSD_HEADS, cum, tot - cum + adt)
            cd_ref[c, 0] = cc
            cd_ref[c, 1] = dtv
            w_end = jnp.exp(tot - cc) * dtv
            dec16 = jnp.broadcast_to(jnp.exp(tot), (16, DT_PAD))
            x = xs[:, :SSD_W]
            for d, e_ref in enumerate((ef_ref, eb_ref)):
                wx = (_expand_heads(w_end, e_ref) * x).astype(BF16)
                for g in range(SSD_GROUPS):
                    bmt = xs[:, SSD_W + g * SSD_STATE:SSD_W + (g + 1) * SSD_STATE].T.astype(BF16)
                    st_ref[c, d, :, g * gw:(g + 1) * gw] = jnp.dot(
                        bmt, wx[:, g * gw:(g + 1) * gw], preferred_element_type=F32)
                dec_ref[c, d] = _expand_heads(dec16, e_ref)[0:8]

    @pl.when(s == n_tiles)
    def _recurrence():
        fwd = list(range(n_chunks))
        bwd = list(range(ctx_chunks - 1, -1, -1)) + list(range(n_chunks - 1, ctx_chunks - 1, -1))
        for d, order in enumerate((fwd, bwd)):
            for col0 in range(0, SSD_W, 128):
                cols = slice(col0, col0 + 128)
                state = jnp.zeros((SSD_STATE, 128), F32)
                for ci in order:
                    contrib = st_ref[ci, d, :, cols]
                    st_ref[ci, d, :, cols] = state
                    state = state * dec_ref[ci, d, 0:1, cols] + contrib

    @pl.when(s >= n_tiles)
    def _phase1():
        p = s - n_tiles + ctx_tiles
        rowi = lax.broadcasted_iota(jnp.int32, (q, q), 0)
        coli = lax.broadcasted_iota(jnp.int32, (q, q), 1)
        lower = coli <= rowi
        upper = coli >= rowi
        lane = lax.broadcasted_iota(jnp.int32, (q, 128), 1)
        neg = jnp.float32(-jnp.inf)
        for i in range(cpt):
            c = p * cpt + i
            rows = slice(i * q, (i + 1) * q)
            xs = xs_ref[p, rows, :]
            x = xs[:, :SSD_W]
            cc = cd_ref[c, 0]
            dtv = cd_ref[c, 1]
            cct = cc.T
            dtt = dtv.T
            ecc = jnp.exp(cc)
            ef = _expand_heads(ecc, ef_ref)
            eb = _expand_heads(ecc, eb_ref)
            for g in range(SSD_GROUPS):
                bm = xs[:, SSD_W + g * SSD_STATE:SSD_W + (g + 1) * SSD_STATE].astype(BF16)
                cm = xs[:, SSD_W + (SSD_GROUPS + g) * SSD_STATE:
                        SSD_W + (SSD_GROUPS + g + 1) * SSD_STATE].astype(BF16)
                gmat = lax.dot_general(cm, bm, (((1,), (1,)), ((), ())),
                                       preferred_element_type=F32)
                sl = slice(g * gw, (g + 1) * gw)
                yoff = (ef[:, sl] * jnp.dot(cm, st_ref[c, 0, :, sl].astype(BF16),
                                            preferred_element_type=F32)
                        + eb[:, sl] * jnp.dot(cm, st_ref[c, 1, :, sl].astype(BF16),
                                              preferred_element_type=F32))
                for pair in range(hpg // 2):
                    col0 = g * gw + pair * 128
                    xpair = x[:, col0:col0 + 128].astype(BF16)
                    res = []
                    for hh in range(2):
                        hd = g * hpg + pair * 2 + hh
                        hb_ = SSD_HEADS + hd
                        lf = jnp.exp(jnp.where(lower, cc[:, hd:hd + 1] - cct[hd:hd + 1, :], neg))
                        lb = jnp.exp(jnp.where(upper, cc[:, hb_:hb_ + 1] - cct[hb_:hb_ + 1, :],
                                               neg))
                        mt = gmat * (lf * dtt[hd:hd + 1, :] + lb * dtt[hb_:hb_ + 1, :])
                        res.append(jnp.dot(mt.astype(BF16), xpair, preferred_element_type=F32))
                    ydiag = jnp.where(lane < SSD_HEAD_DIM, res[0], res[1])
                    y_ref[rows, col0:col0 + 128] = (
                        ydiag + yoff[:, pair * 128:(pair + 1) * 128]
                        + dsk_ref[:, col0:col0 + 128] * x[:, col0:col0 + 128])
        gated = y_ref[...] * _silu(z_ref[0])
        normed = gated * lax.rsqrt(jnp.mean(gated * gated, axis=-1, keepdims=True) + LN_EPS)
        o_ref[0] = (normed * ng_ref[...]).astype(BF16)


def _ssd(xbc, dt, z, conv_w, conv_b, a_log, dt_bias, d_skip, norm_g, n_ctx):
    batch, ta, _ = xbc.shape
    q = SSD_CHUNK
    n_chunks = ta // q
    n_tiles = ta // ROW_TILE
    ctx_tiles = n_ctx // ROW_TILE
    hb = ROW_TILE // HALO
    n_hblk = ta // HALO
    pad24 = lambda v: jnp.pad(v.reshape(1, DT_W), ((0, 0), (0, DT_PAD - DT_W)))
    heads = np.arange(SSD_HEADS)
    ef = np.zeros((DT_PAD, SSD_W), np.float32)
    eb = np.zeros((DT_PAD, SSD_W), np.float32)
    for hd in heads:
        ef[hd, hd * SSD_HEAD_DIM:(hd + 1) * SSD_HEAD_DIM] = 1.0
        eb[SSD_HEADS + hd, hd * SSD_HEAD_DIM:(hd + 1) * SSD_HEAD_DIM] = 1.0
    dsk = jnp.repeat(d_skip.astype(F32), SSD_HEAD_DIM).reshape(1, SSD_W)
    ph0 = lambda s: s < n_tiles
    tile = lambda w: pl.BlockSpec(
        (1, ROW_TILE, w), lambda b, s: (b, jnp.where(ph0(s), s, n_tiles - 1), 0))
    return pl.pallas_call(
        functools.partial(_ssd_kernel, n_tiles=n_tiles, ctx_tiles=ctx_tiles),
        grid=(batch, 2 * n_tiles - ctx_tiles),
        in_specs=[tile(XBC_W),
                  pl.BlockSpec((1, HALO, XBC_W),
                               lambda b, s: (b, jnp.where(ph0(s), jnp.maximum(s * hb - 1, 0), 0), 0)),
                  pl.BlockSpec((1, HALO, XBC_W),
                               lambda b, s: (b, jnp.where(ph0(s), jnp.minimum((s + 1) * hb, n_hblk - 1), 0), 0)),
                  tile(DT_PAD),
                  pl.BlockSpec((1, ROW_TILE, SSD_W),
                               lambda b, s: (b, jnp.where(ph0(s), 0, s - n_tiles + ctx_tiles), 0)),
                  _full((3, XBC_W)), _full((1, XBC_W)), _full((1, DT_PAD)), _full((1, DT_PAD)),
                  _full((1, SSD_W)), _full((1, SSD_W)),
                  _full((DT_PAD, SSD_W)), _full((DT_PAD, SSD_W))],
        out_specs=pl.BlockSpec(
            (1, ROW_TILE, SSD_W), lambda b, s: (b, jnp.where(ph0(s), 0, s - n_tiles), 0)),
        out_shape=jax.ShapeDtypeStruct((batch, ta - n_ctx, SSD_W), BF16),
        scratch_shapes=[pltpu.VMEM((n_tiles, ROW_TILE, XBC_W), F32),
                        pltpu.VMEM((n_chunks, 2, q, DT_PAD), F32),
                        pltpu.VMEM((n_chunks, 2, SSD_STATE, SSD_W), F32),
                        pltpu.VMEM((n_chunks, 2, 8, SSD_W), F32),
                        pltpu.VMEM((ROW_TILE, SSD_W), F32)],
        compiler_params=_cparams("parallel", "arbitrary"),
        name="ssd_bidir",
    )(xbc, xbc, xbc, dt, z, conv_w, conv_b.reshape(1, XBC_W), pad24(a_log), pad24(dt_bias),
      dsk, norm_g.reshape(1, SSD_W), jnp.asarray(ef, BF16), jnp.asarray(eb, BF16))


def _s5_disc_kernel(lr_ref, li_ref, ldt_ref, bre_ref, bim_ref, cre_ref, cim_ref,
                    a_ref, bd_ref, cd_ref):
    lr, li = lr_ref[...], li_ref[...]
    dt = jnp.exp(ldt_ref[...])
    mag = jnp.exp(dt * lr)
    ab_re, ab_im = mag * jnp.cos(dt * li), mag * jnp.sin(dt * li)
    den = lr * lr + li * li
    k_re = ((ab_re - 1.0) * lr + ab_im * li) / den
    k_im = (ab_im * lr - (ab_re - 1.0) * li) / den
    bre, bim = bre_ref[...], bim_ref[...]
    for d in range(2):
        a_ref[d, :, 0:S5_NSTATE] = jnp.broadcast_to(ab_re[d:d + 1], (8, S5_NSTATE))
        a_ref[d, :, S5_NSTATE:] = jnp.broadcast_to(ab_im[d:d + 1], (8, S5_NSTATE))
        kr, ki = k_re[d:d + 1], k_im[d:d + 1]
        bd_ref[d, :, 0:S5_NSTATE] = (kr * bre - ki * bim).astype(BF16)
        bd_ref[d, :, S5_NSTATE:] = (kr * bim + ki * bre).astype(BF16)
        cd_ref[d, 0:S5_NSTATE, :] = cre_ref[d].astype(BF16)
        cd_ref[d, S5_NSTATE:, :] = (-cim_ref[d]).astype(BF16)


def _s5_discretize(lam_re, lam_im, log_dt, b_re, b_im, c_re, c_im):
    eye = jnp.eye(S5_GROUPS, dtype=F32)
    bd = lambda b: jnp.einsum('gph,gk->ghkp', b, eye).reshape(S5_W, S5_NSTATE)
    cd = lambda cc: jnp.einsum('dghp,gk->dgpkh', cc, eye).reshape(2, S5_NSTATE, S5_W)
    ldt = jnp.repeat(log_dt, S5_STATE, axis=-1)
    return pl.pallas_call(
        _s5_disc_kernel,
        out_shape=[jax.ShapeDtypeStruct((2, 8, 2 * S5_NSTATE), F32),
                   jax.ShapeDtypeStruct((2, S5_W, 2 * S5_NSTATE), BF16),
                   jax.ShapeDtypeStruct((2, 2 * S5_NSTATE, S5_W), BF16)],
        compiler_params=pltpu.CompilerParams(vmem_limit_bytes=VMEM_LIMIT_BYTES),
        name="s5_discretize",
    )(lam_re.reshape(2, S5_NSTATE), lam_im.reshape(2, S5_NSTATE), ldt,
      bd(b_re), bd(b_im), cd(c_re), cd(c_im))


S5_TIME_CHUNK = 128


def _s5_scan_kernel(uf_ref, ub_ref, a_ref, bd_ref, cd_ref, yf_ref, yb_ref,
                    hsf_ref, hsb_ref, carry_ref, *, batch):
    j = pl.program_id(0)
    n = S5_NSTATE
    tc = S5_TIME_CHUNK
    n_slab = S5_W // 128
    half = hsf_ref.shape[0] // 2
    chains = ((uf_ref, hsf_ref, yf_ref), (ub_ref, hsb_ref, yb_ref))

    @pl.when(j == 0)
    def _():
        carry_ref[...] = jnp.zeros_like(carry_ref)

    for d, (u_ref, hs_ref, _) in enumerate(chains):
        for r in (0, half):
            u = jnp.concatenate([u_ref[s, r:r + half, :] for s in range(n_slab)], axis=1)
            hs_ref[r:r + half, :] = jnp.dot(u.astype(BF16), bd_ref[d], preferred_element_type=F32)

    for d, (_, hs_ref, _) in enumerate(chains):
        ar = jnp.broadcast_to(a_ref[d, 0:1, 0:n], (batch, n))
        ai = jnp.broadcast_to(a_ref[d, 0:1, n:], (batch, n))
        hr = carry_ref[d, :, 0:n]
        hi = carry_ref[d, :, n:]
        for step in (range(tc) if d == 0 else range(tc - 1, -1, -1)):
            rows = slice(step * batch, (step + 1) * batch)
            hr, hi = (ar * hr - ai * hi + hs_ref[rows, 0:n],
                      ar * hi + ai * hr + hs_ref[rows, n:])
            hs_ref[rows, 0:n] = hr
            hs_ref[rows, n:] = hi
        carry_ref[d, :, 0:n] = hr
        carry_ref[d, :, n:] = hi

    for d, (_, hs_ref, y_ref) in enumerate(chains):
        for r in (0, half):
            y = jnp.dot(hs_ref[r:r + half, :].astype(BF16), cd_ref[d], preferred_element_type=F32)
            for s in range(n_slab):
                y_ref[s, r:r + half, :] = y[:, s * 128:(s + 1) * 128]


def _s5_scan(us_flat, a, bd, cd, batch, n_ctx):
    n_slab, rows_total, _ = us_flat.shape
    ta = rows_total // batch
    tc = S5_TIME_CHUNK
    n_chunks = ta // tc
    ctx_chunks = n_ctx // tc
    blk = tc * batch

    def bwd_chunk(j):
        return jnp.where(j < ctx_chunks, ctx_chunks - 1 - j, n_chunks - 1 - (j - ctx_chunks))

    fwd_spec = pl.BlockSpec((n_slab, blk, 128), lambda j: (0, j, 0))
    bwd_spec = pl.BlockSpec((n_slab, blk, 128), lambda j: (0, bwd_chunk(j), 0))
    return pl.pallas_call(
        functools.partial(_s5_scan_kernel, batch=batch),
        grid=(n_chunks,),
        in_specs=[fwd_spec, bwd_spec, _full((2, 8, 2 * S5_NSTATE)),
                  _full((2, S5_W, 2 * S5_NSTATE)), _full((2, 2 * S5_NSTATE, S5_W))],
        out_specs=[fwd_spec, bwd_spec],
        out_shape=[jax.ShapeDtypeStruct((n_slab, rows_total, 128), F32)] * 2,
        scratch_shapes=[pltpu.VMEM((blk, 2 * S5_NSTATE), F32),
                        pltpu.VMEM((blk, 2 * S5_NSTATE), F32),
                        pltpu.VMEM((2, batch, 2 * S5_NSTATE), F32)],
        compiler_params=_cparams("arbitrary"),
        name="s5_scan",
    )(us_flat, us_flat, a, bd, cd)


def _merge_ln_kernel(gs_ref, yf_ref, yb_ref, us_ref, h_ref, *rest, batch):
    g = BATCH_GROUP
    gate_refs = rest[:g]
    dd_ref, gw_ref, gb_ref, w_ref, g_ref, beta_ref, o_ref, wbf_ref = rest[g:]

    @pl.when(_first_step())
    def _():
        wbf_ref[...] = w_ref[...].astype(BF16)

    def s5_input(i):
        rows = _time_major_rows(pl.program_id(1) * g + i, batch)
        y5 = jnp.concatenate([yf_ref[s, rows, :] + yb_ref[s, rows, :]
                              for s in range(S5_W // 128)], axis=1)
        us = jnp.concatenate([us_ref[s, rows, :] for s in range(S5_W // 128)], axis=1)
        return y5 + dd_ref[...] * us

    ge = jax.nn.gelu(jnp.concatenate([s5_input(i) for i in range(g)], axis=0))
    s5 = ge * jax.nn.sigmoid(
        jnp.dot(ge.astype(BF16), gw_ref[...], preferred_element_type=F32) + gb_ref[...])
    gs = jnp.concatenate([gs_ref[i] for i in range(g)], axis=0)
    y = jnp.dot(gs, wbf_ref[0:SSD_W, :], preferred_element_type=F32)
    y = y + jnp.dot(s5.astype(BF16), wbf_ref[SSD_W:, :], preferred_element_type=F32)
    rows = o_ref.shape[1]
    for i in range(g):
        o_ref[i] = _layer_norm(ALPHA * h_ref[i] + gate_refs[i][0] * y[i * rows:(i + 1) * rows],
                               g_ref[...], beta_ref[...])


def _merge_ln(g_ssd, y5_fwd, y5_bwd, us_t, h, mods, layer, s5_d, glu_w, glu_b, w_out, ln_g, ln_b,
              ctx_tiles):
    batch, ta, _ = h.shape
    nt = ta // ROW_TILE - ctx_tiles
    n_slab = S5_W // 128
    tm_rows = ROW_TILE * batch
    g = BATCH_GROUP
    gate_specs = [
        pl.BlockSpec((1, 1, D_MODEL),
                     lambda t, bg, i=i: ((layer * MOD_ROWS + bg * g + i) * 6 + 2, 0, 0))
        for i in range(g)]
    return pl.pallas_call(
        functools.partial(_merge_ln_kernel, batch=batch),
        grid=(nt, batch // g),
        in_specs=[pl.BlockSpec((g, ROW_TILE, SSD_W), lambda t, b: (b, t, 0)),
                  pl.BlockSpec((n_slab, tm_rows, 128), lambda t, b: (0, t + ctx_tiles, 0)),
                  pl.BlockSpec((n_slab, tm_rows, 128), lambda t, b: (0, t + ctx_tiles, 0)),
                  pl.BlockSpec((n_slab, tm_rows, 128), lambda t, b: (0, t + ctx_tiles, 0)),
                  pl.BlockSpec((g, ROW_TILE, D_MODEL), lambda t, b: (b, t + ctx_tiles, 0))]
        + gate_specs
        + [_full((1, S5_W)), _full((S5_W, S5_W)), _full((1, S5_W)),
           _resident((D_MODEL, D_MODEL)), _full((1, D_MODEL)), _full((1, D_MODEL))],
        out_specs=pl.BlockSpec((g, ROW_TILE, D_MODEL), lambda t, b: (b, t, 0)),
        out_shape=jax.ShapeDtypeStruct((batch, nt * ROW_TILE, D_MODEL), F32),
        scratch_shapes=[pltpu.VMEM((D_MODEL, D_MODEL), BF16)],
        compiler_params=_cparams("arbitrary", "arbitrary"),
        name="merge_outproj_ln",
    )(g_ssd, y5_fwd, y5_bwd, us_t, h, *([mods] * g), s5_d.reshape(1, S5_W), glu_w.astype(BF16),
      glu_b.reshape(1, S5_W), w_out, ln_g.reshape(1, -1), ln_b.reshape(1, -1))


def _attn_layer(h_ctx, h_lat, mods, layer, i, p, keep_ctx):
    n_ctx = h_ctx.shape[1]
    ctx_tiles = n_ctx // ROW_TILE
    lam_init = 0.8 - 0.6 * math.exp(-0.3 * layer)
    cos, sin = _rope_tables(h_lat.shape[1], n_ctx)
    q, k, v, f = _inproj_attn(h_ctx, h_lat, mods, layer, p['attn_w_in'][i],
                              cos, sin, ctx_tiles)
    o_ctx, o_lat = _attention(q, k, v, p['attn_lambda'][i], p['attn_subln_g'][i], lam_init,
                              ctx_tiles)
    fm_ctx, fm_lat = _fourier(f, p['fourier_w'][i], p['fourier_b'][i], n_ctx)
    h1 = _outproj_ln(o_ctx, o_lat, fm_ctx, fm_lat, h_ctx, h_lat, mods, layer,
                     p['attn_w_out'][i],
                     p['ln_g'][layer, 0], p['ln_b'][layer, 0], ctx_tiles)
    return _ffn(h1, mods, layer, p['ffn_w_up_bf16'], p['ffn_b_up'][layer],
                p['ffn_conv_w'][layer], p['ffn_conv_b'][layer],
                p['ffn_w_down'], p['ffn_b_down'][layer],
                p['ln_g'][layer, 1], p['ln_b'][layer, 1], ctx_tiles, 0 if keep_ctx else ctx_tiles,
                group=FFN_BATCH_GROUP)


def _ssm_layer(h, mods, layer, i, n_ctx, p, keep_ctx):
    assert not keep_ctx, "an SSM layer that must also emit context rows is not implemented"
    ctx_tiles = n_ctx // ROW_TILE
    batch = h.shape[0]
    z, xbc, dt, us_t = _inproj_ssm(h, mods, layer, p['ssm_w_in'][i], ctx_tiles)
    g_ssd = _ssd(xbc, dt, z, p['ssd_conv_w'][i], p['ssd_conv_b'][i], p['ssd_a_log'][i],
                 p['ssd_dt_bias'][i], p['ssd_d'][i], p['ssd_norm_g'][i], n_ctx)
    a, bd, cd = _s5_discretize(p['s5_lambda_re'][i], p['s5_lambda_im'][i], p['s5_log_dt'][i],
                               p['s5_b_re'][i], p['s5_b_im'][i], p['s5_c_re'][i], p['s5_c_im'][i])
    y5_fwd, y5_bwd = _s5_scan(us_t, a, bd, cd, batch, n_ctx)
    h1 = _merge_ln(g_ssd, y5_fwd, y5_bwd, us_t, h, mods, layer, p['s5_d'][i], p['s5_glu_w'][i],
                   p['s5_glu_b'][i], p['ssm_w_out'][i],
                   p['ln_g'][layer, 0], p['ln_b'][layer, 0], ctx_tiles)
    return _ffn(h1, mods, layer, p['ffn_w_up_bf16'], p['ffn_b_up'][layer],
                p['ffn_conv_w'][layer], p['ffn_conv_b'][layer],
                p['ffn_w_down'], p['ffn_b_down'][layer],
                p['ln_g'][layer, 1], p['ln_b'][layer, 1], 0, 0, rows=FFN_LAT_ROWS)


def kernel(x, c, ctx, c_ctx, ada_w, ada_b, ln_g, ln_b, ffn_w_up, ffn_b_up, ffn_conv_w, ffn_conv_b, ffn_w_down, ffn_b_down, attn_w_in, attn_lambda, attn_subln_g, fourier_w, fourier_b, attn_w_out, ssm_w_in, ssd_conv_w, ssd_conv_b, ssd_a_log, ssd_dt_bias, ssd_d, ssd_norm_g, s5_lambda_re, s5_lambda_im, s5_log_dt, s5_b_re, s5_b_im, s5_c_re, s5_c_im, s5_d, s5_glu_w, s5_glu_b, ssm_w_out):
    p = dict(ln_g=ln_g, ln_b=ln_b, ffn_w_up=ffn_w_up, ffn_b_up=ffn_b_up, ffn_conv_w=ffn_conv_w,
             ffn_conv_b=ffn_conv_b, ffn_w_down=ffn_w_down, ffn_b_down=ffn_b_down,
             attn_w_in=attn_w_in, attn_lambda=attn_lambda, attn_subln_g=attn_subln_g,
             fourier_w=fourier_w, fourier_b=fourier_b, attn_w_out=attn_w_out, ssm_w_in=ssm_w_in,
             ssd_conv_w=ssd_conv_w, ssd_conv_b=ssd_conv_b, ssd_a_log=ssd_a_log,
             ssd_dt_bias=ssd_dt_bias, ssd_d=ssd_d, ssd_norm_g=ssd_norm_g,
             s5_lambda_re=s5_lambda_re, s5_lambda_im=s5_lambda_im, s5_log_dt=s5_log_dt,
             s5_b_re=s5_b_re, s5_b_im=s5_b_im, s5_c_re=s5_c_re, s5_c_im=s5_c_im, s5_d=s5_d,
             s5_glu_w=s5_glu_w, s5_glu_b=s5_glu_b, ssm_w_out=ssm_w_out)
    batch, n_lat, _ = x.shape
    n_ctx = ctx.shape[1]
    assert n_ctx == ROW_TILE and n_lat % ROW_TILE == 0 and batch < MOD_ROWS
    mods = _ada_mods(c, c_ctx, ada_w, ada_b)
    p['ffn_w_up_bf16'] = ffn_w_up.astype(BF16)
    assert DEPTH == 2
    h = _attn_layer(ctx, x, mods, 0, 0, p, keep_ctx=True)
    return _ssm_layer(h, mods, 1, 0, n_ctx, p, keep_ctx=False)
```

```python
import functools
import math

import numpy as np
import jax
import jax.numpy as jnp
from jax import lax
from jax.experimental import pallas as pl
from jax.experimental.pallas import tpu as pltpu

F32 = jnp.float32
BF16 = jnp.bfloat16

D_MODEL = 1024
DEPTH = 2
GRID_W = 64
ROPE_BASE = 10000.0
LN_EPS = 1e-5
ALPHA = (2 * DEPTH) ** 0.25
ATTN_W = 768
ATTN_HEADS = 6
ATTN_HEAD_DIM = 64
F_W = 256
F_GROUPS = 4
F_GROUP_W = 64
ATTN_IN_W = 2 * ATTN_W + ATTN_W + F_W
SSD_W = 768
SSD_HEADS = 12
SSD_HEAD_DIM = 64
SSD_GROUPS = 2
SSD_STATE = 128
SSD_CHUNK = 128
XBC_W = SSD_W + 2 * SSD_GROUPS * SSD_STATE
DT_W = 2 * SSD_HEADS
DT_PAD = 128
S5_W = 256
S5_GROUPS = 16
S5_STATE = 64
S5_NSTATE = S5_GROUPS * S5_STATE
SSM_IN_W = SSD_W + XBC_W + DT_W + S5_W
SSM_IN_PAD = SSD_W + XBC_W + S5_W + DT_PAD
D_FF = 2816
FF_TILE = 256

ROW_TILE = 256
ATTN_Q_SUB = 256
ATTN_ITEM_TILES = 2
FFN_LAT_ROWS = 512
BATCH_GROUP = 4
FFN_BATCH_GROUP = 2
FOURIER_TILE = 1024
HALO = 8
MOD_ROWS = 16
ADA_CHUNKS_PER_STEP = 2
VMEM_LIMIT_BYTES = 56 * 1024 * 1024


def _cparams(*sem):
    return pltpu.CompilerParams(dimension_semantics=sem, vmem_limit_bytes=VMEM_LIMIT_BYTES)


def _silu(v):
    return v * jax.nn.sigmoid(v)


def _layer_norm(v, g, b):
    mu = jnp.mean(v, axis=-1, keepdims=True)
    d = v - mu
    var = jnp.mean(d * d, axis=-1, keepdims=True)
    return d * lax.rsqrt(var + LN_EPS) * g + b


def _full(shape):
    nd = len(shape)
    return pl.BlockSpec(shape, lambda *_: (0,) * nd)


def _resident(shape):
    nd = len(shape)
    return pl.BlockSpec(shape, lambda *_: (0,) * nd, pipeline_mode=pl.Buffered(1))


def _first_step():
    return (pl.program_id(0) == 0) & (pl.program_id(1) == 0)


def _mod_specs_grouped(layer, j, batch, ctx_tiles):
    def spec(i):
        def idx(bg, t):
            row = jnp.where(t < ctx_tiles, batch, bg * BATCH_GROUP + i)
            return ((layer * MOD_ROWS + row) * 6 + j, 0, 0)
        return pl.BlockSpec((1, 1, D_MODEL), idx)
    return [spec(i) for i in range(BATCH_GROUP)]


def _ada_kernel(c_ref, w_ref, b_ref, o_ref):
    s = _silu(c_ref[...])
    w = w_ref[0]
    s_hi = s.astype(BF16)
    s_lo = (s - s_hi.astype(F32)).astype(BF16)
    w_hi = w.astype(BF16)
    w_lo = (w - w_hi.astype(F32)).astype(BF16)
    dot = functools.partial(jnp.dot, preferred_element_type=F32)
    o_ref[0] = dot(s_hi, w_hi) + (dot(s_lo, w_hi) + dot(s_hi, w_lo)) + b_ref[0]


def _ada_mods(c, c_ctx, ada_w, ada_b):
    batch = c.shape[0]
    nl = ada_w.shape[0]
    c_all = jnp.concatenate(
        [c, c_ctx[None], jnp.zeros((MOD_ROWS - batch - 1, D_MODEL), F32)], axis=0)
    out = pl.pallas_call(
        _ada_kernel,
        grid=(nl, 6 // ADA_CHUNKS_PER_STEP),
        in_specs=[_full((MOD_ROWS, D_MODEL)),
                  pl.BlockSpec((1, D_MODEL, ADA_CHUNKS_PER_STEP * D_MODEL), lambda l, j: (l, 0, j)),
                  pl.BlockSpec((1, 1, ADA_CHUNKS_PER_STEP * D_MODEL), lambda l, j: (l, 0, j))],
        out_specs=pl.BlockSpec((1, MOD_ROWS, ADA_CHUNKS_PER_STEP * D_MODEL),
                               lambda l, j: (l, 0, j)),
        out_shape=jax.ShapeDtypeStruct((nl, MOD_ROWS, 6 * D_MODEL), F32),
        compiler_params=_cparams("arbitrary", "arbitrary"),
        name="ada_mods",
    )(c_all, ada_w, ada_b.reshape(nl, 1, 6 * D_MODEL))
    return out.reshape(nl * MOD_ROWS * 6, 1, D_MODEL)


def _modulated_rows(h_of, shift_refs, scale_refs):
    return jnp.concatenate(
        [(h_of(i) * (1.0 + scale_refs[i][0]) + shift_refs[i][0]).astype(BF16)
         for i in range(BATCH_GROUP)], axis=0)


def _store_rows(ref, val):
    rows = ref.shape[1]
    for i in range(BATCH_GROUP):
        ref[i] = val[i * rows:(i + 1) * rows].astype(ref.dtype)


def _inproj_attn_kernel(hc_ref, hl_ref, *rest, ctx_tiles):
    g = BATCH_GROUP
    shift_refs, scale_refs = rest[:g], rest[g:2 * g]
    w_ref, cos_ref, sin_ref, q_ref, k_ref, v_ref, f_ref, wbf_ref = rest[2 * g:]

    @pl.when(_first_step())
    def _():
        wbf_ref[...] = w_ref[...].astype(BF16)

    is_ctx = pl.program_id(1) < ctx_tiles
    u = _modulated_rows(lambda i: jnp.where(is_ctx, hc_ref[i], hl_ref[i]), shift_refs, scale_refs)
    p = jnp.dot(u, wbf_ref[...], preferred_element_type=F32)
    cos = jnp.concatenate([cos_ref[...]] * g, axis=0)
    sin = jnp.concatenate([sin_ref[...]] * g, axis=0)
    lane = lax.broadcasted_iota(jnp.int32, cos.shape, 1)
    first_half = (lane % ATTN_HEAD_DIM) < (ATTN_HEAD_DIM // 2)

    def rope(blk):
        partner = jnp.where(first_half, pltpu.roll(blk, 128 - 32, 1), pltpu.roll(blk, 32, 1))
        return blk * cos + partner * sin

    qk_scale = ATTN_HEAD_DIM ** -0.5 * math.log2(math.e)
    rows = q_ref.shape[1]
    for i in range(ATTN_HEADS):
        lo, hi = i * 128, (i + 1) * 128
        qh = (rope(p[:, lo:hi]) * qk_scale).astype(BF16)
        kh = rope(p[:, ATTN_W + lo:ATTN_W + hi]).astype(BF16)
        for b in range(g):
            q_ref[b, :, lo:hi] = qh[b * rows:(b + 1) * rows]
            k_ref[b, :, lo:hi] = kh[b * rows:(b + 1) * rows]
    _store_rows(v_ref, p[:, 2 * ATTN_W:3 * ATTN_W])
    _store_rows(f_ref, p[:, 3 * ATTN_W:])


def _rope_tables(n_lat, n_ctx):
    rows = n_lat // GRID_W
    row = jnp.repeat(jnp.arange(rows, dtype=F32), GRID_W)
    col = jnp.tile(jnp.arange(GRID_W, dtype=F32), rows)
    n_freq = ATTN_HEAD_DIM // 4
    inv_freq = ROPE_BASE ** (-jnp.arange(n_freq, dtype=F32) / n_freq)
    ang = jnp.concatenate([row[:, None] * inv_freq, col[:, None] * inv_freq], axis=-1)
    cos, sin = jnp.cos(ang), jnp.sin(ang)
    cos128 = jnp.tile(cos, (1, 4))
    sin128 = jnp.tile(jnp.concatenate([-sin, sin], axis=-1), (1, 2))
    cos_all = jnp.concatenate([jnp.ones((n_ctx, 128), F32), cos128], axis=0)
    sin_all = jnp.concatenate([jnp.zeros((n_ctx, 128), F32), sin128], axis=0)
    return cos_all, sin_all


def _split_specs(width, ctx_tiles):
    g = BATCH_GROUP
    return [pl.BlockSpec((g, ROW_TILE, width), lambda b, t: (b, jnp.minimum(t, ctx_tiles - 1), 0)),
            pl.BlockSpec((g, ROW_TILE, width), lambda b, t: (b, jnp.maximum(t - ctx_tiles, 0), 0))]


def _inproj_attn(h_ctx, h_lat, mods, layer, w_in, cos, sin, ctx_tiles):
    batch = h_lat.shape[0]
    assert batch % BATCH_GROUP == 0
    ta = h_ctx.shape[1] + h_lat.shape[1]
    nt = ta // ROW_TILE
    row = lambda w: pl.BlockSpec((BATCH_GROUP, ROW_TILE, w), lambda b, t: (b, t, 0))
    tab = pl.BlockSpec((ROW_TILE, 128), lambda b, t: (t, 0))
    n_mod = 2 * BATCH_GROUP
    return pl.pallas_call(
        functools.partial(_inproj_attn_kernel, ctx_tiles=ctx_tiles),
        grid=(batch // BATCH_GROUP, nt),
        in_specs=_split_specs(D_MODEL, ctx_tiles)
        + _mod_specs_grouped(layer, 0, batch, ctx_tiles)
        + _mod_specs_grouped(layer, 1, batch, ctx_tiles)
        + [_resident((D_MODEL, ATTN_IN_W)), tab, tab],
        out_specs=[row(ATTN_W), row(ATTN_W), row(ATTN_W), row(F_W)],
        out_shape=[jax.ShapeDtypeStruct((batch, ta, ATTN_W), BF16)] * 3
        + [jax.ShapeDtypeStruct((batch, ta, F_W), BF16)],
        scratch_shapes=[pltpu.VMEM((D_MODEL, ATTN_IN_W), BF16)],
        compiler_params=_cparams("arbitrary", "arbitrary"),
        name="inproj_attn",
    )(h_ctx, h_lat, *([mods] * n_mod), w_in, cos, sin)


def _diff_lambda(lam_ref, lam_init):
    lamv = lam_ref[...]
    l1 = jnp.sum(lamv[0:1] * lamv[1:2], axis=-1, keepdims=True)
    l2 = jnp.sum(lamv[2:3] * lamv[3:4], axis=-1, keepdims=True)
    return jnp.exp(l1) - jnp.exp(l2) + lam_init


def _stack_maps(q):
    lane = lax.broadcasted_iota(jnp.int32, q.shape, 1)
    zero = jnp.zeros_like(q)
    return jnp.concatenate([jnp.where(lane < ATTN_HEAD_DIM, q, zero),
                            jnp.where(lane >= ATTN_HEAD_DIM, q, zero)], axis=0)


def _scores(q2, k):
    return lax.dot_general(q2, k, (((1,), (1,)), ((), ())), preferred_element_type=F32)


def _diff_softmax_pv(load_s0, m0, load_s1, m1, lam, v, gain):
    e0 = jnp.exp2(load_s0() - m0)
    e1 = jnp.exp2(load_s1() - m1)
    l0 = jnp.sum(e0, axis=-1, keepdims=True)
    l1 = jnp.sum(e1, axis=-1, keepdims=True)
    w = e0 - e1 * (lam * l0 / l1)
    o = jnp.dot(w.astype(BF16), v, preferred_element_type=F32) * (1.0 / l0)
    o = o * lax.rsqrt(jnp.mean(o * o, axis=-1, keepdims=True) + LN_EPS)
    return (o * gain).astype(BF16)


def _attn_ctx_kernel(lam_ref, g_ref, q_ref, k_ref, v_ref, o_ref, *, lam_init):
    lam = _diff_lambda(lam_ref, lam_init)
    gain = g_ref[...] * (1.0 - lam_init)
    tq = q_ref.shape[1]
    for b in range(q_ref.shape[0]):
        for hd in range(ATTN_HEADS):
            cols = slice(hd * 128, (hd + 1) * 128)
            s = _scores(_stack_maps(q_ref[b, :, cols]), k_ref[b, :, cols])
            m = jnp.max(s, axis=-1, keepdims=True)

            def half(lo, s=s):
                return lambda: s[lo:lo + tq]

            o_ref[b, :, cols] = _diff_softmax_pv(half(0), m[:tq], half(tq), m[tq:], lam,
                                                 v_ref[b, :, cols], gain)


def _attn_lat_kernel(lam_ref, g_ref, k_ref, v_ref, *rest, lam_init):
    q_refs = rest[:ATTN_ITEM_TILES]
    o_ref, sa_ref, ma_ref, sb_ref, mb_ref = rest[ATTN_ITEM_TILES:]
    t = pl.program_id(0)
    tq = ATTN_ITEM_TILES * ROW_TILE
    sub = ATTN_Q_SUB

    @pl.when(t == 0)
    def _():
        sb_ref[...] = jnp.zeros_like(sb_ref)
        mb_ref[...] = jnp.zeros_like(mb_ref)

    def step(s_new, m_new, s_old, m_old):
        lam = _diff_lambda(lam_ref, lam_init)
        gain = g_ref[...] * (1.0 - lam_init)
        q = jnp.concatenate([q_ref[0] for q_ref in q_refs], axis=0)
        s = _scores(_stack_maps(q), k_ref[0])
        s_new[...] = s
        m_new[...] = jnp.max(s, axis=-1, keepdims=True)
        v = v_ref[0]
        for i in range(tq // sub):
            r0 = slice(i * sub, (i + 1) * sub)
            r1 = slice(tq + i * sub, tq + (i + 1) * sub)
            o_ref[0, r0, :] = _diff_softmax_pv(
                functools.partial(s_old.__getitem__, (r0, slice(None))), m_old[r0, :],
                functools.partial(s_old.__getitem__, (r1, slice(None))), m_old[r1, :],
                lam, v, gain)

    @pl.when(t % 2 == 0)
    def _():
        step(sa_ref, ma_ref, sb_ref, mb_ref)

    @pl.when(t % 2 == 1)
    def _():
        step(sb_ref, mb_ref, sa_ref, ma_ref)


def _attention(q, k, v, lam_vec, subln_g, lam_init, ctx_tiles):
    batch, ta, _ = q.shape
    n_ctx = ctx_tiles * ROW_TILE
    nt = ta // ROW_TILE - ctx_tiles
    g = subln_g.reshape(1, 128)
    small = [_full((4, ATTN_HEAD_DIM)), _full((1, 128))]
    cspec = pl.BlockSpec((BATCH_GROUP, n_ctx, ATTN_W), lambda b: (b, 0, 0))
    o_ctx = pl.pallas_call(
        functools.partial(_attn_ctx_kernel, lam_init=lam_init),
        grid=(batch // BATCH_GROUP,),
        in_specs=small + [cspec, cspec, cspec],
        out_specs=cspec,
        out_shape=jax.ShapeDtypeStruct((batch, n_ctx, ATTN_W), BF16),
        compiler_params=_cparams("arbitrary"),
        name="diff_attention_ctx",
    )(lam_vec, g, q, k, v)

    per = ATTN_ITEM_TILES
    assert nt % per == 0
    ni = nt // per
    n_items = batch * ATTN_HEADS * ni

    def item(j):
        return j // (ATTN_HEADS * ni), (j // ni) % ATTN_HEADS, j % ni

    def score_item(j):
        return item(jnp.minimum(j, n_items - 1))

    def finish_item(j):
        return item(jnp.maximum(j - 1, 0))

    def q_spec(part):
        def idx(j):
            b, h, t = score_item(j)
            return b, ctx_tiles + t * per + part, h
        return pl.BlockSpec((1, ROW_TILE, 128), idx)

    def k_idx(j):
        b, h, _ = score_item(j)
        return b, 0, h

    def v_idx(j):
        b, h, _ = finish_item(j)
        return b, 0, h

    def o_idx(j):
        b, h, t = finish_item(j)
        return b, t, h

    o_lat = pl.pallas_call(
        functools.partial(_attn_lat_kernel, lam_init=lam_init),
        grid=(n_items + 1,),
        in_specs=small + [pl.BlockSpec((1, ta, 128), k_idx), pl.BlockSpec((1, ta, 128), v_idx)]
        + [q_spec(part) for part in range(per)],
        out_specs=pl.BlockSpec((1, per * ROW_TILE, 128), o_idx),
        out_shape=jax.ShapeDtypeStruct((batch, nt * ROW_TILE, ATTN_W), BF16),
        scratch_shapes=[pltpu.VMEM((2 * per * ROW_TILE, ta), F32),
                        pltpu.VMEM((2 * per * ROW_TILE, 1), F32)] * 2,
        compiler_params=_cparams("arbitrary"),
        name="diff_attention",
    )(lam_vec, g, k, v, *([q] * per))
    return o_ctx, o_lat


def _dft_tables(n):
    k = np.arange(n, dtype=np.int64)
    ang = 2.0 * np.pi * ((k[:, None] * k[None, :]) % n).astype(np.float64) / n
    return np.cos(ang), np.sin(ang)


def _fourier_kernel(f_ref, cs_ref, dl_ref, dc_ref, w_ref, b_ref, oc_ref, ol_ref, ac_ref, al_ref,
                    *, n_ctx, n_lat):
    t = pl.program_id(0)
    b = pl.program_id(1)

    def stage1(rows0, n):
        a = jnp.dot(f_ref[0, rows0:rows0 + n, :], cs_ref[...], preferred_element_type=F32)
        return a[:, :F_W].astype(BF16), a[:, F_W:].astype(BF16)

    def stage2(dft, a, n):
        z = jnp.dot(dft, a, preferred_element_type=F32)
        z = z * (1.0 / math.sqrt(n * F_GROUP_W))
        o = jnp.dot(z.astype(BF16), w_ref[...], preferred_element_type=F32) + b_ref[...]
        return o.astype(BF16)

    @pl.when(t == 0)
    def _():
        ac_ref[0:n_ctx, :], ac_ref[n_ctx:, :] = stage1(0, n_ctx)
        oc_ref[0] = stage2(dc_ref[...], ac_ref[...], n_ctx)

    @pl.when(t == 1)
    def _():
        al_ref[b, 0:n_lat, :], al_ref[b, n_lat:, :] = stage1(n_ctx, n_lat)

    @pl.when(t >= 1)
    def _():
        ol_ref[0] = stage2(dl_ref[...], al_ref[b], n_lat)


def _fourier(f, fourier_w, fourier_b, n_ctx):
    batch, ta, _ = f.shape
    n_lat = ta - n_ctx
    ft = min(FOURIER_TILE, n_lat)
    assert n_lat % ft == 0
    cc, sc = _dft_tables(F_GROUP_W)
    eye = np.eye(F_GROUPS)
    cs = jnp.asarray(np.concatenate([np.kron(eye, cc), np.kron(eye, sc)], axis=1), BF16)
    cl, sl = _dft_tables(n_lat)
    dft_lat = jnp.asarray(np.concatenate([cl, -sl], axis=1), BF16)
    cx, sx = _dft_tables(n_ctx)
    dft_ctx = jnp.asarray(np.concatenate([cx, -sx], axis=1), BF16)
    w_blk = jnp.einsum('gce,gh->gche', fourier_w, jnp.eye(F_GROUPS, dtype=F32))
    w_blk = w_blk.reshape(F_W, F_W).astype(BF16)
    return pl.pallas_call(
        functools.partial(_fourier_kernel, n_ctx=n_ctx, n_lat=n_lat),
        grid=(1 + n_lat // ft, batch),
        in_specs=[
                  pl.BlockSpec((1, ta, F_W),
                               lambda t, b: (jnp.where(t <= 1, b, batch - 1), 0, 0)),
                  _full((F_W, 2 * F_W)),
                  pl.BlockSpec((ft, 2 * n_lat), lambda t, b: (jnp.maximum(t - 1, 0), 0)),
                  _full((n_ctx, 2 * n_ctx)),
                  _full((F_W, F_W)), _full((1, F_W))],
        out_specs=[pl.BlockSpec((1, n_ctx, F_W),
                                lambda t, b: (jnp.where(t == 0, b, batch - 1), 0, 0)),
                   pl.BlockSpec((1, ft, F_W),
                                lambda t, b: (jnp.where(t == 0, 0, b), jnp.maximum(t - 1, 0), 0))],
        out_shape=[jax.ShapeDtypeStruct((batch, n_ctx, F_W), BF16),
                   jax.ShapeDtypeStruct((batch, n_lat, F_W), BF16)],
        scratch_shapes=[pltpu.VMEM((2 * n_ctx, F_W), BF16),
                        pltpu.VMEM((batch, 2 * n_lat, F_W), BF16)],
        compiler_params=_cparams("arbitrary", "arbitrary"),
        name="fourier_mix",
    )(f, cs, dft_lat, dft_ctx, w_blk, fourier_b.reshape(1, F_W))


def _outproj_ln_kernel(ac_ref, al_ref, bc_ref, bl_ref, hc_ref, hl_ref, *rest, ctx_tiles):
    g = BATCH_GROUP
    gate_refs = rest[:g]
    w_ref, g_ref, beta_ref, o_ref, wbf_ref = rest[g:]

    @pl.when(_first_step())
    def _():
        wbf_ref[...] = w_ref[...].astype(BF16)

    is_ctx = pl.program_id(1) < ctx_tiles
    a = jnp.concatenate([jnp.where(is_ctx, ac_ref[i], al_ref[i]) for i in range(g)], axis=0)
    b2 = jnp.concatenate([jnp.where(is_ctx, bc_ref[i], bl_ref[i]) for i in range(g)], axis=0)
    wa = a.shape[1]
    y = jnp.dot(a, wbf_ref[0:wa, :], preferred_element_type=F32)
    y = y + jnp.dot(b2, wbf_ref[wa:, :], preferred_element_type=F32)
    rows = o_ref.shape[1]
    for i in range(g):
        h = jnp.where(is_ctx, hc_ref[i], hl_ref[i])
        o_ref[i] = _layer_norm(ALPHA * h + gate_refs[i][0] * y[i * rows:(i + 1) * rows],
                               g_ref[...], beta_ref[...])


def _outproj_ln(a_ctx, a_lat, b_ctx, b_lat, h_ctx, h_lat, mods, layer, w_out, ln_g, ln_b,
                ctx_tiles):
    batch = h_lat.shape[0]
    ta = h_ctx.shape[1] + h_lat.shape[1]
    nt = ta // ROW_TILE
    row = lambda w: pl.BlockSpec((BATCH_GROUP, ROW_TILE, w), lambda b, t: (b, t, 0))
    return pl.pallas_call(
        functools.partial(_outproj_ln_kernel, ctx_tiles=ctx_tiles),
        grid=(batch // BATCH_GROUP, nt),
        in_specs=_split_specs(a_lat.shape[2], ctx_tiles) + _split_specs(b_lat.shape[2], ctx_tiles)
        + _split_specs(D_MODEL, ctx_tiles)
        + _mod_specs_grouped(layer, 2, batch, ctx_tiles)
        + [_resident((D_MODEL, D_MODEL)), _full((1, D_MODEL)), _full((1, D_MODEL))],
        out_specs=row(D_MODEL),
        out_shape=jax.ShapeDtypeStruct((batch, ta, D_MODEL), F32),
        scratch_shapes=[pltpu.VMEM((D_MODEL, D_MODEL), BF16)],
        compiler_params=_cparams("arbitrary", "arbitrary"),
        name="outproj_ln",
    )(a_ctx, a_lat, b_ctx, b_lat, h_ctx, h_lat, *([mods] * BATCH_GROUP), w_out,
      ln_g.reshape(1, D_MODEL),
      ln_b.reshape(1, D_MODEL))


def _ffn_kernel(h_ref, hp_ref, hn_ref, *rest, group, ctx_tiles, tile_off, nt_seq):
    shift_refs, scale_refs, gate_refs = rest[:group], rest[group:2 * group], rest[2 * group:3 * group]
    (wup_ref, bup_ref, cw_ref, cb_ref, wdn_ref, bdn_ref, g_ref, beta_ref,
     o_ref, uext_ref, ubf_ref, act_ref, wdnbf_ref) = rest[3 * group:]

    @pl.when(_first_step())
    def _():
        wdnbf_ref[...] = wdn_ref[...].astype(BF16)

    t = pl.program_id(1) + tile_off
    seg_first = (t == 0) | (t == ctx_tiles)
    seg_last = (t == nt_seq - 1) | (t == ctx_tiles - 1)
    tm = h_ref.shape[1]
    ext = tm + 2 * HALO
    for b in range(group):
        sc = 1.0 + scale_refs[b][0]
        sh = shift_refs[b][0]
        r0 = b * ext
        uext_ref[r0:r0 + HALO, :] = hp_ref[b] * sc + sh
        uext_ref[r0 + HALO:r0 + HALO + tm, :] = h_ref[b] * sc + sh
        uext_ref[r0 + HALO + tm:r0 + ext, :] = hn_ref[b] * sc + sh
    ubf_ref[...] = uext_ref[...].astype(BF16)
    row8 = lax.broadcasted_iota(jnp.int32, (8, FF_TILE), 0)

    def hidden(col0):
        cols = slice(col0, col0 + FF_TILE)
        zr_all = jnp.dot(ubf_ref[...], wup_ref[:, cols], preferred_element_type=F32)
        bup = bup_ref[:, cols]
        cw = cw_ref[:, cols]
        bias = cb_ref[:, cols] + (cw[0:1] + cw[1:2] + cw[2:3]) * bup
        outs = []
        for b in range(group):
            zr = zr_all[b * ext:(b + 1) * ext]
            z0 = zr[HALO:HALO + tm]
            prev = jnp.where(seg_first, -bup, zr[HALO - 1:HALO])
            nxt = jnp.where(seg_last, -bup, zr[HALO + tm:HALO + tm + 1])
            down = pltpu.roll(z0, 1, 0)
            up = pltpu.roll(z0, tm - 1, 0)
            zm1 = jnp.concatenate([jnp.where(row8 == 0, prev, down[0:8]), down[8:]], axis=0)
            zp1 = jnp.concatenate([up[:tm - 8], jnp.where(row8 == 7, nxt, up[tm - 8:])], axis=0)
            outs.append(cw[0:1] * zm1 + cw[1:2] * z0 + cw[2:3] * zp1 + bias)
        return jnp.concatenate(outs, axis=0)

    for j in range(D_FF // FF_TILE):
        val = hidden(j * FF_TILE)
        gat = hidden(D_FF + j * FF_TILE)
        act_ref[:, j * FF_TILE:(j + 1) * FF_TILE] = (val * _silu(gat)).astype(BF16)
    f = jnp.dot(act_ref[...], wdnbf_ref[...], preferred_element_type=F32) + bdn_ref[...]
    for b in range(group):
        o_ref[b] = _layer_norm(ALPHA * h_ref[b] + gate_refs[b][0] * f[b * tm:(b + 1) * tm],
                               g_ref[...], beta_ref[...])


def _ffn(h, mods, layer, w_up, b_up, conv_w, conv_b, w_down, b_down, ln_g, ln_b,
         ctx_tiles, tile_off, rows=ROW_TILE, group=1):
    batch, ta, _ = h.shape
    assert batch % group == 0
    nt_seq = ta // rows
    nt = nt_seq - tile_off
    hb = rows // HALO
    n_hblk = ta // HALO

    def mspecs(j):
        def spec(i):
            def idx(bg, t):
                r = jnp.where(t + tile_off < ctx_tiles, batch, bg * group + i)
                return ((layer * MOD_ROWS + r) * 6 + j, 0, 0)
            return pl.BlockSpec((1, 1, D_MODEL), idx)
        return [spec(i) for i in range(group)]

    resident = lambda shape: pl.BlockSpec((None,) + shape, lambda *_: (layer, 0, 0),
                                          pipeline_mode=pl.Buffered(1))
    ext = rows + 2 * HALO
    return pl.pallas_call(
        functools.partial(_ffn_kernel, group=group, ctx_tiles=ctx_tiles, tile_off=tile_off,
                          nt_seq=nt_seq),
        grid=(batch // group, nt),
        in_specs=[pl.BlockSpec((group, rows, D_MODEL), lambda b, t: (b, t + tile_off, 0)),
                  pl.BlockSpec((group, HALO, D_MODEL),
                               lambda b, t: (b, jnp.maximum((t + tile_off) * hb - 1, 0), 0)),
                  pl.BlockSpec((group, HALO, D_MODEL),
                               lambda b, t: (b, jnp.minimum((t + tile_off + 1) * hb, n_hblk - 1), 0))]
        + mspecs(3) + mspecs(4) + mspecs(5)
        + [resident((D_MODEL, 2 * D_FF)), _full((1, 2 * D_FF)),
           _full((3, 2 * D_FF)), _full((1, 2 * D_FF)),
           resident((D_FF, D_MODEL)), _full((1, D_MODEL)),
           _full((1, D_MODEL)), _full((1, D_MODEL))],
        out_specs=pl.BlockSpec((group, rows, D_MODEL), lambda b, t: (b, t, 0)),
        out_shape=jax.ShapeDtypeStruct((batch, nt * rows, D_MODEL), F32),
        scratch_shapes=[pltpu.VMEM((group * ext, D_MODEL), F32),
                        pltpu.VMEM((group * ext, D_MODEL), BF16),
                        pltpu.VMEM((group * rows, D_FF), BF16),
                        pltpu.VMEM((D_FF, D_MODEL), BF16)],
        compiler_params=_cparams("arbitrary", "arbitrary"),
        name="conv_ffn_ln",
    )(h, h, h, *([mods] * (3 * group)), w_up, b_up.reshape(1, -1), conv_w, conv_b.reshape(1, -1),
      w_down, b_down.reshape(1, -1), ln_g.reshape(1, -1), ln_b.reshape(1, -1))


def _time_major_rows(b, batch):
    return pl.ds(b, ROW_TILE, stride=batch)


def _inproj_ssm_kernel(h_ref, *rest, batch):
    g = BATCH_GROUP
    shift_refs, scale_refs = rest[:g], rest[g:2 * g]
    w_ref, z_ref, xbc_ref, dt_ref, us_ref, wbf_ref = rest[2 * g:]

    @pl.when(_first_step())
    def _():
        head = SSD_W + XBC_W
        wbf_ref[0:head, :] = w_ref[0:head, :].astype(BF16)
        wbf_ref[head:head + S5_W, :] = w_ref[head + DT_W:head + DT_W + S5_W, :].astype(BF16)
        pad = jnp.zeros((DT_PAD - DT_W, D_MODEL), F32)
        wbf_ref[head + S5_W:, :] = jnp.concatenate(
            [w_ref[head:head + DT_W, :], pad], axis=0).astype(BF16)

    u = _modulated_rows(lambda i: h_ref[i], shift_refs, scale_refs)
    p = lax.dot_general(u, wbf_ref[...], (((1,), (1,)), ((), ())), preferred_element_type=F32)
    _store_rows(z_ref, p[:, :SSD_W])
    _store_rows(xbc_ref, p[:, SSD_W:SSD_W + XBC_W])
    _store_rows(dt_ref, p[:, SSD_W + XBC_W + S5_W:])
    for i in range(g):
        rows = _time_major_rows(pl.program_id(1) * g + i, batch)
        for j in range(S5_W // 128):
            col0 = SSD_W + XBC_W + j * 128
            us_ref[j, rows, :] = p[i * ROW_TILE:(i + 1) * ROW_TILE, col0:col0 + 128]


def _inproj_ssm(h, mods, layer, w_in, ctx_tiles):
    batch, ta, _ = h.shape
    assert batch % BATCH_GROUP == 0
    nt = ta // ROW_TILE
    row = lambda w: pl.BlockSpec((BATCH_GROUP, ROW_TILE, w), lambda t, b: (b, t, 0))

    def mspecs(j):
        def spec(i):
            def idx(t, bg):
                r = jnp.where(t < ctx_tiles, batch, bg * BATCH_GROUP + i)
                return ((layer * MOD_ROWS + r) * 6 + j, 0, 0)
            return pl.BlockSpec((1, 1, D_MODEL), idx)
        return [spec(i) for i in range(BATCH_GROUP)]

    return pl.pallas_call(
        functools.partial(_inproj_ssm_kernel, batch=batch),
        grid=(nt, batch // BATCH_GROUP),
        in_specs=[row(D_MODEL)] + mspecs(0) + mspecs(1) + [_resident((SSM_IN_W, D_MODEL))],
        out_specs=[row(SSD_W), row(XBC_W), row(DT_PAD),
                   pl.BlockSpec((S5_W // 128, ROW_TILE * batch, 128), lambda t, b: (0, t, 0))],
        out_shape=[jax.ShapeDtypeStruct((batch, ta, SSD_W), F32),
                   jax.ShapeDtypeStruct((batch, ta, XBC_W), F32),
                   jax.ShapeDtypeStruct((batch, ta, DT_PAD), F32),
                   jax.ShapeDtypeStruct((S5_W // 128, ta * batch, 128), F32)],
        scratch_shapes=[pltpu.VMEM((SSM_IN_PAD, D_MODEL), BF16)],
        compiler_params=_cparams("arbitrary", "arbitrary"),
        name="inproj_ssm",
    )(h, *([mods] * (2 * BATCH_GROUP)), jnp.swapaxes(w_in, 0, 1))


def _cumsum_rows(v):
    n = v.shape[0]
    row = lax.broadcasted_iota(jnp.int32, v.shape, 0)
    s = 1
    while s < n:
        v = v + jnp.where(row >= s, pltpu.roll(v, s, 0), 0.0)
        s *= 2
    return v


def _expand_heads(v, e_ref):
    hi = v.astype(BF16)
    lo = (v - hi.astype(F32)).astype(BF16)
    e = e_ref[...]
    return (jnp.dot(hi, e, preferred_element_type=F32)
            + jnp.dot(lo, e, preferred_element_type=F32))


def _ssd_kernel(xbc_ref, xp_ref, xn_ref, dt_ref, za_ref, zb_ref, cw_ref, cb_ref, alog_ref, dtb_ref,
                dsk_ref, ng_ref, ef_ref, eb_ref, o_ref,
                xs_ref, cd_ref, st_ref, dec_ref, y_ref, *, n_tiles, ctx_tiles):
    s = pl.program_id(1)
    q = SSD_CHUNK
    cpt = ROW_TILE // q
    n_chunks = n_tiles * cpt
    ctx_chunks = ctx_tiles * cpt
    gw = SSD_W // SSD_GROUPS
    hpg = SSD_HEADS // SSD_GROUPS

    @pl.when(s < n_tiles)
    def _phase0():
        p = s
        seg_first = (p == 0) | (p == ctx_tiles)
        seg_last = (p == ctx_tiles - 1) | (p == n_tiles - 1)
        xr = xbc_ref[0]
        prev = jnp.where(seg_first, 0.0, xp_ref[0, HALO - 1:HALO, :])
        nxt = jnp.where(seg_last, 0.0, xn_ref[0, 0:1, :])
        row8 = lax.broadcasted_iota(jnp.int32, (8, XBC_W), 0)
        down = pltpu.roll(xr, 1, 0)
        up = pltpu.roll(xr, ROW_TILE - 1, 0)
        xm1 = jnp.concatenate([jnp.where(row8 == 0, prev, down[0:8]), down[8:]], axis=0)
        xp1 = jnp.concatenate([up[:ROW_TILE - 8], jnp.where(row8 == 7, nxt, up[ROW_TILE - 8:])],
                              axis=0)
        cw = cw_ref[...]
        xs_tile = _silu(cw[0:1] * xm1 + cw[1:2] * xr + cw[2:3] * xp1 + cb_ref[...])
        xs_ref[p] = xs_tile

        raw = dt_ref[0] + dtb_ref[...]
        dt_tile = jnp.maximum(raw, 0.0) + jnp.log1p(jnp.exp(-jnp.abs(raw)))
        a_row = -jnp.exp(alog_ref[...])
        for i in range(cpt):
            c = p * cpt + i
            xs = xs_tile[i * q:(i + 1) * q]
            dtv = dt_tile[i * q:(i + 1) * q]
            adt = dtv * a_row
            cum = _cumsum_rows(adt)
            tot = cum[q - 1:q, :]
            lane = lax.broadcasted_iota(jnp.int32, cum.shape, 1)
            cc = jnp.where(lane < SSD_HEADS, cum, tot - cum + adt)
            cd_ref[c, 0] = cc
            cd_ref[c, 1] = dtv
            w_end = jnp.exp(tot - cc) * dtv
            dec16 = jnp.broadcast_to(jnp.exp(tot), (16, DT_PAD))
            x = xs[:, :SSD_W]
            for d, e_ref in enumerate((ef_ref, eb_ref)):
                wx = (_expand_heads(w_end, e_ref) * x).astype(BF16)
                for g in range(SSD_GROUPS):
                    bmt = xs[:, SSD_W + g * SSD_STATE:SSD_W + (g + 1) * SSD_STATE].T.astype(BF16)
                    st_ref[c, d, :, g * gw:(g + 1) * gw] = jnp.dot(
                        bmt, wx[:, g * gw:(g + 1) * gw], preferred_element_type=F32)
                dec_ref[c, d] = _expand_heads(dec16, e_ref)[0:8]

    @pl.when(s == n_tiles)
    def _recurrence():
        fwd = list(range(n_chunks))
        bwd = list(range(ctx_chunks - 1, -1, -1)) + list(range(n_chunks - 1, ctx_chunks - 1, -1))
        for d, order in enumerate((fwd, bwd)):
            for col0 in range(0, SSD_W, 128):
                cols = slice(col0, col0 + 128)
                state = jnp.zeros((SSD_STATE, 128), F32)
                for ci in order:
                    contrib = st_ref[ci, d, :, cols]
                    st_ref[ci, d, :, cols] = state
                    state = state * dec_ref[ci, d, 0:1, cols] + contrib

    @pl.when(s >= n_tiles)
    def _phase1():
        p = ctx_tiles + 2 * (s - n_tiles)
        rowi = lax.broadcasted_iota(jnp.int32, (q, q), 0)
        coli = lax.broadcasted_iota(jnp.int32, (q, q), 1)
        lower = coli <= rowi
        upper = coli >= rowi
        lane = lax.broadcasted_iota(jnp.int32, (q, 128), 1)
        neg = jnp.float32(-jnp.inf)
        for i in range(2 * cpt):
            c = p * cpt + i
            rows = slice(i * q, (i + 1) * q)
            xs = xs_ref[p + i // cpt, (i % cpt) * q:(i % cpt + 1) * q, :]
            x = xs[:, :SSD_W]
            cc = cd_ref[c, 0]
            dtv = cd_ref[c, 1]
            cct = cc.T
            dtt = dtv.T
            ecc = jnp.exp(cc)
            ef = _expand_heads(ecc, ef_ref)
            eb = _expand_heads(ecc, eb_ref)
            for g in range(SSD_GROUPS):
                bm = xs[:, SSD_W + g * SSD_STATE:SSD_W + (g + 1) * SSD_STATE].astype(BF16)
                cm = xs[:, SSD_W + (SSD_GROUPS + g) * SSD_STATE:
                        SSD_W + (SSD_GROUPS + g + 1) * SSD_STATE].astype(BF16)
                gmat = lax.dot_general(cm, bm, (((1,), (1,)), ((), ())),
                                       preferred_element_type=F32)
                sl = slice(g * gw, (g + 1) * gw)
                yoff = (ef[:, sl] * jnp.dot(cm, st_ref[c, 0, :, sl].astype(BF16),
                                            preferred_element_type=F32)
                        + eb[:, sl] * jnp.dot(cm, st_ref[c, 1, :, sl].astype(BF16),
                                              preferred_element_type=F32))
                for pair in range(hpg // 2):
                    col0 = g * gw + pair * 128
                    xpair = x[:, col0:col0 + 128].astype(BF16)
                    res = []
                    for hh in range(2):
                        hd = g * hpg + pair * 2 + hh
                        hb_ = SSD_HEADS + hd
                        lf = jnp.exp(jnp.where(lower, cc[:, hd:hd + 1] - cct[hd:hd + 1, :], neg))
                        lb = jnp.exp(jnp.where(upper, cc[:, hb_:hb_ + 1] - cct[hb_:hb_ + 1, :],
                                               neg))
                        mt = gmat * (lf * dtt[hd:hd + 1, :] + lb * dtt[hb_:hb_ + 1, :])
                        res.append(jnp.dot(mt.astype(BF16), xpair, preferred_element_type=F32))
                    ydiag = jnp.where(lane < SSD_HEAD_DIM, res[0], res[1])
                    y_ref[rows, col0:col0 + 128] = (
                        ydiag + yoff[:, pair * 128:(pair + 1) * 128]
                        + dsk_ref[:, col0:col0 + 128] * x[:, col0:col0 + 128])
        gated = y_ref[...] * _silu(jnp.concatenate([za_ref[0], zb_ref[0]], axis=0))
        normed = gated * lax.rsqrt(jnp.mean(gated * gated, axis=-1, keepdims=True) + LN_EPS)
        o_ref[0] = (normed * ng_ref[...]).astype(BF16)


def _ssd(xbc, dt, z, conv_w, conv_b, a_log, dt_bias, d_skip, norm_g, n_ctx):
    batch, ta, _ = xbc.shape
    q = SSD_CHUNK
    n_chunks = ta // q
    n_tiles = ta // ROW_TILE
    ctx_tiles = n_ctx // ROW_TILE
    hb = ROW_TILE // HALO
    n_hblk = ta // HALO
    pad24 = lambda v: jnp.pad(v.reshape(1, DT_W), ((0, 0), (0, DT_PAD - DT_W)))
    heads = np.arange(SSD_HEADS)
    ef = np.zeros((DT_PAD, SSD_W), np.float32)
    eb = np.zeros((DT_PAD, SSD_W), np.float32)
    for hd in heads:
        ef[hd, hd * SSD_HEAD_DIM:(hd + 1) * SSD_HEAD_DIM] = 1.0
        eb[SSD_HEADS + hd, hd * SSD_HEAD_DIM:(hd + 1) * SSD_HEAD_DIM] = 1.0
    dsk = jnp.repeat(d_skip.astype(F32), SSD_HEAD_DIM).reshape(1, SSD_W)
    ph0 = lambda s: s < n_tiles
    tile = lambda w: pl.BlockSpec(
        (1, ROW_TILE, w), lambda b, s: (b, jnp.where(ph0(s), s, n_tiles - 1), 0))
    return pl.pallas_call(
        functools.partial(_ssd_kernel, n_tiles=n_tiles, ctx_tiles=ctx_tiles),
        grid=(batch, n_tiles + (n_tiles - ctx_tiles) // 2),
        in_specs=[tile(XBC_W),
                  pl.BlockSpec((1, HALO, XBC_W),
                               lambda b, s: (b, jnp.where(ph0(s), jnp.maximum(s * hb - 1, 0), 0), 0)),
                  pl.BlockSpec((1, HALO, XBC_W),
                               lambda b, s: (b, jnp.where(ph0(s), jnp.minimum((s + 1) * hb, n_hblk - 1), 0), 0)),
                  tile(DT_PAD),
                  pl.BlockSpec((1, ROW_TILE, SSD_W),
                               lambda b, s: (b, jnp.where(ph0(s), 0, ctx_tiles + 2 * (s - n_tiles)), 0)),
                  pl.BlockSpec((1, ROW_TILE, SSD_W),
                               lambda b, s: (b, jnp.where(ph0(s), 0, ctx_tiles + 2 * (s - n_tiles) + 1), 0)),
                  _full((3, XBC_W)), _full((1, XBC_W)), _full((1, DT_PAD)), _full((1, DT_PAD)),
                  _full((1, SSD_W)), _full((1, SSD_W)),
                  _full((DT_PAD, SSD_W)), _full((DT_PAD, SSD_W))],
        out_specs=pl.BlockSpec(
            (1, 2 * ROW_TILE, SSD_W), lambda b, s: (b, jnp.where(ph0(s), 0, s - n_tiles), 0)),
        out_shape=jax.ShapeDtypeStruct((batch, ta - n_ctx, SSD_W), BF16),
        scratch_shapes=[pltpu.VMEM((n_tiles, ROW_TILE, XBC_W), F32),
                        pltpu.VMEM((n_chunks, 2, q, DT_PAD), F32),
                        pltpu.VMEM((n_chunks, 2, SSD_STATE, SSD_W), F32),
                        pltpu.VMEM((n_chunks, 2, 8, SSD_W), F32),
                        pltpu.VMEM((2 * ROW_TILE, SSD_W), F32)],
        compiler_params=_cparams("parallel", "arbitrary"),
        name="ssd_bidir",
    )(xbc, xbc, xbc, dt, z, z, conv_w, conv_b.reshape(1, XBC_W), pad24(a_log), pad24(dt_bias),
      dsk, norm_g.reshape(1, SSD_W), jnp.asarray(ef, BF16), jnp.asarray(eb, BF16))


def _s5_disc_kernel(lr_ref, li_ref, ldt_ref, bre_ref, bim_ref, cre_ref, cim_ref,
                    a_ref, bd_ref, cd_ref):
    lr, li = lr_ref[...], li_ref[...]
    dt = jnp.exp(ldt_ref[...])
    mag = jnp.exp(dt * lr)
    ab_re, ab_im = mag * jnp.cos(dt * li), mag * jnp.sin(dt * li)
    den = lr * lr + li * li
    k_re = ((ab_re - 1.0) * lr + ab_im * li) / den
    k_im = (ab_im * lr - (ab_re - 1.0) * li) / den
    bre, bim = bre_ref[...], bim_ref[...]
    for d in range(2):
        a_ref[d, :, 0:S5_NSTATE] = jnp.broadcast_to(ab_re[d:d + 1], (8, S5_NSTATE))
        a_ref[d, :, S5_NSTATE:] = jnp.broadcast_to(ab_im[d:d + 1], (8, S5_NSTATE))
        kr, ki = k_re[d:d + 1], k_im[d:d + 1]
        bd_ref[d, :, 0:S5_NSTATE] = (kr * bre - ki * bim).astype(BF16)
        bd_ref[d, :, S5_NSTATE:] = (kr * bim + ki * bre).astype(BF16)
        cd_ref[d, 0:S5_NSTATE, :] = cre_ref[d].astype(BF16)
        cd_ref[d, S5_NSTATE:, :] = (-cim_ref[d]).astype(BF16)


def _s5_discretize(lam_re, lam_im, log_dt, b_re, b_im, c_re, c_im):
    eye = jnp.eye(S5_GROUPS, dtype=F32)
    bd = lambda b: jnp.einsum('gph,gk->ghkp', b, eye).reshape(S5_W, S5_NSTATE)
    cd = lambda cc: jnp.einsum('dghp,gk->dgpkh', cc, eye).reshape(2, S5_NSTATE, S5_W)
    ldt = jnp.repeat(log_dt, S5_STATE, axis=-1)
    return pl.pallas_call(
        _s5_disc_kernel,
        out_shape=[jax.ShapeDtypeStruct((2, 8, 2 * S5_NSTATE), F32),
                   jax.ShapeDtypeStruct((2, S5_W, 2 * S5_NSTATE), BF16),
                   jax.ShapeDtypeStruct((2, 2 * S5_NSTATE, S5_W), BF16)],
        compiler_params=pltpu.CompilerParams(vmem_limit_bytes=VMEM_LIMIT_BYTES),
        name="s5_discretize",
    )(lam_re.reshape(2, S5_NSTATE), lam_im.reshape(2, S5_NSTATE), ldt,
      bd(b_re), bd(b_im), cd(c_re), cd(c_im))


S5_TIME_CHUNK = 128


def _s5_scan_kernel(uf_ref, ub_ref, a_ref, bd_ref, cd_ref, yf_ref, yb_ref,
                    hsf_ref, hsb_ref, carry_ref, *, batch):
    j = pl.program_id(0)
    n = S5_NSTATE
    tc = S5_TIME_CHUNK
    n_slab = S5_W // 128
    half = hsf_ref.shape[0] // 2
    chains = ((uf_ref, hsf_ref, yf_ref), (ub_ref, hsb_ref, yb_ref))

    @pl.when(j == 0)
    def _():
        carry_ref[...] = jnp.zeros_like(carry_ref)

    for d, (u_ref, hs_ref, _) in enumerate(chains):
        for r in (0, half):
            u = jnp.concatenate([u_ref[s, r:r + half, :] for s in range(n_slab)], axis=1)
            hs_ref[r:r + half, :] = jnp.dot(u.astype(BF16), bd_ref[d], preferred_element_type=F32)

    for d, (_, hs_ref, _) in enumerate(chains):
        ar = jnp.broadcast_to(a_ref[d, 0:1, 0:n], (batch, n))
        ai = jnp.broadcast_to(a_ref[d, 0:1, n:], (batch, n))
        hr = carry_ref[d, :, 0:n]
        hi = carry_ref[d, :, n:]
        for step in (range(tc) if d == 0 else range(tc - 1, -1, -1)):
            rows = slice(step * batch, (step + 1) * batch)
            hr, hi = (ar * hr - ai * hi + hs_ref[rows, 0:n],
                      ar * hi + ai * hr + hs_ref[rows, n:])
            hs_ref[rows, 0:n] = hr
            hs_ref[rows, n:] = hi
        carry_ref[d, :, 0:n] = hr
        carry_ref[d, :, n:] = hi

    for d, (_, hs_ref, y_ref) in enumerate(chains):
        for r in (0, half):
            y = jnp.dot(hs_ref[r:r + half, :].astype(BF16), cd_ref[d], preferred_element_type=F32)
            for s in range(n_slab):
                y_ref[s, r:r + half, :] = y[:, s * 128:(s + 1) * 128]


def _s5_scan(us_flat, a, bd, cd, batch, n_ctx):
    n_slab, rows_total, _ = us_flat.shape
    ta = rows_total // batch
    tc = S5_TIME_CHUNK
    n_chunks = ta // tc
    ctx_chunks = n_ctx // tc
    blk = tc * batch

    def bwd_chunk(j):
        return jnp.where(j < ctx_chunks, ctx_chunks - 1 - j, n_chunks - 1 - (j - ctx_chunks))

    fwd_spec = pl.BlockSpec((n_slab, blk, 128), lambda j: (0, j, 0))
    bwd_spec = pl.BlockSpec((n_slab, blk, 128), lambda j: (0, bwd_chunk(j), 0))
    return pl.pallas_call(
        functools.partial(_s5_scan_kernel, batch=batch),
        grid=(n_chunks,),
        in_specs=[fwd_spec, bwd_spec, _full((2, 8, 2 * S5_NSTATE)),
                  _full((2, S5_W, 2 * S5_NSTATE)), _full((2, 2 * S5_NSTATE, S5_W))],
        out_specs=[fwd_spec, bwd_spec],
        out_shape=[jax.ShapeDtypeStruct((n_slab, rows_total, 128), F32)] * 2,
        scratch_shapes=[pltpu.VMEM((blk, 2 * S5_NSTATE), F32),
                        pltpu.VMEM((blk, 2 * S5_NSTATE), F32),
                        pltpu.VMEM((2, batch, 2 * S5_NSTATE), F32)],
        compiler_params=_cparams("arbitrary"),
        name="s5_scan",
    )(us_flat, us_flat, a, bd, cd)


def _merge_ln_kernel(gs_ref, yf_ref, yb_ref, us_ref, h_ref, *rest, batch):
    g = BATCH_GROUP
    gate_refs = rest[:g]
    dd_ref, gw_ref, gb_ref, w_ref, g_ref, beta_ref, o_ref, wbf_ref = rest[g:]

    @pl.when(_first_step())
    def _():
        wbf_ref[...] = w_ref[...].astype(BF16)

    def s5_input(i):
        rows = _time_major_rows(pl.program_id(1) * g + i, batch)
        y5 = jnp.concatenate([yf_ref[s, rows, :] + yb_ref[s, rows, :]
                              for s in range(S5_W // 128)], axis=1)
        us = jnp.concatenate([us_ref[s, rows, :] for s in range(S5_W // 128)], axis=1)
        return y5 + dd_ref[...] * us

    ge = jax.nn.gelu(jnp.concatenate([s5_input(i) for i in range(g)], axis=0))
    s5 = ge * jax.nn.sigmoid(
        jnp.dot(ge.astype(BF16), gw_ref[...], preferred_element_type=F32) + gb_ref[...])
    gs = jnp.concatenate([gs_ref[i] for i in range(g)], axis=0)
    y = jnp.dot(gs, wbf_ref[0:SSD_W, :], preferred_element_type=F32)
    y = y + jnp.dot(s5.astype(BF16), wbf_ref[SSD_W:, :], preferred_element_type=F32)
    rows = o_ref.shape[1]
    for i in range(g):
        o_ref[i] = _layer_norm(ALPHA * h_ref[i] + gate_refs[i][0] * y[i * rows:(i + 1) * rows],
                               g_ref[...], beta_ref[...])


def _merge_ln(g_ssd, y5_fwd, y5_bwd, us_t, h, mods, layer, s5_d, glu_w, glu_b, w_out, ln_g, ln_b,
              ctx_tiles):
    batch, ta, _ = h.shape
    nt = ta // ROW_TILE - ctx_tiles
    n_slab = S5_W // 128
    tm_rows = ROW_TILE * batch
    g = BATCH_GROUP
    gate_specs = [
        pl.BlockSpec((1, 1, D_MODEL),
                     lambda t, bg, i=i: ((layer * MOD_ROWS + bg * g + i) * 6 + 2, 0, 0))
        for i in range(g)]
    return pl.pallas_call(
        functools.partial(_merge_ln_kernel, batch=batch),
        grid=(nt, batch // g),
        in_specs=[pl.BlockSpec((g, ROW_TILE, SSD_W), lambda t, b: (b, t, 0)),
                  pl.BlockSpec((n_slab, tm_rows, 128), lambda t, b: (0, t + ctx_tiles, 0)),
                  pl.BlockSpec((n_slab, tm_rows, 128), lambda t, b: (0, t + ctx_tiles, 0)),
                  pl.BlockSpec((n_slab, tm_rows, 128), lambda t, b: (0, t + ctx_tiles, 0)),
                  pl.BlockSpec((g, ROW_TILE, D_MODEL), lambda t, b: (b, t + ctx_tiles, 0))]
        + gate_specs
        + [_full((1, S5_W)), _full((S5_W, S5_W)), _full((1, S5_W)),
           _resident((D_MODEL, D_MODEL)), _full((1, D_MODEL)), _full((1, D_MODEL))],
        out_specs=pl.BlockSpec((g, ROW_TILE, D_MODEL), lambda t, b: (b, t, 0)),
        out_shape=jax.ShapeDtypeStruct((batch, nt * ROW_TILE, D_MODEL), F32),
        scratch_shapes=[pltpu.VMEM((D_MODEL, D_MODEL), BF16)],
        compiler_params=_cparams("arbitrary", "arbitrary"),
        name="merge_outproj_ln",
    )(g_ssd, y5_fwd, y5_bwd, us_t, h, *([mods] * g), s5_d.reshape(1, S5_W), glu_w.astype(BF16),
      glu_b.reshape(1, S5_W), w_out, ln_g.reshape(1, -1), ln_b.reshape(1, -1))


def _attn_layer(h_ctx, h_lat, mods, layer, i, p, keep_ctx):
    n_ctx = h_ctx.shape[1]
    ctx_tiles = n_ctx // ROW_TILE
    lam_init = 0.8 - 0.6 * math.exp(-0.3 * layer)
    cos, sin = _rope_tables(h_lat.shape[1], n_ctx)
    q, k, v, f = _inproj_attn(h_ctx, h_lat, mods, layer, p['attn_w_in'][i],
                              cos, sin, ctx_tiles)
    o_ctx, o_lat = _attention(q, k, v, p['attn_lambda'][i], p['attn_subln_g'][i], lam_init,
                              ctx_tiles)
    fm_ctx, fm_lat = _fourier(f, p['fourier_w'][i], p['fourier_b'][i], n_ctx)
    h1 = _outproj_ln(o_ctx, o_lat, fm_ctx, fm_lat, h_ctx, h_lat, mods, layer,
                     p['attn_w_out'][i],
                     p['ln_g'][layer, 0], p['ln_b'][layer, 0], ctx_tiles)
    return _ffn(h1, mods, layer, p['ffn_w_up_bf16'], p['ffn_b_up'][layer],
                p['ffn_conv_w'][layer], p['ffn_conv_b'][layer],
                p['ffn_w_down'], p['ffn_b_down'][layer],
                p['ln_g'][layer, 1], p['ln_b'][layer, 1], ctx_tiles, 0 if keep_ctx else ctx_tiles,
                group=FFN_BATCH_GROUP)


def _ssm_layer(h, mods, layer, i, n_ctx, p, keep_ctx):
    assert not keep_ctx, "an SSM layer that must also emit context rows is not implemented"
    ctx_tiles = n_ctx // ROW_TILE
    batch = h.shape[0]
    z, xbc, dt, us_t = _inproj_ssm(h, mods, layer, p['ssm_w_in'][i], ctx_tiles)
    g_ssd = _ssd(xbc, dt, z, p['ssd_conv_w'][i], p['ssd_conv_b'][i], p['ssd_a_log'][i],
                 p['ssd_dt_bias'][i], p['ssd_d'][i], p['ssd_norm_g'][i], n_ctx)
    a, bd, cd = _s5_discretize(p['s5_lambda_re'][i], p['s5_lambda_im'][i], p['s5_log_dt'][i],
                               p['s5_b_re'][i], p['s5_b_im'][i], p['s5_c_re'][i], p['s5_c_im'][i])
    y5_fwd, y5_bwd = _s5_scan(us_t, a, bd, cd, batch, n_ctx)
    h1 = _merge_ln(g_ssd, y5_fwd, y5_bwd, us_t, h, mods, layer, p['s5_d'][i], p['s5_glu_w'][i],
                   p['s5_glu_b'][i], p['ssm_w_out'][i],
                   p['ln_g'][layer, 0], p['ln_b'][layer, 0], ctx_tiles)
    return _ffn(h1, mods, layer, p['ffn_w_up_bf16'], p['ffn_b_up'][layer],
                p['ffn_conv_w'][layer], p['ffn_conv_b'][layer],
                p['ffn_w_down'], p['ffn_b_down'][layer],
                p['ln_g'][layer, 1], p['ln_b'][layer, 1], 0, 0, rows=FFN_LAT_ROWS)


def kernel(x, c, ctx, c_ctx, ada_w, ada_b, ln_g, ln_b, ffn_w_up, ffn_b_up, ffn_conv_w, ffn_conv_b, ffn_w_down, ffn_b_down, attn_w_in, attn_lambda, attn_subln_g, fourier_w, fourier_b, attn_w_out, ssm_w_in, ssd_conv_w, ssd_conv_b, ssd_a_log, ssd_dt_bias, ssd_d, ssd_norm_g, s5_lambda_re, s5_lambda_im, s5_log_dt, s5_b_re, s5_b_im, s5_c_re, s5_c_im, s5_d, s5_glu_w, s5_glu_b, ssm_w_out):
    p = dict(ln_g=ln_g, ln_b=ln_b, ffn_w_up=ffn_w_up, ffn_b_up=ffn_b_up, ffn_conv_w=ffn_conv_w,
             ffn_conv_b=ffn_conv_b, ffn_w_down=ffn_w_down, ffn_b_down=ffn_b_down,
             attn_w_in=attn_w_in, attn_lambda=attn_lambda, attn_subln_g=attn_subln_g,
             fourier_w=fourier_w, fourier_b=fourier_b, attn_w_out=attn_w_out, ssm_w_in=ssm_w_in,
             ssd_conv_w=ssd_conv_w, ssd_conv_b=ssd_conv_b, ssd_a_log=ssd_a_log,
             ssd_dt_bias=ssd_dt_bias, ssd_d=ssd_d, ssd_norm_g=ssd_norm_g,
             s5_lambda_re=s5_lambda_re, s5_lambda_im=s5_lambda_im, s5_log_dt=s5_log_dt,
             s5_b_re=s5_b_re, s5_b_im=s5_b_im, s5_c_re=s5_c_re, s5_c_im=s5_c_im, s5_d=s5_d,
             s5_glu_w=s5_glu_w, s5_glu_b=s5_glu_b, ssm_w_out=ssm_w_out)
    batch, n_lat, _ = x.shape
    n_ctx = ctx.shape[1]
    assert n_ctx == ROW_TILE and n_lat % ROW_TILE == 0 and batch < MOD_ROWS
    mods = _ada_mods(c, c_ctx, ada_w, ada_b)
    p['ffn_w_up_bf16'] = ffn_w_up.astype(BF16)
    assert DEPTH == 2
    h = _attn_layer(ctx, x, mods, 0, 0, p, keep_ctx=True)
    return _ssm_layer(h, mods, 1, 0, n_ctx, p, keep_ctx=False)
```
